```python
import math
import jax, jax.numpy as jnp
from jax import lax
import numpy as np

D_MODEL = 1024
BATCH = 16
SEQ = 256
DEPTH = 1
DEC_BATCH = 4
DEC_SEQ = 2048
PAST_LEN = 256

GRID_W = 64
N_HEADS_A = 8
DK_A = 64
DV_A = 2 * DK_A
E_A = N_HEADS_A * DV_A
N_HEADS_R = 8
DK_R = 64
DV_R = 2 * DK_R
E_R = N_HEADS_R * DV_R
CHUNK = 128
Q_BLOCK = 128
ROPE_BASE = 10000.0
QA_W = N_HEADS_A * 2 * DK_A
KA_W = N_HEADS_A * 2 * DK_A
VA_W = E_A
ZA_W = E_A
QR_W = N_HEADS_R * DK_R
KR_W = N_HEADS_R * DK_R
VR_W = E_R
ZR_W = E_R
IN_W = QA_W + KA_W + VA_W + ZA_W + QR_W + KR_W + VR_W + ZR_W
IN_SPLITS = (QA_W, QA_W + KA_W, QA_W + KA_W + VA_W, QA_W + KA_W + VA_W + ZA_W,
             QA_W + KA_W + VA_W + ZA_W + QR_W, QA_W + KA_W + VA_W + ZA_W + QR_W + KR_W,
             QA_W + KA_W + VA_W + ZA_W + QR_W + KR_W + VR_W)
DEEPNORM_ALPHA = (2.0 * DEPTH) ** 0.25
DEEPNORM_BETA = (8.0 * DEPTH) ** -0.25
MOD_EPS = 1e-6
LN_EPS = 1e-5

kernel_name = "diff_retention_adaln_prefix_step"

F32 = jnp.float32


def _layernorm(x, eps):
    xf = x.astype(F32)
    mu = jnp.mean(xf, axis=-1, keepdims=True)
    var = jnp.mean(jnp.square(xf - mu), axis=-1, keepdims=True)
    return (xf - mu) * lax.rsqrt(var + eps)


def _axial_rope(n_tokens):
    rows = n_tokens // GRID_W
    r = jnp.broadcast_to(jnp.arange(rows, dtype=F32)[:, None], (rows, GRID_W)).reshape(-1)
    col = jnp.broadcast_to(jnp.arange(GRID_W, dtype=F32)[None, :], (rows, GRID_W)).reshape(-1)
    n_freq = DK_A // 4
    inv = ROPE_BASE ** (-jnp.arange(n_freq, dtype=F32) / n_freq)
    ang = jnp.concatenate([r[:, None] * inv, col[:, None] * inv], axis=-1)
    return jnp.cos(ang), jnp.sin(ang)


def _apply_rope(x, cos, sin):
    xf = x.astype(F32)
    x1 = xf[..., 0::2]
    x2 = xf[..., 1::2]
    c = cos[None, :, None, None, :]
    s = sin[None, :, None, None, :]
    out = jnp.stack([x1 * c - x2 * s, x1 * s + x2 * c], axis=-1).reshape(x.shape)
    return out.astype(x.dtype)


def _diff_attention(q, k, v, lam):
    b, nq = q.shape[0], q.shape[1]
    nb = nq // Q_BLOCK
    qb = jnp.moveaxis(q.reshape(b, nb, Q_BLOCK, N_HEADS_A, 2, DK_A), 1, 0)
    kf = k.astype(F32)
    vf = v.astype(F32)
    scale = DK_A ** -0.5

    def block(qblk):
        s = jnp.einsum('bqhmd,bkhmd->bmhqk', qblk.astype(F32), kf) * scale
        p = jax.nn.softmax(s, axis=-1)
        w = p[:, 0] - lam * p[:, 1]
        return jnp.einsum('bhqk,bkhd->bqhd', w, vf)

    o = lax.map(block, qb)
    return jnp.moveaxis(o, 0, 1).reshape(b, nq, N_HEADS_A, DV_A)


def _retention_scan(q, k, v, log_gamma, s0):
    b, n, h, dk = q.shape
    dv = v.shape[-1]
    nc = n // CHUNK
    idx = jnp.arange(CHUNK, dtype=F32)
    lg = log_gamma.astype(F32)
    rel = idx[:, None] - idx[None, :]
    decay_in = jnp.where(rel[None] >= 0, jnp.exp(lg[:, None, None] * jnp.maximum(rel, 0.0)[None]), 0.0)
    q_dec = jnp.exp(lg[:, None] * (idx + 1.0))[..., None]
    k_dec = jnp.exp(lg[:, None] * (CHUNK - 1.0 - idx))[..., None]
    chunk_dec = jnp.exp(lg * CHUNK)[:, None, None]

    def to_chunks(a):
        return a.astype(F32).reshape(b, nc, CHUNK, h, a.shape[-1]).transpose(1, 0, 3, 2, 4)

    def step(s, inp):
        qb, kb, vb = inp
        inner = jnp.einsum('bhik,bhjk->bhij', qb, kb) * decay_in
        o = jnp.einsum('bhij,bhjv->bhiv', inner, vb) + jnp.einsum('bhik,bhkv->bhiv', qb, s) * q_dec
        s = s * chunk_dec + jnp.einsum('bhjk,bhjv->bhkv', kb * k_dec, vb)
        return s, o

    s_fin, o = lax.scan(step, s0.astype(F32), (to_chunks(q), to_chunks(k), to_chunks(v)))
    o = o.transpose(1, 0, 3, 2, 4).reshape(b, n, h, dv)
    return o, s_fin


def _mixer_layer(x, cond, p, layer_idx, ctx):
    b, n, _ = x.shape
    mod = jax.nn.silu(cond.astype(F32)) @ p['w_mod'].astype(F32) + p['b_mod'].astype(F32)
    shift, scale, gate = jnp.split(mod[:, None, :], 3, axis=-1)
    h = (_layernorm(x, MOD_EPS) * (1.0 + scale) + shift).astype(x.dtype)

    proj = h @ p['w_in']
    qa, ka, va, za, qr, kr, vr, zr = jnp.split(proj, IN_SPLITS, axis=-1)
    qa = qa.reshape(b, n, N_HEADS_A, 2, DK_A)
    ka = ka.reshape(b, n, N_HEADS_A, 2, DK_A)
    va = va.reshape(b, n, N_HEADS_A, DV_A)
    qr = qr.reshape(b, n, N_HEADS_R, DK_R)
    kr = kr.reshape(b, n, N_HEADS_R, DK_R) * (DK_R ** -0.5)
    vr = vr.reshape(b, n, N_HEADS_R, DV_R)

    if ctx is None:
        k_all, v_all = ka, va
    else:
        cos, sin = _axial_rope(n)
        qa = _apply_rope(qa, cos, sin)
        ka_rot = _apply_rope(ka, cos, sin)
        ctx_k = ctx[0].reshape(b, ctx[0].shape[1], N_HEADS_A, 2, DK_A).astype(ka.dtype)
        k_all = jnp.concatenate([ctx_k, ka_rot], axis=1)
        v_all = jnp.concatenate([ctx[1].astype(va.dtype), va], axis=1)
    lam_init = 0.8 - 0.6 * math.exp(-0.3 * layer_idx)
    lp = p['lam'].astype(F32)
    lam = jnp.exp(jnp.sum(lp[0] * lp[1])) - jnp.exp(jnp.sum(lp[2] * lp[3])) + lam_init
    oa = _diff_attention(qa, k_all, v_all, lam)
    oa = oa * lax.rsqrt(jnp.mean(jnp.square(oa), axis=-1, keepdims=True) + LN_EPS)
    oa = oa * p['subln_g'].astype(F32) * (1.0 - lam_init)
    oa = (oa.reshape(b, n, E_A).astype(x.dtype) * jax.nn.silu(za)) @ p['w_pa']

    log_g = jnp.log1p(-jnp.exp2(p['ret_decay'].astype(F32)))
    if ctx is None:
        s0f = jnp.zeros((b, N_HEADS_R, DK_R, DV_R), F32)
        s0b = s0f
    else:
        s0f, s0b = ctx[2], ctx[3]
    of, sf = _retention_scan(qr, kr, vr, log_g[0], s0f)
    ob, sb = _retention_scan(qr[:, ::-1], kr[:, ::-1], vr[:, ::-1], log_g[1], s0b)
    orr = of + ob[:, ::-1]
    orr = _layernorm(orr, LN_EPS).reshape(b, n, E_R) * p['ret_gn_g'].astype(F32)
    orr = (orr.astype(x.dtype) * jax.nn.silu(zr)) @ p['w_pr']

    g = jax.nn.sigmoid((h @ p['w_gate'] + p['b_gate']).astype(F32))
    ga, gr = jnp.split(g, 2, axis=-1)
    m = (ga * oa.astype(F32) + gr * orr.astype(F32)).astype(x.dtype)
    out = (m @ p['w_out']).astype(F32)
    y = _layernorm(DEEPNORM_ALPHA * x.astype(F32) + gate * out, LN_EPS)
    y = (y * p['ln_g'].astype(F32) + p['ln_b'].astype(F32)).astype(x.dtype)
    if ctx is None:
        ctx_out = (ka.reshape(b, n, N_HEADS_A, 2 * DK_A), va, sf.astype(x.dtype), sb.astype(x.dtype))
        return y, ctx_out
    return y, None


def setup_inputs(seed: int = 0) -> dict:
    key = jax.random.key(seed)
    ks = jax.random.split(key, 24)

    def nrm(k, shape, s):
        return jax.random.normal(k, shape, F32) * s

    return {
        "x_prompt": nrm(ks[0], (BATCH, SEQ, D_MODEL), 1.0),
        "x_sample": nrm(ks[1], (DEC_BATCH, DEC_SEQ, D_MODEL), 1.0),
        "cache_attn_k": nrm(ks[2], (DEC_BATCH, DEPTH, PAST_LEN, N_HEADS_A, 2 * DK_A), 1.0),
        "cache_attn_v": nrm(ks[3], (DEC_BATCH, DEPTH, PAST_LEN, N_HEADS_A, DV_A), 1.0),
        "state_ret_fwd": nrm(ks[4], (DEC_BATCH, DEPTH, N_HEADS_R, DK_R, DV_R), 0.5),
        "state_ret_bwd": nrm(ks[5], (DEC_BATCH, DEPTH, N_HEADS_R, DK_R, DV_R), 0.5),
        "c": nrm(ks[6], (DEC_BATCH, D_MODEL), 1.0),
        "c_ctx": nrm(ks[7], (D_MODEL,), 1.0),
        "w_mod": nrm(ks[8], (DEPTH, D_MODEL, 3 * D_MODEL), 0.5 * D_MODEL ** -0.5),
        "b_mod": nrm(ks[9], (DEPTH, 3 * D_MODEL), 0.02),
        "w_in": nrm(ks[10], (DEPTH, D_MODEL, IN_W), D_MODEL ** -0.5),
        "lam_params": nrm(ks[11], (DEPTH, 4, DK_A), 0.1),
        "subln_g": 1.0 + nrm(ks[12], (DEPTH, DV_A), 0.02),
        "ret_decay": (-5.0 - jnp.arange(N_HEADS_R, dtype=F32))[None, None, :] + nrm(ks[13], (DEPTH, 2, N_HEADS_R), 0.1),
        "ret_gn_g": 1.0 + nrm(ks[14], (DEPTH, E_R), 0.02),
        "w_pa": nrm(ks[15], (DEPTH, E_A, D_MODEL), DEEPNORM_BETA * E_A ** -0.5),
        "w_pr": nrm(ks[16], (DEPTH, E_R, D_MODEL), DEEPNORM_BETA * E_R ** -0.5),
        "w_gate": nrm(ks[17], (DEPTH, D_MODEL, 2 * D_MODEL), D_MODEL ** -0.5),
        "b_gate": nrm(ks[18], (DEPTH, 2 * D_MODEL), 0.02),
        "w_out": nrm(ks[19], (DEPTH, D_MODEL, D_MODEL), DEEPNORM_BETA * D_MODEL ** -0.5),
        "ln_g": 1.0 + nrm(ks[20], (DEPTH, D_MODEL), 0.02),
        "ln_b": nrm(ks[21], (DEPTH, D_MODEL), 0.02),
    }


def reference(x_prompt, x_sample, cache_attn_k, cache_attn_v, state_ret_fwd, state_ret_bwd,
              c, c_ctx, w_mod, b_mod, w_in, lam_params, subln_g, ret_decay, ret_gn_g,
              w_pa, w_pr, w_gate, b_gate, w_out, ln_g, ln_b):
    y_prompt = x_prompt
    y_sample = x_sample
    cond_ctx = jnp.broadcast_to(c_ctx[None, :], (x_prompt.shape[0], D_MODEL))
    k_list, v_list, sf_list, sb_list = [], [], [], []
    for l in range(DEPTH):
        p = {"w_mod": w_mod[l], "b_mod": b_mod[l], "w_in": w_in[l], "lam": lam_params[l],
             "subln_g": subln_g[l], "ret_decay": ret_decay[l], "ret_gn_g": ret_gn_g[l],
             "w_pa": w_pa[l], "w_pr": w_pr[l], "w_gate": w_gate[l], "b_gate": b_gate[l],
             "w_out": w_out[l], "ln_g": ln_g[l], "ln_b": ln_b[l]}
        y_prompt, (k_l, v_l, sf_l, sb_l) = _mixer_layer(y_prompt, cond_ctx, p, l, None)
        k_list.append(k_l)
        v_list.append(v_l)
        sf_list.append(sf_l)
        sb_list.append(sb_l)
        ctx = (cache_attn_k[:, l], cache_attn_v[:, l], state_ret_fwd[:, l], state_ret_bwd[:, l])
        y_sample, _ = _mixer_layer(y_sample, c, p, l, ctx)
    new_attn_k = jnp.stack(k_list, axis=1)
    new_attn_v = jnp.stack(v_list, axis=1)
    new_ret_fwd = jnp.stack(sf_list, axis=1)
    new_ret_bwd = jnp.stack(sb_list, axis=1)
    return (y_prompt, y_sample, new_attn_k, new_attn_v, new_ret_fwd, new_ret_bwd)
```

```python
import functools
import math

import jax
import jax.numpy as jnp
import numpy as np
from jax import lax
from jax.experimental import pallas as pl
from jax.experimental.pallas import tpu as pltpu

F32 = jnp.float32
BF16 = jnp.bfloat16

D_MODEL = 1024
N_HEADS = 8
DK_A = 64
DV = 128
DK_R = 64
CHUNK = 128
GRID_W = 64
ROPE_BASE = 10000.0
MOD_EPS = 1e-6
LN_EPS = 1e-5
DEPTH = 1
DEEPNORM_ALPHA = (2.0 * DEPTH) ** 0.25
LAM_INIT = 0.8 - 0.6 * math.exp(-0.3 * 0)

LANES = 128
QA_BLK, KA_BLK, VA_BLK, ZA_BLK = 0, 8, 16, 24
QR_BLK, KR_BLK, VR_BLK, ZR_BLK = 32, 36, 40, 48
P_WIDTH = 9 * D_MODEL
VMEM_LIMIT = 56 * 1024 * 1024


def _params(sem):
    return pltpu.CompilerParams(dimension_semantics=sem, vmem_limit_bytes=VMEM_LIMIT)


def _silu(z):
    return z * (1.0 / (1.0 + jnp.exp(-z)))


def _dot(a, b):
    return jnp.dot(a, b, preferred_element_type=F32)


def _dot_nt(a, b):
    return lax.dot_general(a, b, (((1,), (1,)), ((), ())), preferred_element_type=F32)


def _dot_tn(a, b):
    return lax.dot_general(a, b, (((0,), (0,)), ((), ())), preferred_element_type=F32)


def _mod_kernel(cond_ref, w_ref, b_ref, o_ref):
    s = _silu(cond_ref[...])
    o_ref[...] = _dot(s.astype(BF16), w_ref[...].astype(BF16)) + b_ref[...]


def _mod_call(cond8, w_mod, b_mod):
    tn = D_MODEL
    return pl.pallas_call(
        _mod_kernel,
        grid=(3 * D_MODEL // tn,),
        in_specs=[pl.BlockSpec((8, D_MODEL), lambda j: (0, 0)),
                  pl.BlockSpec((D_MODEL, tn), lambda j: (0, j)),
                  pl.BlockSpec((1, tn), lambda j: (0, j))],
        out_specs=pl.BlockSpec((8, tn), lambda j: (0, j)),
        out_shape=jax.ShapeDtypeStruct((8, 3 * D_MODEL), F32),
        compiler_params=_params(("arbitrary",)),
        name="mod",
    )(cond8, w_mod, b_mod)


def _rope(acc, cos, sin_even, sin_odd):
    outs = []
    for hh in range(N_HEADS):
        xs = acc[:, hh * LANES:(hh + 1) * LANES]
        nxt = pltpu.roll(xs, LANES - 1, 1)
        prv = pltpu.roll(xs, 1, 1)
        outs.append(xs * cos + nxt * sin_even + prv * sin_odd)
    return jnp.concatenate(outs, axis=1)


def _proj_kernel(*refs, latent):
    if latent:
        x_ref, mod_ref, w_ref, bg_ref, cos_ref, se_ref, so_ref, p_ref = refs
    else:
        x_ref, mod_ref, w_ref, bg_ref, p_ref, k_ref, v_ref = refs
    x = x_ref[...]
    mu = jnp.mean(x, axis=-1, keepdims=True)
    xc = x - mu
    var = jnp.mean(xc * xc, axis=-1, keepdims=True)
    shift = mod_ref[0, :, 0:D_MODEL]
    scale = mod_ref[0, :, D_MODEL:2 * D_MODEL]
    h = (xc * lax.rsqrt(var + MOD_EPS) * (1.0 + scale) + shift).astype(BF16)

    def seg(j):
        return _dot(h, w_ref[:, j * D_MODEL:(j + 1) * D_MODEL])

    def put(j, val):
        p_ref[:, j * D_MODEL:(j + 1) * D_MODEL] = val.astype(BF16)

    qa = seg(0)
    ka = seg(1)
    if latent:
        cos, se, so = cos_ref[...], se_ref[...], so_ref[...]
        qa = _rope(qa, cos, se, so)
        ka = _rope(ka, cos, se, so)
    else:
        k_ref[...] = ka
    put(0, qa * (DK_A ** -0.5))
    put(1, ka)
    va = seg(2)
    if not latent:
        v_ref[...] = va
    put(2, va)
    put(3, _silu(seg(3)))
    qk = seg(4)
    put(4, jnp.concatenate([qk[:, :D_MODEL // 2], qk[:, D_MODEL // 2:] * (DK_R ** -0.5)], axis=1))
    put(5, seg(5))
    put(6, _silu(seg(6)))
    for j in (7, 8):
        g = seg(j) + bg_ref[:, (j - 7) * D_MODEL:(j - 6) * D_MODEL]
        put(j, 1.0 / (1.0 + jnp.exp(-g)))


def _proj_call(x2d, mod3, w_all, b_gate, rope, *, latent, seq, mod_row0):
    m = x2d.shape[0]
    tm = 256
    nt = m // tm
    per_seq = seq // tm

    def row(i):
        return (mod_row0 + i // per_seq) if latent else mod_row0

    in_specs = [
        pl.BlockSpec((tm, D_MODEL), lambda i: (i, 0)),
        pl.BlockSpec((1, 1, 3 * D_MODEL), lambda i: (row(i), 0, 0)),
        pl.BlockSpec((D_MODEL, P_WIDTH), lambda i: (0, 0), pipeline_mode=pl.Buffered(1)),
        pl.BlockSpec((1, 2 * D_MODEL), lambda i: (0, 0)),
    ]
    args = [x2d, mod3, w_all, b_gate]
    out_shape = [jax.ShapeDtypeStruct((m, P_WIDTH), BF16)]
    out_specs = [pl.BlockSpec((tm, P_WIDTH), lambda i: (i, 0))]
    if latent:
        for t in rope:
            in_specs.append(pl.BlockSpec((tm, LANES), lambda i: (i % per_seq, 0)))
            args.append(t)
    else:
        for _ in range(2):
            out_shape.append(jax.ShapeDtypeStruct((m, D_MODEL), F32))
            out_specs.append(pl.BlockSpec((tm, D_MODEL), lambda i: (i, 0)))
    return pl.pallas_call(
        functools.partial(_proj_kernel, latent=latent),
        grid=(nt,),
        in_specs=in_specs,
        out_specs=out_specs,
        out_shape=out_shape,
        compiler_params=_params(("arbitrary",)),
        name="proj_lat" if latent else "proj_ctx",
    )(*args)


def _lam(lp_ref):
    lp = lp_ref[...]
    a = jnp.sum(lp[0:1] * lp[1:2], axis=-1, keepdims=True)
    b = jnp.sum(lp[2:3] * lp[3:4], axis=-1, keepdims=True)
    return jnp.exp(a) - jnp.exp(b) + LAM_INIT


def _attn_rows(q, k_all, v_all, z, lam, g):
    lane = lax.broadcasted_iota(jnp.int32, (1, LANES), 1)
    zero = jnp.zeros_like(q)
    outs = []
    for qm in (jnp.where(lane < DK_A, q, zero), jnp.where(lane >= DK_A, q, zero)):
        s = _dot_nt(qm, k_all)
        mx = jnp.max(s, axis=-1, keepdims=True)
        p = jnp.exp(s - mx)
        l = jnp.sum(p, axis=-1, keepdims=True)
        outs.append(_dot(p.astype(BF16), v_all) * (1.0 / l))
    oa = outs[0] - lam * outs[1]
    oa = oa * lax.rsqrt(jnp.mean(oa * oa, axis=-1, keepdims=True) + LN_EPS)
    oa = oa * g * (1.0 - LAM_INIT)
    return oa * z.astype(F32)


def _attn_ctx_kernel(q_ref, k_ref, v_ref, z_ref, lp_ref, g_ref, o_ref):
    lam = _lam(lp_ref)
    g = g_ref[...]
    for h in range(N_HEADS):
        sl = slice(h * LANES, (h + 1) * LANES)
        o_ref[:, sl] = _attn_rows(q_ref[:, sl], k_ref[:, sl], v_ref[:, sl], z_ref[:, sl],
                                  lam, g).astype(BF16)


def _attn_ctx_call(p, lam_params, subln_g, *, batch, seq):
    blk = lambda j: pl.BlockSpec((seq, D_MODEL), lambda b: (b, j))
    return pl.pallas_call(
        _attn_ctx_kernel,
        grid=(batch,),
        in_specs=[blk(0), blk(1), blk(2), blk(3),
                  pl.BlockSpec((4, DK_A), lambda b: (0, 0)),
                  pl.BlockSpec((1, DV), lambda b: (0, 0))],
        out_specs=pl.BlockSpec((seq, D_MODEL), lambda b: (b, 0)),
        out_shape=jax.ShapeDtypeStruct((batch * seq, D_MODEL), BF16),
        compiler_params=_params(("arbitrary",)),
        name="attn_ctx",
    )(p, p, p, p, lam_params, subln_g)


def _attn_lat_kernel(q_ref, k_ref, v_ref, z_ref, kc_ref, vc_ref, lp_ref, g_ref, o_ref,
                     k_all, v_all, *, past, tc):
    @pl.when(pl.program_id(2) == 0)
    def _():
        k_all[0:past, :] = kc_ref[...].astype(BF16)
        v_all[0:past, :] = vc_ref[...].astype(BF16)
        k_all[past:, :] = k_ref[...]
        v_all[past:, :] = v_ref[...]

    lam = _lam(lp_ref)
    g = g_ref[...]

    def body(c, carry):
        rows = pl.ds(pl.multiple_of(c * tc, tc), tc)
        o_ref[rows, :] = _attn_rows(q_ref[rows, :], k_all[...], v_all[...], z_ref[rows, :],
                                    lam, g).astype(BF16)
        return carry

    lax.fori_loop(0, q_ref.shape[0] // tc, body, 0)


def _attn_lat_call(p, cache_k, cache_v, lam_params, subln_g, *, batch, seq, past):
    tq, tc = 1024, 256
    nq = seq // tq
    return pl.pallas_call(
        functools.partial(_attn_lat_kernel, past=past, tc=tc),
        grid=(batch, N_HEADS, nq),
        in_specs=[
            pl.BlockSpec((tq, LANES), lambda b, h, i: (b * nq + i, QA_BLK + h)),
            pl.BlockSpec((seq, LANES), lambda b, h, i: (b, KA_BLK + h)),
            pl.BlockSpec((seq, LANES), lambda b, h, i: (b, VA_BLK + h)),
            pl.BlockSpec((tq, LANES), lambda b, h, i: (b * nq + i, ZA_BLK + h)),
            pl.BlockSpec((None, past, LANES), lambda b, h, i: (b, 0, h)),
            pl.BlockSpec((None, past, LANES), lambda b, h, i: (b, 0, h)),
            pl.BlockSpec((4, DK_A), lambda b, h, i: (0, 0)),
            pl.BlockSpec((1, DV), lambda b, h, i: (0, 0)),
        ],
        out_specs=pl.BlockSpec((tq, LANES), lambda b, h, i: (b * nq + i, h)),
        out_shape=jax.ShapeDtypeStruct((batch * seq, D_MODEL), BF16),
        scratch_shapes=[pltpu.VMEM((past + seq, LANES), BF16),
                        pltpu.VMEM((past + seq, LANES), BF16)],
        compiler_params=_params(("arbitrary", "arbitrary", "arbitrary")),
        name="attn_lat",
    )(p, p, p, p, cache_k, cache_v, lam_params, subln_g)


def _ret_kernel(*refs, latent, nc):
    if latent:
        (rd_ref, qp_ref, kp_ref, v_ref, z_ref, g_ref, s0f_ref, s0b_ref,
         o_ref, sb_scr, s_scr) = refs
    else:
        (rd_ref, qp_ref, kp_ref, v_ref, z_ref, g_ref,
         o_ref, sf_ref, sbo_ref, sb_scr, s_scr) = refs
    h = pl.program_id(1)
    par = h % 2
    half = pl.ds(pl.multiple_of(par * DK_R, DK_R), DK_R)

    def log_gamma(d):
        r = jnp.full((1, LANES), rd_ref[d, h], F32)
        return jnp.log1p(-jnp.exp2(r))

    lgf, lgb = log_gamma(0), log_gamma(1)
    ii = lax.broadcasted_iota(jnp.int32, (CHUNK, CHUNK), 0).astype(F32)
    jj = lax.broadcasted_iota(jnp.int32, (CHUNK, CHUNK), 1).astype(F32)
    rel = ii - jj
    dec = (jnp.where(rel >= 0, jnp.exp(lgf * jnp.maximum(rel, 0.0)), 0.0)
           + jnp.where(rel <= 0, jnp.exp(lgb * jnp.maximum(-rel, 0.0)), 0.0))
    qdec_f = jnp.exp(lgf * (ii + 1.0))
    qdec_b = jnp.exp(lgb * (CHUNK - ii))
    kdec_f = jnp.exp(lgf * (CHUNK - 1.0 - ii))
    kdec_b = jnp.exp(lgb * ii)
    cd_f = jnp.exp(lgf * CHUNK)
    cd_b = jnp.exp(lgb * CHUNK)

    def rows(c):
        return pl.ds(pl.multiple_of(c * CHUNK, CHUNK), CHUNK)

    def init_state(s0_ref):
        s_scr[...] = jnp.zeros((2 * DK_R, DV), F32)
        if latent:
            s_scr[half, :] = s0_ref[...]
        return s_scr[...]

    def bwd(t, s):
        c = nc - 1 - t
        sb_scr[c] = s.astype(BF16)
        kd = (kp_ref[rows(c), :].astype(F32) * kdec_b).astype(BF16)
        return s * cd_b + _dot_tn(kd, v_ref[rows(c), :])

    s_b = lax.fori_loop(0, nc, bwd, init_state(s0b_ref if latent else None))
    if not latent:
        s_scr[...] = s_b
        sbo_ref[...] = s_scr[half, :]

    lane = lax.broadcasted_iota(jnp.int32, (1, LANES), 1)
    mine = (lane // DK_R) == par
    g = g_ref[...]

    def fwd(c, s):
        r = rows(c)
        qp = qp_ref[r, :]
        q = jnp.where(mine, qp, jnp.zeros_like(qp))
        k = kp_ref[r, :]
        v = v_ref[r, :]
        a = _dot_nt(q, k) * dec
        o = (_dot(a.astype(BF16), v)
             + _dot(q, s.astype(BF16)) * qdec_f
             + _dot(q, sb_scr[c]) * qdec_b)
        mu = jnp.mean(o, axis=-1, keepdims=True)
        oc = o - mu
        var = jnp.mean(oc * oc, axis=-1, keepdims=True)
        o = oc * lax.rsqrt(var + LN_EPS) * g * z_ref[r, :].astype(F32)
        o_ref[r, :] = o.astype(BF16)
        kd = (k.astype(F32) * kdec_f).astype(BF16)
        return s * cd_f + _dot_tn(kd, v)

    s_f = lax.fori_loop(0, nc, fwd, init_state(s0f_ref if latent else None))
    if not latent:
        s_scr[...] = s_f
        sf_ref[...] = s_scr[half, :]


def _ret_call(p, ret_decay, gn_g, s0f, s0b, *, latent, batch, seq):
    nc = seq // CHUNK
    col = lambda base, per: pl.BlockSpec((seq, LANES), lambda b, h: (b, base + h // per))
    state = pl.BlockSpec((None, None, DK_R, DV), lambda b, h: (b, h, 0, 0))
    in_specs = [pl.BlockSpec(memory_space=pltpu.SMEM),
                col(QR_BLK, 2), col(KR_BLK, 2), col(VR_BLK, 1), col(ZR_BLK, 1),
                pl.BlockSpec((1, DV), lambda b, h: (0, h))]
    args = [ret_decay, p, p, p, p, gn_g]
    out_shape = [jax.ShapeDtypeStruct((batch * seq, D_MODEL), BF16)]
    out_specs = [pl.BlockSpec((seq, LANES), lambda b, h: (b, h))]
    if latent:
        in_specs += [state, state]
        args += [s0f, s0b]
    else:
        out_shape += [jax.ShapeDtypeStruct((batch, N_HEADS, DK_R, DV), F32)] * 2
        out_specs += [state, state]
    return pl.pallas_call(
        functools.partial(_ret_kernel, latent=latent, nc=nc),
        grid=(batch, N_HEADS),
        in_specs=in_specs,
        out_specs=out_specs,
        out_shape=out_shape,
        scratch_shapes=[pltpu.VMEM((nc, 2 * DK_R, DV), BF16),
                        pltpu.VMEM((2 * DK_R, DV), F32)],
        compiler_params=_params(("arbitrary", "arbitrary")),
        name="ret_lat" if latent else "ret_ctx",
    )(*args)


def _out_kernel(oa_ref, or_ref, ga_ref, gr_ref, x_ref, mod_ref, wpa_ref, wpr_ref, wo_ref,
                lng_ref, lnb_ref, y_ref):
    a = _dot(oa_ref[...], wpa_ref[...])
    r = _dot(or_ref[...], wpr_ref[...])
    m = ga_ref[...].astype(F32) * a + gr_ref[...].astype(F32) * r
    out = _dot(m.astype(BF16), wo_ref[...])
    gate = mod_ref[0, :, 2 * D_MODEL:3 * D_MODEL]
    t = DEEPNORM_ALPHA * x_ref[...] + gate * out
    mu = jnp.mean(t, axis=-1, keepdims=True)
    tc = t - mu
    var = jnp.mean(tc * tc, axis=-1, keepdims=True)
    y_ref[...] = tc * lax.rsqrt(var + LN_EPS) * lng_ref[...] + lnb_ref[...]


def _out_call(oa, orr, p, x2d, mod3, w_pa, w_pr, w_out, ln_g, ln_b, *, latent, seq, mod_row0):
    m = x2d.shape[0]
    tm = 512
    per_seq = max(seq // tm, 1)

    def row(i):
        return (mod_row0 + i // per_seq) if latent else mod_row0

    tile = lambda j: pl.BlockSpec((tm, D_MODEL), lambda i: (i, j))
    full = lambda shape: pl.BlockSpec(shape, lambda i: tuple(0 for _ in shape))
    return pl.pallas_call(
        _out_kernel,
        grid=(m // tm,),
        in_specs=[tile(0), tile(0), tile(7), tile(8), tile(0),
                  pl.BlockSpec((1, 1, 3 * D_MODEL), lambda i: (row(i), 0, 0)),
                  full((D_MODEL, D_MODEL)), full((D_MODEL, D_MODEL)), full((D_MODEL, D_MODEL)),
                  full((1, D_MODEL)), full((1, D_MODEL))],
        out_specs=tile(0),
        out_shape=jax.ShapeDtypeStruct((m, D_MODEL), F32),
        compiler_params=_params(("arbitrary",)),
        name="out_lat" if latent else "out_ctx",
    )(oa, orr, p, p, x2d, mod3, w_pa, w_pr, w_out, ln_g, ln_b)


def _rope_tables(n_tokens):
    rows = n_tokens // GRID_W
    r = np.repeat(np.arange(rows, dtype=np.float32), GRID_W)
    col = np.tile(np.arange(GRID_W, dtype=np.float32), rows)
    n_freq = DK_A // 4
    inv = jnp.asarray(ROPE_BASE, F32) ** (-jnp.arange(n_freq, dtype=F32) / n_freq)
    ang = jnp.concatenate([jnp.asarray(r)[:, None] * inv, jnp.asarray(col)[:, None] * inv], axis=-1)
    cos = jnp.repeat(jnp.cos(ang), 2, axis=-1)
    sin = jnp.repeat(jnp.sin(ang), 2, axis=-1)
    even = (jnp.arange(DK_A) % 2 == 0)[None, :]
    sin_even = jnp.where(even, -sin, 0.0)
    sin_odd = jnp.where(even, 0.0, sin)
    two = lambda t: jnp.concatenate([t, t], axis=-1).astype(F32)
    return two(cos), two(sin_even), two(sin_odd)


def kernel(x_prompt, x_sample, cache_attn_k, cache_attn_v, state_ret_fwd, state_ret_bwd,
           c, c_ctx, w_mod, b_mod, w_in, lam_params, subln_g, ret_decay, ret_gn_g,
           w_pa, w_pr, w_gate, b_gate, w_out, ln_g, ln_b):
    batch, seq, _ = x_prompt.shape
    dbatch, dseq, _ = x_sample.shape
    past = cache_attn_k.shape[2]
    l = 0

    cond8 = jnp.concatenate([c_ctx[None, :], c, jnp.zeros((8 - 1 - dbatch, D_MODEL), F32)], axis=0)
    mod3 = _mod_call(cond8, w_mod[l], b_mod[l][None, :]).reshape(8, 1, 3 * D_MODEL)

    w_all = jnp.concatenate([w_in[l], w_gate[l]], axis=1).astype(BF16)
    bg = b_gate[l][None, :]
    wpa, wpr, wo = w_pa[l].astype(BF16), w_pr[l].astype(BF16), w_out[l].astype(BF16)
    lp, sg = lam_params[l], subln_g[l][None, :]
    rd, gg = ret_decay[l], ret_gn_g[l][None, :]
    lng, lnb = ln_g[l][None, :], ln_b[l][None, :]

    xc = x_prompt.reshape(batch * seq, D_MODEL)
    p_c, k_c, v_c = _proj_call(xc, mod3, w_all, bg, None, latent=False, seq=seq, mod_row0=0)
    oa_c = _attn_ctx_call(p_c, lp, sg, batch=batch, seq=seq)
    or_c, sf_c, sb_c = _ret_call(p_c, rd, gg, None, None, latent=False, batch=batch, seq=seq)
    y_c = _out_call(oa_c, or_c, p_c, xc, mod3, wpa, wpr, wo, lng, lnb,
                    latent=False, seq=seq, mod_row0=0)

    xs = x_sample.reshape(dbatch * dseq, D_MODEL)
    (p_s,) = _proj_call(xs, mod3, w_all, bg, _rope_tables(dseq), latent=True, seq=dseq, mod_row0=1)
    ck = cache_attn_k[:, l].reshape(dbatch, past, N_HEADS * 2 * DK_A)
    cv = cache_attn_v[:, l].reshape(dbatch, past, N_HEADS * DV)
    oa_s = _attn_lat_call(p_s, ck, cv, lp, sg, batch=dbatch, seq=dseq, past=past)
    (or_s,) = _ret_call(p_s, rd, gg, state_ret_fwd[:, l], state_ret_bwd[:, l],
                        latent=True, batch=dbatch, seq=dseq)
    y_s = _out_call(oa_s, or_s, p_s, xs, mod3, wpa, wpr, wo, lng, lnb,
                    latent=True, seq=dseq, mod_row0=1)

    return (y_c.reshape(batch, seq, D_MODEL),
            y_s.reshape(dbatch, dseq, D_MODEL),
            k_c.reshape(batch, 1, seq, N_HEADS, 2 * DK_A),
            v_c.reshape(batch, 1, seq, N_HEADS, DV),
            sf_c.reshape(batch, 1, N_HEADS, DK_R, DV),
            sb_c.reshape(batch, 1, N_HEADS, DK_R, DV))
```

```python
import functools
import math

import jax
import jax.numpy as jnp
import numpy as np
from jax import lax
from jax.experimental import pallas as pl
from jax.experimental.pallas import tpu as pltpu

F32 = jnp.float32
BF16 = jnp.bfloat16

D_MODEL = 1024
N_HEADS = 8
DK_A = 64
DV = 128
DK_R = 64
CHUNK = 128
GRID_W = 64
ROPE_BASE = 10000.0
MOD_EPS = 1e-6
LN_EPS = 1e-5
DEPTH = 1
DEEPNORM_ALPHA = (2.0 * DEPTH) ** 0.25
LAM_INIT = 0.8 - 0.6 * math.exp(-0.3 * 0)

LANES = 128
QA_BLK, KA_BLK, VA_BLK, ZA_BLK = 0, 8, 16, 24
QR_BLK, KR_BLK, VR_BLK, ZR_BLK = 32, 36, 40, 48
P_WIDTH = 9 * D_MODEL
VMEM_LIMIT = 56 * 1024 * 1024


def _params(sem):
    return pltpu.CompilerParams(dimension_semantics=sem, vmem_limit_bytes=VMEM_LIMIT)


def _silu(z):
    return z * (1.0 / (1.0 + jnp.exp(-z)))


def _dot(a, b):
    return jnp.dot(a, b, preferred_element_type=F32)


def _dot_nt(a, b):
    return lax.dot_general(a, b, (((1,), (1,)), ((), ())), preferred_element_type=F32)


def _dot_tn(a, b):
    return lax.dot_general(a, b, (((0,), (0,)), ((), ())), preferred_element_type=F32)


def _mod_kernel(cond_ref, w_ref, b_ref, o_ref):
    s = _silu(cond_ref[...])
    o_ref[...] = _dot(s.astype(BF16), w_ref[...].astype(BF16)) + b_ref[...]


def _mod_call(cond8, w_mod, b_mod):
    tn = D_MODEL
    return pl.pallas_call(
        _mod_kernel,
        grid=(3 * D_MODEL // tn,),
        in_specs=[pl.BlockSpec((8, D_MODEL), lambda j: (0, 0)),
                  pl.BlockSpec((D_MODEL, tn), lambda j: (0, j)),
                  pl.BlockSpec((1, tn), lambda j: (0, j))],
        out_specs=pl.BlockSpec((8, tn), lambda j: (0, j)),
        out_shape=jax.ShapeDtypeStruct((8, 3 * D_MODEL), F32),
        compiler_params=_params(("arbitrary",)),
        name="mod",
    )(cond8, w_mod, b_mod)


def _rope(acc, cos, sin_even, sin_odd):
    outs = []
    for hh in range(N_HEADS):
        xs = acc[:, hh * LANES:(hh + 1) * LANES]
        nxt = pltpu.roll(xs, LANES - 1, 1)
        prv = pltpu.roll(xs, 1, 1)
        outs.append(xs * cos + nxt * sin_even + prv * sin_odd)
    return jnp.concatenate(outs, axis=1)


def _proj_kernel(*refs, latent):
    if latent:
        x_ref, mod_ref, w_ref, bg_ref, cos_ref, se_ref, so_ref, p_ref = refs
    else:
        x_ref, mod_ref, w_ref, bg_ref, p_ref, k_ref, v_ref = refs
    x = x_ref[...]
    mu = jnp.mean(x, axis=-1, keepdims=True)
    xc = x - mu
    var = jnp.mean(xc * xc, axis=-1, keepdims=True)
    shift = mod_ref[0, :, 0:D_MODEL]
    scale = mod_ref[0, :, D_MODEL:2 * D_MODEL]
    h = (xc * lax.rsqrt(var + MOD_EPS) * (1.0 + scale) + shift).astype(BF16)

    def seg(j):
        return _dot(h, w_ref[:, j * D_MODEL:(j + 1) * D_MODEL])

    def put(j, val):
        p_ref[:, j * D_MODEL:(j + 1) * D_MODEL] = val.astype(BF16)

    qa = seg(0)
    ka = seg(1)
    if latent:
        cos, se, so = cos_ref[...], se_ref[...], so_ref[...]
        qa = _rope(qa, cos, se, so)
        ka = _rope(ka, cos, se, so)
    else:
        k_ref[...] = ka
    put(0, qa * (DK_A ** -0.5))
    put(1, ka)
    va = seg(2)
    if not latent:
        v_ref[...] = va
    put(2, va)
    put(3, _silu(seg(3)))
    qk = seg(4)
    put(4, jnp.concatenate([qk[:, :D_MODEL // 2], qk[:, D_MODEL // 2:] * (DK_R ** -0.5)], axis=1))
    put(5, seg(5))
    put(6, _silu(seg(6)))
    for j in (7, 8):
        g = seg(j) + bg_ref[:, (j - 7) * D_MODEL:(j - 6) * D_MODEL]
        put(j, 1.0 / (1.0 + jnp.exp(-g)))


def _proj_call(x2d, mod3, w_all, b_gate, rope, *, latent, seq, mod_row0):
    m = x2d.shape[0]
    tm = 256
    nt = m // tm
    per_seq = seq // tm

    def row(i):
        return (mod_row0 + i // per_seq) if latent else mod_row0

    in_specs = [
        pl.BlockSpec((tm, D_MODEL), lambda i: (i, 0)),
        pl.BlockSpec((1, 1, 3 * D_MODEL), lambda i: (row(i), 0, 0)),
        pl.BlockSpec((D_MODEL, P_WIDTH), lambda i: (0, 0), pipeline_mode=pl.Buffered(1)),
        pl.BlockSpec((1, 2 * D_MODEL), lambda i: (0, 0)),
    ]
    args = [x2d, mod3, w_all, b_gate]
    out_shape = [jax.ShapeDtypeStruct((m, P_WIDTH), BF16)]
    out_specs = [pl.BlockSpec((tm, P_WIDTH), lambda i: (i, 0))]
    if latent:
        for t in rope:
            in_specs.append(pl.BlockSpec((tm, LANES), lambda i: (i % per_seq, 0)))
            args.append(t)
    else:
        for _ in range(2):
            out_shape.append(jax.ShapeDtypeStruct((m, D_MODEL), F32))
            out_specs.append(pl.BlockSpec((tm, D_MODEL), lambda i: (i, 0)))
    return pl.pallas_call(
        functools.partial(_proj_kernel, latent=latent),
        grid=(nt,),
        in_specs=in_specs,
        out_specs=out_specs,
        out_shape=out_shape,
        compiler_params=_params(("arbitrary",)),
        name="proj_lat" if latent else "proj_ctx",
    )(*args)


def _lam(lp_ref):
    lp = lp_ref[...]
    a = jnp.sum(lp[0:1] * lp[1:2], axis=-1, keepdims=True)
    b = jnp.sum(lp[2:3] * lp[3:4], axis=-1, keepdims=True)
    return jnp.exp(a) - jnp.exp(b) + LAM_INIT


def _attn_rows(q, k_all, v_all, z, lam, g):
    lane = lax.broadcasted_iota(jnp.int32, (1, LANES), 1)
    zero = jnp.zeros_like(q)
    outs = []
    for qm in (jnp.where(lane < DK_A, q, zero), jnp.where(lane >= DK_A, q, zero)):
        s = _dot_nt(qm, k_all)
        mx = jnp.max(s, axis=-1, keepdims=True)
        p = jnp.exp(s - mx)
        l = jnp.sum(p, axis=-1, keepdims=True)
        outs.append(_dot(p.astype(BF16), v_all) * (1.0 / l))
    oa = outs[0] - lam * outs[1]
    oa = oa * lax.rsqrt(jnp.mean(oa * oa, axis=-1, keepdims=True) + LN_EPS)
    oa = oa * g * (1.0 - LAM_INIT)
    return oa * z.astype(F32)


def _attn_ctx_kernel(q_ref, k_ref, v_ref, z_ref, lp_ref, g_ref, o_ref):
    lam = _lam(lp_ref)
    g = g_ref[...]
    for h in range(N_HEADS):
        sl = slice(h * LANES, (h + 1) * LANES)
        o_ref[:, sl] = _attn_rows(q_ref[:, sl], k_ref[:, sl], v_ref[:, sl], z_ref[:, sl],
                                  lam, g).astype(BF16)


def _attn_ctx_call(p, lam_params, subln_g, *, batch, seq):
    blk = lambda j: pl.BlockSpec((seq, D_MODEL), lambda b: (b, j))
    return pl.pallas_call(
        _attn_ctx_kernel,
        grid=(batch,),
        in_specs=[blk(0), blk(1), blk(2), blk(3),
                  pl.BlockSpec((4, DK_A), lambda b: (0, 0)),
                  pl.BlockSpec((1, DV), lambda b: (0, 0))],
        out_specs=pl.BlockSpec((seq, D_MODEL), lambda b: (b, 0)),
        out_shape=jax.ShapeDtypeStruct((batch * seq, D_MODEL), BF16),
        compiler_params=_params(("arbitrary",)),
        name="attn_ctx",
    )(p, p, p, p, lam_params, subln_g)


def _attn_lat_kernel(q_ref, k_ref, v_ref, z_ref, kc_ref, vc_ref, lp_ref, g_ref, o_ref,
                     k_all, v_all, *, past, tc):
    @pl.when(pl.program_id(2) == 0)
    def _():
        k_all[0:past, :] = kc_ref[...].astype(BF16)
        v_all[0:past, :] = vc_ref[...].astype(BF16)
        k_all[past:, :] = k_ref[...]
        v_all[past:, :] = v_ref[...]

    lam = _lam(lp_ref)
    g = g_ref[...]

    def body(c, carry):
        rows = pl.ds(pl.multiple_of(c * tc, tc), tc)
        o_ref[rows, :] = _attn_rows(q_ref[rows, :], k_all[...], v_all[...], z_ref[rows, :],
                                    lam, g).astype(BF16)
        return carry

    lax.fori_loop(0, q_ref.shape[0] // tc, body, 0)


def _attn_lat_call(p, cache_k, cache_v, lam_params, subln_g, *, batch, seq, past):
    tq, tc = 1024, 256
    nq = seq // tq
    return pl.pallas_call(
        functools.partial(_attn_lat_kernel, past=past, tc=tc),
        grid=(batch, N_HEADS, nq),
        in_specs=[
            pl.BlockSpec((tq, LANES), lambda b, h, i: (b * nq + i, QA_BLK + h)),
            pl.BlockSpec((seq, LANES), lambda b, h, i: (b, KA_BLK + h)),
            pl.BlockSpec((seq, LANES), lambda b, h, i: (b, VA_BLK + h)),
            pl.BlockSpec((tq, LANES), lambda b, h, i: (b * nq + i, ZA_BLK + h)),
            pl.BlockSpec((None, past, LANES), lambda b, h, i: (b, 0, h)),
            pl.BlockSpec((None, past, LANES), lambda b, h, i: (b, 0, h)),
            pl.BlockSpec((4, DK_A), lambda b, h, i: (0, 0)),
            pl.BlockSpec((1, DV), lambda b, h, i: (0, 0)),
        ],
        out_specs=pl.BlockSpec((tq, LANES), lambda b, h, i: (b * nq + i, h)),
        out_shape=jax.ShapeDtypeStruct((batch * seq, D_MODEL), BF16),
        scratch_shapes=[pltpu.VMEM((past + seq, LANES), BF16),
                        pltpu.VMEM((past + seq, LANES), BF16)],
        compiler_params=_params(("arbitrary", "arbitrary", "arbitrary")),
        name="attn_lat",
    )(p, p, p, p, cache_k, cache_v, lam_params, subln_g)


def _ret_kernel(*refs, latent, nc, npairs):
    if latent:
        (rd_ref, q_ref, k_ref, v_ref, z_ref, g_ref, s0f_ref, s0b_ref,
         o_ref, uf_scr, ub_scr, sf_scr, sb_scr) = refs
    else:
        (rd_ref, q_ref, k_ref, v_ref, z_ref, g_ref,
         o_ref, sfo_ref, sbo_ref, uf_scr, ub_scr, sf_scr, sb_scr) = refs
    pair0 = pl.program_id(1) * npairs
    pw, vw = 2 * DK_R, 2 * DV

    lane_q = lax.broadcasted_iota(jnp.int32, (1, pw), 1)
    lane_v = lax.broadcasted_iota(jnp.int32, (1, vw), 1)
    row_i = lax.broadcasted_iota(jnp.int32, (CHUNK, CHUNK), 0).astype(F32)
    col_j = lax.broadcasted_iota(jnp.int32, (CHUNK, CHUNK), 1).astype(F32)
    rel = row_i - col_j
    row_v = lax.broadcasted_iota(jnp.int32, (CHUNK, vw), 0).astype(F32)
    srow = lax.broadcasted_iota(jnp.int32, (pw, vw), 0)
    scol = lax.broadcasted_iota(jnp.int32, (pw, vw), 1)
    diag_blocks = (srow // DK_R) == (scol // DV)

    def rows(c):
        return slice(c * CHUNK, (c + 1) * CHUNK)

    for pi in range(npairs):
        qs = slice(pi * pw, (pi + 1) * pw)
        vs = slice(pi * vw, (pi + 1) * vw)

        def log_gamma(d, hh):
            r = jnp.full((1, LANES), rd_ref[d, 2 * (pair0 + pi) + hh], F32)
            return jnp.log1p(-jnp.exp2(r))

        lg = [[log_gamma(d, hh) for hh in range(2)] for d in range(2)]
        lg_q = [jnp.where(lane_q < DK_R, lg[d][0], lg[d][1]) for d in range(2)]
        lg_v = [jnp.where(lane_v < DV, jnp.concatenate([lg[d][0]] * 2, axis=1),
                          jnp.concatenate([lg[d][1]] * 2, axis=1)) for d in range(2)]
        dec = [jnp.where(rel >= 0, jnp.exp(lg[0][hh] * jnp.maximum(rel, 0.0)), 0.0)
               + jnp.where(rel <= 0, jnp.exp(lg[1][hh] * jnp.maximum(-rel, 0.0)), 0.0)
               for hh in range(2)]
        qdec_f = jnp.exp(lg_v[0] * (row_v + 1.0))
        qdec_b = jnp.exp(lg_v[1] * (CHUNK - row_v))
        kdec_f = jnp.exp(lg_q[0] * (CHUNK - 1.0 - row_i))
        kdec_b = jnp.exp(lg_q[1] * row_i)
        cd_f = jnp.exp(lg_v[0] * CHUNK)
        cd_b = jnp.exp(lg_v[1] * CHUNK)

        for c in range(nc):
            k = k_ref[rows(c), qs].astype(F32)
            v = v_ref[rows(c), vs]
            uf_scr[pi, c] = jnp.where(diag_blocks, _dot_tn((k * kdec_f).astype(BF16), v), 0.0)
            ub_scr[pi, c] = jnp.where(diag_blocks, _dot_tn((k * kdec_b).astype(BF16), v), 0.0)

        def init_state(s0_ref):
            if not latent:
                return jnp.zeros((pw, vw), F32)
            z = jnp.zeros((DK_R, DV), F32)
            return jnp.concatenate(
                [jnp.concatenate([s0_ref[2 * pi], z], axis=1),
                 jnp.concatenate([z, s0_ref[2 * pi + 1]], axis=1)], axis=0)

        s = init_state(s0f_ref if latent else None)
        for c in range(nc):
            sf_scr[pi, c] = s.astype(BF16)
            s = s * cd_f + uf_scr[pi, c]
        if not latent:
            sfo_ref[2 * pi] = s[:DK_R, :DV]
            sfo_ref[2 * pi + 1] = s[DK_R:, DV:]
        s = init_state(s0b_ref if latent else None)
        for c in reversed(range(nc)):
            sb_scr[pi, c] = s.astype(BF16)
            s = s * cd_b + ub_scr[pi, c]
        if not latent:
            sbo_ref[2 * pi] = s[:DK_R, :DV]
            sbo_ref[2 * pi + 1] = s[DK_R:, DV:]

        g = g_ref[:, vs]
        for c in range(nc):
            q = q_ref[rows(c), qs]
            k = k_ref[rows(c), qs]
            v = v_ref[rows(c), vs]
            zero = jnp.zeros_like(q)
            intra = []
            for hh in range(2):
                qh = jnp.where((lane_q // DK_R) == hh, q, zero)
                a = _dot_nt(qh, k) * dec[hh]
                intra.append(_dot(a.astype(BF16), v[:, hh * DV:(hh + 1) * DV]))
            o = (jnp.concatenate(intra, axis=1)
                 + _dot(q, sf_scr[pi, c]) * qdec_f
                 + _dot(q, sb_scr[pi, c]) * qdec_b)
            outs = []
            for hh in range(2):
                oh = o[:, hh * DV:(hh + 1) * DV]
                mu = jnp.mean(oh, axis=-1, keepdims=True)
                oc = oh - mu
                var = jnp.mean(oc * oc, axis=-1, keepdims=True)
                outs.append(oc * lax.rsqrt(var + LN_EPS))
            o = jnp.concatenate(outs, axis=1) * g * z_ref[rows(c), vs].astype(F32)
            o_ref[rows(c), vs] = o.astype(BF16)


def _ret_call(p, ret_decay, gn_g, s0f, s0b, *, latent, batch, seq):
    nc = seq // CHUNK
    hps = 2 if latent else N_HEADS
    npairs = hps // 2
    qw, vw = DK_R * hps, DV * hps
    col = lambda w, off: pl.BlockSpec((seq, w), lambda b, j: (b, off // w + j))
    state = pl.BlockSpec((None, hps, DK_R, DV), lambda b, j: (b, j, 0, 0))
    in_specs = [pl.BlockSpec(memory_space=pltpu.SMEM),
                col(qw, QR_BLK * LANES), col(qw, KR_BLK * LANES),
                col(vw, VR_BLK * LANES), col(vw, ZR_BLK * LANES),
                pl.BlockSpec((1, vw), lambda b, j: (0, j))]
    args = [ret_decay, p, p, p, p, gn_g]
    out_shape = [jax.ShapeDtypeStruct((batch * seq, D_MODEL), BF16)]
    out_specs = [pl.BlockSpec((seq, vw), lambda b, j: (b, j))]
    if latent:
        in_specs += [state, state]
        args += [s0f, s0b]
    else:
        out_shape += [jax.ShapeDtypeStruct((batch, N_HEADS, DK_R, DV), F32)] * 2
        out_specs += [state, state]
    pair_state = (npairs, nc, 2 * DK_R, 2 * DV)
    return pl.pallas_call(
        functools.partial(_ret_kernel, latent=latent, nc=nc, npairs=npairs),
        grid=(batch, N_HEADS // hps),
        in_specs=in_specs,
        out_specs=out_specs,
        out_shape=out_shape,
        scratch_shapes=[pltpu.VMEM(pair_state, F32), pltpu.VMEM(pair_state, F32),
                        pltpu.VMEM(pair_state, BF16), pltpu.VMEM(pair_state, BF16)],
        compiler_params=_params(("arbitrary", "arbitrary")),
        name="ret_lat" if latent else "ret_ctx",
    )(*args)


def _out_kernel(oa_ref, or_ref, ga_ref, gr_ref, x_ref, mod_ref, wpa_ref, wpr_ref, wo_ref,
                lng_ref, lnb_ref, y_ref):
    a = _dot(oa_ref[...], wpa_ref[...])
    r = _dot(or_ref[...], wpr_ref[...])
    m = ga_ref[...].astype(F32) * a + gr_ref[...].astype(F32) * r
    out = _dot(m.astype(BF16), wo_ref[...])
    gate = mod_ref[0, :, 2 * D_MODEL:3 * D_MODEL]
    t = DEEPNORM_ALPHA * x_ref[...] + gate * out
    mu = jnp.mean(t, axis=-1, keepdims=True)
    tc = t - mu
    var = jnp.mean(tc * tc, axis=-1, keepdims=True)
    y_ref[...] = tc * lax.rsqrt(var + LN_EPS) * lng_ref[...] + lnb_ref[...]


def _out_call(oa, orr, p, x2d, mod3, w_pa, w_pr, w_out, ln_g, ln_b, *, latent, seq, mod_row0):
    m = x2d.shape[0]
    tm = 512
    per_seq = max(seq // tm, 1)

    def row(i):
        return (mod_row0 + i // per_seq) if latent else mod_row0

    tile = lambda j: pl.BlockSpec((tm, D_MODEL), lambda i: (i, j))
    full = lambda shape: pl.BlockSpec(shape, lambda i: tuple(0 for _ in shape))
    return pl.pallas_call(
        _out_kernel,
        grid=(m // tm,),
        in_specs=[tile(0), tile(0), tile(7), tile(8), tile(0),
                  pl.BlockSpec((1, 1, 3 * D_MODEL), lambda i: (row(i), 0, 0)),
                  full((D_MODEL, D_MODEL)), full((D_MODEL, D_MODEL)), full((D_MODEL, D_MODEL)),
                  full((1, D_MODEL)), full((1, D_MODEL))],
        out_specs=tile(0),
        out_shape=jax.ShapeDtypeStruct((m, D_MODEL), F32),
        compiler_params=_params(("arbitrary",)),
        name="out_lat" if latent else "out_ctx",
    )(oa, orr, p, p, x2d, mod3, w_pa, w_pr, w_out, ln_g, ln_b)


def _rope_tables(n_tokens):
    rows = n_tokens // GRID_W
    r = np.repeat(np.arange(rows, dtype=np.float32), GRID_W)
    col = np.tile(np.arange(GRID_W, dtype=np.float32), rows)
    n_freq = DK_A // 4
    inv = jnp.asarray(ROPE_BASE, F32) ** (-jnp.arange(n_freq, dtype=F32) / n_freq)
    ang = jnp.concatenate([jnp.asarray(r)[:, None] * inv, jnp.asarray(col)[:, None] * inv], axis=-1)
    cos = jnp.repeat(jnp.cos(ang), 2, axis=-1)
    sin = jnp.repeat(jnp.sin(ang), 2, axis=-1)
    even = (jnp.arange(DK_A) % 2 == 0)[None, :]
    sin_even = jnp.where(even, -sin, 0.0)
    sin_odd = jnp.where(even, 0.0, sin)
    two = lambda t: jnp.concatenate([t, t], axis=-1).astype(F32)
    return two(cos), two(sin_even), two(sin_odd)


def kernel(x_prompt, x_sample, cache_attn_k, cache_attn_v, state_ret_fwd, state_ret_bwd,
           c, c_ctx, w_mod, b_mod, w_in, lam_params, subln_g, ret_decay, ret_gn_g,
           w_pa, w_pr, w_gate, b_gate, w_out, ln_g, ln_b):
    batch, seq, _ = x_prompt.shape
    dbatch, dseq, _ = x_sample.shape
    past = cache_attn_k.shape[2]
    l = 0

    cond8 = jnp.concatenate([c_ctx[None, :], c, jnp.zeros((8 - 1 - dbatch, D_MODEL), F32)], axis=0)
    mod3 = _mod_call(cond8, w_mod[l], b_mod[l][None, :]).reshape(8, 1, 3 * D_MODEL)

    w_all = jnp.concatenate([w_in[l], w_gate[l]], axis=1).astype(BF16)
    bg = b_gate[l][None, :]
    wpa, wpr, wo = w_pa[l].astype(BF16), w_pr[l].astype(BF16), w_out[l].astype(BF16)
    lp, sg = lam_params[l], subln_g[l][None, :]
    rd, gg = ret_decay[l], ret_gn_g[l][None, :]
    lng, lnb = ln_g[l][None, :], ln_b[l][None, :]

    xc = x_prompt.reshape(batch * seq, D_MODEL)
    p_c, k_c, v_c = _proj_call(xc, mod3, w_all, bg, None, latent=False, seq=seq, mod_row0=0)
    oa_c = _attn_ctx_call(p_c, lp, sg, batch=batch, seq=seq)
    or_c, sf_c, sb_c = _ret_call(p_c, rd, gg, None, None, latent=False, batch=batch, seq=seq)
    y_c = _out_call(oa_c, or_c, p_c, xc, mod3, wpa, wpr, wo, lng, lnb,
                    latent=False, seq=seq, mod_row0=0)

    xs = x_sample.reshape(dbatch * dseq, D_MODEL)
    (p_s,) = _proj_call(xs, mod3, w_all, bg, _rope_tables(dseq), latent=True, seq=dseq, mod_row0=1)
    ck = cache_attn_k[:, l].reshape(dbatch, past, N_HEADS * 2 * DK_A)
    cv = cache_attn_v[:, l].reshape(dbatch, past, N_HEADS * DV)
    oa_s = _attn_lat_call(p_s, ck, cv, lp, sg, batch=dbatch, seq=dseq, past=past)
    (or_s,) = _ret_call(p_s, rd, gg, state_ret_fwd[:, l], state_ret_bwd[:, l],
                        latent=True, batch=dbatch, seq=dseq)
    y_s = _out_call(oa_s, or_s, p_s, xs, mod3, wpa, wpr, wo, lng, lnb,
                    latent=True, seq=dseq, mod_row0=1)

    return (y_c.reshape(batch, seq, D_MODEL),
            y_s.reshape(dbatch, dseq, D_MODEL),
            k_c.reshape(batch, 1, seq, N_HEADS, 2 * DK_A),
            v_c.reshape(batch, 1, seq, N_HEADS, DV),
            sf_c.reshape(batch, 1, N_HEADS, DK_R, DV),
            sb_c.reshape(batch, 1, N_HEADS, DK_R, DV))
```

```python
import functools
import math

import jax
import jax.numpy as jnp
import numpy as np
from jax import lax
from jax.experimental import pallas as pl
from jax.experimental.pallas import tpu as pltpu

F32 = jnp.float32
BF16 = jnp.bfloat16

D_MODEL = 1024
N_HEADS = 8
DK_A = 64
DV = 128
DK_R = 64
CHUNK = 128
GRID_W = 64
ROPE_BASE = 10000.0
MOD_EPS = 1e-6
LN_EPS = 1e-5
DEPTH = 1
DEEPNORM_ALPHA = (2.0 * DEPTH) ** 0.25
LAM_INIT = 0.8 - 0.6 * math.exp(-0.3 * 0)
LOG2E = math.log2(math.e)

LANES = 128
QA_BLK, KA_BLK, VA_BLK, ZA_BLK = 0, 8, 16, 24
QR_BLK, KR_BLK, VR_BLK, ZR_BLK = 32, 36, 40, 48
P_WIDTH = 9 * D_MODEL
VMEM_LIMIT = 56 * 1024 * 1024


def _params(sem):
    return pltpu.CompilerParams(dimension_semantics=sem, vmem_limit_bytes=VMEM_LIMIT)


def _silu(z):
    return z * (1.0 / (1.0 + jnp.exp(-z)))


def _dot(a, b):
    return jnp.dot(a, b, preferred_element_type=F32)


def _dot_nt(a, b):
    return lax.dot_general(a, b, (((1,), (1,)), ((), ())), preferred_element_type=F32)


def _dot_tn(a, b):
    return lax.dot_general(a, b, (((0,), (0,)), ((), ())), preferred_element_type=F32)


def _mod_kernel(cond_ref, w_ref, b_ref, o_ref):
    s = _silu(cond_ref[...])
    o_ref[...] = _dot(s.astype(BF16), w_ref[...].astype(BF16)) + b_ref[...]


def _mod_call(cond8, w_mod, b_mod):
    tn = D_MODEL
    return pl.pallas_call(
        _mod_kernel,
        grid=(3 * D_MODEL // tn,),
        in_specs=[pl.BlockSpec((8, D_MODEL), lambda j: (0, 0)),
                  pl.BlockSpec((D_MODEL, tn), lambda j: (0, j)),
                  pl.BlockSpec((1, tn), lambda j: (0, j))],
        out_specs=pl.BlockSpec((8, tn), lambda j: (0, j)),
        out_shape=jax.ShapeDtypeStruct((8, 3 * D_MODEL), F32),
        compiler_params=_params(("arbitrary",)),
        name="mod",
    )(cond8, w_mod, b_mod)


def _rope(acc, cos, sin_even, sin_odd):
    outs = []
    for hh in range(N_HEADS):
        xs = acc[:, hh * LANES:(hh + 1) * LANES]
        nxt = pltpu.roll(xs, LANES - 1, 1)
        prv = pltpu.roll(xs, 1, 1)
        outs.append(xs * cos + nxt * sin_even + prv * sin_odd)
    return jnp.concatenate(outs, axis=1)


def _proj_kernel(*refs, latent):
    if latent:
        x_ref, mod_ref, w_ref, bg_ref, cos_ref, se_ref, so_ref, p_ref = refs
    else:
        x_ref, mod_ref, w_ref, bg_ref, p_ref, k_ref, v_ref = refs
    x = x_ref[...]
    mu = jnp.mean(x, axis=-1, keepdims=True)
    xc = x - mu
    var = jnp.mean(xc * xc, axis=-1, keepdims=True)
    shift = mod_ref[0, :, 0:D_MODEL]
    scale = mod_ref[0, :, D_MODEL:2 * D_MODEL]
    h = (xc * lax.rsqrt(var + MOD_EPS) * (1.0 + scale) + shift).astype(BF16)

    def seg(j):
        return _dot(h, w_ref[:, j * D_MODEL:(j + 1) * D_MODEL])

    def put(j, val):
        p_ref[:, j * D_MODEL:(j + 1) * D_MODEL] = val.astype(BF16)

    qa = seg(0)
    ka = seg(1)
    if latent:
        cos, se, so = cos_ref[...], se_ref[...], so_ref[...]
        qa = _rope(qa, cos, se, so)
        ka = _rope(ka, cos, se, so)
    else:
        k_ref[...] = ka
    put(0, qa * (DK_A ** -0.5 * LOG2E))
    put(1, ka)
    va = seg(2)
    if not latent:
        v_ref[...] = va
    put(2, va)
    put(3, _silu(seg(3)))
    qk = seg(4)
    put(4, jnp.concatenate([qk[:, :D_MODEL // 2], qk[:, D_MODEL // 2:] * (DK_R ** -0.5)], axis=1))
    put(5, seg(5))
    put(6, _silu(seg(6)))
    for j in (7, 8):
        g = seg(j) + bg_ref[:, (j - 7) * D_MODEL:(j - 6) * D_MODEL]
        put(j, 1.0 / (1.0 + jnp.exp(-g)))


def _proj_call(x2d, mod3, w_all, b_gate, rope, *, latent, seq, mod_row0):
    m = x2d.shape[0]
    tm = 256
    nt = m // tm
    per_seq = seq // tm

    def row(i):
        return (mod_row0 + i // per_seq) if latent else mod_row0

    in_specs = [
        pl.BlockSpec((tm, D_MODEL), lambda i: (i, 0)),
        pl.BlockSpec((1, 1, 3 * D_MODEL), lambda i: (row(i), 0, 0)),
        pl.BlockSpec((D_MODEL, P_WIDTH), lambda i: (0, 0), pipeline_mode=pl.Buffered(1)),
        pl.BlockSpec((1, 2 * D_MODEL), lambda i: (0, 0)),
    ]
    args = [x2d, mod3, w_all, b_gate]
    out_shape = [jax.ShapeDtypeStruct((m, P_WIDTH), BF16)]
    out_specs = [pl.BlockSpec((tm, P_WIDTH), lambda i: (i, 0))]
    if latent:
        for t in rope:
            in_specs.append(pl.BlockSpec((tm, LANES), lambda i: (i % per_seq, 0)))
            args.append(t)
    else:
        for _ in range(2):
            out_shape.append(jax.ShapeDtypeStruct((m, D_MODEL), F32))
            out_specs.append(pl.BlockSpec((tm, D_MODEL), lambda i: (i, 0)))
    return pl.pallas_call(
        functools.partial(_proj_kernel, latent=latent),
        grid=(nt,),
        in_specs=in_specs,
        out_specs=out_specs,
        out_shape=out_shape,
        compiler_params=_params(("arbitrary",)),
        name="proj_lat" if latent else "proj_ctx",
    )(*args)


def _lam(lp_ref):
    lp = lp_ref[...]
    a = jnp.sum(lp[0:1] * lp[1:2], axis=-1, keepdims=True)
    b = jnp.sum(lp[2:3] * lp[3:4], axis=-1, keepdims=True)
    return jnp.exp(a) - jnp.exp(b) + LAM_INIT


def _attn_rows(q, k_all, v_ones, z, lam, g):
    lane = lax.broadcasted_iota(jnp.int32, (1, LANES), 1)
    zero = jnp.zeros_like(q)
    outs = []
    for qm in (jnp.where(lane < DK_A, q, zero), jnp.where(lane >= DK_A, q, zero)):
        s = _dot_nt(qm, k_all)
        mx = jnp.max(s, axis=-1, keepdims=True)
        p = jnp.exp2(s - mx).astype(BF16)
        ol = _dot(p, v_ones)
        outs.append(ol[:, :DV] * (1.0 / ol[:, DV:]))
    oa = outs[0] - lam * outs[1]
    oa = oa * lax.rsqrt(jnp.mean(oa * oa, axis=-1, keepdims=True) + LN_EPS)
    oa = oa * g * (1.0 - LAM_INIT)
    return oa * z.astype(F32)


def _attn_ctx_kernel(q_ref, k_ref, v_ref, z_ref, lp_ref, g_ref, o_ref):
    lam = _lam(lp_ref)
    g = g_ref[...]
    ones = jnp.ones((q_ref.shape[0], DV), BF16)
    for h in range(N_HEADS):
        sl = slice(h * LANES, (h + 1) * LANES)
        v_ones = jnp.concatenate([v_ref[:, sl], ones], axis=1)
        o_ref[:, sl] = _attn_rows(q_ref[:, sl], k_ref[:, sl], v_ones, z_ref[:, sl],
                                  lam, g).astype(BF16)


def _attn_ctx_call(p, lam_params, subln_g, *, batch, seq):
    blk = lambda j: pl.BlockSpec((seq, D_MODEL), lambda b: (b, j))
    return pl.pallas_call(
        _attn_ctx_kernel,
        grid=(batch,),
        in_specs=[blk(0), blk(1), blk(2), blk(3),
                  pl.BlockSpec((4, DK_A), lambda b: (0, 0)),
                  pl.BlockSpec((1, DV), lambda b: (0, 0))],
        out_specs=pl.BlockSpec((seq, D_MODEL), lambda b: (b, 0)),
        out_shape=jax.ShapeDtypeStruct((batch * seq, D_MODEL), BF16),
        compiler_params=_params(("arbitrary",)),
        name="attn_ctx",
    )(p, p, p, p, lam_params, subln_g)


def _attn_lat_kernel(q_ref, k_ref, v_ref, z_ref, kc_ref, vc_ref, lp_ref, g_ref, o_ref,
                     k_all, v_all, *, past, tc):
    @pl.when(pl.program_id(2) == 0)
    def _():
        k_all[0:past, :] = kc_ref[...].astype(BF16)
        v_all[0:past, 0:DV] = vc_ref[...].astype(BF16)
        k_all[past:, :] = k_ref[...]
        v_all[past:, 0:DV] = v_ref[...]
        v_all[:, DV:] = jnp.ones((v_all.shape[0], DV), BF16)

    lam = _lam(lp_ref)
    g = g_ref[...]

    def body(c, carry):
        rows = pl.ds(pl.multiple_of(c * tc, tc), tc)
        o_ref[rows, :] = _attn_rows(q_ref[rows, :], k_all[...], v_all[...], z_ref[rows, :],
                                    lam, g).astype(BF16)
        return carry

    lax.fori_loop(0, q_ref.shape[0] // tc, body, 0, unroll=True)


def _attn_lat_call(p, cache_k, cache_v, lam_params, subln_g, *, batch, seq, past):
    tq, tc = 2048, 256
    nq = seq // tq
    return pl.pallas_call(
        functools.partial(_attn_lat_kernel, past=past, tc=tc),
        grid=(batch, N_HEADS, nq),
        in_specs=[
            pl.BlockSpec((tq, LANES), lambda b, h, i: (b * nq + i, QA_BLK + h)),
            pl.BlockSpec((seq, LANES), lambda b, h, i: (b, KA_BLK + h)),
            pl.BlockSpec((seq, LANES), lambda b, h, i: (b, VA_BLK + h)),
            pl.BlockSpec((tq, LANES), lambda b, h, i: (b * nq + i, ZA_BLK + h)),
            pl.BlockSpec((None, past, LANES), lambda b, h, i: (b, 0, h)),
            pl.BlockSpec((None, past, LANES), lambda b, h, i: (b, 0, h)),
            pl.BlockSpec((4, DK_A), lambda b, h, i: (0, 0)),
            pl.BlockSpec((1, DV), lambda b, h, i: (0, 0)),
        ],
        out_specs=pl.BlockSpec((tq, LANES), lambda b, h, i: (b * nq + i, h)),
        out_shape=jax.ShapeDtypeStruct((batch * seq, D_MODEL), BF16),
        scratch_shapes=[pltpu.VMEM((past + seq, LANES), BF16),
                        pltpu.VMEM((past + seq, 2 * DV), BF16)],
        compiler_params=_params(("arbitrary", "arbitrary", "arbitrary")),
        name="attn_lat",
    )(p, p, p, p, cache_k, cache_v, lam_params, subln_g)


def _ret_kernel(*refs, latent, nc, npairs):
    if latent:
        (rd_ref, q_ref, k_ref, v_ref, z_ref, g_ref, s0f_ref, s0b_ref,
         o_ref, uf_scr, ub_scr, sf_scr, sb_scr) = refs
    else:
        (rd_ref, q_ref, k_ref, v_ref, z_ref, g_ref,
         o_ref, sfo_ref, sbo_ref, uf_scr, ub_scr, sf_scr, sb_scr) = refs
    pair0 = pl.program_id(1) * npairs
    pw, vw = 2 * DK_R, 2 * DV

    lane_q = lax.broadcasted_iota(jnp.int32, (1, pw), 1)
    lane_v = lax.broadcasted_iota(jnp.int32, (1, vw), 1)
    row_i = lax.broadcasted_iota(jnp.int32, (CHUNK, CHUNK), 0).astype(F32)
    col_j = lax.broadcasted_iota(jnp.int32, (CHUNK, CHUNK), 1).astype(F32)
    rel = row_i - col_j
    row_v = lax.broadcasted_iota(jnp.int32, (CHUNK, vw), 0).astype(F32)
    srow = lax.broadcasted_iota(jnp.int32, (pw, vw), 0)
    scol = lax.broadcasted_iota(jnp.int32, (pw, vw), 1)
    diag_blocks = (srow // DK_R) == (scol // DV)

    def rows(c):
        return slice(c * CHUNK, (c + 1) * CHUNK)

    for pi in range(npairs):
        qs = slice(pi * pw, (pi + 1) * pw)
        vs = slice(pi * vw, (pi + 1) * vw)

        def log_gamma(d, hh):
            r = jnp.full((1, LANES), rd_ref[d, 2 * (pair0 + pi) + hh], F32)
            return jnp.log1p(-jnp.exp2(r))

        lg = [[log_gamma(d, hh) for hh in range(2)] for d in range(2)]
        lg_q = [jnp.where(lane_q < DK_R, lg[d][0], lg[d][1]) for d in range(2)]
        lg_v = [jnp.where(lane_v < DV, jnp.concatenate([lg[d][0]] * 2, axis=1),
                          jnp.concatenate([lg[d][1]] * 2, axis=1)) for d in range(2)]
        dec = [jnp.where(rel >= 0, jnp.exp(lg[0][hh] * jnp.maximum(rel, 0.0)), 0.0)
               + jnp.where(rel <= 0, jnp.exp(lg[1][hh] * jnp.maximum(-rel, 0.0)), 0.0)
               for hh in range(2)]
        qdec_f = jnp.exp(lg_v[0] * (row_v + 1.0))
        qdec_b = jnp.exp(lg_v[1] * (CHUNK - row_v))
        kdec_f = jnp.exp(lg_q[0] * (CHUNK - 1.0 - row_i))
        kdec_b = jnp.exp(lg_q[1] * row_i)
        cd_f = jnp.exp(lg_v[0] * CHUNK)
        cd_b = jnp.exp(lg_v[1] * CHUNK)

        for c in range(nc):
            k = k_ref[rows(c), qs].astype(F32)
            v = v_ref[rows(c), vs]
            uf_scr[pi, c] = jnp.where(diag_blocks, _dot_tn((k * kdec_f).astype(BF16), v), 0.0)
            ub_scr[pi, c] = jnp.where(diag_blocks, _dot_tn((k * kdec_b).astype(BF16), v), 0.0)

        def init_state(s0_ref):
            if not latent:
                return jnp.zeros((pw, vw), F32)
            z = jnp.zeros((DK_R, DV), F32)
            return jnp.concatenate(
                [jnp.concatenate([s0_ref[2 * pi], z], axis=1),
                 jnp.concatenate([z, s0_ref[2 * pi + 1]], axis=1)], axis=0)

        s = init_state(s0f_ref if latent else None)
        for c in range(nc):
            sf_scr[pi, c] = s.astype(BF16)
            s = s * cd_f + uf_scr[pi, c]
        if not latent:
            sfo_ref[2 * pi] = s[:DK_R, :DV]
            sfo_ref[2 * pi + 1] = s[DK_R:, DV:]
        s = init_state(s0b_ref if latent else None)
        for c in reversed(range(nc)):
            sb_scr[pi, c] = s.astype(BF16)
            s = s * cd_b + ub_scr[pi, c]
        if not latent:
            sbo_ref[2 * pi] = s[:DK_R, :DV]
            sbo_ref[2 * pi + 1] = s[DK_R:, DV:]

        g = g_ref[:, vs]
        for c in range(nc):
            q = q_ref[rows(c), qs]
            k = k_ref[rows(c), qs]
            v = v_ref[rows(c), vs]
            zero = jnp.zeros_like(q)
            intra = []
            for hh in range(2):
                qh = jnp.where((lane_q // DK_R) == hh, q, zero)
                a = _dot_nt(qh, k) * dec[hh]
                intra.append(_dot(a.astype(BF16), v[:, hh * DV:(hh + 1) * DV]))
            o = (jnp.concatenate(intra, axis=1)
                 + _dot(q, sf_scr[pi, c]) * qdec_f
                 + _dot(q, sb_scr[pi, c]) * qdec_b)
            outs = []
            for hh in range(2):
                oh = o[:, hh * DV:(hh + 1) * DV]
                mu = jnp.mean(oh, axis=-1, keepdims=True)
                oc = oh - mu
                var = jnp.mean(oc * oc, axis=-1, keepdims=True)
                outs.append(oc * lax.rsqrt(var + LN_EPS))
            o = jnp.concatenate(outs, axis=1) * g * z_ref[rows(c), vs].astype(F32)
            o_ref[rows(c), vs] = o.astype(BF16)


def _ret_call(p, ret_decay, gn_g, s0f, s0b, *, latent, batch, seq):
    nc = seq // CHUNK
    hps = 2 if latent else N_HEADS
    npairs = hps // 2
    qw, vw = DK_R * hps, DV * hps
    col = lambda w, off: pl.BlockSpec((seq, w), lambda b, j: (b, off // w + j))
    state = pl.BlockSpec((None, hps, DK_R, DV), lambda b, j: (b, j, 0, 0))
    in_specs = [pl.BlockSpec(memory_space=pltpu.SMEM),
                col(qw, QR_BLK * LANES), col(qw, KR_BLK * LANES),
                col(vw, VR_BLK * LANES), col(vw, ZR_BLK * LANES),
                pl.BlockSpec((1, vw), lambda b, j: (0, j))]
    args = [ret_decay, p, p, p, p, gn_g]
    out_shape = [jax.ShapeDtypeStruct((batch * seq, D_MODEL), BF16)]
    out_specs = [pl.BlockSpec((seq, vw), lambda b, j: (b, j))]
    if latent:
        in_specs += [state, state]
        args += [s0f, s0b]
    else:
        out_shape += [jax.ShapeDtypeStruct((batch, N_HEADS, DK_R, DV), F32)] * 2
        out_specs += [state, state]
    pair_state = (npairs, nc, 2 * DK_R, 2 * DV)
    return pl.pallas_call(
        functools.partial(_ret_kernel, latent=latent, nc=nc, npairs=npairs),
        grid=(batch, N_HEADS // hps),
        in_specs=in_specs,
        out_specs=out_specs,
        out_shape=out_shape,
        scratch_shapes=[pltpu.VMEM(pair_state, F32), pltpu.VMEM(pair_state, F32),
                        pltpu.VMEM(pair_state, BF16), pltpu.VMEM(pair_state, BF16)],
        compiler_params=_params(("arbitrary", "arbitrary")),
        name="ret_lat" if latent else "ret_ctx",
    )(*args)


def _out_kernel(oa_ref, or_ref, ga_ref, gr_ref, x_ref, mod_ref, wpa_ref, wpr_ref, wo_ref,
                lng_ref, lnb_ref, y_ref):
    a = _dot(oa_ref[...], wpa_ref[...])
    r = _dot(or_ref[...], wpr_ref[...])
    m = ga_ref[...].astype(F32) * a + gr_ref[...].astype(F32) * r
    out = _dot(m.astype(BF16), wo_ref[...])
    gate = mod_ref[0, :, 2 * D_MODEL:3 * D_MODEL]
    t = DEEPNORM_ALPHA * x_ref[...] + gate * out
    mu = jnp.mean(t, axis=-1, keepdims=True)
    tc = t - mu
    var = jnp.mean(tc * tc, axis=-1, keepdims=True)
    y_ref[...] = tc * lax.rsqrt(var + LN_EPS) * lng_ref[...] + lnb_ref[...]


def _out_call(oa, orr, p, x2d, mod3, w_pa, w_pr, w_out, ln_g, ln_b, *, latent, seq, mod_row0):
    m = x2d.shape[0]
    tm = 512
    per_seq = max(seq // tm, 1)

    def row(i):
        return (mod_row0 + i // per_seq) if latent else mod_row0

    tile = lambda j: pl.BlockSpec((tm, D_MODEL), lambda i: (i, j))
    full = lambda shape: pl.BlockSpec(shape, lambda i: tuple(0 for _ in shape))
    return pl.pallas_call(
        _out_kernel,
        grid=(m // tm,),
        in_specs=[tile(0), tile(0), tile(7), tile(8), tile(0),
                  pl.BlockSpec((1, 1, 3 * D_MODEL), lambda i: (row(i), 0, 0)),
                  full((D_MODEL, D_MODEL)), full((D_MODEL, D_MODEL)), full((D_MODEL, D_MODEL)),
                  full((1, D_MODEL)), full((1, D_MODEL))],
        out_specs=tile(0),
        out_shape=jax.ShapeDtypeStruct((m, D_MODEL), F32),
        compiler_params=_params(("arbitrary",)),
        name="out_lat" if latent else "out_ctx",
    )(oa, orr, p, p, x2d, mod3, w_pa, w_pr, w_out, ln_g, ln_b)


def _rope_tables(n_tokens):
    rows = n_tokens // GRID_W
    r = np.repeat(np.arange(rows, dtype=np.float32), GRID_W)
    col = np.tile(np.arange(GRID_W, dtype=np.float32), rows)
    n_freq = DK_A // 4
    inv = jnp.asarray(ROPE_BASE, F32) ** (-jnp.arange(n_freq, dtype=F32) / n_freq)
    ang = jnp.concatenate([jnp.asarray(r)[:, None] * inv, jnp.asarray(col)[:, None] * inv], axis=-1)
    cos = jnp.repeat(jnp.cos(ang), 2, axis=-1)
    sin = jnp.repeat(jnp.sin(ang), 2, axis=-1)
    even = (jnp.arange(DK_A) % 2 == 0)[None, :]
    sin_even = jnp.where(even, -sin, 0.0)
    sin_odd = jnp.where(even, 0.0, sin)
    two = lambda t: jnp.concatenate([t, t], axis=-1).astype(F32)
    return two(cos), two(sin_even), two(sin_odd)


def kernel(x_prompt, x_sample, cache_attn_k, cache_attn_v, state_ret_fwd, state_ret_bwd,
           c, c_ctx, w_mod, b_mod, w_in, lam_params, subln_g, ret_decay, ret_gn_g,
           w_pa, w_pr, w_gate, b_gate, w_out, ln_g, ln_b):
    batch, seq, _ = x_prompt.shape
    dbatch, dseq, _ = x_sample.shape
    past = cache_attn_k.shape[2]
    l = 0

    cond8 = jnp.concatenate([c_ctx[None, :], c, jnp.zeros((8 - 1 - dbatch, D_MODEL), F32)], axis=0)
    mod3 = _mod_call(cond8, w_mod[l], b_mod[l][None, :]).reshape(8, 1, 3 * D_MODEL)

    w_all = jnp.concatenate([w_in[l], w_gate[l]], axis=1).astype(BF16)
    bg = b_gate[l][None, :]
    wpa, wpr, wo = w_pa[l].astype(BF16), w_pr[l].astype(BF16), w_out[l].astype(BF16)
    lp, sg = lam_params[l], subln_g[l][None, :]
    rd, gg = ret_decay[l], ret_gn_g[l][None, :]
    lng, lnb = ln_g[l][None, :], ln_b[l][None, :]

    xc = x_prompt.reshape(batch * seq, D_MODEL)
    p_c, k_c, v_c = _proj_call(xc, mod3, w_all, bg, None, latent=False, seq=seq, mod_row0=0)
    oa_c = _attn_ctx_call(p_c, lp, sg, batch=batch, seq=seq)
    or_c, sf_c, sb_c = _ret_call(p_c, rd, gg, None, None, latent=False, batch=batch, seq=seq)
    y_c = _out_call(oa_c, or_c, p_c, xc, mod3, wpa, wpr, wo, lng, lnb,
                    latent=False, seq=seq, mod_row0=0)

    xs = x_sample.reshape(dbatch * dseq, D_MODEL)
    (p_s,) = _proj_call(xs, mod3, w_all, bg, _rope_tables(dseq), latent=True, seq=dseq, mod_row0=1)
    ck = cache_attn_k[:, l].reshape(dbatch, past, N_HEADS * 2 * DK_A)
    cv = cache_attn_v[:, l].reshape(dbatch, past, N_HEADS * DV)
    oa_s = _attn_lat_call(p_s, ck, cv, lp, sg, batch=dbatch, seq=dseq, past=past)
    (or_s,) = _ret_call(p_s, rd, gg, state_ret_fwd[:, l], state_ret_bwd[:, l],
                        latent=True, batch=dbatch, seq=dseq)
    y_s = _out_call(oa_s, or_s, p_s, xs, mod3, wpa, wpr, wo, lng, lnb,
                    latent=True, seq=dseq, mod_row0=1)

    return (y_c.reshape(batch, seq, D_MODEL),
            y_s.reshape(dbatch, dseq, D_MODEL),
            k_c.reshape(batch, 1, seq, N_HEADS, 2 * DK_A),
            v_c.reshape(batch, 1, seq, N_HEADS, DV),
            sf_c.reshape(batch, 1, N_HEADS, DK_R, DV),
            sb_c.reshape(batch, 1, N_HEADS, DK_R, DV))
```

```python
import functools
import math

import jax
import jax.numpy as jnp
import numpy as np
from jax import lax
from jax.experimental import pallas as pl
from jax.experimental.pallas import tpu as pltpu

F32 = jnp.float32
BF16 = jnp.bfloat16

D_MODEL = 1024
N_HEADS = 8
DK_A = 64
DV = 128
DK_R = 64
CHUNK = 256
GRID_W = 64
ROPE_BASE = 10000.0
MOD_EPS = 1e-6
LN_EPS = 1e-5
DEPTH = 1
DEEPNORM_ALPHA = (2.0 * DEPTH) ** 0.25
LAM_INIT = 0.8 - 0.6 * math.exp(-0.3 * 0)
LOG2E = math.log2(math.e)

LANES = 128
QA_BLK, KA_BLK, VA_BLK, ZA_BLK = 0, 8, 16, 24
QR_BLK, KR_BLK, VR_BLK, ZR_BLK = 32, 36, 40, 48
P_WIDTH = 9 * D_MODEL
VMEM_LIMIT = 56 * 1024 * 1024


def _params(sem):
    return pltpu.CompilerParams(dimension_semantics=sem, vmem_limit_bytes=VMEM_LIMIT)


def _silu(z):
    return z * (1.0 / (1.0 + jnp.exp(-z)))


def _dot(a, b):
    return jnp.dot(a, b, preferred_element_type=F32)


def _dot_nt(a, b):
    return lax.dot_general(a, b, (((1,), (1,)), ((), ())), preferred_element_type=F32)


def _dot_tn(a, b):
    return lax.dot_general(a, b, (((0,), (0,)), ((), ())), preferred_element_type=F32)


def _mod_kernel(cond_ref, w_ref, b_ref, o_ref):
    s = _silu(cond_ref[...])
    o_ref[...] = _dot(s.astype(BF16), w_ref[...].astype(BF16)) + b_ref[...]


def _mod_call(cond8, w_mod, b_mod):
    tn = D_MODEL
    return pl.pallas_call(
        _mod_kernel,
        grid=(3 * D_MODEL // tn,),
        in_specs=[pl.BlockSpec((8, D_MODEL), lambda j: (0, 0)),
                  pl.BlockSpec((D_MODEL, tn), lambda j: (0, j)),
                  pl.BlockSpec((1, tn), lambda j: (0, j))],
        out_specs=pl.BlockSpec((8, tn), lambda j: (0, j)),
        out_shape=jax.ShapeDtypeStruct((8, 3 * D_MODEL), F32),
        compiler_params=_params(("arbitrary",)),
        name="mod",
    )(cond8, w_mod, b_mod)


def _rope(acc, cos, sin_even, sin_odd):
    outs = []
    for hh in range(N_HEADS):
        xs = acc[:, hh * LANES:(hh + 1) * LANES]
        nxt = pltpu.roll(xs, LANES - 1, 1)
        prv = pltpu.roll(xs, 1, 1)
        outs.append(xs * cos + nxt * sin_even + prv * sin_odd)
    return jnp.concatenate(outs, axis=1)


def _proj_kernel(*refs, latent):
    if latent:
        x_ref, mod_ref, w_ref, bg_ref, cos_ref, se_ref, so_ref, p_ref = refs
    else:
        x_ref, mod_ref, w_ref, bg_ref, p_ref, k_ref, v_ref = refs
    x = x_ref[...]
    mu = jnp.mean(x, axis=-1, keepdims=True)
    xc = x - mu
    var = jnp.mean(xc * xc, axis=-1, keepdims=True)
    shift = mod_ref[0, :, 0:D_MODEL]
    scale = mod_ref[0, :, D_MODEL:2 * D_MODEL]
    h = (xc * lax.rsqrt(var + MOD_EPS) * (1.0 + scale) + shift).astype(BF16)

    def seg(j):
        return _dot(h, w_ref[:, j * D_MODEL:(j + 1) * D_MODEL])

    def put(j, val):
        p_ref[:, j * D_MODEL:(j + 1) * D_MODEL] = val.astype(BF16)

    qa = seg(0)
    ka = seg(1)
    if latent:
        cos, se, so = cos_ref[...], se_ref[...], so_ref[...]
        qa = _rope(qa, cos, se, so)
        ka = _rope(ka, cos, se, so)
    else:
        k_ref[...] = ka
    put(0, qa * (DK_A ** -0.5 * LOG2E))
    put(1, ka)
    va = seg(2)
    if not latent:
        v_ref[...] = va
    put(2, va)
    put(3, _silu(seg(3)))
    qk = seg(4)
    put(4, jnp.concatenate([qk[:, :D_MODEL // 2], qk[:, D_MODEL // 2:] * (DK_R ** -0.5)], axis=1))
    put(5, seg(5))
    put(6, _silu(seg(6)))
    for j in (7, 8):
        g = seg(j) + bg_ref[:, (j - 7) * D_MODEL:(j - 6) * D_MODEL]
        put(j, 1.0 / (1.0 + jnp.exp(-g)))


def _proj_call(x2d, mod3, w_all, b_gate, rope, *, latent, seq, mod_row0):
    m = x2d.shape[0]
    tm = 256
    nt = m // tm
    per_seq = seq // tm

    def row(i):
        return (mod_row0 + i // per_seq) if latent else mod_row0

    in_specs = [
        pl.BlockSpec((tm, D_MODEL), lambda i: (i, 0)),
        pl.BlockSpec((1, 1, 3 * D_MODEL), lambda i: (row(i), 0, 0)),
        pl.BlockSpec((D_MODEL, P_WIDTH), lambda i: (0, 0), pipeline_mode=pl.Buffered(1)),
        pl.BlockSpec((1, 2 * D_MODEL), lambda i: (0, 0)),
    ]
    args = [x2d, mod3, w_all, b_gate]
    out_shape = [jax.ShapeDtypeStruct((m, P_WIDTH), BF16)]
    out_specs = [pl.BlockSpec((tm, P_WIDTH), lambda i: (i, 0))]
    if latent:
        for t in rope:
            in_specs.append(pl.BlockSpec((tm, LANES), lambda i: (i % per_seq, 0)))
            args.append(t)
    else:
        for _ in range(2):
            out_shape.append(jax.ShapeDtypeStruct((m, D_MODEL), F32))
            out_specs.append(pl.BlockSpec((tm, D_MODEL), lambda i: (i, 0)))
    return pl.pallas_call(
        functools.partial(_proj_kernel, latent=latent),
        grid=(nt,),
        in_specs=in_specs,
        out_specs=out_specs,
        out_shape=out_shape,
        compiler_params=_params(("arbitrary",)),
        name="proj_lat" if latent else "proj_ctx",
    )(*args)


def _lam(lp_ref):
    lp = lp_ref[...]
    a = jnp.sum(lp[0:1] * lp[1:2], axis=-1, keepdims=True)
    b = jnp.sum(lp[2:3] * lp[3:4], axis=-1, keepdims=True)
    return jnp.exp(a) - jnp.exp(b) + LAM_INIT


def _attn_rows(q, k_all, v_ones, z, lam, g):
    lane = lax.broadcasted_iota(jnp.int32, (1, LANES), 1)
    zero = jnp.zeros_like(q)
    outs = []
    for qm in (jnp.where(lane < DK_A, q, zero), jnp.where(lane >= DK_A, q, zero)):
        s = _dot_nt(qm, k_all)
        mx = jnp.max(s, axis=-1, keepdims=True)
        p = jnp.exp2(s - mx).astype(BF16)
        ol = _dot(p, v_ones)
        outs.append(ol[:, :DV] * (1.0 / ol[:, DV:]))
    oa = outs[0] - lam * outs[1]
    oa = oa * lax.rsqrt(jnp.mean(oa * oa, axis=-1, keepdims=True) + LN_EPS)
    oa = oa * g * (1.0 - LAM_INIT)
    return oa * z.astype(F32)


def _attn_ctx_kernel(q_ref, k_ref, v_ref, z_ref, lp_ref, g_ref, o_ref):
    lam = _lam(lp_ref)
    g = g_ref[...]
    ones = jnp.ones((q_ref.shape[0], DV), BF16)
    for h in range(N_HEADS):
        sl = slice(h * LANES, (h + 1) * LANES)
        v_ones = jnp.concatenate([v_ref[:, sl], ones], axis=1)
        o_ref[:, sl] = _attn_rows(q_ref[:, sl], k_ref[:, sl], v_ones, z_ref[:, sl],
                                  lam, g).astype(BF16)


def _attn_ctx_call(p, lam_params, subln_g, *, batch, seq):
    blk = lambda j: pl.BlockSpec((seq, D_MODEL), lambda b: (b, j))
    return pl.pallas_call(
        _attn_ctx_kernel,
        grid=(batch,),
        in_specs=[blk(0), blk(1), blk(2), blk(3),
                  pl.BlockSpec((4, DK_A), lambda b: (0, 0)),
                  pl.BlockSpec((1, DV), lambda b: (0, 0))],
        out_specs=pl.BlockSpec((seq, D_MODEL), lambda b: (b, 0)),
        out_shape=jax.ShapeDtypeStruct((batch * seq, D_MODEL), BF16),
        compiler_params=_params(("arbitrary",)),
        name="attn_ctx",
    )(p, p, p, p, lam_params, subln_g)


def _attn_lat_kernel(q_ref, k_ref, v_ref, z_ref, kc_ref, vc_ref, lp_ref, g_ref, o_ref,
                     k_all, v_all, *, past, tc):
    @pl.when(pl.program_id(2) == 0)
    def _():
        k_all[0:past, :] = kc_ref[...].astype(BF16)
        v_all[0:past, 0:DV] = vc_ref[...].astype(BF16)
        k_all[past:, :] = k_ref[...]
        v_all[past:, 0:DV] = v_ref[...]
        v_all[:, DV:] = jnp.ones((v_all.shape[0], DV), BF16)

    lam = _lam(lp_ref)
    g = g_ref[...]

    def body(c, carry):
        rows = pl.ds(pl.multiple_of(c * tc, tc), tc)
        o_ref[rows, :] = _attn_rows(q_ref[rows, :], k_all[...], v_all[...], z_ref[rows, :],
                                    lam, g).astype(BF16)
        return carry

    lax.fori_loop(0, q_ref.shape[0] // tc, body, 0, unroll=True)


def _attn_lat_call(p, cache_k, cache_v, lam_params, subln_g, *, batch, seq, past):
    tq, tc = 2048, 256
    nq = seq // tq
    return pl.pallas_call(
        functools.partial(_attn_lat_kernel, past=past, tc=tc),
        grid=(batch, N_HEADS, nq),
        in_specs=[
            pl.BlockSpec((tq, LANES), lambda b, h, i: (b * nq + i, QA_BLK + h)),
            pl.BlockSpec((seq, LANES), lambda b, h, i: (b, KA_BLK + h)),
            pl.BlockSpec((seq, LANES), lambda b, h, i: (b, VA_BLK + h)),
            pl.BlockSpec((tq, LANES), lambda b, h, i: (b * nq + i, ZA_BLK + h)),
            pl.BlockSpec((None, past, LANES), lambda b, h, i: (b, 0, h)),
            pl.BlockSpec((None, past, LANES), lambda b, h, i: (b, 0, h)),
            pl.BlockSpec((4, DK_A), lambda b, h, i: (0, 0)),
            pl.BlockSpec((1, DV), lambda b, h, i: (0, 0)),
        ],
        out_specs=pl.BlockSpec((tq, LANES), lambda b, h, i: (b * nq + i, h)),
        out_shape=jax.ShapeDtypeStruct((batch * seq, D_MODEL), BF16),
        scratch_shapes=[pltpu.VMEM((past + seq, LANES), BF16),
                        pltpu.VMEM((past + seq, 2 * DV), BF16)],
        compiler_params=_params(("arbitrary", "arbitrary", "arbitrary")),
        name="attn_lat",
    )(p, p, p, p, cache_k, cache_v, lam_params, subln_g)


def _ret_kernel(*refs, latent, nc, npairs):
    if latent:
        (rd_ref, q_ref, k_ref, v_ref, z_ref, g_ref, s0f_ref, s0b_ref,
         o_ref, u_scr, s_scr, dec_scr, qdec_scr, kdec_scr) = refs
    else:
        (rd_ref, q_ref, k_ref, v_ref, z_ref, g_ref,
         o_ref, sfo_ref, sbo_ref, u_scr, s_scr, dec_scr, qdec_scr, kdec_scr) = refs
    cross = latent or nc > 1
    pair0 = pl.program_id(0) * npairs
    first_seq = pl.program_id(1) == 0
    pw, vw = 2 * DK_R, 2 * DV

    lane_q = lax.broadcasted_iota(jnp.int32, (1, pw), 1)
    lane_v = lax.broadcasted_iota(jnp.int32, (1, vw), 1)
    rel = (lax.broadcasted_iota(jnp.int32, (CHUNK, CHUNK), 0)
           - lax.broadcasted_iota(jnp.int32, (CHUNK, CHUNK), 1)).astype(F32)
    row_q = lax.broadcasted_iota(jnp.int32, (CHUNK, pw), 0).astype(F32)
    srow = lax.broadcasted_iota(jnp.int32, (2 * pw, vw), 0)
    scol = lax.broadcasted_iota(jnp.int32, (2 * pw, vw), 1)
    diag_blocks = ((srow % pw) // DK_R) == (scol // DV)

    def rows(c):
        return slice(c * CHUNK, (c + 1) * CHUNK)

    def log_gammas(pi):
        def one(d, hh):
            r = jnp.full((1, vw), rd_ref[d, 2 * (pair0 + pi) + hh], F32)
            return jnp.log1p(-jnp.exp2(r))
        return [[one(d, hh) for hh in range(2)] for d in range(2)]

    @pl.when(first_seq)
    def _():
        for pi in range(npairs):
            lg = log_gammas(pi)
            lg_q = [jnp.where(lane_q < DK_R, lg[d][0][:, :pw], lg[d][1][:, :pw])
                    for d in range(2)]
            for hh in range(2):
                dec_scr[pi, hh] = (
                    jnp.where(rel >= 0, jnp.exp(lg[0][hh][:, :1] * jnp.maximum(rel, 0.0)), 0.0)
                    + jnp.where(rel <= 0, jnp.exp(lg[1][hh][:, :1] * jnp.maximum(-rel, 0.0)), 0.0))
            qdec_scr[pi] = jnp.concatenate([jnp.exp(lg_q[0] * (row_q + 1.0)),
                                            jnp.exp(lg_q[1] * (CHUNK - row_q))], axis=1)
            kdec_scr[pi] = jnp.concatenate([jnp.exp(lg_q[0] * (CHUNK - 1.0 - row_q)),
                                            jnp.exp(lg_q[1] * row_q)], axis=1)

    for pi in range(npairs):
        qs = slice(pi * pw, (pi + 1) * pw)
        vs = slice(pi * vw, (pi + 1) * vw)
        lg = log_gammas(pi)
        lg_v = [jnp.where(lane_v < DV, lg[d][0], lg[d][1]) for d in range(2)]
        cd_f = jnp.exp(lg_v[0] * CHUNK)
        cd_b = jnp.exp(lg_v[1] * CHUNK)

        for c in range(nc):
            k = k_ref[rows(c), qs].astype(F32)
            kd = (jnp.concatenate([k, k], axis=1) * kdec_scr[pi]).astype(BF16)
            u_scr[pi, c] = jnp.where(diag_blocks, _dot_tn(kd, v_ref[rows(c), vs]), 0.0)

        def init_state(s0_ref):
            if not latent:
                return jnp.zeros((pw, vw), F32)
            z = jnp.zeros((DK_R, DV), F32)
            return jnp.concatenate(
                [jnp.concatenate([s0_ref[2 * pi], z], axis=1),
                 jnp.concatenate([z, s0_ref[2 * pi + 1]], axis=1)], axis=0)

        s = init_state(s0f_ref if latent else None)
        for c in range(nc):
            if cross:
                s_scr[pi, c, 0:pw, :] = s.astype(BF16)
            s = s * cd_f + u_scr[pi, c, 0:pw, :]
        if not latent:
            sfo_ref[2 * pi] = s[:DK_R, :DV]
            sfo_ref[2 * pi + 1] = s[DK_R:, DV:]
        s = init_state(s0b_ref if latent else None)
        for c in reversed(range(nc)):
            if cross:
                s_scr[pi, c, pw:2 * pw, :] = s.astype(BF16)
            s = s * cd_b + u_scr[pi, c, pw:2 * pw, :]
        if not latent:
            sbo_ref[2 * pi] = s[:DK_R, :DV]
            sbo_ref[2 * pi + 1] = s[DK_R:, DV:]

        g = g_ref[:, vs]
        for c in range(nc):
            q = q_ref[rows(c), qs]
            k = k_ref[rows(c), qs]
            v = v_ref[rows(c), vs]
            zero = jnp.zeros_like(q)
            q2 = jnp.concatenate([jnp.where(lane_q < DK_R, q, zero),
                                  jnp.where(lane_q >= DK_R, q, zero)], axis=0)
            a2 = _dot_nt(q2, k)
            o = jnp.concatenate(
                [_dot((a2[hh * CHUNK:(hh + 1) * CHUNK] * dec_scr[pi, hh]).astype(BF16),
                      v[:, hh * DV:(hh + 1) * DV]) for hh in range(2)], axis=1)
            if cross:
                qf = q.astype(F32)
                qq = (jnp.concatenate([qf, qf], axis=1) * qdec_scr[pi]).astype(BF16)
                o = o + _dot(qq, s_scr[pi, c])
            outs = []
            for hh in range(2):
                oh = o[:, hh * DV:(hh + 1) * DV]
                mu = jnp.mean(oh, axis=-1, keepdims=True)
                oc = oh - mu
                var = jnp.mean(oc * oc, axis=-1, keepdims=True)
                outs.append(oc * lax.rsqrt(var + LN_EPS))
            o = jnp.concatenate(outs, axis=1) * g * z_ref[rows(c), vs].astype(F32)
            o_ref[rows(c), vs] = o.astype(BF16)


def _ret_call(p, ret_decay, gn_g, s0f, s0b, *, latent, batch, seq):
    nc = seq // CHUNK
    hps = 2 if latent else N_HEADS
    npairs = hps // 2
    qw, vw = DK_R * hps, DV * hps
    col = lambda w, off: pl.BlockSpec((seq, w), lambda j, b: (b, off // w + j))
    state = pl.BlockSpec((None, hps, DK_R, DV), lambda j, b: (b, j, 0, 0))
    in_specs = [pl.BlockSpec(memory_space=pltpu.SMEM),
                col(qw, QR_BLK * LANES), col(qw, KR_BLK * LANES),
                col(vw, VR_BLK * LANES), col(vw, ZR_BLK * LANES),
                pl.BlockSpec((1, vw), lambda j, b: (0, j))]
    args = [ret_decay, p, p, p, p, gn_g]
    out_shape = [jax.ShapeDtypeStruct((batch * seq, D_MODEL), BF16)]
    out_specs = [pl.BlockSpec((seq, vw), lambda j, b: (b, j))]
    if latent:
        in_specs += [state, state]
        args += [s0f, s0b]
    else:
        out_shape += [jax.ShapeDtypeStruct((batch, N_HEADS, DK_R, DV), F32)] * 2
        out_specs += [state, state]
    stacked_state = (npairs, nc, 4 * DK_R, 2 * DV)
    return pl.pallas_call(
        functools.partial(_ret_kernel, latent=latent, nc=nc, npairs=npairs),
        grid=(N_HEADS // hps, batch),
        in_specs=in_specs,
        out_specs=out_specs,
        out_shape=out_shape,
        scratch_shapes=[pltpu.VMEM(stacked_state, F32), pltpu.VMEM(stacked_state, BF16),
                        pltpu.VMEM((npairs, 2, CHUNK, CHUNK), F32),
                        pltpu.VMEM((npairs, CHUNK, 4 * DK_R), F32),
                        pltpu.VMEM((npairs, CHUNK, 4 * DK_R), F32)],
        compiler_params=_params(("arbitrary", "arbitrary")),
        name="ret_lat" if latent else "ret_ctx",
    )(*args)


def _out_kernel(oa_ref, or_ref, ga_ref, gr_ref, x_ref, mod_ref, wpa_ref, wpr_ref, wo_ref,
                lng_ref, lnb_ref, y_ref):
    a = _dot(oa_ref[...], wpa_ref[...])
    r = _dot(or_ref[...], wpr_ref[...])
    m = ga_ref[...].astype(F32) * a + gr_ref[...].astype(F32) * r
    out = _dot(m.astype(BF16), wo_ref[...])
    gate = mod_ref[0, :, 2 * D_MODEL:3 * D_MODEL]
    t = DEEPNORM_ALPHA * x_ref[...] + gate * out
    mu = jnp.mean(t, axis=-1, keepdims=True)
    tc = t - mu
    var = jnp.mean(tc * tc, axis=-1, keepdims=True)
    y_ref[...] = tc * lax.rsqrt(var + LN_EPS) * lng_ref[...] + lnb_ref[...]


def _out_call(oa, orr, p, x2d, mod3, w_pa, w_pr, w_out, ln_g, ln_b, *, latent, seq, mod_row0):
    m = x2d.shape[0]
    tm = 512
    per_seq = max(seq // tm, 1)

    def row(i):
        return (mod_row0 + i // per_seq) if latent else mod_row0

    tile = lambda j: pl.BlockSpec((tm, D_MODEL), lambda i: (i, j))
    full = lambda shape: pl.BlockSpec(shape, lambda i: tuple(0 for _ in shape))
    return pl.pallas_call(
        _out_kernel,
        grid=(m // tm,),
        in_specs=[tile(0), tile(0), tile(7), tile(8), tile(0),
                  pl.BlockSpec((1, 1, 3 * D_MODEL), lambda i: (row(i), 0, 0)),
                  full((D_MODEL, D_MODEL)), full((D_MODEL, D_MODEL)), full((D_MODEL, D_MODEL)),
                  full((1, D_MODEL)), full((1, D_MODEL))],
        out_specs=tile(0),
        out_shape=jax.ShapeDtypeStruct((m, D_MODEL), F32),
        compiler_params=_params(("arbitrary",)),
        name="out_lat" if latent else "out_ctx",
    )(oa, orr, p, p, x2d, mod3, w_pa, w_pr, w_out, ln_g, ln_b)


def _rope_tables(n_tokens):
    rows = n_tokens // GRID_W
    r = np.repeat(np.arange(rows, dtype=np.float32), GRID_W)
    col = np.tile(np.arange(GRID_W, dtype=np.float32), rows)
    n_freq = DK_A // 4
    inv = jnp.asarray(ROPE_BASE, F32) ** (-jnp.arange(n_freq, dtype=F32) / n_freq)
    ang = jnp.concatenate([jnp.asarray(r)[:, None] * inv, jnp.asarray(col)[:, None] * inv], axis=-1)
    cos = jnp.repeat(jnp.cos(ang), 2, axis=-1)
    sin = jnp.repeat(jnp.sin(ang), 2, axis=-1)
    even = (jnp.arange(DK_A) % 2 == 0)[None, :]
    sin_even = jnp.where(even, -sin, 0.0)
    sin_odd = jnp.where(even, 0.0, sin)
    two = lambda t: jnp.concatenate([t, t], axis=-1).astype(F32)
    return two(cos), two(sin_even), two(sin_odd)


def kernel(x_prompt, x_sample, cache_attn_k, cache_attn_v, state_ret_fwd, state_ret_bwd,
           c, c_ctx, w_mod, b_mod, w_in, lam_params, subln_g, ret_decay, ret_gn_g,
           w_pa, w_pr, w_gate, b_gate, w_out, ln_g, ln_b):
    batch, seq, _ = x_prompt.shape
    dbatch, dseq, _ = x_sample.shape
    past = cache_attn_k.shape[2]
    l = 0

    cond8 = jnp.concatenate([c_ctx[None, :], c, jnp.zeros((8 - 1 - dbatch, D_MODEL), F32)], axis=0)
    mod3 = _mod_call(cond8, w_mod[l], b_mod[l][None, :]).reshape(8, 1, 3 * D_MODEL)

    w_all = jnp.concatenate([w_in[l], w_gate[l]], axis=1).astype(BF16)
    bg = b_gate[l][None, :]
    wpa, wpr, wo = w_pa[l].astype(BF16), w_pr[l].astype(BF16), w_out[l].astype(BF16)
    lp, sg = lam_params[l], subln_g[l][None, :]
    rd, gg = ret_decay[l], ret_gn_g[l][None, :]
    lng, lnb = ln_g[l][None, :], ln_b[l][None, :]

    xc = x_prompt.reshape(batch * seq, D_MODEL)
    p_c, k_c, v_c = _proj_call(xc, mod3, w_all, bg, None, latent=False, seq=seq, mod_row0=0)
    oa_c = _attn_ctx_call(p_c, lp, sg, batch=batch, seq=seq)
    or_c, sf_c, sb_c = _ret_call(p_c, rd, gg, None, None, latent=False, batch=batch, seq=seq)
    y_c = _out_call(oa_c, or_c, p_c, xc, mod3, wpa, wpr, wo, lng, lnb,
                    latent=False, seq=seq, mod_row0=0)

    xs = x_sample.reshape(dbatch * dseq, D_MODEL)
    (p_s,) = _proj_call(xs, mod3, w_all, bg, _rope_tables(dseq), latent=True, seq=dseq, mod_row0=1)
    ck = cache_attn_k[:, l].reshape(dbatch, past, N_HEADS * 2 * DK_A)
    cv = cache_attn_v[:, l].reshape(dbatch, past, N_HEADS * DV)
    oa_s = _attn_lat_call(p_s, ck, cv, lp, sg, batch=dbatch, seq=dseq, past=past)
    (or_s,) = _ret_call(p_s, rd, gg, state_ret_fwd[:, l], state_ret_bwd[:, l],
                        latent=True, batch=dbatch, seq=dseq)
    y_s = _out_call(oa_s, or_s, p_s, xs, mod3, wpa, wpr, wo, lng, lnb,
                    latent=True, seq=dseq, mod_row0=1)

    return (y_c.reshape(batch, seq, D_MODEL),
            y_s.reshape(dbatch, dseq, D_MODEL),
            k_c.reshape(batch, 1, seq, N_HEADS, 2 * DK_A),
            v_c.reshape(batch, 1, seq, N_HEADS, DV),
            sf_c.reshape(batch, 1, N_HEADS, DK_R, DV),
            sb_c.reshape(batch, 1, N_HEADS, DK_R, DV))
```

```python
import functools
import math

import jax
import jax.numpy as jnp
import numpy as np
from jax import lax
from jax.experimental import pallas as pl
from jax.experimental.pallas import tpu as pltpu

F32 = jnp.float32
BF16 = jnp.bfloat16

D_MODEL = 1024
N_HEADS = 8
DK_A = 64
DV = 128
DK_R = 64
CHUNK = 256
GRID_W = 64
ROPE_BASE = 10000.0
MOD_EPS = 1e-6
LN_EPS = 1e-5
DEPTH = 1
DEEPNORM_ALPHA = (2.0 * DEPTH) ** 0.25
LAM_INIT = 0.8 - 0.6 * math.exp(-0.3 * 0)
LOG2E = math.log2(math.e)

LANES = 128
QA_BLK, KA_BLK, VA_BLK, ZA_BLK = 0, 8, 16, 24
QR_BLK, KR_BLK, VR_BLK, ZR_BLK = 32, 36, 40, 48
IN_SEGS = 7
P_WIDTH = (IN_SEGS + 2) * D_MODEL
VMEM_LIMIT = 56 * 1024 * 1024


def _params(sem):
    return pltpu.CompilerParams(dimension_semantics=sem, vmem_limit_bytes=VMEM_LIMIT)


def _silu(z):
    return z * (1.0 / (1.0 + jnp.exp(-z)))


def _dot(a, b):
    return jnp.dot(a, b, preferred_element_type=F32)


def _dot_nt(a, b):
    return lax.dot_general(a, b, (((1,), (1,)), ((), ())), preferred_element_type=F32)


def _dot_tn(a, b):
    return lax.dot_general(a, b, (((0,), (0,)), ((), ())), preferred_element_type=F32)


def _mod_kernel(cond_ref, w_ref, b_ref, o_ref):
    s = _silu(cond_ref[...])
    o_ref[...] = _dot(s.astype(BF16), w_ref[...].astype(BF16)) + b_ref[...]


def _mod_call(cond8, w_mod, b_mod):
    tn = D_MODEL
    return pl.pallas_call(
        _mod_kernel,
        grid=(3 * D_MODEL // tn,),
        in_specs=[pl.BlockSpec((8, D_MODEL), lambda j: (0, 0)),
                  pl.BlockSpec((D_MODEL, tn), lambda j: (0, j)),
                  pl.BlockSpec((1, tn), lambda j: (0, j))],
        out_specs=pl.BlockSpec((8, tn), lambda j: (0, j)),
        out_shape=jax.ShapeDtypeStruct((8, 3 * D_MODEL), F32),
        compiler_params=_params(("arbitrary",)),
        name="mod",
    )(cond8, w_mod, b_mod)


def _rope(acc, cos, sin_even, sin_odd):
    outs = []
    for hh in range(N_HEADS):
        xs = acc[:, hh * LANES:(hh + 1) * LANES]
        nxt = pltpu.roll(xs, LANES - 1, 1)
        prv = pltpu.roll(xs, 1, 1)
        outs.append(xs * cos + nxt * sin_even + prv * sin_odd)
    return jnp.concatenate(outs, axis=1)


def _modulated_ln(x_ref, mod_ref):
    x = x_ref[...]
    mu = jnp.mean(x, axis=-1, keepdims=True)
    xc = x - mu
    var = jnp.mean(xc * xc, axis=-1, keepdims=True)
    shift = mod_ref[0, :, 0:D_MODEL]
    scale = mod_ref[0, :, D_MODEL:2 * D_MODEL]
    return (xc * lax.rsqrt(var + MOD_EPS) * (1.0 + scale) + shift).astype(BF16)


def _proj_kernel(*refs, latent):
    h_even, h_odd = refs[-2:]
    i = pl.program_id(0)

    @pl.when(i == 0)
    def _():
        h_even[...] = _modulated_ln(refs[2], refs[3])

    @pl.when(i % 2 == 0)
    def _():
        _proj_body(refs, h_even, h_odd, latent)

    @pl.when(i % 2 == 1)
    def _():
        _proj_body(refs, h_odd, h_even, latent)


def _proj_body(refs, h_cur, h_nxt, latent):
    if latent:
        (xn_ref, modn_ref, _, _, win_ref, wg_ref, bg_ref, cos_ref, se_ref, so_ref,
         p_ref, _, _) = refs
    else:
        xn_ref, modn_ref, _, _, win_ref, wg_ref, bg_ref, p_ref, k_ref, v_ref, _, _ = refs
    h_nxt[...] = _modulated_ln(xn_ref, modn_ref)
    h = h_cur[...]

    def seg(j):
        if j < IN_SEGS:
            return _dot(h, win_ref[:, j * D_MODEL:(j + 1) * D_MODEL])
        return _dot(h, wg_ref[:, (j - IN_SEGS) * D_MODEL:(j - IN_SEGS + 1) * D_MODEL])

    def put(j, val):
        p_ref[:, j * D_MODEL:(j + 1) * D_MODEL] = val.astype(BF16)

    qa = seg(0)
    ka = seg(1)
    if latent:
        cos, se, so = cos_ref[...], se_ref[...], so_ref[...]
        qa = _rope(qa, cos, se, so)
        ka = _rope(ka, cos, se, so)
    else:
        k_ref[...] = ka
    put(0, qa * (DK_A ** -0.5 * LOG2E))
    put(1, ka)
    va = seg(2)
    if not latent:
        v_ref[...] = va
    put(2, va)
    put(3, _silu(seg(3)))
    qk = seg(4)
    put(4, jnp.concatenate([qk[:, :D_MODEL // 2], qk[:, D_MODEL // 2:] * (DK_R ** -0.5)], axis=1))
    put(5, seg(5))
    put(6, _silu(seg(6)))
    for j in (IN_SEGS, IN_SEGS + 1):
        g = seg(j) + bg_ref[:, (j - IN_SEGS) * D_MODEL:(j - IN_SEGS + 1) * D_MODEL]
        put(j, 1.0 / (1.0 + jnp.exp(-g)))


def _proj_call(x2d, mod3, w_in, w_gate, b_gate, rope, *, latent, seq, mod_row0):
    m = x2d.shape[0]
    tm = 256
    nt = m // tm
    per_seq = seq // tm

    def row(i):
        return (mod_row0 + i // per_seq) if latent else mod_row0

    nxt = lambda i: jnp.minimum(i + 1, nt - 1)
    in_specs = [
        pl.BlockSpec((tm, D_MODEL), lambda i: (nxt(i), 0)),
        pl.BlockSpec((1, 1, 3 * D_MODEL), lambda i: (row(nxt(i)), 0, 0)),
        pl.BlockSpec((tm, D_MODEL), lambda i: (0, 0)),
        pl.BlockSpec((1, 1, 3 * D_MODEL), lambda i: (row(0), 0, 0)),
        pl.BlockSpec((D_MODEL, IN_SEGS * D_MODEL), lambda i: (0, 0), pipeline_mode=pl.Buffered(1)),
        pl.BlockSpec((D_MODEL, 2 * D_MODEL), lambda i: (0, 0), pipeline_mode=pl.Buffered(1)),
        pl.BlockSpec((1, 2 * D_MODEL), lambda i: (0, 0)),
    ]
    args = [x2d, mod3, x2d, mod3, w_in, w_gate, b_gate]
    out_shape = [jax.ShapeDtypeStruct((m, P_WIDTH), BF16)]
    out_specs = [pl.BlockSpec((tm, P_WIDTH), lambda i: (i, 0))]
    if latent:
        for t in rope:
            in_specs.append(pl.BlockSpec((tm, LANES), lambda i: (i % per_seq, 0)))
            args.append(t)
    else:
        for _ in range(2):
            out_shape.append(jax.ShapeDtypeStruct((m, D_MODEL), F32))
            out_specs.append(pl.BlockSpec((tm, D_MODEL), lambda i: (i, 0)))
    return pl.pallas_call(
        functools.partial(_proj_kernel, latent=latent),
        grid=(nt,),
        in_specs=in_specs,
        out_specs=out_specs,
        out_shape=out_shape,
        scratch_shapes=[pltpu.VMEM((tm, D_MODEL), BF16), pltpu.VMEM((tm, D_MODEL), BF16)],
        compiler_params=_params(("arbitrary",)),
        name="proj_lat" if latent else "proj_ctx",
    )(*args)


def _lam(lp_ref):
    lp = lp_ref[...]
    a = jnp.sum(lp[0:1] * lp[1:2], axis=-1, keepdims=True)
    b = jnp.sum(lp[2:3] * lp[3:4], axis=-1, keepdims=True)
    return jnp.exp(a) - jnp.exp(b) + LAM_INIT


def _attn_rows(q, k_all, v_ones, z, lam, g):
    lane = lax.broadcasted_iota(jnp.int32, (1, LANES), 1)
    zero = jnp.zeros_like(q)
    outs = []
    for qm in (jnp.where(lane < DK_A, q, zero), jnp.where(lane >= DK_A, q, zero)):
        s = _dot_nt(qm, k_all)
        mx = jnp.max(s, axis=-1, keepdims=True)
        p = jnp.exp2(s - mx).astype(BF16)
        ol = _dot(p, v_ones)
        outs.append(ol[:, :DV] * (1.0 / ol[:, DV:]))
    oa = outs[0] - lam * outs[1]
    oa = oa * lax.rsqrt(jnp.mean(oa * oa, axis=-1, keepdims=True) + LN_EPS)
    oa = oa * g * (1.0 - LAM_INIT)
    return oa * z.astype(F32)


def _attn_ctx_kernel(q_ref, k_ref, v_ref, z_ref, lp_ref, g_ref, o_ref):
    lam = _lam(lp_ref)
    g = g_ref[...]
    ones = jnp.ones((q_ref.shape[0], DV), BF16)
    for h in range(N_HEADS):
        sl = slice(h * LANES, (h + 1) * LANES)
        v_ones = jnp.concatenate([v_ref[:, sl], ones], axis=1)
        o_ref[:, sl] = _attn_rows(q_ref[:, sl], k_ref[:, sl], v_ones, z_ref[:, sl],
                                  lam, g).astype(BF16)


def _attn_ctx_call(p, lam_params, subln_g, *, batch, seq):
    blk = lambda j: pl.BlockSpec((seq, D_MODEL), lambda b: (b, j))
    return pl.pallas_call(
        _attn_ctx_kernel,
        grid=(batch,),
        in_specs=[blk(0), blk(1), blk(2), blk(3),
                  pl.BlockSpec((4, DK_A), lambda b: (0, 0)),
                  pl.BlockSpec((1, DV), lambda b: (0, 0))],
        out_specs=pl.BlockSpec((seq, D_MODEL), lambda b: (b, 0)),
        out_shape=jax.ShapeDtypeStruct((batch * seq, D_MODEL), BF16),
        compiler_params=_params(("arbitrary",)),
        name="attn_ctx",
    )(p, p, p, p, lam_params, subln_g)


def _attn_lat_kernel(q_ref, k_ref, v_ref, z_ref, kc_ref, vc_ref, lp_ref, g_ref, o_ref,
                     k_all, v_all, *, past, tc):
    @pl.when(pl.program_id(2) == 0)
    def _():
        k_all[0:past, :] = kc_ref[...].astype(BF16)
        v_all[0:past, 0:DV] = vc_ref[...].astype(BF16)
        k_all[past:, :] = k_ref[...]
        v_all[past:, 0:DV] = v_ref[...]
        v_all[:, DV:] = jnp.ones((v_all.shape[0], DV), BF16)

    lam = _lam(lp_ref)
    g = g_ref[...]

    def body(c, carry):
        rows = pl.ds(pl.multiple_of(c * tc, tc), tc)
        o_ref[rows, :] = _attn_rows(q_ref[rows, :], k_all[...], v_all[...], z_ref[rows, :],
                                    lam, g).astype(BF16)
        return carry

    lax.fori_loop(0, q_ref.shape[0] // tc, body, 0, unroll=True)


def _attn_lat_call(p, cache_k, cache_v, lam_params, subln_g, *, batch, seq, past):
    tq, tc = 2048, 256
    nq = seq // tq
    return pl.pallas_call(
        functools.partial(_attn_lat_kernel, past=past, tc=tc),
        grid=(batch, N_HEADS, nq),
        in_specs=[
            pl.BlockSpec((tq, LANES), lambda b, h, i: (b * nq + i, QA_BLK + h)),
            pl.BlockSpec((seq, LANES), lambda b, h, i: (b, KA_BLK + h)),
            pl.BlockSpec((seq, LANES), lambda b, h, i: (b, VA_BLK + h)),
            pl.BlockSpec((tq, LANES), lambda b, h, i: (b * nq + i, ZA_BLK + h)),
            pl.BlockSpec((None, past, LANES), lambda b, h, i: (b, 0, h)),
            pl.BlockSpec((None, past, LANES), lambda b, h, i: (b, 0, h)),
            pl.BlockSpec((4, DK_A), lambda b, h, i: (0, 0)),
            pl.BlockSpec((1, DV), lambda b, h, i: (0, 0)),
        ],
        out_specs=pl.BlockSpec((tq, LANES), lambda b, h, i: (b * nq + i, h)),
        out_shape=jax.ShapeDtypeStruct((batch * seq, D_MODEL), BF16),
        scratch_shapes=[pltpu.VMEM((past + seq, LANES), BF16),
                        pltpu.VMEM((past + seq, 2 * DV), BF16)],
        compiler_params=_params(("arbitrary", "arbitrary", "arbitrary")),
        name="attn_lat",
    )(p, p, p, p, cache_k, cache_v, lam_params, subln_g)


def _ret_kernel(*refs, latent, nc, npairs):
    if latent:
        (rd_ref, q_ref, k_ref, v_ref, z_ref, g_ref, s0f_ref, s0b_ref,
         o_ref, u_scr, s_scr, dec_scr, qdec_scr, kdec_scr) = refs
    else:
        (rd_ref, q_ref, k_ref, v_ref, z_ref, g_ref,
         o_ref, sfo_ref, sbo_ref, u_scr, s_scr, dec_scr, qdec_scr, kdec_scr) = refs
    cross = latent or nc > 1
    pair0 = pl.program_id(0) * npairs
    first_seq = pl.program_id(1) == 0
    pw, vw = 2 * DK_R, 2 * DV

    lane_q = lax.broadcasted_iota(jnp.int32, (1, pw), 1)
    lane_v = lax.broadcasted_iota(jnp.int32, (1, vw), 1)
    rel = (lax.broadcasted_iota(jnp.int32, (CHUNK, CHUNK), 0)
           - lax.broadcasted_iota(jnp.int32, (CHUNK, CHUNK), 1)).astype(F32)
    row_q = lax.broadcasted_iota(jnp.int32, (CHUNK, pw), 0).astype(F32)
    srow = lax.broadcasted_iota(jnp.int32, (2 * pw, vw), 0)
    scol = lax.broadcasted_iota(jnp.int32, (2 * pw, vw), 1)
    diag_blocks = ((srow % pw) // DK_R) == (scol // DV)

    def rows(c):
        return slice(c * CHUNK, (c + 1) * CHUNK)

    def log_gammas(pi):
        def one(d, hh):
            r = jnp.full((1, vw), rd_ref[d, 2 * (pair0 + pi) + hh], F32)
            return jnp.log1p(-jnp.exp2(r))
        return [[one(d, hh) for hh in range(2)] for d in range(2)]

    @pl.when(first_seq)
    def _():
        for pi in range(npairs):
            lg = log_gammas(pi)
            lg_q = [jnp.where(lane_q < DK_R, lg[d][0][:, :pw], lg[d][1][:, :pw])
                    for d in range(2)]
            for hh in range(2):
                dec_scr[pi, hh] = (
                    jnp.where(rel >= 0, jnp.exp(lg[0][hh][:, :1] * jnp.maximum(rel, 0.0)), 0.0)
                    + jnp.where(rel <= 0, jnp.exp(lg[1][hh][:, :1] * jnp.maximum(-rel, 0.0)), 0.0))
            qdec_scr[pi] = jnp.concatenate([jnp.exp(lg_q[0] * (row_q + 1.0)),
                                            jnp.exp(lg_q[1] * (CHUNK - row_q))], axis=1)
            kdec_scr[pi] = jnp.concatenate([jnp.exp(lg_q[0] * (CHUNK - 1.0 - row_q)),
                                            jnp.exp(lg_q[1] * row_q)], axis=1)

    for pi in range(npairs):
        qs = slice(pi * pw, (pi + 1) * pw)
        vs = slice(pi * vw, (pi + 1) * vw)
        lg = log_gammas(pi)
        lg_v = [jnp.where(lane_v < DV, lg[d][0], lg[d][1]) for d in range(2)]
        cd_f = jnp.exp(lg_v[0] * CHUNK)
        cd_b = jnp.exp(lg_v[1] * CHUNK)

        for c in range(nc):
            k = k_ref[rows(c), qs].astype(F32)
            kd = (jnp.concatenate([k, k], axis=1) * kdec_scr[pi]).astype(BF16)
            u_scr[pi, c] = jnp.where(diag_blocks, _dot_tn(kd, v_ref[rows(c), vs]), 0.0)

        def init_state(s0_ref):
            if not latent:
                return jnp.zeros((pw, vw), F32)
            z = jnp.zeros((DK_R, DV), F32)
            return jnp.concatenate(
                [jnp.concatenate([s0_ref[2 * pi], z], axis=1),
                 jnp.concatenate([z, s0_ref[2 * pi + 1]], axis=1)], axis=0)

        s = init_state(s0f_ref if latent else None)
        for c in range(nc):
            if cross:
                s_scr[pi, c, 0:pw, :] = s.astype(BF16)
            s = s * cd_f + u_scr[pi, c, 0:pw, :]
        if not latent:
            sfo_ref[2 * pi] = s[:DK_R, :DV]
            sfo_ref[2 * pi + 1] = s[DK_R:, DV:]
        s = init_state(s0b_ref if latent else None)
        for c in reversed(range(nc)):
            if cross:
                s_scr[pi, c, pw:2 * pw, :] = s.astype(BF16)
            s = s * cd_b + u_scr[pi, c, pw:2 * pw, :]
        if not latent:
            sbo_ref[2 * pi] = s[:DK_R, :DV]
            sbo_ref[2 * pi + 1] = s[DK_R:, DV:]

        g = g_ref[:, vs]
        for c in range(nc):
            q = q_ref[rows(c), qs]
            k = k_ref[rows(c), qs]
            v = v_ref[rows(c), vs]
            zero = jnp.zeros_like(q)
            q2 = jnp.concatenate([jnp.where(lane_q < DK_R, q, zero),
                                  jnp.where(lane_q >= DK_R, q, zero)], axis=0)
            a2 = _dot_nt(q2, k)
            o = jnp.concatenate(
                [_dot((a2[hh * CHUNK:(hh + 1) * CHUNK] * dec_scr[pi, hh]).astype(BF16),
                      v[:, hh * DV:(hh + 1) * DV]) for hh in range(2)], axis=1)
            if cross:
                qf = q.astype(F32)
                qq = (jnp.concatenate([qf, qf], axis=1) * qdec_scr[pi]).astype(BF16)
                o = o + _dot(qq, s_scr[pi, c])
            outs = []
            for hh in range(2):
                oh = o[:, hh * DV:(hh + 1) * DV]
                mu = jnp.mean(oh, axis=-1, keepdims=True)
                oc = oh - mu
                var = jnp.mean(oc * oc, axis=-1, keepdims=True)
                outs.append(oc * lax.rsqrt(var + LN_EPS))
            o = jnp.concatenate(outs, axis=1) * g * z_ref[rows(c), vs].astype(F32)
            o_ref[rows(c), vs] = o.astype(BF16)


def _ret_call(p, ret_decay, gn_g, s0f, s0b, *, latent, batch, seq):
    nc = seq // CHUNK
    hps = 2 if latent else N_HEADS
    npairs = hps // 2
    qw, vw = DK_R * hps, DV * hps
    col = lambda w, off: pl.BlockSpec((seq, w), lambda j, b: (b, off // w + j))
    state = pl.BlockSpec((None, hps, DK_R, DV), lambda j, b: (b, j, 0, 0))
    in_specs = [pl.BlockSpec(memory_space=pltpu.SMEM),
                col(qw, QR_BLK * LANES), col(qw, KR_BLK * LANES),
                col(vw, VR_BLK * LANES), col(vw, ZR_BLK * LANES),
                pl.BlockSpec((1, vw), lambda j, b: (0, j))]
    args = [ret_decay, p, p, p, p, gn_g]
    out_shape = [jax.ShapeDtypeStruct((batch * seq, D_MODEL), BF16)]
    out_specs = [pl.BlockSpec((seq, vw), lambda j, b: (b, j))]
    if latent:
        in_specs += [state, state]
        args += [s0f, s0b]
    else:
        out_shape += [jax.ShapeDtypeStruct((batch, N_HEADS, DK_R, DV), F32)] * 2
        out_specs += [state, state]
    stacked_state = (npairs, nc, 4 * DK_R, 2 * DV)
    return pl.pallas_call(
        functools.partial(_ret_kernel, latent=latent, nc=nc, npairs=npairs),
        grid=(N_HEADS // hps, batch),
        in_specs=in_specs,
        out_specs=out_specs,
        out_shape=out_shape,
        scratch_shapes=[pltpu.VMEM(stacked_state, F32), pltpu.VMEM(stacked_state, BF16),
                        pltpu.VMEM((npairs, 2, CHUNK, CHUNK), F32),
                        pltpu.VMEM((npairs, CHUNK, 4 * DK_R), F32),
                        pltpu.VMEM((npairs, CHUNK, 4 * DK_R), F32)],
        compiler_params=_params(("arbitrary", "arbitrary")),
        name="ret_lat" if latent else "ret_ctx",
    )(*args)


def _out_kernel(oa_ref, or_ref, ga_ref, gr_ref, x_ref, mod_ref, wpa_ref, wpr_ref, wo_ref,
                lng_ref, lnb_ref, y_ref, *, sub):
    gate = mod_ref[0, :, 2 * D_MODEL:3 * D_MODEL]
    for s in range(y_ref.shape[0] // sub):
        rows = slice(s * sub, (s + 1) * sub)
        a = _dot(oa_ref[rows, :], wpa_ref[...])
        r = _dot(or_ref[rows, :], wpr_ref[...])
        m = ga_ref[rows, :].astype(F32) * a + gr_ref[rows, :].astype(F32) * r
        out = _dot(m.astype(BF16), wo_ref[...])
        t = DEEPNORM_ALPHA * x_ref[rows, :] + gate * out
        mu = jnp.mean(t, axis=-1, keepdims=True)
        tc = t - mu
        var = jnp.mean(tc * tc, axis=-1, keepdims=True)
        y_ref[rows, :] = tc * lax.rsqrt(var + LN_EPS) * lng_ref[...] + lnb_ref[...]


def _out_call(oa, orr, p, x2d, mod3, w_pa, w_pr, w_out, ln_g, ln_b, *, latent, seq, mod_row0):
    m = x2d.shape[0]
    tm, sub = 1024, 512
    per_seq = max(seq // tm, 1)

    def row(i):
        return (mod_row0 + i // per_seq) if latent else mod_row0

    tile = lambda j: pl.BlockSpec((tm, D_MODEL), lambda i: (i, j))
    full = lambda shape: pl.BlockSpec(shape, lambda i: tuple(0 for _ in shape))
    return pl.pallas_call(
        functools.partial(_out_kernel, sub=sub),
        grid=(m // tm,),
        in_specs=[tile(0), tile(0), tile(7), tile(8), tile(0),
                  pl.BlockSpec((1, 1, 3 * D_MODEL), lambda i: (row(i), 0, 0)),
                  full((D_MODEL, D_MODEL)), full((D_MODEL, D_MODEL)), full((D_MODEL, D_MODEL)),
                  full((1, D_MODEL)), full((1, D_MODEL))],
        out_specs=tile(0),
        out_shape=jax.ShapeDtypeStruct((m, D_MODEL), F32),
        compiler_params=_params(("arbitrary",)),
        name="out_lat" if latent else "out_ctx",
    )(oa, orr, p, p, x2d, mod3, w_pa, w_pr, w_out, ln_g, ln_b)


def _rope_tables(n_tokens):
    rows = n_tokens // GRID_W
    r = np.repeat(np.arange(rows, dtype=np.float32), GRID_W)
    col = np.tile(np.arange(GRID_W, dtype=np.float32), rows)
    n_freq = DK_A // 4
    inv = np.float32(ROPE_BASE) ** (-np.arange(n_freq, dtype=np.float32) / np.float32(n_freq))
    ang = np.concatenate([r[:, None] * inv, col[:, None] * inv], axis=-1).astype(np.float32)
    cos = np.repeat(np.cos(ang), 2, axis=-1)
    sin = np.repeat(np.sin(ang), 2, axis=-1)
    even = (np.arange(DK_A) % 2 == 0)[None, :]
    sin_even = np.where(even, -sin, 0.0)
    sin_odd = np.where(even, 0.0, sin)
    two = lambda t: jnp.asarray(np.concatenate([t, t], axis=-1), F32)
    return two(cos), two(sin_even), two(sin_odd)


def kernel(x_prompt, x_sample, cache_attn_k, cache_attn_v, state_ret_fwd, state_ret_bwd,
           c, c_ctx, w_mod, b_mod, w_in, lam_params, subln_g, ret_decay, ret_gn_g,
           w_pa, w_pr, w_gate, b_gate, w_out, ln_g, ln_b):
    batch, seq, _ = x_prompt.shape
    dbatch, dseq, _ = x_sample.shape
    past = cache_attn_k.shape[2]
    l = 0

    cond8 = jnp.concatenate([c_ctx[None, :], c, jnp.zeros((8 - 1 - dbatch, D_MODEL), F32)], axis=0)
    mod3 = _mod_call(cond8, w_mod[l], b_mod[l][None, :]).reshape(8, 1, 3 * D_MODEL)

    win, wg = w_in[l].astype(BF16), w_gate[l].astype(BF16)
    bg = b_gate[l][None, :]
    wpa, wpr, wo = w_pa[l].astype(BF16), w_pr[l].astype(BF16), w_out[l].astype(BF16)
    lp, sg = lam_params[l], subln_g[l][None, :]
    rd, gg = ret_decay[l], ret_gn_g[l][None, :]
    lng, lnb = ln_g[l][None, :], ln_b[l][None, :]

    xc = x_prompt.reshape(batch * seq, D_MODEL)
    p_c, k_c, v_c = _proj_call(xc, mod3, win, wg, bg, None, latent=False, seq=seq, mod_row0=0)
    oa_c = _attn_ctx_call(p_c, lp, sg, batch=batch, seq=seq)
    or_c, sf_c, sb_c = _ret_call(p_c, rd, gg, None, None, latent=False, batch=batch, seq=seq)
    y_c = _out_call(oa_c, or_c, p_c, xc, mod3, wpa, wpr, wo, lng, lnb,
                    latent=False, seq=seq, mod_row0=0)

    xs = x_sample.reshape(dbatch * dseq, D_MODEL)
    (p_s,) = _proj_call(xs, mod3, win, wg, bg, _rope_tables(dseq), latent=True, seq=dseq, mod_row0=1)
    ck = cache_attn_k[:, l].reshape(dbatch, past, N_HEADS * 2 * DK_A)
    cv = cache_attn_v[:, l].reshape(dbatch, past, N_HEADS * DV)
    oa_s = _attn_lat_call(p_s, ck, cv, lp, sg, batch=dbatch, seq=dseq, past=past)
    (or_s,) = _ret_call(p_s, rd, gg, state_ret_fwd[:, l], state_ret_bwd[:, l],
                        latent=True, batch=dbatch, seq=dseq)
    y_s = _out_call(oa_s, or_s, p_s, xs, mod3, wpa, wpr, wo, lng, lnb,
                    latent=True, seq=dseq, mod_row0=1)

    return (y_c.reshape(batch, seq, D_MODEL),
            y_s.reshape(dbatch, dseq, D_MODEL),
            k_c.reshape(batch, 1, seq, N_HEADS, 2 * DK_A),
            v_c.reshape(batch, 1, seq, N_HEADS, DV),
            sf_c.reshape(batch, 1, N_HEADS, DK_R, DV),
            sb_c.reshape(batch, 1, N_HEADS, DK_R, DV))
```

```python
import functools
import math

import jax
import jax.numpy as jnp
import numpy as np
from jax import lax
from jax.experimental import pallas as pl
from jax.experimental.pallas import tpu as pltpu

F32 = jnp.float32
BF16 = jnp.bfloat16

D_MODEL = 1024
N_HEADS = 8
DK_A = 64
DV = 128
DK_R = 64
CHUNK = 256
GRID_W = 64
ROPE_BASE = 10000.0
MOD_EPS = 1e-6
LN_EPS = 1e-5
DEPTH = 1
DEEPNORM_ALPHA = (2.0 * DEPTH) ** 0.25
LAM_INIT = 0.8 - 0.6 * math.exp(-0.3 * 0)
LOG2E = math.log2(math.e)

LANES = 128
QA_BLK, KA_BLK, VA_BLK, ZA_BLK = 0, 8, 16, 24
QR_BLK, KR_BLK, VR_BLK, ZR_BLK = 32, 36, 40, 48
IN_SEGS = 7
P_WIDTH = (IN_SEGS + 2) * D_MODEL
VMEM_LIMIT = 56 * 1024 * 1024


def _params(sem):
    return pltpu.CompilerParams(dimension_semantics=sem, vmem_limit_bytes=VMEM_LIMIT)


def _silu(z):
    return z * (1.0 / (1.0 + jnp.exp(-z)))


def _dot(a, b):
    return jnp.dot(a, b, preferred_element_type=F32)


def _dot_nt(a, b):
    return lax.dot_general(a, b, (((1,), (1,)), ((), ())), preferred_element_type=F32)


def _dot_tn(a, b):
    return lax.dot_general(a, b, (((0,), (0,)), ((), ())), preferred_element_type=F32)


def _mod_kernel(cond_ref, w_ref, b_ref, o_ref):
    s = _silu(cond_ref[...])
    o_ref[...] = _dot(s.astype(BF16), w_ref[...].astype(BF16)) + b_ref[...]


def _mod_call(cond8, w_mod, b_mod):
    tn = D_MODEL
    return pl.pallas_call(
        _mod_kernel,
        grid=(3 * D_MODEL // tn,),
        in_specs=[pl.BlockSpec((8, D_MODEL), lambda j: (0, 0)),
                  pl.BlockSpec((D_MODEL, tn), lambda j: (0, j)),
                  pl.BlockSpec((1, tn), lambda j: (0, j))],
        out_specs=pl.BlockSpec((8, tn), lambda j: (0, j)),
        out_shape=jax.ShapeDtypeStruct((8, 3 * D_MODEL), F32),
        compiler_params=_params(("arbitrary",)),
        name="mod",
    )(cond8, w_mod, b_mod)


def _rope(acc, cos, sin_even, sin_odd):
    outs = []
    for hh in range(N_HEADS):
        xs = acc[:, hh * LANES:(hh + 1) * LANES]
        nxt = pltpu.roll(xs, LANES - 1, 1)
        prv = pltpu.roll(xs, 1, 1)
        outs.append(xs * cos + nxt * sin_even + prv * sin_odd)
    return jnp.concatenate(outs, axis=1)


def _modulated_ln(x_ref, mod_ref):
    x = x_ref[...]
    mu = jnp.mean(x, axis=-1, keepdims=True)
    xc = x - mu
    var = jnp.mean(xc * xc, axis=-1, keepdims=True)
    shift = mod_ref[0, :, 0:D_MODEL]
    scale = mod_ref[0, :, D_MODEL:2 * D_MODEL]
    return (xc * lax.rsqrt(var + MOD_EPS) * (1.0 + scale) + shift).astype(BF16)


def _proj_kernel(*refs, latent):
    h_even, h_odd = refs[-2:]
    i = pl.program_id(0)

    @pl.when(i == 0)
    def _():
        h_even[...] = _modulated_ln(refs[2], refs[3])

    @pl.when(i % 2 == 0)
    def _():
        _proj_body(refs, h_even, h_odd, latent)

    @pl.when(i % 2 == 1)
    def _():
        _proj_body(refs, h_odd, h_even, latent)


def _proj_body(refs, h_cur, h_nxt, latent):
    if latent:
        (xn_ref, modn_ref, _, _, win_ref, wg_ref, bg_ref, cos_ref, se_ref, so_ref,
         p_ref, _, _) = refs
    else:
        xn_ref, modn_ref, _, _, win_ref, wg_ref, bg_ref, p_ref, k_ref, v_ref, _, _ = refs
    h_nxt[...] = _modulated_ln(xn_ref, modn_ref)
    h = h_cur[...]

    def seg(j):
        if j < IN_SEGS:
            return _dot(h, win_ref[:, j * D_MODEL:(j + 1) * D_MODEL])
        return _dot(h, wg_ref[:, (j - IN_SEGS) * D_MODEL:(j - IN_SEGS + 1) * D_MODEL])

    def put(j, val):
        p_ref[:, j * D_MODEL:(j + 1) * D_MODEL] = val.astype(BF16)

    qa = seg(0)
    ka = seg(1)
    if latent:
        cos, se, so = cos_ref[...], se_ref[...], so_ref[...]
        qa = _rope(qa, cos, se, so)
        ka = _rope(ka, cos, se, so)
    else:
        k_ref[...] = ka
    put(0, qa * (DK_A ** -0.5 * LOG2E))
    put(1, ka)
    va = seg(2)
    if not latent:
        v_ref[...] = va
    put(2, va)
    put(3, _silu(seg(3)))
    qk = seg(4)
    put(4, jnp.concatenate([qk[:, :D_MODEL // 2], qk[:, D_MODEL // 2:] * (DK_R ** -0.5)], axis=1))
    put(5, seg(5))
    put(6, _silu(seg(6)))
    for j in (IN_SEGS, IN_SEGS + 1):
        g = seg(j) + bg_ref[:, (j - IN_SEGS) * D_MODEL:(j - IN_SEGS + 1) * D_MODEL]
        put(j, 1.0 / (1.0 + jnp.exp(-g)))


def _proj_call(x2d, mod3, w_in, w_gate, b_gate, rope, *, latent, seq, mod_row0):
    m = x2d.shape[0]
    tm = 256
    nt = m // tm
    per_seq = seq // tm

    def row(i):
        return (mod_row0 + i // per_seq) if latent else mod_row0

    nxt = lambda i: jnp.minimum(i + 1, nt - 1)
    in_specs = [
        pl.BlockSpec((tm, D_MODEL), lambda i: (nxt(i), 0)),
        pl.BlockSpec((1, 1, 3 * D_MODEL), lambda i: (row(nxt(i)), 0, 0)),
        pl.BlockSpec((tm, D_MODEL), lambda i: (0, 0)),
        pl.BlockSpec((1, 1, 3 * D_MODEL), lambda i: (row(0), 0, 0)),
        pl.BlockSpec((D_MODEL, IN_SEGS * D_MODEL), lambda i: (0, 0), pipeline_mode=pl.Buffered(1)),
        pl.BlockSpec((D_MODEL, 2 * D_MODEL), lambda i: (0, 0), pipeline_mode=pl.Buffered(1)),
        pl.BlockSpec((1, 2 * D_MODEL), lambda i: (0, 0)),
    ]
    args = [x2d, mod3, x2d, mod3, w_in, w_gate, b_gate]
    out_shape = [jax.ShapeDtypeStruct((m, P_WIDTH), BF16)]
    out_specs = [pl.BlockSpec((tm, P_WIDTH), lambda i: (i, 0))]
    if latent:
        for t in rope:
            in_specs.append(pl.BlockSpec((tm, LANES), lambda i: (i % per_seq, 0)))
            args.append(t)
    else:
        for _ in range(2):
            out_shape.append(jax.ShapeDtypeStruct((m, D_MODEL), F32))
            out_specs.append(pl.BlockSpec((tm, D_MODEL), lambda i: (i, 0)))
    return pl.pallas_call(
        functools.partial(_proj_kernel, latent=latent),
        grid=(nt,),
        in_specs=in_specs,
        out_specs=out_specs,
        out_shape=out_shape,
        scratch_shapes=[pltpu.VMEM((tm, D_MODEL), BF16), pltpu.VMEM((tm, D_MODEL), BF16)],
        compiler_params=_params(("arbitrary",)),
        name="proj_lat" if latent else "proj_ctx",
    )(*args)


def _lam(lp_ref):
    lp = lp_ref[...]
    a = jnp.sum(lp[0:1] * lp[1:2], axis=-1, keepdims=True)
    b = jnp.sum(lp[2:3] * lp[3:4], axis=-1, keepdims=True)
    return jnp.exp(a) - jnp.exp(b) + LAM_INIT


def _attn_rows(q, k_all, v_ones, z, lam, g):
    lane = lax.broadcasted_iota(jnp.int32, (1, LANES), 1)
    zero = jnp.zeros_like(q)
    outs = []
    for qm in (jnp.where(lane < DK_A, q, zero), jnp.where(lane >= DK_A, q, zero)):
        s = _dot_nt(qm, k_all)
        mx = jnp.max(s, axis=-1, keepdims=True)
        p = jnp.exp2(s - mx).astype(BF16)
        ol = _dot(p, v_ones)
        outs.append(ol[:, :DV] * (1.0 / ol[:, DV:]))
    oa = outs[0] - lam * outs[1]
    oa = oa * lax.rsqrt(jnp.mean(oa * oa, axis=-1, keepdims=True) + LN_EPS)
    oa = oa * g * (1.0 - LAM_INIT)
    return oa * z.astype(F32)


def _attn_ctx_kernel(q_ref, k_ref, v_ref, z_ref, lp_ref, g_ref, o_ref):
    lam = _lam(lp_ref)
    g = g_ref[...]
    ones = jnp.ones((q_ref.shape[0], DV), BF16)
    for h in range(N_HEADS):
        sl = slice(h * LANES, (h + 1) * LANES)
        v_ones = jnp.concatenate([v_ref[:, sl], ones], axis=1)
        o_ref[:, sl] = _attn_rows(q_ref[:, sl], k_ref[:, sl], v_ones, z_ref[:, sl],
                                  lam, g).astype(BF16)


def _attn_ctx_call(p, lam_params, subln_g, *, batch, seq):
    blk = lambda j: pl.BlockSpec((seq, D_MODEL), lambda b: (b, j))
    return pl.pallas_call(
        _attn_ctx_kernel,
        grid=(batch,),
        in_specs=[blk(0), blk(1), blk(2), blk(3),
                  pl.BlockSpec((4, DK_A), lambda b: (0, 0)),
                  pl.BlockSpec((1, DV), lambda b: (0, 0))],
        out_specs=pl.BlockSpec((seq, D_MODEL), lambda b: (b, 0)),
        out_shape=jax.ShapeDtypeStruct((batch * seq, D_MODEL), BF16),
        compiler_params=_params(("arbitrary",)),
        name="attn_ctx",
    )(p, p, p, p, lam_params, subln_g)


def _attn_lat_kernel(q_ref, k_ref, v_ref, z_ref, kc_ref, vc_ref, lp_ref, g_ref, o_ref,
                     k_all, v_all, *, past, tc):
    @pl.when(pl.program_id(2) == 0)
    def _():
        head_rows = pl.ds(pl.program_id(1), past, stride=N_HEADS)
        k_all[0:past, :] = kc_ref[head_rows, :].astype(BF16)
        v_all[0:past, 0:DV] = vc_ref[head_rows, :].astype(BF16)
        k_all[past:, :] = k_ref[...]
        v_all[past:, 0:DV] = v_ref[...]
        v_all[:, DV:] = jnp.ones((v_all.shape[0], DV), BF16)

    lam = _lam(lp_ref)
    g = g_ref[...]

    tq = q_ref.shape[0]
    edges = list(range(0, tq + 1, tc // 2))
    for lo, hi in zip(edges[:-1], edges[1:]):
        o_ref[lo:hi, :] = _attn_rows(q_ref[lo:hi, :], k_all[...], v_all[...], z_ref[lo:hi, :],
                                     lam, g).astype(BF16)


def _attn_lat_call(p, cache_k, cache_v, lam_params, subln_g, *, batch, seq, past):
    tq, tc = 2048, 256
    nq = seq // tq
    return pl.pallas_call(
        functools.partial(_attn_lat_kernel, past=past, tc=tc),
        grid=(batch, N_HEADS, nq),
        in_specs=[
            pl.BlockSpec((tq, LANES), lambda b, h, i: (b * nq + i, QA_BLK + h)),
            pl.BlockSpec((seq, LANES), lambda b, h, i: (b, KA_BLK + h)),
            pl.BlockSpec((seq, LANES), lambda b, h, i: (b, VA_BLK + h)),
            pl.BlockSpec((tq, LANES), lambda b, h, i: (b * nq + i, ZA_BLK + h)),
            pl.BlockSpec((None, past * N_HEADS, LANES), lambda b, h, i: (b, 0, 0)),
            pl.BlockSpec((None, past * N_HEADS, LANES), lambda b, h, i: (b, 0, 0)),
            pl.BlockSpec((4, DK_A), lambda b, h, i: (0, 0)),
            pl.BlockSpec((1, DV), lambda b, h, i: (0, 0)),
        ],
        out_specs=pl.BlockSpec((tq, LANES), lambda b, h, i: (b * nq + i, h)),
        out_shape=jax.ShapeDtypeStruct((batch * seq, D_MODEL), BF16),
        scratch_shapes=[pltpu.VMEM((past + seq, LANES), BF16),
                        pltpu.VMEM((past + seq, 2 * DV), BF16)],
        compiler_params=_params(("arbitrary", "arbitrary", "arbitrary")),
        name="attn_lat",
    )(p, p, p, p, cache_k, cache_v, lam_params, subln_g)


def _ret_kernel(*refs, latent, nc, npairs):
    if latent:
        (rd_ref, q_ref, k_ref, v_ref, z_ref, g_ref, s0f_ref, s0b_ref,
         o_ref, u_scr, s_scr, dec_scr, qdec_scr, kdec_scr) = refs
    else:
        (rd_ref, q_ref, k_ref, v_ref, z_ref, g_ref,
         o_ref, sfo_ref, sbo_ref, u_scr, s_scr, dec_scr, qdec_scr, kdec_scr) = refs
    cross = latent or nc > 1
    pair0 = pl.program_id(0) * npairs
    first_seq = pl.program_id(1) == 0
    pw, vw = 2 * DK_R, 2 * DV

    lane_q = lax.broadcasted_iota(jnp.int32, (1, pw), 1)
    lane_v = lax.broadcasted_iota(jnp.int32, (1, vw), 1)
    rel = (lax.broadcasted_iota(jnp.int32, (CHUNK, CHUNK), 0)
           - lax.broadcasted_iota(jnp.int32, (CHUNK, CHUNK), 1)).astype(F32)
    row_q = lax.broadcasted_iota(jnp.int32, (CHUNK, pw), 0).astype(F32)
    srow = lax.broadcasted_iota(jnp.int32, (2 * pw, vw), 0)
    scol = lax.broadcasted_iota(jnp.int32, (2 * pw, vw), 1)
    diag_blocks = ((srow % pw) // DK_R) == (scol // DV)

    def rows(c):
        return slice(c * CHUNK, (c + 1) * CHUNK)

    def log_gammas(pi):
        def one(d, hh):
            r = jnp.full((1, vw), rd_ref[d, 2 * (pair0 + pi) + hh], F32)
            return jnp.log1p(-jnp.exp2(r))
        return [[one(d, hh) for hh in range(2)] for d in range(2)]

    @pl.when(first_seq)
    def _():
        for pi in range(npairs):
            lg = log_gammas(pi)
            lg_q = [jnp.where(lane_q < DK_R, lg[d][0][:, :pw], lg[d][1][:, :pw])
                    for d in range(2)]
            for hh in range(2):
                dec_scr[pi, hh] = (
                    jnp.where(rel >= 0, jnp.exp(lg[0][hh][:, :1] * jnp.maximum(rel, 0.0)), 0.0)
                    + jnp.where(rel <= 0, jnp.exp(lg[1][hh][:, :1] * jnp.maximum(-rel, 0.0)), 0.0))
            qdec_scr[pi] = jnp.concatenate([jnp.exp(lg_q[0] * (row_q + 1.0)),
                                            jnp.exp(lg_q[1] * (CHUNK - row_q))], axis=1)
            kdec_scr[pi] = jnp.concatenate([jnp.exp(lg_q[0] * (CHUNK - 1.0 - row_q)),
                                            jnp.exp(lg_q[1] * row_q)], axis=1)

    for pi in range(npairs):
        qs = slice(pi * pw, (pi + 1) * pw)
        vs = slice(pi * vw, (pi + 1) * vw)
        lg = log_gammas(pi)
        lg_v = [jnp.where(lane_v < DV, lg[d][0], lg[d][1]) for d in range(2)]
        cd_f = jnp.exp(lg_v[0] * CHUNK)
        cd_b = jnp.exp(lg_v[1] * CHUNK)

        for c in range(nc):
            k = k_ref[rows(c), qs].astype(F32)
            kd = (jnp.concatenate([k, k], axis=1) * kdec_scr[pi]).astype(BF16)
            u_scr[pi, c] = jnp.where(diag_blocks, _dot_tn(kd, v_ref[rows(c), vs]), 0.0)

        def init_state(s0_ref):
            if not latent:
                return jnp.zeros((pw, vw), F32)
            z = jnp.zeros((DK_R, DV), F32)
            return jnp.concatenate(
                [jnp.concatenate([s0_ref[2 * pi], z], axis=1),
                 jnp.concatenate([z, s0_ref[2 * pi + 1]], axis=1)], axis=0)

        s = init_state(s0f_ref if latent else None)
        for c in range(nc):
            if cross:
                s_scr[pi, c, 0:pw, :] = s.astype(BF16)
            s = s * cd_f + u_scr[pi, c, 0:pw, :]
        if not latent:
            sfo_ref[2 * pi] = s[:DK_R, :DV]
            sfo_ref[2 * pi + 1] = s[DK_R:, DV:]
        s = init_state(s0b_ref if latent else None)
        for c in reversed(range(nc)):
            if cross:
                s_scr[pi, c, pw:2 * pw, :] = s.astype(BF16)
            s = s * cd_b + u_scr[pi, c, pw:2 * pw, :]
        if not latent:
            sbo_ref[2 * pi] = s[:DK_R, :DV]
            sbo_ref[2 * pi + 1] = s[DK_R:, DV:]

        g = g_ref[:, vs]
        for c in range(nc):
            q = q_ref[rows(c), qs]
            k = k_ref[rows(c), qs]
            v = v_ref[rows(c), vs]
            zero = jnp.zeros_like(q)
            q2 = jnp.concatenate([jnp.where(lane_q < DK_R, q, zero),
                                  jnp.where(lane_q >= DK_R, q, zero)], axis=0)
            a2 = _dot_nt(q2, k)
            o = jnp.concatenate(
                [_dot((a2[hh * CHUNK:(hh + 1) * CHUNK] * dec_scr[pi, hh]).astype(BF16),
                      v[:, hh * DV:(hh + 1) * DV]) for hh in range(2)], axis=1)
            if cross:
                qf = q.astype(F32)
                qq = (jnp.concatenate([qf, qf], axis=1) * qdec_scr[pi]).astype(BF16)
                o = o + _dot(qq, s_scr[pi, c])
            outs = []
            for hh in range(2):
                oh = o[:, hh * DV:(hh + 1) * DV]
                mu = jnp.mean(oh, axis=-1, keepdims=True)
                oc = oh - mu
                var = jnp.mean(oc * oc, axis=-1, keepdims=True)
                outs.append(oc * lax.rsqrt(var + LN_EPS))
            o = jnp.concatenate(outs, axis=1) * g * z_ref[rows(c), vs].astype(F32)
            o_ref[rows(c), vs] = o.astype(BF16)


def _ret_call(p, ret_decay, gn_g, s0f, s0b, *, latent, batch, seq):
    nc = seq // CHUNK
    hps = 2 if latent else N_HEADS
    npairs = hps // 2
    qw, vw = DK_R * hps, DV * hps
    col = lambda w, off: pl.BlockSpec((seq, w), lambda j, b: (b, off // w + j))
    state = pl.BlockSpec((None, hps, DK_R, DV), lambda j, b: (b, j, 0, 0))
    in_specs = [pl.BlockSpec(memory_space=pltpu.SMEM),
                col(qw, QR_BLK * LANES), col(qw, KR_BLK * LANES),
                col(vw, VR_BLK * LANES), col(vw, ZR_BLK * LANES),
                pl.BlockSpec((1, vw), lambda j, b: (0, j))]
    args = [ret_decay, p, p, p, p, gn_g]
    out_shape = [jax.ShapeDtypeStruct((batch * seq, D_MODEL), BF16)]
    out_specs = [pl.BlockSpec((seq, vw), lambda j, b: (b, j))]
    if latent:
        in_specs += [state, state]
        args += [s0f, s0b]
    else:
        out_shape += [jax.ShapeDtypeStruct((batch, N_HEADS, DK_R, DV), F32)] * 2
        out_specs += [state, state]
    stacked_state = (npairs, nc, 4 * DK_R, 2 * DV)
    return pl.pallas_call(
        functools.partial(_ret_kernel, latent=latent, nc=nc, npairs=npairs),
        grid=(N_HEADS // hps, batch),
        in_specs=in_specs,
        out_specs=out_specs,
        out_shape=out_shape,
        scratch_shapes=[pltpu.VMEM(stacked_state, F32), pltpu.VMEM(stacked_state, BF16),
                        pltpu.VMEM((npairs, 2, CHUNK, CHUNK), F32),
                        pltpu.VMEM((npairs, CHUNK, 4 * DK_R), F32),
                        pltpu.VMEM((npairs, CHUNK, 4 * DK_R), F32)],
        compiler_params=_params(("arbitrary", "arbitrary")),
        name="ret_lat" if latent else "ret_ctx",
    )(*args)


def _out_kernel(oa_ref, or_ref, ga_ref, gr_ref, x_ref, mod_ref, wpa_ref, wpr_ref, wo_ref,
                lng_ref, lnb_ref, y_ref, *, sub):
    gate = mod_ref[0, :, 2 * D_MODEL:3 * D_MODEL]
    for s in range(y_ref.shape[0] // sub):
        rows = slice(s * sub, (s + 1) * sub)
        a = _dot(oa_ref[rows, :], wpa_ref[...])
        r = _dot(or_ref[rows, :], wpr_ref[...])
        m = ga_ref[rows, :].astype(F32) * a + gr_ref[rows, :].astype(F32) * r
        out = _dot(m.astype(BF16), wo_ref[...])
        t = DEEPNORM_ALPHA * x_ref[rows, :] + gate * out
        mu = jnp.mean(t, axis=-1, keepdims=True)
        tc = t - mu
        var = jnp.mean(tc * tc, axis=-1, keepdims=True)
        y_ref[rows, :] = tc * lax.rsqrt(var + LN_EPS) * lng_ref[...] + lnb_ref[...]


def _out_call(oa, orr, p, x2d, mod3, w_pa, w_pr, w_out, ln_g, ln_b, *, latent, seq, mod_row0):
    m = x2d.shape[0]
    tm, sub = 1024, 512
    per_seq = max(seq // tm, 1)

    def row(i):
        return (mod_row0 + i // per_seq) if latent else mod_row0

    tile = lambda j: pl.BlockSpec((tm, D_MODEL), lambda i: (i, j))
    full = lambda shape: pl.BlockSpec(shape, lambda i: tuple(0 for _ in shape))
    return pl.pallas_call(
        functools.partial(_out_kernel, sub=sub),
        grid=(m // tm,),
        in_specs=[tile(0), tile(0), tile(7), tile(8), tile(0),
                  pl.BlockSpec((1, 1, 3 * D_MODEL), lambda i: (row(i), 0, 0)),
                  full((D_MODEL, D_MODEL)), full((D_MODEL, D_MODEL)), full((D_MODEL, D_MODEL)),
                  full((1, D_MODEL)), full((1, D_MODEL))],
        out_specs=tile(0),
        out_shape=jax.ShapeDtypeStruct((m, D_MODEL), F32),
        compiler_params=_params(("arbitrary",)),
        name="out_lat" if latent else "out_ctx",
    )(oa, orr, p, p, x2d, mod3, w_pa, w_pr, w_out, ln_g, ln_b)


def _rope_tables(n_tokens):
    rows = n_tokens // GRID_W
    r = np.repeat(np.arange(rows, dtype=np.float32), GRID_W)
    col = np.tile(np.arange(GRID_W, dtype=np.float32), rows)
    n_freq = DK_A // 4
    inv = np.float32(ROPE_BASE) ** (-np.arange(n_freq, dtype=np.float32) / np.float32(n_freq))
    ang = np.concatenate([r[:, None] * inv, col[:, None] * inv], axis=-1).astype(np.float32)
    cos = np.repeat(np.cos(ang), 2, axis=-1)
    sin = np.repeat(np.sin(ang), 2, axis=-1)
    even = (np.arange(DK_A) % 2 == 0)[None, :]
    sin_even = np.where(even, -sin, 0.0)
    sin_odd = np.where(even, 0.0, sin)
    two = lambda t: jnp.asarray(np.concatenate([t, t], axis=-1), F32)
    return two(cos), two(sin_even), two(sin_odd)


def kernel(x_prompt, x_sample, cache_attn_k, cache_attn_v, state_ret_fwd, state_ret_bwd,
           c, c_ctx, w_mod, b_mod, w_in, lam_params, subln_g, ret_decay, ret_gn_g,
           w_pa, w_pr, w_gate, b_gate, w_out, ln_g, ln_b):
    batch, seq, _ = x_prompt.shape
    dbatch, dseq, _ = x_sample.shape
    past = cache_attn_k.shape[2]
    l = 0

    cond8 = jnp.concatenate([c_ctx[None, :], c, jnp.zeros((8 - 1 - dbatch, D_MODEL), F32)], axis=0)
    mod3 = _mod_call(cond8, w_mod[l], b_mod[l][None, :]).reshape(8, 1, 3 * D_MODEL)

    win, wg = w_in[l].astype(BF16), w_gate[l].astype(BF16)
    bg = b_gate[l][None, :]
    wpa, wpr, wo = w_pa[l].astype(BF16), w_pr[l].astype(BF16), w_out[l].astype(BF16)
    lp, sg = lam_params[l], subln_g[l][None, :]
    rd, gg = ret_decay[l], ret_gn_g[l][None, :]
    lng, lnb = ln_g[l][None, :], ln_b[l][None, :]

    xc = x_prompt.reshape(batch * seq, D_MODEL)
    p_c, k_c, v_c = _proj_call(xc, mod3, win, wg, bg, None, latent=False, seq=seq, mod_row0=0)
    oa_c = _attn_ctx_call(p_c, lp, sg, batch=batch, seq=seq)
    or_c, sf_c, sb_c = _ret_call(p_c, rd, gg, None, None, latent=False, batch=batch, seq=seq)
    y_c = _out_call(oa_c, or_c, p_c, xc, mod3, wpa, wpr, wo, lng, lnb,
                    latent=False, seq=seq, mod_row0=0)

    xs = x_sample.reshape(dbatch * dseq, D_MODEL)
    (p_s,) = _proj_call(xs, mod3, win, wg, bg, _rope_tables(dseq), latent=True, seq=dseq, mod_row0=1)
    ck = cache_attn_k[:, l].reshape(dbatch, past * N_HEADS, 2 * DK_A)
    cv = cache_attn_v[:, l].reshape(dbatch, past * N_HEADS, DV)
    oa_s = _attn_lat_call(p_s, ck, cv, lp, sg, batch=dbatch, seq=dseq, past=past)
    (or_s,) = _ret_call(p_s, rd, gg, state_ret_fwd[:, l], state_ret_bwd[:, l],
                        latent=True, batch=dbatch, seq=dseq)
    y_s = _out_call(oa_s, or_s, p_s, xs, mod3, wpa, wpr, wo, lng, lnb,
                    latent=True, seq=dseq, mod_row0=1)

    return (y_c.reshape(batch, seq, D_MODEL),
            y_s.reshape(dbatch, dseq, D_MODEL),
            k_c.reshape(batch, 1, seq, N_HEADS, 2 * DK_A),
            v_c.reshape(batch, 1, seq, N_HEADS, DV),
            sf_c.reshape(batch, 1, N_HEADS, DK_R, DV),
            sb_c.reshape(batch, 1, N_HEADS, DK_R, DV))
```

```python
import functools
import math

import jax
import jax.numpy as jnp
import numpy as np
from jax import lax
from jax.experimental import pallas as pl
from jax.experimental.pallas import tpu as pltpu

F32 = jnp.float32
BF16 = jnp.bfloat16

D_MODEL = 1024
N_HEADS = 8
DK_A = 64
DV = 128
DK_R = 64
CHUNK = 256
GRID_W = 64
ROPE_BASE = 10000.0
MOD_EPS = 1e-6
LN_EPS = 1e-5
DEPTH = 1
DEEPNORM_ALPHA = (2.0 * DEPTH) ** 0.25
LAM_INIT = 0.8 - 0.6 * math.exp(-0.3 * 0)
LOG2E = math.log2(math.e)

LANES = 128
QA_BLK, KA_BLK, VA_BLK, ZA_BLK = 0, 8, 16, 24
QR_BLK, KR_BLK, VR_BLK, ZR_BLK = 32, 36, 40, 48
GA_BLK, GR_BLK = 56, 64
IN_SEGS = 7
P_WIDTH = (IN_SEGS + 2) * D_MODEL
VMEM_LIMIT = 56 * 1024 * 1024


def _params(sem):
    return pltpu.CompilerParams(dimension_semantics=sem, vmem_limit_bytes=VMEM_LIMIT)


def _resident(shape):
    return pl.BlockSpec(shape, lambda *_: tuple(0 for _ in shape), pipeline_mode=pl.Buffered(1))


def _silu(z):
    return z * (1.0 / (1.0 + jnp.exp(-z)))


def _dot(a, b):
    return jnp.dot(a, b, preferred_element_type=F32)


def _dot_nt(a, b):
    return lax.dot_general(a, b, (((1,), (1,)), ((), ())), preferred_element_type=F32)


def _dot_tn(a, b):
    return lax.dot_general(a, b, (((0,), (0,)), ((), ())), preferred_element_type=F32)


def _mod_kernel(cond_ref, w_ref, b_ref, o_ref):
    s = _silu(cond_ref[...])
    o_ref[...] = _dot(s.astype(BF16), w_ref[...].astype(BF16)) + b_ref[...]


def _mod_call(cond8, w_mod, b_mod):
    tn = D_MODEL
    return pl.pallas_call(
        _mod_kernel,
        grid=(3 * D_MODEL // tn,),
        in_specs=[pl.BlockSpec((8, D_MODEL), lambda j: (0, 0)),
                  pl.BlockSpec((D_MODEL, tn), lambda j: (0, j)),
                  pl.BlockSpec((1, tn), lambda j: (0, j))],
        out_specs=pl.BlockSpec((8, tn), lambda j: (0, j)),
        out_shape=jax.ShapeDtypeStruct((8, 3 * D_MODEL), F32),
        compiler_params=_params(("arbitrary",)),
        name="mod",
    )(cond8, w_mod, b_mod)


def _rope(acc, cos, sin_even, sin_odd):
    outs = []
    for hh in range(N_HEADS):
        xs = acc[:, hh * LANES:(hh + 1) * LANES]
        nxt = pltpu.roll(xs, LANES - 1, 1)
        prv = pltpu.roll(xs, 1, 1)
        outs.append(xs * cos + nxt * sin_even + prv * sin_odd)
    return jnp.concatenate(outs, axis=1)


def _modulated_ln(x_ref, mod_ref):
    x = x_ref[...]
    mu = jnp.mean(x, axis=-1, keepdims=True)
    xc = x - mu
    var = jnp.mean(xc * xc, axis=-1, keepdims=True)
    shift = mod_ref[0, :, 0:D_MODEL]
    scale = mod_ref[0, :, D_MODEL:2 * D_MODEL]
    return (xc * lax.rsqrt(var + MOD_EPS) * (1.0 + scale) + shift).astype(BF16)


def _project(h, win_ref, wg_ref, bg_ref, p_ref, rope=None, kv_out=None):
    def seg(j):
        if j < IN_SEGS:
            return _dot(h, win_ref[:, j * D_MODEL:(j + 1) * D_MODEL])
        return _dot(h, wg_ref[:, (j - IN_SEGS) * D_MODEL:(j - IN_SEGS + 1) * D_MODEL])

    def put(j, val):
        p_ref[:, j * D_MODEL:(j + 1) * D_MODEL] = val.astype(BF16)

    qa = seg(0)
    ka = seg(1)
    if rope is not None:
        cos, se, so = (t[...] for t in rope)
        qa = _rope(qa, cos, se, so)
        ka = _rope(ka, cos, se, so)
    if kv_out is not None:
        kv_out[0][...] = ka
    put(0, qa * (DK_A ** -0.5 * LOG2E))
    put(1, ka)
    va = seg(2)
    if kv_out is not None:
        kv_out[1][...] = va
    put(2, va)
    put(3, _silu(seg(3)))
    qk = seg(4)
    put(4, jnp.concatenate([qk[:, :D_MODEL // 2], qk[:, D_MODEL // 2:] * (DK_R ** -0.5)], axis=1))
    put(5, seg(5))
    put(6, _silu(seg(6)))
    for j in (IN_SEGS, IN_SEGS + 1):
        g = seg(j) + bg_ref[:, (j - IN_SEGS) * D_MODEL:(j - IN_SEGS + 1) * D_MODEL]
        put(j, 1.0 / (1.0 + jnp.exp(-g)))


def _proj_lat_kernel(xn_ref, modn_ref, x0_ref, mod0_ref, win_ref, wg_ref, bg_ref,
                     cos_ref, se_ref, so_ref, p_ref, h_even, h_odd):
    i = pl.program_id(0)

    def body(h_cur, h_nxt):
        h_nxt[...] = _modulated_ln(xn_ref, modn_ref)
        _project(h_cur[...], win_ref, wg_ref, bg_ref, p_ref, rope=(cos_ref, se_ref, so_ref))

    @pl.when(i == 0)
    def _():
        h_even[...] = _modulated_ln(x0_ref, mod0_ref)

    @pl.when(i % 2 == 0)
    def _():
        body(h_even, h_odd)

    @pl.when(i % 2 == 1)
    def _():
        body(h_odd, h_even)


def _proj_lat_call(x2d, mod3, w_in, w_gate, b_gate, rope, *, seq, mod_row0):
    m = x2d.shape[0]
    tm = 256
    nt = m // tm
    per_seq = seq // tm
    row = lambda i: mod_row0 + i // per_seq
    nxt = lambda i: jnp.minimum(i + 1, nt - 1)
    return pl.pallas_call(
        _proj_lat_kernel,
        grid=(nt,),
        in_specs=[
            pl.BlockSpec((tm, D_MODEL), lambda i: (nxt(i), 0)),
            pl.BlockSpec((1, 1, 3 * D_MODEL), lambda i: (row(nxt(i)), 0, 0)),
            pl.BlockSpec((tm, D_MODEL), lambda i: (0, 0)),
            pl.BlockSpec((1, 1, 3 * D_MODEL), lambda i: (row(0), 0, 0)),
            _resident((D_MODEL, IN_SEGS * D_MODEL)),
            _resident((D_MODEL, 2 * D_MODEL)),
            pl.BlockSpec((1, 2 * D_MODEL), lambda i: (0, 0)),
        ] + [pl.BlockSpec((tm, LANES), lambda i: (i % per_seq, 0))] * 3,
        out_specs=pl.BlockSpec((tm, P_WIDTH), lambda i: (i, 0)),
        out_shape=jax.ShapeDtypeStruct((m, P_WIDTH), BF16),
        scratch_shapes=[pltpu.VMEM((tm, D_MODEL), BF16), pltpu.VMEM((tm, D_MODEL), BF16)],
        compiler_params=_params(("arbitrary",)),
        name="proj_lat",
    )(x2d, mod3, x2d, mod3, w_in, w_gate, b_gate, *rope)


def _lam(lp_ref):
    lp = lp_ref[...]
    a = jnp.sum(lp[0:1] * lp[1:2], axis=-1, keepdims=True)
    b = jnp.sum(lp[2:3] * lp[3:4], axis=-1, keepdims=True)
    return jnp.exp(a) - jnp.exp(b) + LAM_INIT


def _attn_rows(q, k_all, v_ones, z, lam, g):
    lane = lax.broadcasted_iota(jnp.int32, (1, LANES), 1)
    zero = jnp.zeros_like(q)
    outs = []
    for qm in (jnp.where(lane < DK_A, q, zero), jnp.where(lane >= DK_A, q, zero)):
        s = _dot_nt(qm, k_all)
        mx = jnp.max(s, axis=-1, keepdims=True)
        p = jnp.exp2(s - mx).astype(BF16)
        ol = _dot(p, v_ones)
        outs.append(ol[:, :DV] * (1.0 / ol[:, DV:]))
    oa = outs[0] - lam * outs[1]
    oa = oa * lax.rsqrt(jnp.mean(oa * oa, axis=-1, keepdims=True) + LN_EPS)
    oa = oa * g * (1.0 - LAM_INIT)
    return oa * z.astype(F32)


def _attn_ctx_heads(q_ref, k_ref, v_ref, z_ref, lam, g, o_ref):
    ones = jnp.ones((q_ref.shape[0], DV), BF16)
    for h in range(N_HEADS):
        sl = slice(h * LANES, (h + 1) * LANES)
        v_ones = jnp.concatenate([v_ref[:, sl], ones], axis=1)
        o_ref[:, sl] = _attn_rows(q_ref[:, sl], k_ref[:, sl], v_ones, z_ref[:, sl],
                                  lam, g).astype(BF16)


def _attn_lat_kernel(q_ref, k_ref, v_ref, z_ref, kc_ref, vc_ref, lp_ref, g_ref, o_ref,
                     k_all, v_all, *, past, tc):
    @pl.when(pl.program_id(2) == 0)
    def _():
        head_rows = pl.ds(pl.program_id(1), past, stride=N_HEADS)
        k_all[0:past, :] = kc_ref[head_rows, :].astype(BF16)
        v_all[0:past, 0:DV] = vc_ref[head_rows, :].astype(BF16)
        k_all[past:, :] = k_ref[...]
        v_all[past:, 0:DV] = v_ref[...]
        v_all[:, DV:] = jnp.ones((v_all.shape[0], DV), BF16)

    lam = _lam(lp_ref)
    g = g_ref[...]
    for lo in range(0, q_ref.shape[0], tc):
        o_ref[lo:lo + tc, :] = _attn_rows(q_ref[lo:lo + tc, :], k_all[...], v_all[...],
                                          z_ref[lo:lo + tc, :], lam, g).astype(BF16)


def _attn_lat_call(p, cache_k, cache_v, lam_params, subln_g, *, batch, seq, past):
    tq, tc = 2048, 128
    nq = seq // tq
    return pl.pallas_call(
        functools.partial(_attn_lat_kernel, past=past, tc=tc),
        grid=(batch, N_HEADS, nq),
        in_specs=[
            pl.BlockSpec((tq, LANES), lambda b, h, i: (b * nq + i, QA_BLK + h)),
            pl.BlockSpec((seq, LANES), lambda b, h, i: (b, KA_BLK + h)),
            pl.BlockSpec((seq, LANES), lambda b, h, i: (b, VA_BLK + h)),
            pl.BlockSpec((tq, LANES), lambda b, h, i: (b * nq + i, ZA_BLK + h)),
            pl.BlockSpec((None, past * N_HEADS, LANES), lambda b, h, i: (b, 0, 0)),
            pl.BlockSpec((None, past * N_HEADS, LANES), lambda b, h, i: (b, 0, 0)),
            pl.BlockSpec((4, DK_A), lambda b, h, i: (0, 0)),
            pl.BlockSpec((1, DV), lambda b, h, i: (0, 0)),
        ],
        out_specs=pl.BlockSpec((tq, LANES), lambda b, h, i: (b * nq + i, h)),
        out_shape=jax.ShapeDtypeStruct((batch * seq, D_MODEL), BF16),
        scratch_shapes=[pltpu.VMEM((past + seq, LANES), BF16),
                        pltpu.VMEM((past + seq, 2 * DV), BF16)],
        compiler_params=_params(("arbitrary", "arbitrary", "arbitrary")),
        name="attn_lat",
    )(p, p, p, p, cache_k, cache_v, lam_params, subln_g)


PW, VW = 2 * DK_R, 2 * DV


def _ret_scratch(npairs, nc):
    stacked_state = (npairs, nc, 2 * PW, VW)
    return [pltpu.VMEM(stacked_state, F32),
            pltpu.VMEM(stacked_state, BF16),
            pltpu.VMEM((npairs, 2, CHUNK, CHUNK), F32),
            pltpu.VMEM((npairs, CHUNK, 2 * PW), F32),
            pltpu.VMEM((npairs, CHUNK, 2 * PW), F32)]


def _log_gammas(rd_ref, pair):
    def one(d, hh):
        r = jnp.full((1, VW), rd_ref[d, 2 * pair + hh], F32)
        return jnp.log1p(-jnp.exp2(r))
    return [[one(d, hh) for hh in range(2)] for d in range(2)]


def _ret_build_tables(rd_ref, dec_scr, qdec_scr, kdec_scr, *, npairs, pair0):
    lane_q = lax.broadcasted_iota(jnp.int32, (1, PW), 1)
    rel = (lax.broadcasted_iota(jnp.int32, (CHUNK, CHUNK), 0)
           - lax.broadcasted_iota(jnp.int32, (CHUNK, CHUNK), 1)).astype(F32)
    row_q = lax.broadcasted_iota(jnp.int32, (CHUNK, PW), 0).astype(F32)
    for pi in range(npairs):
        lg = _log_gammas(rd_ref, pair0 + pi)
        lg_q = [jnp.where(lane_q < DK_R, lg[d][0][:, :PW], lg[d][1][:, :PW])
                for d in range(2)]
        for hh in range(2):
            dec_scr[pi, hh] = (
                jnp.where(rel >= 0, jnp.exp(lg[0][hh][:, :1] * jnp.maximum(rel, 0.0)), 0.0)
                + jnp.where(rel <= 0, jnp.exp(lg[1][hh][:, :1] * jnp.maximum(-rel, 0.0)), 0.0))
        qdec_scr[pi] = jnp.concatenate([jnp.exp(lg_q[0] * (row_q + 1.0)),
                                        jnp.exp(lg_q[1] * (CHUNK - row_q))], axis=1)
        kdec_scr[pi] = jnp.concatenate([jnp.exp(lg_q[0] * (CHUNK - 1.0 - row_q)),
                                        jnp.exp(lg_q[1] * row_q)], axis=1)


def _ret_pairs(rd_ref, q_ref, k_ref, v_ref, z_ref, g_ref, s0_refs, o_ref, sout_refs,
               u_scr, s_scr, dec_scr, qdec_scr, kdec_scr, *, nc, npairs, pair0):
    cross = s0_refs is not None or nc > 1
    lane_q = lax.broadcasted_iota(jnp.int32, (1, PW), 1)
    lane_v = lax.broadcasted_iota(jnp.int32, (1, VW), 1)
    srow = lax.broadcasted_iota(jnp.int32, (2 * PW, VW), 0)
    scol = lax.broadcasted_iota(jnp.int32, (2 * PW, VW), 1)
    diag_blocks = ((srow % PW) // DK_R) == (scol // DV)

    def rows(c):
        return slice(c * CHUNK, (c + 1) * CHUNK)

    for pi in range(npairs):
        qs = slice(pi * PW, (pi + 1) * PW)
        vs = slice(pi * VW, (pi + 1) * VW)
        lg = _log_gammas(rd_ref, pair0 + pi)
        lg_v = [jnp.where(lane_v < DV, lg[d][0], lg[d][1]) for d in range(2)]
        chunk_decay = [jnp.exp(lg_v[d] * CHUNK) for d in range(2)]

        for c in range(nc):
            k = k_ref[rows(c), qs].astype(F32)
            kd = (jnp.concatenate([k, k], axis=1) * kdec_scr[pi]).astype(BF16)
            u_scr[pi, c] = jnp.where(diag_blocks, _dot_tn(kd, v_ref[rows(c), vs]), 0.0)

        for d, order in ((0, range(nc)), (1, reversed(range(nc)))):
            half = slice(d * PW, (d + 1) * PW)
            if s0_refs is None:
                s = jnp.zeros((PW, VW), F32)
            else:
                zero = jnp.zeros((DK_R, DV), F32)
                s = jnp.concatenate(
                    [jnp.concatenate([s0_refs[d][2 * pi], zero], axis=1),
                     jnp.concatenate([zero, s0_refs[d][2 * pi + 1]], axis=1)], axis=0)
            for c in order:
                if cross:
                    s_scr[pi, c, half, :] = s.astype(BF16)
                s = s * chunk_decay[d] + u_scr[pi, c, half, :]
            if sout_refs is not None:
                sout_refs[d][2 * pi] = s[:DK_R, :DV]
                sout_refs[d][2 * pi + 1] = s[DK_R:, DV:]

        g = g_ref[:, vs]
        for c in range(nc):
            q = q_ref[rows(c), qs]
            k = k_ref[rows(c), qs]
            v = v_ref[rows(c), vs]
            zero = jnp.zeros_like(q)
            q2 = jnp.concatenate([jnp.where(lane_q < DK_R, q, zero),
                                  jnp.where(lane_q >= DK_R, q, zero)], axis=0)
            a2 = _dot_nt(q2, k)
            o = jnp.concatenate(
                [_dot((a2[hh * CHUNK:(hh + 1) * CHUNK] * dec_scr[pi, hh]).astype(BF16),
                      v[:, hh * DV:(hh + 1) * DV]) for hh in range(2)], axis=1)
            if cross:
                qf = q.astype(F32)
                qq = (jnp.concatenate([qf, qf], axis=1) * qdec_scr[pi]).astype(BF16)
                o = o + _dot(qq, s_scr[pi, c])
            outs = []
            for hh in range(2):
                oh = o[:, hh * DV:(hh + 1) * DV]
                mu = jnp.mean(oh, axis=-1, keepdims=True)
                oc = oh - mu
                var = jnp.mean(oc * oc, axis=-1, keepdims=True)
                outs.append(oc * lax.rsqrt(var + LN_EPS))
            o = jnp.concatenate(outs, axis=1) * g * z_ref[rows(c), vs].astype(F32)
            o_ref[rows(c), vs] = o.astype(BF16)


def _ret_lat_kernel(rd_ref, q_ref, k_ref, v_ref, z_ref, g_ref, s0f_ref, s0b_ref, o_ref,
                    u_scr, s_scr, dec_scr, qdec_scr, kdec_scr, *, nc, npairs):
    pair0 = pl.program_id(0) * npairs

    @pl.when(pl.program_id(1) == 0)
    def _():
        _ret_build_tables(rd_ref, dec_scr, qdec_scr, kdec_scr, npairs=npairs, pair0=pair0)

    _ret_pairs(rd_ref, q_ref, k_ref, v_ref, z_ref, g_ref, (s0f_ref, s0b_ref), o_ref, None,
               u_scr, s_scr, dec_scr, qdec_scr, kdec_scr, nc=nc, npairs=npairs, pair0=pair0)


def _ret_lat_call(p, ret_decay, gn_g, s0f, s0b, *, batch, seq):
    nc = seq // CHUNK
    hps = 2
    npairs = hps // 2
    qw, vw = DK_R * hps, DV * hps
    col = lambda w, blk: pl.BlockSpec((seq, w), lambda j, b: (b, blk * LANES // w + j))
    state = pl.BlockSpec((None, hps, DK_R, DV), lambda j, b: (b, j, 0, 0))
    return pl.pallas_call(
        functools.partial(_ret_lat_kernel, nc=nc, npairs=npairs),
        grid=(N_HEADS // hps, batch),
        in_specs=[pl.BlockSpec(memory_space=pltpu.SMEM),
                  col(qw, QR_BLK), col(qw, KR_BLK), col(vw, VR_BLK), col(vw, ZR_BLK),
                  pl.BlockSpec((1, vw), lambda j, b: (0, j)), state, state],
        out_specs=pl.BlockSpec((seq, vw), lambda j, b: (b, j)),
        out_shape=jax.ShapeDtypeStruct((batch * seq, D_MODEL), BF16),
        scratch_shapes=_ret_scratch(npairs, nc),
        compiler_params=_params(("arbitrary", "arbitrary")),
        name="ret_lat",
    )(ret_decay, p, p, p, p, gn_g, s0f, s0b)


def _out_rows(oa_ref, or_ref, ga_ref, gr_ref, x_ref, gate, wpa_ref, wpr_ref, wo_ref,
              lng_ref, lnb_ref, y_ref, *, sub):
    for s in range(y_ref.shape[0] // sub):
        rows = slice(s * sub, (s + 1) * sub)
        a = _dot(oa_ref[rows, :], wpa_ref[...])
        r = _dot(or_ref[rows, :], wpr_ref[...])
        m = ga_ref[rows, :].astype(F32) * a + gr_ref[rows, :].astype(F32) * r
        out = _dot(m.astype(BF16), wo_ref[...])
        t = DEEPNORM_ALPHA * x_ref[rows, :] + gate * out
        mu = jnp.mean(t, axis=-1, keepdims=True)
        tc = t - mu
        var = jnp.mean(tc * tc, axis=-1, keepdims=True)
        y_ref[rows, :] = tc * lax.rsqrt(var + LN_EPS) * lng_ref[...] + lnb_ref[...]


def _out_lat_kernel(oa_ref, or_ref, ga_ref, gr_ref, x_ref, mod_ref, wpa_ref, wpr_ref, wo_ref,
                    lng_ref, lnb_ref, y_ref, *, sub):
    gate = mod_ref[0, :, 2 * D_MODEL:3 * D_MODEL]
    _out_rows(oa_ref, or_ref, ga_ref, gr_ref, x_ref, gate, wpa_ref, wpr_ref, wo_ref,
              lng_ref, lnb_ref, y_ref, sub=sub)


def _out_lat_call(oa, orr, p, x2d, mod3, w_pa, w_pr, w_out, ln_g, ln_b, *, seq, mod_row0):
    m = x2d.shape[0]
    tm, sub = 1024, 512
    per_seq = seq // tm
    tile = lambda j: pl.BlockSpec((tm, D_MODEL), lambda i: (i, j))
    vec = pl.BlockSpec((1, D_MODEL), lambda i: (0, 0))
    return pl.pallas_call(
        functools.partial(_out_lat_kernel, sub=sub),
        grid=(m // tm,),
        in_specs=[tile(0), tile(0), tile(GA_BLK * LANES // D_MODEL), tile(GR_BLK * LANES // D_MODEL),
                  tile(0),
                  pl.BlockSpec((1, 1, 3 * D_MODEL), lambda i: (mod_row0 + i // per_seq, 0, 0)),
                  _resident((D_MODEL, D_MODEL)), _resident((D_MODEL, D_MODEL)),
                  _resident((D_MODEL, D_MODEL)), vec, vec],
        out_specs=tile(0),
        out_shape=jax.ShapeDtypeStruct((m, D_MODEL), F32),
        compiler_params=_params(("arbitrary",)),
        name="out_lat",
    )(oa, orr, p, p, x2d, mod3, w_pa, w_pr, w_out, ln_g, ln_b)


def _ctx_kernel(x_ref, mod_ref, win_ref, wg_ref, bg_ref, lp_ref, sg_ref, rd_ref, gg_ref,
                wpa_ref, wpr_ref, wo_ref, lng_ref, lnb_ref,
                y_ref, k_ref, v_ref, sf_ref, sb_ref,
                p_scr, oa_scr, or_scr, u_scr, s_scr, dec_scr, qdec_scr, kdec_scr, *, npairs):
    @pl.when(pl.program_id(0) == 0)
    def _():
        _ret_build_tables(rd_ref, dec_scr, qdec_scr, kdec_scr, npairs=npairs, pair0=0)

    cols = lambda blk, w: p_scr.at[:, blk * LANES:blk * LANES + w]
    _project(_modulated_ln(x_ref, mod_ref), win_ref, wg_ref, bg_ref, p_scr, kv_out=(k_ref, v_ref))
    _attn_ctx_heads(cols(QA_BLK, D_MODEL), cols(KA_BLK, D_MODEL), cols(VA_BLK, D_MODEL),
                    cols(ZA_BLK, D_MODEL), _lam(lp_ref), sg_ref[...], oa_scr)
    _ret_pairs(rd_ref, cols(QR_BLK, D_MODEL // 2), cols(KR_BLK, D_MODEL // 2),
               cols(VR_BLK, D_MODEL), cols(ZR_BLK, D_MODEL), gg_ref, None, or_scr,
               (sf_ref, sb_ref), u_scr, s_scr, dec_scr, qdec_scr, kdec_scr,
               nc=x_ref.shape[0] // CHUNK, npairs=npairs, pair0=0)
    gate = mod_ref[0, :, 2 * D_MODEL:3 * D_MODEL]
    _out_rows(oa_scr, or_scr, cols(GA_BLK, D_MODEL), cols(GR_BLK, D_MODEL), x_ref, gate,
              wpa_ref, wpr_ref, wo_ref, lng_ref, lnb_ref, y_ref, sub=x_ref.shape[0])


def _ctx_call(x2d, mod3, w_in, w_gate, b_gate, lam_params, subln_g, ret_decay, gn_g,
              w_pa, w_pr, w_out, ln_g, ln_b, *, batch, seq, mod_row):
    npairs = N_HEADS // 2
    nc = seq // CHUNK
    tile = pl.BlockSpec((seq, D_MODEL), lambda b: (b, 0))
    vec = lambda w: pl.BlockSpec((1, w), lambda b: (0, 0))
    state = pl.BlockSpec((None, N_HEADS, DK_R, DV), lambda b: (b, 0, 0, 0))
    tok = jax.ShapeDtypeStruct((batch * seq, D_MODEL), F32)
    st = jax.ShapeDtypeStruct((batch, N_HEADS, DK_R, DV), F32)
    return pl.pallas_call(
        functools.partial(_ctx_kernel, npairs=npairs),
        grid=(batch,),
        in_specs=[tile,
                  pl.BlockSpec((1, 1, 3 * D_MODEL), lambda b: (mod_row, 0, 0)),
                  _resident((D_MODEL, IN_SEGS * D_MODEL)), _resident((D_MODEL, 2 * D_MODEL)),
                  vec(2 * D_MODEL),
                  pl.BlockSpec((4, DK_A), lambda b: (0, 0)), vec(DV),
                  pl.BlockSpec(memory_space=pltpu.SMEM), vec(D_MODEL),
                  _resident((D_MODEL, D_MODEL)), _resident((D_MODEL, D_MODEL)),
                  _resident((D_MODEL, D_MODEL)), vec(D_MODEL), vec(D_MODEL)],
        out_specs=[tile, tile, tile, state, state],
        out_shape=[tok, tok, tok, st, st],
        scratch_shapes=[pltpu.VMEM((seq, P_WIDTH), BF16),
                        pltpu.VMEM((seq, D_MODEL), BF16),
                        pltpu.VMEM((seq, D_MODEL), BF16)] + _ret_scratch(npairs, nc),
        compiler_params=_params(("arbitrary",)),
        name="ctx",
    )(x2d, mod3, w_in, w_gate, b_gate, lam_params, subln_g, ret_decay, gn_g,
      w_pa, w_pr, w_out, ln_g, ln_b)


def _rope_tables(n_tokens):
    rows = n_tokens // GRID_W
    r = np.repeat(np.arange(rows, dtype=np.float32), GRID_W)
    col = np.tile(np.arange(GRID_W, dtype=np.float32), rows)
    n_freq = DK_A // 4
    inv = np.float32(ROPE_BASE) ** (-np.arange(n_freq, dtype=np.float32) / np.float32(n_freq))
    ang = np.concatenate([r[:, None] * inv, col[:, None] * inv], axis=-1).astype(np.float32)
    cos = np.repeat(np.cos(ang), 2, axis=-1)
    sin = np.repeat(np.sin(ang), 2, axis=-1)
    even = (np.arange(DK_A) % 2 == 0)[None, :]
    sin_even = np.where(even, -sin, 0.0)
    sin_odd = np.where(even, 0.0, sin)
    two = lambda t: jnp.asarray(np.concatenate([t, t], axis=-1), F32)
    return two(cos), two(sin_even), two(sin_odd)


def kernel(x_prompt, x_sample, cache_attn_k, cache_attn_v, state_ret_fwd, state_ret_bwd,
           c, c_ctx, w_mod, b_mod, w_in, lam_params, subln_g, ret_decay, ret_gn_g,
           w_pa, w_pr, w_gate, b_gate, w_out, ln_g, ln_b):
    batch, seq, _ = x_prompt.shape
    dbatch, dseq, _ = x_sample.shape
    past = cache_attn_k.shape[2]
    l = 0

    cond8 = jnp.concatenate([c_ctx[None, :], c, jnp.zeros((8 - 1 - dbatch, D_MODEL), F32)], axis=0)
    mod3 = _mod_call(cond8, w_mod[l], b_mod[l][None, :]).reshape(8, 1, 3 * D_MODEL)

    win, wg = w_in[l].astype(BF16), w_gate[l].astype(BF16)
    bg = b_gate[l][None, :]
    wpa, wpr, wo = w_pa[l].astype(BF16), w_pr[l].astype(BF16), w_out[l].astype(BF16)
    lp, sg = lam_params[l], subln_g[l][None, :]
    rd, gg = ret_decay[l], ret_gn_g[l][None, :]
    lng, lnb = ln_g[l][None, :], ln_b[l][None, :]

    xc = x_prompt.reshape(batch * seq, D_MODEL)
    y_c, k_c, v_c, sf_c, sb_c = _ctx_call(xc, mod3, win, wg, bg, lp, sg, rd, gg, wpa, wpr, wo,
                                          lng, lnb, batch=batch, seq=seq, mod_row=0)

    xs = x_sample.reshape(dbatch * dseq, D_MODEL)
    p_s = _proj_lat_call(xs, mod3, win, wg, bg, _rope_tables(dseq), seq=dseq, mod_row0=1)
    ck = cache_attn_k[:, l].reshape(dbatch, past * N_HEADS, 2 * DK_A)
    cv = cache_attn_v[:, l].reshape(dbatch, past * N_HEADS, DV)
    oa_s = _attn_lat_call(p_s, ck, cv, lp, sg, batch=dbatch, seq=dseq, past=past)
    or_s = _ret_lat_call(p_s, rd, gg, state_ret_fwd[:, l], state_ret_bwd[:, l],
                         batch=dbatch, seq=dseq)
    y_s = _out_lat_call(oa_s, or_s, p_s, xs, mod3, wpa, wpr, wo, lng, lnb, seq=dseq, mod_row0=1)

    return (y_c.reshape(batch, seq, D_MODEL),
            y_s.reshape(dbatch, dseq, D_MODEL),
            k_c.reshape(batch, 1, seq, N_HEADS, 2 * DK_A),
            v_c.reshape(batch, 1, seq, N_HEADS, DV),
            sf_c.reshape(batch, 1, N_HEADS, DK_R, DV),
            sb_c.reshape(batch, 1, N_HEADS, DK_R, DV))
```

```python
import functools
import math

import jax
import jax.numpy as jnp
import numpy as np
from jax import lax
from jax.experimental import pallas as pl
from jax.experimental.pallas import tpu as pltpu

F32 = jnp.float32
BF16 = jnp.bfloat16

D_MODEL = 1024
N_HEADS = 8
DK_A = 64
DV = 128
DK_R = 64
CHUNK = 256
GRID_W = 64
ROPE_BASE = 10000.0
MOD_EPS = 1e-6
LN_EPS = 1e-5
DEPTH = 1
DEEPNORM_ALPHA = (2.0 * DEPTH) ** 0.25
LAM_INIT = 0.8 - 0.6 * math.exp(-0.3 * 0)
LOG2E = math.log2(math.e)

LANES = 128
QA_BLK, KA_BLK, VA_BLK, ZA_BLK = 0, 8, 16, 24
QR_BLK, KR_BLK, VR_BLK, ZR_BLK = 32, 36, 40, 48
GA_BLK, GR_BLK = 56, 64
IN_SEGS = 7
P_WIDTH = (IN_SEGS + 2) * D_MODEL
VMEM_LIMIT = 56 * 1024 * 1024


def _params(sem):
    return pltpu.CompilerParams(dimension_semantics=sem, vmem_limit_bytes=VMEM_LIMIT)


def _resident(shape):
    return pl.BlockSpec(shape, lambda *_: tuple(0 for _ in shape), pipeline_mode=pl.Buffered(1))


def _silu(z):
    return z * (1.0 / (1.0 + jnp.exp(-z)))


def _dot(a, b):
    return jnp.dot(a, b, preferred_element_type=F32)


def _dot_nt(a, b):
    return lax.dot_general(a, b, (((1,), (1,)), ((), ())), preferred_element_type=F32)


def _dot_tn(a, b):
    return lax.dot_general(a, b, (((0,), (0,)), ((), ())), preferred_element_type=F32)


def _mod_kernel(cond_ref, w_ref, b_ref, o_ref):
    s = _silu(cond_ref[...])
    o_ref[...] = _dot(s.astype(BF16), w_ref[...].astype(BF16)) + b_ref[...]


def _mod_call(cond8, w_mod, b_mod):
    tn = D_MODEL
    return pl.pallas_call(
        _mod_kernel,
        grid=(3 * D_MODEL // tn,),
        in_specs=[pl.BlockSpec((8, D_MODEL), lambda j: (0, 0)),
                  pl.BlockSpec((D_MODEL, tn), lambda j: (0, j)),
                  pl.BlockSpec((1, tn), lambda j: (0, j))],
        out_specs=pl.BlockSpec((8, tn), lambda j: (0, j)),
        out_shape=jax.ShapeDtypeStruct((8, 3 * D_MODEL), F32),
        compiler_params=_params(("arbitrary",)),
        name="mod",
    )(cond8, w_mod, b_mod)


def _rope(acc, cos, sin_even, sin_odd):
    outs = []
    for hh in range(N_HEADS):
        xs = acc[:, hh * LANES:(hh + 1) * LANES]
        nxt = pltpu.roll(xs, LANES - 1, 1)
        prv = pltpu.roll(xs, 1, 1)
        outs.append(xs * cos + nxt * sin_even + prv * sin_odd)
    return jnp.concatenate(outs, axis=1)


def _modulated_ln(x_ref, mod_ref):
    x = x_ref[...]
    mu = jnp.mean(x, axis=-1, keepdims=True)
    xc = x - mu
    var = jnp.mean(xc * xc, axis=-1, keepdims=True)
    shift = mod_ref[0, :, 0:D_MODEL]
    scale = mod_ref[0, :, D_MODEL:2 * D_MODEL]
    return (xc * lax.rsqrt(var + MOD_EPS) * (1.0 + scale) + shift).astype(BF16)


def _store_heads(ref, val):
    for h in range(N_HEADS):
        ref[pl.ds(h, val.shape[0], stride=N_HEADS), :] = val[:, h * LANES:(h + 1) * LANES]


def _project(h, win_ref, wg_ref, bg_ref, p_ref, rope=None, kv_out=None):
    def seg(j):
        if j < IN_SEGS:
            return _dot(h, win_ref[:, j * D_MODEL:(j + 1) * D_MODEL])
        return _dot(h, wg_ref[:, (j - IN_SEGS) * D_MODEL:(j - IN_SEGS + 1) * D_MODEL])

    def put(j, val):
        p_ref[:, j * D_MODEL:(j + 1) * D_MODEL] = val.astype(BF16)

    qa = seg(0)
    ka = seg(1)
    if rope is not None:
        cos, se, so = (t[...] for t in rope)
        qa = _rope(qa, cos, se, so)
        ka = _rope(ka, cos, se, so)
    if kv_out is not None:
        _store_heads(kv_out[0], ka)
    put(0, qa * (DK_A ** -0.5 * LOG2E))
    put(1, ka)
    va = seg(2)
    if kv_out is not None:
        _store_heads(kv_out[1], va)
    put(2, va)
    put(3, _silu(seg(3)))
    qk = seg(4)
    put(4, jnp.concatenate([qk[:, :D_MODEL // 2], qk[:, D_MODEL // 2:] * (DK_R ** -0.5)], axis=1))
    put(5, seg(5))
    put(6, _silu(seg(6)))
    for j in (IN_SEGS, IN_SEGS + 1):
        g = seg(j) + bg_ref[:, (j - IN_SEGS) * D_MODEL:(j - IN_SEGS + 1) * D_MODEL]
        put(j, 1.0 / (1.0 + jnp.exp(-g)))


def _proj_lat_kernel(xn_ref, modn_ref, x0_ref, mod0_ref, win_ref, wg_ref, bg_ref,
                     cos_ref, se_ref, so_ref, p_ref, h_even, h_odd):
    i = pl.program_id(0)

    def body(h_cur, h_nxt):
        h_nxt[...] = _modulated_ln(xn_ref, modn_ref)
        _project(h_cur[...], win_ref, wg_ref, bg_ref, p_ref, rope=(cos_ref, se_ref, so_ref))

    @pl.when(i == 0)
    def _():
        h_even[...] = _modulated_ln(x0_ref, mod0_ref)

    @pl.when(i % 2 == 0)
    def _():
        body(h_even, h_odd)

    @pl.when(i % 2 == 1)
    def _():
        body(h_odd, h_even)


def _proj_lat_call(x2d, mod3, w_in, w_gate, b_gate, rope, *, seq, mod_row0):
    m = x2d.shape[0]
    tm = 256
    nt = m // tm
    per_seq = seq // tm
    row = lambda i: mod_row0 + i // per_seq
    nxt = lambda i: jnp.minimum(i + 1, nt - 1)
    return pl.pallas_call(
        _proj_lat_kernel,
        grid=(nt,),
        in_specs=[
            pl.BlockSpec((tm, D_MODEL), lambda i: (nxt(i), 0)),
            pl.BlockSpec((1, 1, 3 * D_MODEL), lambda i: (row(nxt(i)), 0, 0)),
            pl.BlockSpec((tm, D_MODEL), lambda i: (0, 0)),
            pl.BlockSpec((1, 1, 3 * D_MODEL), lambda i: (row(0), 0, 0)),
            _resident((D_MODEL, IN_SEGS * D_MODEL)),
            _resident((D_MODEL, 2 * D_MODEL)),
            pl.BlockSpec((1, 2 * D_MODEL), lambda i: (0, 0)),
        ] + [pl.BlockSpec((tm, LANES), lambda i: (i % per_seq, 0))] * 3,
        out_specs=pl.BlockSpec((tm, P_WIDTH), lambda i: (i, 0)),
        out_shape=jax.ShapeDtypeStruct((m, P_WIDTH), BF16),
        scratch_shapes=[pltpu.VMEM((tm, D_MODEL), BF16), pltpu.VMEM((tm, D_MODEL), BF16)],
        compiler_params=_params(("arbitrary",)),
        name="proj_lat",
    )(x2d, mod3, x2d, mod3, w_in, w_gate, b_gate, *rope)


def _lam(lp_ref):
    lp = lp_ref[...]
    a = jnp.sum(lp[0:1] * lp[1:2], axis=-1, keepdims=True)
    b = jnp.sum(lp[2:3] * lp[3:4], axis=-1, keepdims=True)
    return jnp.exp(a) - jnp.exp(b) + LAM_INIT


def _attn_rows(q, k_all, v_ones, z, lam, g):
    lane = lax.broadcasted_iota(jnp.int32, (1, LANES), 1)
    zero = jnp.zeros_like(q)
    outs = []
    for qm in (jnp.where(lane < DK_A, q, zero), jnp.where(lane >= DK_A, q, zero)):
        s = _dot_nt(qm, k_all)
        mx = jnp.max(s, axis=-1, keepdims=True)
        p = jnp.exp2(s - mx).astype(BF16)
        ol = _dot(p, v_ones)
        outs.append(ol[:, :DV] * (1.0 / ol[:, DV:]))
    oa = outs[0] - lam * outs[1]
    oa = oa * lax.rsqrt(jnp.mean(oa * oa, axis=-1, keepdims=True) + LN_EPS)
    oa = oa * g * (1.0 - LAM_INIT)
    return oa * z.astype(F32)


def _attn_ctx_heads(q_ref, k_ref, v_ref, z_ref, lam, g, o_ref):
    ones = jnp.ones((q_ref.shape[0], DV), BF16)
    for h in range(N_HEADS):
        sl = slice(h * LANES, (h + 1) * LANES)
        v_ones = jnp.concatenate([v_ref[:, sl], ones], axis=1)
        o_ref[:, sl] = _attn_rows(q_ref[:, sl], k_ref[:, sl], v_ones, z_ref[:, sl],
                                  lam, g).astype(BF16)


def _attn_lat_kernel(q_ref, k_ref, v_ref, z_ref, kc_ref, vc_ref, lp_ref, g_ref, o_ref,
                     k_all, v_all, *, past, tc):
    @pl.when(pl.program_id(2) == 0)
    def _():
        head_rows = pl.ds(pl.program_id(1), past, stride=N_HEADS)
        k_all[0:past, :] = kc_ref[head_rows, :].astype(BF16)
        v_all[0:past, 0:DV] = vc_ref[head_rows, :].astype(BF16)
        k_all[past:, :] = k_ref[...]
        v_all[past:, 0:DV] = v_ref[...]
        v_all[:, DV:] = jnp.ones((v_all.shape[0], DV), BF16)

    lam = _lam(lp_ref)
    g = g_ref[...]
    for lo in range(0, q_ref.shape[0], tc):
        o_ref[lo:lo + tc, :] = _attn_rows(q_ref[lo:lo + tc, :], k_all[...], v_all[...],
                                          z_ref[lo:lo + tc, :], lam, g).astype(BF16)


def _attn_lat_call(p, cache_k, cache_v, lam_params, subln_g, *, batch, seq, past):
    tq, tc = 2048, 128
    nq = seq // tq
    return pl.pallas_call(
        functools.partial(_attn_lat_kernel, past=past, tc=tc),
        grid=(batch, N_HEADS, nq),
        in_specs=[
            pl.BlockSpec((tq, LANES), lambda b, h, i: (b * nq + i, QA_BLK + h)),
            pl.BlockSpec((seq, LANES), lambda b, h, i: (b, KA_BLK + h)),
            pl.BlockSpec((seq, LANES), lambda b, h, i: (b, VA_BLK + h)),
            pl.BlockSpec((tq, LANES), lambda b, h, i: (b * nq + i, ZA_BLK + h)),
            pl.BlockSpec((None, past * N_HEADS, LANES), lambda b, h, i: (b, 0, 0)),
            pl.BlockSpec((None, past * N_HEADS, LANES), lambda b, h, i: (b, 0, 0)),
            pl.BlockSpec((4, DK_A), lambda b, h, i: (0, 0)),
            pl.BlockSpec((1, DV), lambda b, h, i: (0, 0)),
        ],
        out_specs=pl.BlockSpec((tq, LANES), lambda b, h, i: (b * nq + i, h)),
        out_shape=jax.ShapeDtypeStruct((batch * seq, D_MODEL), BF16),
        scratch_shapes=[pltpu.VMEM((past + seq, LANES), BF16),
                        pltpu.VMEM((past + seq, 2 * DV), BF16)],
        compiler_params=_params(("arbitrary", "arbitrary", "arbitrary")),
        name="attn_lat",
    )(p, p, p, p, cache_k, cache_v, lam_params, subln_g)


PW, VW = 2 * DK_R, 2 * DV


def _ret_scratch(npairs, nc):
    stacked_state = (npairs, nc, 2 * PW, VW)
    return [pltpu.VMEM(stacked_state, F32),
            pltpu.VMEM(stacked_state, BF16),
            pltpu.VMEM((npairs, 2, CHUNK, CHUNK), F32),
            pltpu.VMEM((npairs, CHUNK, 2 * PW), F32),
            pltpu.VMEM((npairs, CHUNK, 2 * PW), F32)]


def _log_gammas(rd_ref, pair):
    def one(d, hh):
        r = jnp.full((1, VW), rd_ref[d, 2 * pair + hh], F32)
        return jnp.log1p(-jnp.exp2(r))
    return [[one(d, hh) for hh in range(2)] for d in range(2)]


def _ret_build_tables(rd_ref, dec_scr, qdec_scr, kdec_scr, *, npairs, pair0):
    lane_q = lax.broadcasted_iota(jnp.int32, (1, PW), 1)
    rel = (lax.broadcasted_iota(jnp.int32, (CHUNK, CHUNK), 0)
           - lax.broadcasted_iota(jnp.int32, (CHUNK, CHUNK), 1)).astype(F32)
    row_q = lax.broadcasted_iota(jnp.int32, (CHUNK, PW), 0).astype(F32)
    for pi in range(npairs):
        lg = _log_gammas(rd_ref, pair0 + pi)
        lg_q = [jnp.where(lane_q < DK_R, lg[d][0][:, :PW], lg[d][1][:, :PW])
                for d in range(2)]
        for hh in range(2):
            dec_scr[pi, hh] = (
                jnp.where(rel >= 0, jnp.exp(lg[0][hh][:, :1] * jnp.maximum(rel, 0.0)), 0.0)
                + jnp.where(rel <= 0, jnp.exp(lg[1][hh][:, :1] * jnp.maximum(-rel, 0.0)), 0.0))
        qdec_scr[pi] = jnp.concatenate([jnp.exp(lg_q[0] * (row_q + 1.0)),
                                        jnp.exp(lg_q[1] * (CHUNK - row_q))], axis=1)
        kdec_scr[pi] = jnp.concatenate([jnp.exp(lg_q[0] * (CHUNK - 1.0 - row_q)),
                                        jnp.exp(lg_q[1] * row_q)], axis=1)


def _ret_pairs(rd_ref, q_ref, k_ref, v_ref, z_ref, g_ref, s0_refs, o_ref, sout_refs,
               u_scr, s_scr, dec_scr, qdec_scr, kdec_scr, *, nc, npairs, pair0):
    cross = s0_refs is not None or nc > 1
    lane_q = lax.broadcasted_iota(jnp.int32, (1, PW), 1)
    lane_v = lax.broadcasted_iota(jnp.int32, (1, VW), 1)
    srow = lax.broadcasted_iota(jnp.int32, (2 * PW, VW), 0)
    scol = lax.broadcasted_iota(jnp.int32, (2 * PW, VW), 1)
    diag_blocks = ((srow % PW) // DK_R) == (scol // DV)

    def rows(c):
        return slice(c * CHUNK, (c + 1) * CHUNK)

    for pi in range(npairs):
        qs = slice(pi * PW, (pi + 1) * PW)
        vs = slice(pi * VW, (pi + 1) * VW)
        lg = _log_gammas(rd_ref, pair0 + pi)
        lg_v = [jnp.where(lane_v < DV, lg[d][0], lg[d][1]) for d in range(2)]
        chunk_decay = [jnp.exp(lg_v[d] * CHUNK) for d in range(2)]

        for c in range(nc):
            k = k_ref[rows(c), qs].astype(F32)
            kd = (jnp.concatenate([k, k], axis=1) * kdec_scr[pi]).astype(BF16)
            u_scr[pi, c] = jnp.where(diag_blocks, _dot_tn(kd, v_ref[rows(c), vs]), 0.0)

        for d, order in ((0, range(nc)), (1, reversed(range(nc)))):
            half = slice(d * PW, (d + 1) * PW)
            if s0_refs is None:
                s = jnp.zeros((PW, VW), F32)
            else:
                zero = jnp.zeros((DK_R, DV), F32)
                s = jnp.concatenate(
                    [jnp.concatenate([s0_refs[d][2 * pi], zero], axis=1),
                     jnp.concatenate([zero, s0_refs[d][2 * pi + 1]], axis=1)], axis=0)
            for c in order:
                if cross:
                    s_scr[pi, c, half, :] = s.astype(BF16)
                s = s * chunk_decay[d] + u_scr[pi, c, half, :]
            if sout_refs is not None:
                sout_refs[d][2 * pi] = s[:DK_R, :DV]
                sout_refs[d][2 * pi + 1] = s[DK_R:, DV:]

        g = g_ref[:, vs]
        for c in range(nc):
            q = q_ref[rows(c), qs]
            k = k_ref[rows(c), qs]
            v = v_ref[rows(c), vs]
            zero = jnp.zeros_like(q)
            q2 = jnp.concatenate([jnp.where(lane_q < DK_R, q, zero),
                                  jnp.where(lane_q >= DK_R, q, zero)], axis=0)
            a2 = _dot_nt(q2, k)
            o = jnp.concatenate(
                [_dot((a2[hh * CHUNK:(hh + 1) * CHUNK] * dec_scr[pi, hh]).astype(BF16),
                      v[:, hh * DV:(hh + 1) * DV]) for hh in range(2)], axis=1)
            if cross:
                qf = q.astype(F32)
                qq = (jnp.concatenate([qf, qf], axis=1) * qdec_scr[pi]).astype(BF16)
                o = o + _dot(qq, s_scr[pi, c])
            outs = []
            for hh in range(2):
                oh = o[:, hh * DV:(hh + 1) * DV]
                mu = jnp.mean(oh, axis=-1, keepdims=True)
                oc = oh - mu
                var = jnp.mean(oc * oc, axis=-1, keepdims=True)
                outs.append(oc * lax.rsqrt(var + LN_EPS))
            o = jnp.concatenate(outs, axis=1) * g * z_ref[rows(c), vs].astype(F32)
            o_ref[rows(c), vs] = o.astype(BF16)


def _ret_lat_kernel(rd_ref, q_ref, k_ref, v_ref, z_ref, g_ref, s0f_ref, s0b_ref, o_ref,
                    u_scr, s_scr, dec_scr, qdec_scr, kdec_scr, *, nc, npairs):
    pair0 = pl.program_id(0) * npairs

    @pl.when(pl.program_id(1) == 0)
    def _():
        _ret_build_tables(rd_ref, dec_scr, qdec_scr, kdec_scr, npairs=npairs, pair0=pair0)

    _ret_pairs(rd_ref, q_ref, k_ref, v_ref, z_ref, g_ref, (s0f_ref, s0b_ref), o_ref, None,
               u_scr, s_scr, dec_scr, qdec_scr, kdec_scr, nc=nc, npairs=npairs, pair0=pair0)


def _ret_lat_call(p, ret_decay, gn_g, s0f, s0b, *, batch, seq):
    nc = seq // CHUNK
    hps = 2
    npairs = hps // 2
    qw, vw = DK_R * hps, DV * hps
    col = lambda w, blk: pl.BlockSpec((seq, w), lambda j, b: (b, blk * LANES // w + j))
    state = pl.BlockSpec((None, hps, DK_R, DV), lambda j, b: (b, j, 0, 0))
    return pl.pallas_call(
        functools.partial(_ret_lat_kernel, nc=nc, npairs=npairs),
        grid=(N_HEADS // hps, batch),
        in_specs=[pl.BlockSpec(memory_space=pltpu.SMEM),
                  col(qw, QR_BLK), col(qw, KR_BLK), col(vw, VR_BLK), col(vw, ZR_BLK),
                  pl.BlockSpec((1, vw), lambda j, b: (0, j)), state, state],
        out_specs=pl.BlockSpec((seq, vw), lambda j, b: (b, j)),
        out_shape=jax.ShapeDtypeStruct((batch * seq, D_MODEL), BF16),
        scratch_shapes=_ret_scratch(npairs, nc),
        compiler_params=_params(("arbitrary", "arbitrary")),
        name="ret_lat",
    )(ret_decay, p, p, p, p, gn_g, s0f, s0b)


def _out_rows(oa_ref, or_ref, ga_ref, gr_ref, x_ref, gate, wpa_ref, wpr_ref, wo_ref,
              lng_ref, lnb_ref, y_ref, *, sub):
    for s in range(y_ref.shape[0] // sub):
        rows = slice(s * sub, (s + 1) * sub)
        a = _dot(oa_ref[rows, :], wpa_ref[...])
        r = _dot(or_ref[rows, :], wpr_ref[...])
        m = ga_ref[rows, :].astype(F32) * a + gr_ref[rows, :].astype(F32) * r
        out = _dot(m.astype(BF16), wo_ref[...])
        t = DEEPNORM_ALPHA * x_ref[rows, :] + gate * out
        mu = jnp.mean(t, axis=-1, keepdims=True)
        tc = t - mu
        var = jnp.mean(tc * tc, axis=-1, keepdims=True)
        y_ref[rows, :] = tc * lax.rsqrt(var + LN_EPS) * lng_ref[...] + lnb_ref[...]


def _out_lat_kernel(oa_ref, or_ref, ga_ref, gr_ref, x_ref, mod_ref, wpa_ref, wpr_ref, wo_ref,
                    lng_ref, lnb_ref, y_ref, *, sub):
    gate = mod_ref[0, :, 2 * D_MODEL:3 * D_MODEL]
    _out_rows(oa_ref, or_ref, ga_ref, gr_ref, x_ref, gate, wpa_ref, wpr_ref, wo_ref,
              lng_ref, lnb_ref, y_ref, sub=sub)


def _out_lat_call(oa, orr, p, x2d, mod3, w_pa, w_pr, w_out, ln_g, ln_b, *, seq, mod_row0):
    m = x2d.shape[0]
    tm, sub = 1024, 512
    per_seq = seq // tm
    tile = lambda j: pl.BlockSpec((tm, D_MODEL), lambda i: (i, j))
    vec = pl.BlockSpec((1, D_MODEL), lambda i: (0, 0))
    return pl.pallas_call(
        functools.partial(_out_lat_kernel, sub=sub),
        grid=(m // tm,),
        in_specs=[tile(0), tile(0), tile(GA_BLK * LANES // D_MODEL), tile(GR_BLK * LANES // D_MODEL),
                  tile(0),
                  pl.BlockSpec((1, 1, 3 * D_MODEL), lambda i: (mod_row0 + i // per_seq, 0, 0)),
                  _resident((D_MODEL, D_MODEL)), _resident((D_MODEL, D_MODEL)),
                  _resident((D_MODEL, D_MODEL)), vec, vec],
        out_specs=tile(0),
        out_shape=jax.ShapeDtypeStruct((m, D_MODEL), F32),
        compiler_params=_params(("arbitrary",)),
        name="out_lat",
    )(oa, orr, p, p, x2d, mod3, w_pa, w_pr, w_out, ln_g, ln_b)


def _ctx_kernel(x_ref, mod_ref, win_ref, wg_ref, bg_ref, lp_ref, sg_ref, rd_ref, gg_ref,
                wpa_ref, wpr_ref, wo_ref, lng_ref, lnb_ref,
                y_ref, k_ref, v_ref, sf_ref, sb_ref,
                p_scr, oa_scr, or_scr, u_scr, s_scr, dec_scr, qdec_scr, kdec_scr, *, npairs):
    @pl.when(pl.program_id(0) == 0)
    def _():
        _ret_build_tables(rd_ref, dec_scr, qdec_scr, kdec_scr, npairs=npairs, pair0=0)

    cols = lambda blk, w: p_scr.at[:, blk * LANES:blk * LANES + w]
    _project(_modulated_ln(x_ref, mod_ref), win_ref, wg_ref, bg_ref, p_scr, kv_out=(k_ref, v_ref))
    _attn_ctx_heads(cols(QA_BLK, D_MODEL), cols(KA_BLK, D_MODEL), cols(VA_BLK, D_MODEL),
                    cols(ZA_BLK, D_MODEL), _lam(lp_ref), sg_ref[...], oa_scr)
    _ret_pairs(rd_ref, cols(QR_BLK, D_MODEL // 2), cols(KR_BLK, D_MODEL // 2),
               cols(VR_BLK, D_MODEL), cols(ZR_BLK, D_MODEL), gg_ref, None, or_scr,
               (sf_ref, sb_ref), u_scr, s_scr, dec_scr, qdec_scr, kdec_scr,
               nc=x_ref.shape[0] // CHUNK, npairs=npairs, pair0=0)
    gate = mod_ref[0, :, 2 * D_MODEL:3 * D_MODEL]
    _out_rows(oa_scr, or_scr, cols(GA_BLK, D_MODEL), cols(GR_BLK, D_MODEL), x_ref, gate,
              wpa_ref, wpr_ref, wo_ref, lng_ref, lnb_ref, y_ref, sub=x_ref.shape[0])


def _ctx_call(x2d, mod3, w_in, w_gate, b_gate, lam_params, subln_g, ret_decay, gn_g,
              w_pa, w_pr, w_out, ln_g, ln_b, *, batch, seq, mod_row):
    npairs = N_HEADS // 2
    nc = seq // CHUNK
    tile = pl.BlockSpec((seq, D_MODEL), lambda b: (b, 0))
    vec = lambda w: pl.BlockSpec((1, w), lambda b: (0, 0))
    state = pl.BlockSpec((None, N_HEADS, DK_R, DV), lambda b: (b, 0, 0, 0))
    heads = pl.BlockSpec((seq * N_HEADS, LANES), lambda b: (b, 0))
    tok = jax.ShapeDtypeStruct((batch * seq, D_MODEL), F32)
    tok_heads = jax.ShapeDtypeStruct((batch * seq * N_HEADS, LANES), F32)
    st = jax.ShapeDtypeStruct((batch, N_HEADS, DK_R, DV), F32)
    return pl.pallas_call(
        functools.partial(_ctx_kernel, npairs=npairs),
        grid=(batch,),
        in_specs=[tile,
                  pl.BlockSpec((1, 1, 3 * D_MODEL), lambda b: (mod_row, 0, 0)),
                  _resident((D_MODEL, IN_SEGS * D_MODEL)), _resident((D_MODEL, 2 * D_MODEL)),
                  vec(2 * D_MODEL),
                  pl.BlockSpec((4, DK_A), lambda b: (0, 0)), vec(DV),
                  pl.BlockSpec(memory_space=pltpu.SMEM), vec(D_MODEL),
                  _resident((D_MODEL, D_MODEL)), _resident((D_MODEL, D_MODEL)),
                  _resident((D_MODEL, D_MODEL)), vec(D_MODEL), vec(D_MODEL)],
        out_specs=[tile, heads, heads, state, state],
        out_shape=[tok, tok_heads, tok_heads, st, st],
        scratch_shapes=[pltpu.VMEM((seq, P_WIDTH), BF16),
                        pltpu.VMEM((seq, D_MODEL), BF16),
                        pltpu.VMEM((seq, D_MODEL), BF16)] + _ret_scratch(npairs, nc),
        compiler_params=_params(("arbitrary",)),
        name="ctx",
    )(x2d, mod3, w_in, w_gate, b_gate, lam_params, subln_g, ret_decay, gn_g,
      w_pa, w_pr, w_out, ln_g, ln_b)


def _rope_tables(n_tokens):
    rows = n_tokens // GRID_W
    r = np.repeat(np.arange(rows, dtype=np.float32), GRID_W)
    col = np.tile(np.arange(GRID_W, dtype=np.float32), rows)
    n_freq = DK_A // 4
    inv = np.float32(ROPE_BASE) ** (-np.arange(n_freq, dtype=np.float32) / np.float32(n_freq))
    ang = np.concatenate([r[:, None] * inv, col[:, None] * inv], axis=-1).astype(np.float32)
    cos = np.repeat(np.cos(ang), 2, axis=-1)
    sin = np.repeat(np.sin(ang), 2, axis=-1)
    even = (np.arange(DK_A) % 2 == 0)[None, :]
    sin_even = np.where(even, -sin, 0.0)
    sin_odd = np.where(even, 0.0, sin)
    two = lambda t: jnp.asarray(np.concatenate([t, t], axis=-1), F32)
    return two(cos), two(sin_even), two(sin_odd)


def kernel(x_prompt, x_sample, cache_attn_k, cache_attn_v, state_ret_fwd, state_ret_bwd,
           c, c_ctx, w_mod, b_mod, w_in, lam_params, subln_g, ret_decay, ret_gn_g,
           w_pa, w_pr, w_gate, b_gate, w_out, ln_g, ln_b):
    batch, seq, _ = x_prompt.shape
    dbatch, dseq, _ = x_sample.shape
    past = cache_attn_k.shape[2]
    l = 0

    cond8 = jnp.concatenate([c_ctx[None, :], c, jnp.zeros((8 - 1 - dbatch, D_MODEL), F32)], axis=0)
    mod3 = _mod_call(cond8, w_mod[l], b_mod[l][None, :]).reshape(8, 1, 3 * D_MODEL)

    win, wg = w_in[l].astype(BF16), w_gate[l].astype(BF16)
    bg = b_gate[l][None, :]
    wpa, wpr, wo = w_pa[l].astype(BF16), w_pr[l].astype(BF16), w_out[l].astype(BF16)
    lp, sg = lam_params[l], subln_g[l][None, :]
    rd, gg = ret_decay[l], ret_gn_g[l][None, :]
    lng, lnb = ln_g[l][None, :], ln_b[l][None, :]

    xc = x_prompt.reshape(batch * seq, D_MODEL)
    y_c, k_c, v_c, sf_c, sb_c = _ctx_call(xc, mod3, win, wg, bg, lp, sg, rd, gg, wpa, wpr, wo,
                                          lng, lnb, batch=batch, seq=seq, mod_row=0)

    xs = x_sample.reshape(dbatch * dseq, D_MODEL)
    p_s = _proj_lat_call(xs, mod3, win, wg, bg, _rope_tables(dseq), seq=dseq, mod_row0=1)
    ck = cache_attn_k[:, l].reshape(dbatch, past * N_HEADS, 2 * DK_A)
    cv = cache_attn_v[:, l].reshape(dbatch, past * N_HEADS, DV)
    oa_s = _attn_lat_call(p_s, ck, cv, lp, sg, batch=dbatch, seq=dseq, past=past)
    or_s = _ret_lat_call(p_s, rd, gg, state_ret_fwd[:, l], state_ret_bwd[:, l],
                         batch=dbatch, seq=dseq)
    y_s = _out_lat_call(oa_s, or_s, p_s, xs, mod3, wpa, wpr, wo, lng, lnb, seq=dseq, mod_row0=1)

    return (y_c.reshape(batch, seq, D_MODEL),
            y_s.reshape(dbatch, dseq, D_MODEL),
            k_c.reshape(batch, 1, seq, N_HEADS, 2 * DK_A),
            v_c.reshape(batch, 1, seq, N_HEADS, DV),
            sf_c.reshape(batch, 1, N_HEADS, DK_R, DV),
            sb_c.reshape(batch, 1, N_HEADS, DK_R, DV))
```

```python
import functools
import math

import jax
import jax.numpy as jnp
import numpy as np
from jax import lax
from jax.experimental import pallas as pl
from jax.experimental.pallas import tpu as pltpu

F32 = jnp.float32
BF16 = jnp.bfloat16

D_MODEL = 1024
N_HEADS = 8
DK_A = 64
DV = 128
DK_R = 64
CHUNK = 256
GRID_W = 64
ROPE_BASE = 10000.0
MOD_EPS = 1e-6
LN_EPS = 1e-5
DEPTH = 1
DEEPNORM_ALPHA = (2.0 * DEPTH) ** 0.25
LAM_INIT = 0.8 - 0.6 * math.exp(-0.3 * 0)
LOG2E = math.log2(math.e)

LANES = 128
QA_BLK, KA_BLK, VA_BLK, ZA_BLK = 0, 8, 16, 24
QR_BLK, KR_BLK, VR_BLK, ZR_BLK = 32, 36, 40, 48
GA_BLK, GR_BLK = 56, 64
IN_SEGS = 7
P_WIDTH = (IN_SEGS + 2) * D_MODEL
VMEM_LIMIT = 56 * 1024 * 1024


def _params(sem):
    return pltpu.CompilerParams(dimension_semantics=sem, vmem_limit_bytes=VMEM_LIMIT)


def _resident(shape):
    return pl.BlockSpec(shape, lambda *_: tuple(0 for _ in shape), pipeline_mode=pl.Buffered(1))


def _silu(z):
    return z * (1.0 / (1.0 + jnp.exp(-z)))


def _dot(a, b):
    return jnp.dot(a, b, preferred_element_type=F32)


def _dot_nt(a, b):
    return lax.dot_general(a, b, (((1,), (1,)), ((), ())), preferred_element_type=F32)


def _dot_tn(a, b):
    return lax.dot_general(a, b, (((0,), (0,)), ((), ())), preferred_element_type=F32)


def _mod_kernel(cond_ref, w_ref, b_ref, o_ref):
    s = _silu(cond_ref[...])
    o_ref[...] = _dot(s.astype(BF16), w_ref[...].astype(BF16)) + b_ref[...]


def _mod_call(cond8, w_mod, b_mod):
    tn = D_MODEL // 2
    return pl.pallas_call(
        _mod_kernel,
        grid=(3 * D_MODEL // tn,),
        in_specs=[pl.BlockSpec((8, D_MODEL), lambda j: (0, 0)),
                  pl.BlockSpec((D_MODEL, tn), lambda j: (0, j)),
                  pl.BlockSpec((1, tn), lambda j: (0, j))],
        out_specs=pl.BlockSpec((8, tn), lambda j: (0, j)),
        out_shape=jax.ShapeDtypeStruct((8, 3 * D_MODEL), F32),
        compiler_params=_params(("arbitrary",)),
        name="mod",
    )(cond8, w_mod, b_mod)


def _rope(acc, cos, sin_even, sin_odd):
    outs = []
    for hh in range(N_HEADS):
        xs = acc[:, hh * LANES:(hh + 1) * LANES]
        nxt = pltpu.roll(xs, LANES - 1, 1)
        prv = pltpu.roll(xs, 1, 1)
        outs.append(xs * cos + nxt * sin_even + prv * sin_odd)
    return jnp.concatenate(outs, axis=1)


def _modulated_ln(x_ref, mod_ref):
    x = x_ref[...]
    mu = jnp.mean(x, axis=-1, keepdims=True)
    xc = x - mu
    var = jnp.mean(xc * xc, axis=-1, keepdims=True)
    shift = mod_ref[0, :, 0:D_MODEL]
    scale = mod_ref[0, :, D_MODEL:2 * D_MODEL]
    return (xc * lax.rsqrt(var + MOD_EPS) * (1.0 + scale) + shift).astype(BF16)


def _store_heads(ref, val):
    for h in range(N_HEADS):
        ref[pl.ds(h, val.shape[0], stride=N_HEADS), :] = val[:, h * LANES:(h + 1) * LANES]


def _project(h, win_ref, wg_ref, bg_ref, p_ref, rope=None, kv_out=None):
    def seg(j):
        if j < IN_SEGS:
            return _dot(h, win_ref[:, j * D_MODEL:(j + 1) * D_MODEL])
        return _dot(h, wg_ref[:, (j - IN_SEGS) * D_MODEL:(j - IN_SEGS + 1) * D_MODEL])

    def put(j, val):
        p_ref[:, j * D_MODEL:(j + 1) * D_MODEL] = val.astype(BF16)

    qa = seg(0)
    ka = seg(1)
    if rope is not None:
        cos, se, so = (t[...] for t in rope)
        qa = _rope(qa, cos, se, so)
        ka = _rope(ka, cos, se, so)
    if kv_out is not None:
        _store_heads(kv_out[0], ka)
    put(0, qa * (DK_A ** -0.5 * LOG2E))
    put(1, ka)
    va = seg(2)
    if kv_out is not None:
        _store_heads(kv_out[1], va)
    put(2, va)
    put(3, _silu(seg(3)))
    qk = seg(4)
    put(4, jnp.concatenate([qk[:, :D_MODEL // 2], qk[:, D_MODEL // 2:] * (DK_R ** -0.5)], axis=1))
    put(5, seg(5))
    put(6, _silu(seg(6)))
    for j in (IN_SEGS, IN_SEGS + 1):
        g = seg(j) + bg_ref[:, (j - IN_SEGS) * D_MODEL:(j - IN_SEGS + 1) * D_MODEL]
        put(j, 1.0 / (1.0 + jnp.exp(-g)))


def _proj_lat_kernel(xn_ref, modn_ref, x0_ref, mod0_ref, win_ref, wg_ref, bg_ref,
                     cos_ref, se_ref, so_ref, p_ref, h_even, h_odd):
    i = pl.program_id(0)

    def body(h_cur, h_nxt):
        h_nxt[...] = _modulated_ln(xn_ref, modn_ref)
        _project(h_cur[...], win_ref, wg_ref, bg_ref, p_ref, rope=(cos_ref, se_ref, so_ref))

    @pl.when(i == 0)
    def _():
        h_even[...] = _modulated_ln(x0_ref, mod0_ref)

    @pl.when(i % 2 == 0)
    def _():
        body(h_even, h_odd)

    @pl.when(i % 2 == 1)
    def _():
        body(h_odd, h_even)


def _proj_lat_call(x2d, mod3, w_in, w_gate, b_gate, rope, *, seq, mod_row0):
    m = x2d.shape[0]
    tm = 512
    nt = m // tm
    per_seq = seq // tm
    row = lambda i: mod_row0 + i // per_seq
    nxt = lambda i: jnp.minimum(i + 1, nt - 1)
    return pl.pallas_call(
        _proj_lat_kernel,
        grid=(nt,),
        in_specs=[
            pl.BlockSpec((tm, D_MODEL), lambda i: (nxt(i), 0)),
            pl.BlockSpec((1, 1, 3 * D_MODEL), lambda i: (row(nxt(i)), 0, 0)),
            pl.BlockSpec((tm, D_MODEL), lambda i: (0, 0)),
            pl.BlockSpec((1, 1, 3 * D_MODEL), lambda i: (row(0), 0, 0)),
            _resident((D_MODEL, IN_SEGS * D_MODEL)),
            _resident((D_MODEL, 2 * D_MODEL)),
            pl.BlockSpec((1, 2 * D_MODEL), lambda i: (0, 0)),
        ] + [pl.BlockSpec((tm, LANES), lambda i: (i % per_seq, 0))] * 3,
        out_specs=pl.BlockSpec((tm, P_WIDTH), lambda i: (i, 0)),
        out_shape=jax.ShapeDtypeStruct((m, P_WIDTH), BF16),
        scratch_shapes=[pltpu.VMEM((tm, D_MODEL), BF16), pltpu.VMEM((tm, D_MODEL), BF16)],
        compiler_params=_params(("arbitrary",)),
        name="proj_lat",
    )(x2d, mod3, x2d, mod3, w_in, w_gate, b_gate, *rope)


def _lam(lp_ref):
    lp = lp_ref[...]
    a = jnp.sum(lp[0:1] * lp[1:2], axis=-1, keepdims=True)
    b = jnp.sum(lp[2:3] * lp[3:4], axis=-1, keepdims=True)
    return jnp.exp(a) - jnp.exp(b) + LAM_INIT


def _attn_rows(q, k_all, v_ones, z, lam, g):
    lane = lax.broadcasted_iota(jnp.int32, (1, LANES), 1)
    zero = jnp.zeros_like(q)
    outs = []
    for qm in (jnp.where(lane < DK_A, q, zero), jnp.where(lane >= DK_A, q, zero)):
        s = _dot_nt(qm, k_all)
        mx = jnp.max(s, axis=-1, keepdims=True)
        p = jnp.exp2(s - mx).astype(BF16)
        ol = _dot(p, v_ones)
        outs.append(ol[:, :DV] * (1.0 / ol[:, DV:]))
    oa = outs[0] - lam * outs[1]
    oa = oa * lax.rsqrt(jnp.mean(oa * oa, axis=-1, keepdims=True) + LN_EPS)
    oa = oa * g * (1.0 - LAM_INIT)
    return oa * z.astype(F32)


def _attn_ctx_heads(q_ref, k_ref, v_ref, z_ref, lam, g, o_ref):
    ones = jnp.ones((q_ref.shape[0], DV), BF16)
    for h in range(N_HEADS):
        sl = slice(h * LANES, (h + 1) * LANES)
        v_ones = jnp.concatenate([v_ref[:, sl], ones], axis=1)
        o_ref[:, sl] = _attn_rows(q_ref[:, sl], k_ref[:, sl], v_ones, z_ref[:, sl],
                                  lam, g).astype(BF16)


def _attn_lat_kernel(q_ref, k_ref, v_ref, z_ref, kc_ref, vc_ref, lp_ref, g_ref, o_ref,
                     k_all, v_all, *, past, tc):
    @pl.when(pl.program_id(2) == 0)
    def _():
        head_rows = pl.ds(pl.program_id(1), past, stride=N_HEADS)
        k_all[0:past, :] = kc_ref[head_rows, :].astype(BF16)
        v_all[0:past, 0:DV] = vc_ref[head_rows, :].astype(BF16)
        k_all[past:, :] = k_ref[...]
        v_all[past:, 0:DV] = v_ref[...]
        v_all[:, DV:] = jnp.ones((v_all.shape[0], DV), BF16)

    lam = _lam(lp_ref)
    g = g_ref[...]
    for lo in range(0, q_ref.shape[0], tc):
        o_ref[lo:lo + tc, :] = _attn_rows(q_ref[lo:lo + tc, :], k_all[...], v_all[...],
                                          z_ref[lo:lo + tc, :], lam, g).astype(BF16)


def _attn_lat_call(p, cache_k, cache_v, lam_params, subln_g, *, batch, seq, past):
    tq, tc = 2048, 128
    nq = seq // tq
    return pl.pallas_call(
        functools.partial(_attn_lat_kernel, past=past, tc=tc),
        grid=(batch, N_HEADS, nq),
        in_specs=[
            pl.BlockSpec((tq, LANES), lambda b, h, i: (b * nq + i, QA_BLK + h)),
            pl.BlockSpec((seq, LANES), lambda b, h, i: (b, KA_BLK + h)),
            pl.BlockSpec((seq, LANES), lambda b, h, i: (b, VA_BLK + h)),
            pl.BlockSpec((tq, LANES), lambda b, h, i: (b * nq + i, ZA_BLK + h)),
            pl.BlockSpec((None, past * N_HEADS, LANES), lambda b, h, i: (b, 0, 0)),
            pl.BlockSpec((None, past * N_HEADS, LANES), lambda b, h, i: (b, 0, 0)),
            pl.BlockSpec((4, DK_A), lambda b, h, i: (0, 0)),
            pl.BlockSpec((1, DV), lambda b, h, i: (0, 0)),
        ],
        out_specs=pl.BlockSpec((tq, LANES), lambda b, h, i: (b * nq + i, h)),
        out_shape=jax.ShapeDtypeStruct((batch * seq, D_MODEL), BF16),
        scratch_shapes=[pltpu.VMEM((past + seq, LANES), BF16),
                        pltpu.VMEM((past + seq, 2 * DV), BF16)],
        compiler_params=_params(("arbitrary", "arbitrary", "arbitrary")),
        name="attn_lat",
    )(p, p, p, p, cache_k, cache_v, lam_params, subln_g)


PW, VW = 2 * DK_R, 2 * DV


def _ret_scratch(npairs, nc):
    stacked_state = (npairs, nc, 2 * PW, VW)
    return [pltpu.VMEM(stacked_state, F32),
            pltpu.VMEM(stacked_state, BF16),
            pltpu.VMEM((npairs, 2, CHUNK, CHUNK), F32),
            pltpu.VMEM((npairs, CHUNK, 2 * PW), F32),
            pltpu.VMEM((npairs, CHUNK, 2 * PW), F32)]


def _log_gammas(rd_ref, pair):
    def one(d, hh):
        r = jnp.full((1, VW), rd_ref[d, 2 * pair + hh], F32)
        return jnp.log1p(-jnp.exp2(r))
    return [[one(d, hh) for hh in range(2)] for d in range(2)]


def _ret_build_tables(rd_ref, dec_scr, qdec_scr, kdec_scr, *, npairs, pair0):
    lane_q = lax.broadcasted_iota(jnp.int32, (1, PW), 1)
    rel = (lax.broadcasted_iota(jnp.int32, (CHUNK, CHUNK), 0)
           - lax.broadcasted_iota(jnp.int32, (CHUNK, CHUNK), 1)).astype(F32)
    row_q = lax.broadcasted_iota(jnp.int32, (CHUNK, PW), 0).astype(F32)
    for pi in range(npairs):
        lg = _log_gammas(rd_ref, pair0 + pi)
        lg_q = [jnp.where(lane_q < DK_R, lg[d][0][:, :PW], lg[d][1][:, :PW])
                for d in range(2)]
        for hh in range(2):
            dec_scr[pi, hh] = (
                jnp.where(rel >= 0, jnp.exp(lg[0][hh][:, :1] * jnp.maximum(rel, 0.0)), 0.0)
                + jnp.where(rel <= 0, jnp.exp(lg[1][hh][:, :1] * jnp.maximum(-rel, 0.0)), 0.0))
        qdec_scr[pi] = jnp.concatenate([jnp.exp(lg_q[0] * (row_q + 1.0)),
                                        jnp.exp(lg_q[1] * (CHUNK - row_q))], axis=1)
        kdec_scr[pi] = jnp.concatenate([jnp.exp(lg_q[0] * (CHUNK - 1.0 - row_q)),
                                        jnp.exp(lg_q[1] * row_q)], axis=1)


def _ret_pairs(rd_ref, q_ref, k_ref, v_ref, z_ref, g_ref, s0_refs, o_ref, sout_refs,
               u_scr, s_scr, dec_scr, qdec_scr, kdec_scr, *, nc, npairs, pair0):
    cross = s0_refs is not None or nc > 1
    lane_q = lax.broadcasted_iota(jnp.int32, (1, PW), 1)
    lane_v = lax.broadcasted_iota(jnp.int32, (1, VW), 1)
    srow = lax.broadcasted_iota(jnp.int32, (2 * PW, VW), 0)
    scol = lax.broadcasted_iota(jnp.int32, (2 * PW, VW), 1)
    diag_blocks = ((srow % PW) // DK_R) == (scol // DV)

    def rows(c):
        return slice(c * CHUNK, (c + 1) * CHUNK)

    for pi in range(npairs):
        qs = slice(pi * PW, (pi + 1) * PW)
        vs = slice(pi * VW, (pi + 1) * VW)
        lg = _log_gammas(rd_ref, pair0 + pi)
        lg_v = [jnp.where(lane_v < DV, lg[d][0], lg[d][1]) for d in range(2)]
        chunk_decay = [jnp.exp(lg_v[d] * CHUNK) for d in range(2)]

        for c in range(nc):
            k = k_ref[rows(c), qs].astype(F32)
            kd = (jnp.concatenate([k, k], axis=1) * kdec_scr[pi]).astype(BF16)
            u_scr[pi, c] = jnp.where(diag_blocks, _dot_tn(kd, v_ref[rows(c), vs]), 0.0)

        for d, order in ((0, range(nc)), (1, reversed(range(nc)))):
            half = slice(d * PW, (d + 1) * PW)
            if s0_refs is None:
                s = jnp.zeros((PW, VW), F32)
            else:
                zero = jnp.zeros((DK_R, DV), F32)
                s = jnp.concatenate(
                    [jnp.concatenate([s0_refs[d][2 * pi], zero], axis=1),
                     jnp.concatenate([zero, s0_refs[d][2 * pi + 1]], axis=1)], axis=0)
            for c in order:
                if cross:
                    s_scr[pi, c, half, :] = s.astype(BF16)
                s = s * chunk_decay[d] + u_scr[pi, c, half, :]
            if sout_refs is not None:
                sout_refs[d][2 * pi] = s[:DK_R, :DV]
                sout_refs[d][2 * pi + 1] = s[DK_R:, DV:]

        g = g_ref[:, vs]
        for c in range(nc):
            q = q_ref[rows(c), qs]
            k = k_ref[rows(c), qs]
            v = v_ref[rows(c), vs]
            zero = jnp.zeros_like(q)
            q2 = jnp.concatenate([jnp.where(lane_q < DK_R, q, zero),
                                  jnp.where(lane_q >= DK_R, q, zero)], axis=0)
            a2 = _dot_nt(q2, k)
            o = jnp.concatenate(
                [_dot((a2[hh * CHUNK:(hh + 1) * CHUNK] * dec_scr[pi, hh]).astype(BF16),
                      v[:, hh * DV:(hh + 1) * DV]) for hh in range(2)], axis=1)
            if cross:
                qf = q.astype(F32)
                qq = (jnp.concatenate([qf, qf], axis=1) * qdec_scr[pi]).astype(BF16)
                o = o + _dot(qq, s_scr[pi, c])
            outs = []
            for hh in range(2):
                oh = o[:, hh * DV:(hh + 1) * DV]
                mu = jnp.mean(oh, axis=-1, keepdims=True)
                oc = oh - mu
                var = jnp.mean(oc * oc, axis=-1, keepdims=True)
                outs.append(oc * lax.rsqrt(var + LN_EPS))
            o = jnp.concatenate(outs, axis=1) * g * z_ref[rows(c), vs].astype(F32)
            o_ref[rows(c), vs] = o.astype(BF16)


def _ret_lat_kernel(rd_ref, q_ref, k_ref, v_ref, z_ref, g_ref, s0f_ref, s0b_ref, o_ref,
                    u_scr, s_scr, dec_scr, qdec_scr, kdec_scr, *, nc, npairs):
    pair0 = pl.program_id(0) * npairs

    @pl.when(pl.program_id(1) == 0)
    def _():
        _ret_build_tables(rd_ref, dec_scr, qdec_scr, kdec_scr, npairs=npairs, pair0=pair0)

    _ret_pairs(rd_ref, q_ref, k_ref, v_ref, z_ref, g_ref, (s0f_ref, s0b_ref), o_ref, None,
               u_scr, s_scr, dec_scr, qdec_scr, kdec_scr, nc=nc, npairs=npairs, pair0=pair0)


def _ret_lat_call(p, ret_decay, gn_g, s0f, s0b, *, batch, seq):
    nc = seq // CHUNK
    hps = 2
    npairs = hps // 2
    qw, vw = DK_R * hps, DV * hps
    col = lambda w, blk: pl.BlockSpec((seq, w), lambda j, b: (b, blk * LANES // w + j))
    state = pl.BlockSpec((None, hps, DK_R, DV), lambda j, b: (b, j, 0, 0))
    return pl.pallas_call(
        functools.partial(_ret_lat_kernel, nc=nc, npairs=npairs),
        grid=(N_HEADS // hps, batch),
        in_specs=[pl.BlockSpec(memory_space=pltpu.SMEM),
                  col(qw, QR_BLK), col(qw, KR_BLK), col(vw, VR_BLK), col(vw, ZR_BLK),
                  pl.BlockSpec((1, vw), lambda j, b: (0, j)), state, state],
        out_specs=pl.BlockSpec((seq, vw), lambda j, b: (b, j)),
        out_shape=jax.ShapeDtypeStruct((batch * seq, D_MODEL), BF16),
        scratch_shapes=_ret_scratch(npairs, nc),
        compiler_params=_params(("arbitrary", "arbitrary")),
        name="ret_lat",
    )(ret_decay, p, p, p, p, gn_g, s0f, s0b)


def _out_rows(oa_ref, or_ref, ga_ref, gr_ref, x_ref, gate, wpa_ref, wpr_ref, wo_ref,
              lng_ref, lnb_ref, y_ref, *, sub):
    for s in range(y_ref.shape[0] // sub):
        rows = slice(s * sub, (s + 1) * sub)
        a = _dot(oa_ref[rows, :], wpa_ref[...])
        r = _dot(or_ref[rows, :], wpr_ref[...])
        m = ga_ref[rows, :].astype(F32) * a + gr_ref[rows, :].astype(F32) * r
        out = _dot(m.astype(BF16), wo_ref[...])
        t = DEEPNORM_ALPHA * x_ref[rows, :] + gate * out
        mu = jnp.mean(t, axis=-1, keepdims=True)
        tc = t - mu
        var = jnp.mean(tc * tc, axis=-1, keepdims=True)
        y_ref[rows, :] = tc * lax.rsqrt(var + LN_EPS) * lng_ref[...] + lnb_ref[...]


def _out_lat_kernel(oa_ref, or_ref, ga_ref, gr_ref, x_ref, mod_ref, wpa_ref, wpr_ref, wo_ref,
                    lng_ref, lnb_ref, y_ref, *, sub):
    gate = mod_ref[0, :, 2 * D_MODEL:3 * D_MODEL]
    _out_rows(oa_ref, or_ref, ga_ref, gr_ref, x_ref, gate, wpa_ref, wpr_ref, wo_ref,
              lng_ref, lnb_ref, y_ref, sub=sub)


def _out_lat_call(oa, orr, p, x2d, mod3, w_pa, w_pr, w_out, ln_g, ln_b, *, seq, mod_row0):
    m = x2d.shape[0]
    tm, sub = 1024, 512
    per_seq = seq // tm
    tile = lambda j: pl.BlockSpec((tm, D_MODEL), lambda i: (i, j))
    vec = pl.BlockSpec((1, D_MODEL), lambda i: (0, 0))
    return pl.pallas_call(
        functools.partial(_out_lat_kernel, sub=sub),
        grid=(m // tm,),
        in_specs=[tile(0), tile(0), tile(GA_BLK * LANES // D_MODEL), tile(GR_BLK * LANES // D_MODEL),
                  tile(0),
                  pl.BlockSpec((1, 1, 3 * D_MODEL), lambda i: (mod_row0 + i // per_seq, 0, 0)),
                  _resident((D_MODEL, D_MODEL)), _resident((D_MODEL, D_MODEL)),
                  _resident((D_MODEL, D_MODEL)), vec, vec],
        out_specs=tile(0),
        out_shape=jax.ShapeDtypeStruct((m, D_MODEL), F32),
        compiler_params=_params(("arbitrary",)),
        name="out_lat",
    )(oa, orr, p, p, x2d, mod3, w_pa, w_pr, w_out, ln_g, ln_b)


def _ctx_kernel(x_ref, mod_ref, win_ref, wg_ref, bg_ref, lp_ref, sg_ref, rd_ref, gg_ref,
                wpa_ref, wpr_ref, wo_ref, lng_ref, lnb_ref,
                y_ref, k_ref, v_ref, sf_ref, sb_ref,
                p_scr, oa_scr, or_scr, u_scr, s_scr, dec_scr, qdec_scr, kdec_scr, *, npairs):
    @pl.when(pl.program_id(0) == 0)
    def _():
        _ret_build_tables(rd_ref, dec_scr, qdec_scr, kdec_scr, npairs=npairs, pair0=0)

    cols = lambda blk, w: p_scr.at[:, blk * LANES:blk * LANES + w]
    _project(_modulated_ln(x_ref, mod_ref), win_ref, wg_ref, bg_ref, p_scr, kv_out=(k_ref, v_ref))
    _attn_ctx_heads(cols(QA_BLK, D_MODEL), cols(KA_BLK, D_MODEL), cols(VA_BLK, D_MODEL),
                    cols(ZA_BLK, D_MODEL), _lam(lp_ref), sg_ref[...], oa_scr)
    _ret_pairs(rd_ref, cols(QR_BLK, D_MODEL // 2), cols(KR_BLK, D_MODEL // 2),
               cols(VR_BLK, D_MODEL), cols(ZR_BLK, D_MODEL), gg_ref, None, or_scr,
               (sf_ref, sb_ref), u_scr, s_scr, dec_scr, qdec_scr, kdec_scr,
               nc=x_ref.shape[0] // CHUNK, npairs=npairs, pair0=0)
    gate = mod_ref[0, :, 2 * D_MODEL:3 * D_MODEL]
    _out_rows(oa_scr, or_scr, cols(GA_BLK, D_MODEL), cols(GR_BLK, D_MODEL), x_ref, gate,
              wpa_ref, wpr_ref, wo_ref, lng_ref, lnb_ref, y_ref, sub=x_ref.shape[0])


def _ctx_call(x2d, mod3, w_in, w_gate, b_gate, lam_params, subln_g, ret_decay, gn_g,
              w_pa, w_pr, w_out, ln_g, ln_b, *, batch, seq, mod_row):
    npairs = N_HEADS // 2
    nc = seq // CHUNK
    tile = pl.BlockSpec((seq, D_MODEL), lambda b: (b, 0))
    vec = lambda w: pl.BlockSpec((1, w), lambda b: (0, 0))
    state = pl.BlockSpec((None, N_HEADS, DK_R, DV), lambda b: (b, 0, 0, 0))
    heads = pl.BlockSpec((seq * N_HEADS, LANES), lambda b: (b, 0))
    tok = jax.ShapeDtypeStruct((batch * seq, D_MODEL), F32)
    tok_heads = jax.ShapeDtypeStruct((batch * seq * N_HEADS, LANES), F32)
    st = jax.ShapeDtypeStruct((batch, N_HEADS, DK_R, DV), F32)
    return pl.pallas_call(
        functools.partial(_ctx_kernel, npairs=npairs),
        grid=(batch,),
        in_specs=[tile,
                  pl.BlockSpec((1, 1, 3 * D_MODEL), lambda b: (mod_row, 0, 0)),
                  _resident((D_MODEL, IN_SEGS * D_MODEL)), _resident((D_MODEL, 2 * D_MODEL)),
                  vec(2 * D_MODEL),
                  pl.BlockSpec((4, DK_A), lambda b: (0, 0)), vec(DV),
                  pl.BlockSpec(memory_space=pltpu.SMEM), vec(D_MODEL),
                  _resident((D_MODEL, D_MODEL)), _resident((D_MODEL, D_MODEL)),
                  _resident((D_MODEL, D_MODEL)), vec(D_MODEL), vec(D_MODEL)],
        out_specs=[tile, heads, heads, state, state],
        out_shape=[tok, tok_heads, tok_heads, st, st],
        scratch_shapes=[pltpu.VMEM((seq, P_WIDTH), BF16),
                        pltpu.VMEM((seq, D_MODEL), BF16),
                        pltpu.VMEM((seq, D_MODEL), BF16)] + _ret_scratch(npairs, nc),
        compiler_params=_params(("arbitrary",)),
        name="ctx",
    )(x2d, mod3, w_in, w_gate, b_gate, lam_params, subln_g, ret_decay, gn_g,
      w_pa, w_pr, w_out, ln_g, ln_b)


def _rope_tables(n_tokens):
    rows = n_tokens // GRID_W
    r = np.repeat(np.arange(rows, dtype=np.float32), GRID_W)
    col = np.tile(np.arange(GRID_W, dtype=np.float32), rows)
    n_freq = DK_A // 4
    inv = np.float32(ROPE_BASE) ** (-np.arange(n_freq, dtype=np.float32) / np.float32(n_freq))
    ang = np.concatenate([r[:, None] * inv, col[:, None] * inv], axis=-1).astype(np.float32)
    cos = np.repeat(np.cos(ang), 2, axis=-1)
    sin = np.repeat(np.sin(ang), 2, axis=-1)
    even = (np.arange(DK_A) % 2 == 0)[None, :]
    sin_even = np.where(even, -sin, 0.0)
    sin_odd = np.where(even, 0.0, sin)
    two = lambda t: jnp.asarray(np.concatenate([t, t], axis=-1), F32)
    return two(cos), two(sin_even), two(sin_odd)


def kernel(x_prompt, x_sample, cache_attn_k, cache_attn_v, state_ret_fwd, state_ret_bwd,
           c, c_ctx, w_mod, b_mod, w_in, lam_params, subln_g, ret_decay, ret_gn_g,
           w_pa, w_pr, w_gate, b_gate, w_out, ln_g, ln_b):
    batch, seq, _ = x_prompt.shape
    dbatch, dseq, _ = x_sample.shape
    past = cache_attn_k.shape[2]
    l = 0

    cond8 = jnp.concatenate([c_ctx[None, :], c, jnp.zeros((8 - 1 - dbatch, D_MODEL), F32)], axis=0)
    mod3 = _mod_call(cond8, w_mod[l], b_mod[l][None, :]).reshape(8, 1, 3 * D_MODEL)

    win, wg = w_in[l].astype(BF16), w_gate[l].astype(BF16)
    bg = b_gate[l][None, :]
    wpa, wpr, wo = w_pa[l].astype(BF16), w_pr[l].astype(BF16), w_out[l].astype(BF16)
    lp, sg = lam_params[l], subln_g[l][None, :]
    rd, gg = ret_decay[l], ret_gn_g[l][None, :]
    lng, lnb = ln_g[l][None, :], ln_b[l][None, :]

    xc = x_prompt.reshape(batch * seq, D_MODEL)
    y_c, k_c, v_c, sf_c, sb_c = _ctx_call(xc, mod3, win, wg, bg, lp, sg, rd, gg, wpa, wpr, wo,
                                          lng, lnb, batch=batch, seq=seq, mod_row=0)

    xs = x_sample.reshape(dbatch * dseq, D_MODEL)
    p_s = _proj_lat_call(xs, mod3, win, wg, bg, _rope_tables(dseq), seq=dseq, mod_row0=1)
    ck = cache_attn_k[:, l].reshape(dbatch, past * N_HEADS, 2 * DK_A)
    cv = cache_attn_v[:, l].reshape(dbatch, past * N_HEADS, DV)
    oa_s = _attn_lat_call(p_s, ck, cv, lp, sg, batch=dbatch, seq=dseq, past=past)
    or_s = _ret_lat_call(p_s, rd, gg, state_ret_fwd[:, l], state_ret_bwd[:, l],
                         batch=dbatch, seq=dseq)
    y_s = _out_lat_call(oa_s, or_s, p_s, xs, mod3, wpa, wpr, wo, lng, lnb, seq=dseq, mod_row0=1)

    return (y_c.reshape(batch, seq, D_MODEL),
            y_s.reshape(dbatch, dseq, D_MODEL),
            k_c.reshape(batch, 1, seq, N_HEADS, 2 * DK_A),
            v_c.reshape(batch, 1, seq, N_HEADS, DV),
            sf_c.reshape(batch, 1, N_HEADS, DK_R, DV),
            sb_c.reshape(batch, 1, N_HEADS, DK_R, DV))
```

```python
import functools
import math

import jax
import jax.numpy as jnp
import numpy as np
from jax import lax
from jax.experimental import pallas as pl
from jax.experimental.pallas import tpu as pltpu

F32 = jnp.float32
BF16 = jnp.bfloat16

D_MODEL = 1024
N_HEADS = 8
DK_A = 64
DV = 128
DK_R = 64
CHUNK = 256
GRID_W = 64
ROPE_BASE = 10000.0
MOD_EPS = 1e-6
LN_EPS = 1e-5
DEPTH = 1
DEEPNORM_ALPHA = (2.0 * DEPTH) ** 0.25
LAM_INIT = 0.8 - 0.6 * math.exp(-0.3 * 0)
LOG2E = math.log2(math.e)

LANES = 128
QA_BLK, KA_BLK, VA_BLK, ZA_BLK = 0, 8, 16, 24
QR_BLK, KR_BLK, VR_BLK, ZR_BLK = 32, 36, 40, 48
GA_BLK, GR_BLK = 56, 64
IN_SEGS = 7
P_WIDTH = (IN_SEGS + 2) * D_MODEL
VMEM_LIMIT = 56 * 1024 * 1024


def _params(sem):
    return pltpu.CompilerParams(dimension_semantics=sem, vmem_limit_bytes=VMEM_LIMIT)


def _resident(shape):
    return pl.BlockSpec(shape, lambda *_: tuple(0 for _ in shape), pipeline_mode=pl.Buffered(1))


def _silu(z):
    return z * (1.0 / (1.0 + jnp.exp(-z)))


def _dot(a, b):
    return jnp.dot(a, b, preferred_element_type=F32)


def _dot_nt(a, b):
    return lax.dot_general(a, b, (((1,), (1,)), ((), ())), preferred_element_type=F32)


def _dot_tn(a, b):
    return lax.dot_general(a, b, (((0,), (0,)), ((), ())), preferred_element_type=F32)


def _mod_kernel(cond_ref, w_ref, b_ref, o_ref):
    s = _silu(cond_ref[...])
    o_ref[...] = _dot(s.astype(BF16), w_ref[...].astype(BF16)) + b_ref[...]


def _mod_call(cond8, w_mod, b_mod):
    tn = D_MODEL
    return pl.pallas_call(
        _mod_kernel,
        grid=(3 * D_MODEL // tn,),
        in_specs=[pl.BlockSpec((8, D_MODEL), lambda j: (0, 0)),
                  pl.BlockSpec((D_MODEL, tn), lambda j: (0, j)),
                  pl.BlockSpec((1, tn), lambda j: (0, j))],
        out_specs=pl.BlockSpec((8, tn), lambda j: (0, j)),
        out_shape=jax.ShapeDtypeStruct((8, 3 * D_MODEL), F32),
        compiler_params=_params(("arbitrary",)),
        name="mod",
    )(cond8, w_mod, b_mod)


def _rope(acc, cos, sin_even, sin_odd):
    outs = []
    for hh in range(N_HEADS):
        xs = acc[:, hh * LANES:(hh + 1) * LANES]
        nxt = pltpu.roll(xs, LANES - 1, 1)
        prv = pltpu.roll(xs, 1, 1)
        outs.append(xs * cos + nxt * sin_even + prv * sin_odd)
    return jnp.concatenate(outs, axis=1)


def _modulated_ln(x_ref, mod_ref):
    x = x_ref[...]
    mu = jnp.mean(x, axis=-1, keepdims=True)
    xc = x - mu
    var = jnp.mean(xc * xc, axis=-1, keepdims=True)
    shift = mod_ref[0, :, 0:D_MODEL]
    scale = mod_ref[0, :, D_MODEL:2 * D_MODEL]
    return (xc * lax.rsqrt(var + MOD_EPS) * (1.0 + scale) + shift).astype(BF16)


def _store_heads(ref, val):
    for h in range(N_HEADS):
        ref[pl.ds(h, val.shape[0], stride=N_HEADS), :] = val[:, h * LANES:(h + 1) * LANES]


def _project(h, win_ref, wg_ref, bg_ref, p_ref, rope=None, kv_out=None):
    def seg(j):
        if j < IN_SEGS:
            return _dot(h, win_ref[:, j * D_MODEL:(j + 1) * D_MODEL])
        return _dot(h, wg_ref[:, (j - IN_SEGS) * D_MODEL:(j - IN_SEGS + 1) * D_MODEL])

    def put(j, val):
        p_ref[:, j * D_MODEL:(j + 1) * D_MODEL] = val.astype(BF16)

    qa = seg(0)
    ka = seg(1)
    if rope is not None:
        cos, se, so = (t[...] for t in rope)
        qa = _rope(qa, cos, se, so)
        ka = _rope(ka, cos, se, so)
    if kv_out is not None:
        _store_heads(kv_out[0], ka)
    put(0, qa * (DK_A ** -0.5 * LOG2E))
    put(1, ka)
    va = seg(2)
    if kv_out is not None:
        _store_heads(kv_out[1], va)
    put(2, va)
    put(3, _silu(seg(3)))
    qk = seg(4)
    put(4, jnp.concatenate([qk[:, :D_MODEL // 2], qk[:, D_MODEL // 2:] * (DK_R ** -0.5)], axis=1))
    put(5, seg(5))
    put(6, _silu(seg(6)))
    for j in (IN_SEGS, IN_SEGS + 1):
        g = seg(j) + bg_ref[:, (j - IN_SEGS) * D_MODEL:(j - IN_SEGS + 1) * D_MODEL]
        put(j, 1.0 / (1.0 + jnp.exp(-g)))


def _proj_lat_kernel(x_ref, mod_ref, win_ref, wg_ref, bg_ref, cos_ref, se_ref, so_ref, p_ref):
    _project(_modulated_ln(x_ref, mod_ref), win_ref, wg_ref, bg_ref, p_ref,
             rope=(cos_ref, se_ref, so_ref))


def _proj_lat_call(x2d, mod3, w_in, w_gate, b_gate, rope, *, seq, mod_row0):
    m = x2d.shape[0]
    tm = 256
    per_seq = seq // tm
    return pl.pallas_call(
        _proj_lat_kernel,
        grid=(m // tm,),
        in_specs=[
            pl.BlockSpec((tm, D_MODEL), lambda i: (i, 0)),
            pl.BlockSpec((1, 1, 3 * D_MODEL), lambda i: (mod_row0 + i // per_seq, 0, 0)),
            _resident((D_MODEL, IN_SEGS * D_MODEL)),
            _resident((D_MODEL, 2 * D_MODEL)),
            pl.BlockSpec((1, 2 * D_MODEL), lambda i: (0, 0)),
        ] + [pl.BlockSpec((tm, LANES), lambda i: (i % per_seq, 0))] * 3,
        out_specs=pl.BlockSpec((tm, P_WIDTH), lambda i: (i, 0)),
        out_shape=jax.ShapeDtypeStruct((m, P_WIDTH), BF16),
        compiler_params=_params(("arbitrary",)),
        name="proj_lat",
    )(x2d, mod3, w_in, w_gate, b_gate, *rope)


def _lam(lp_ref):
    lp = lp_ref[...]
    a = jnp.sum(lp[0:1] * lp[1:2], axis=-1, keepdims=True)
    b = jnp.sum(lp[2:3] * lp[3:4], axis=-1, keepdims=True)
    return jnp.exp(a) - jnp.exp(b) + LAM_INIT


def _attn_rows(q, k_all, v_ones, z, lam, g):
    lane = lax.broadcasted_iota(jnp.int32, (1, LANES), 1)
    zero = jnp.zeros_like(q)
    outs = []
    for qm in (jnp.where(lane < DK_A, q, zero), jnp.where(lane >= DK_A, q, zero)):
        s = _dot_nt(qm, k_all)
        mx = jnp.max(s, axis=-1, keepdims=True)
        p = jnp.exp2(s - mx).astype(BF16)
        ol = _dot(p, v_ones)
        outs.append(ol[:, :DV] * (1.0 / ol[:, DV:]))
    oa = outs[0] - lam * outs[1]
    oa = oa * lax.rsqrt(jnp.mean(oa * oa, axis=-1, keepdims=True) + LN_EPS)
    oa = oa * g * (1.0 - LAM_INIT)
    return oa * z.astype(F32)


def _attn_ctx_heads(q_ref, k_ref, v_ref, z_ref, lam, g, o_ref):
    ones = jnp.ones((q_ref.shape[0], DV), BF16)
    for h in range(N_HEADS):
        sl = slice(h * LANES, (h + 1) * LANES)
        v_ones = jnp.concatenate([v_ref[:, sl], ones], axis=1)
        o_ref[:, sl] = _attn_rows(q_ref[:, sl], k_ref[:, sl], v_ones, z_ref[:, sl],
                                  lam, g).astype(BF16)


def _attn_lat_kernel(q_ref, k_ref, v_ref, z_ref, kc_ref, vc_ref, lp_ref, g_ref, o_ref,
                     k_all, v_all, *, past, tc):
    @pl.when(pl.program_id(2) == 0)
    def _():
        head_rows = pl.ds(pl.program_id(1), past, stride=N_HEADS)
        k_all[0:past, :] = kc_ref[head_rows, :].astype(BF16)
        v_all[0:past, 0:DV] = vc_ref[head_rows, :].astype(BF16)
        k_all[past:, :] = k_ref[...]
        v_all[past:, 0:DV] = v_ref[...]
        v_all[:, DV:] = jnp.ones((v_all.shape[0], DV), BF16)

    lam = _lam(lp_ref)
    g = g_ref[...]
    for lo in range(0, q_ref.shape[0], tc):
        o_ref[lo:lo + tc, :] = _attn_rows(q_ref[lo:lo + tc, :], k_all[...], v_all[...],
                                          z_ref[lo:lo + tc, :], lam, g).astype(BF16)


def _attn_lat_call(p, cache_k, cache_v, lam_params, subln_g, *, batch, seq, past):
    tq, tc = 2048, 128
    nq = seq // tq
    return pl.pallas_call(
        functools.partial(_attn_lat_kernel, past=past, tc=tc),
        grid=(batch, N_HEADS, nq),
        in_specs=[
            pl.BlockSpec((tq, LANES), lambda b, h, i: (b * nq + i, QA_BLK + h)),
            pl.BlockSpec((seq, LANES), lambda b, h, i: (b, KA_BLK + h)),
            pl.BlockSpec((seq, LANES), lambda b, h, i: (b, VA_BLK + h)),
            pl.BlockSpec((tq, LANES), lambda b, h, i: (b * nq + i, ZA_BLK + h)),
            pl.BlockSpec((None, past * N_HEADS, LANES), lambda b, h, i: (b, 0, 0)),
            pl.BlockSpec((None, past * N_HEADS, LANES), lambda b, h, i: (b, 0, 0)),
            pl.BlockSpec((4, DK_A), lambda b, h, i: (0, 0)),
            pl.BlockSpec((1, DV), lambda b, h, i: (0, 0)),
        ],
        out_specs=pl.BlockSpec((tq, LANES), lambda b, h, i: (b * nq + i, h)),
        out_shape=jax.ShapeDtypeStruct((batch * seq, D_MODEL), BF16),
        scratch_shapes=[pltpu.VMEM((past + seq, LANES), BF16),
                        pltpu.VMEM((past + seq, 2 * DV), BF16)],
        compiler_params=_params(("arbitrary", "arbitrary", "arbitrary")),
        name="attn_lat",
    )(p, p, p, p, cache_k, cache_v, lam_params, subln_g)


PW, VW = 2 * DK_R, 2 * DV


def _ret_scratch(npairs, nc):
    stacked_state = (npairs, nc, 2 * PW, VW)
    return [pltpu.VMEM(stacked_state, F32),
            pltpu.VMEM(stacked_state, BF16),
            pltpu.VMEM((npairs, 2, CHUNK, CHUNK), F32),
            pltpu.VMEM((npairs, CHUNK, 2 * PW), F32),
            pltpu.VMEM((npairs, CHUNK, 2 * PW), F32)]


def _log_gammas(rd_ref, pair):
    def one(d, hh):
        r = jnp.full((1, VW), rd_ref[d, 2 * pair + hh], F32)
        return jnp.log1p(-jnp.exp2(r))
    return [[one(d, hh) for hh in range(2)] for d in range(2)]


def _ret_build_tables(rd_ref, dec_scr, qdec_scr, kdec_scr, *, npairs, pair0):
    lane_q = lax.broadcasted_iota(jnp.int32, (1, PW), 1)
    rel = (lax.broadcasted_iota(jnp.int32, (CHUNK, CHUNK), 0)
           - lax.broadcasted_iota(jnp.int32, (CHUNK, CHUNK), 1)).astype(F32)
    row_q = lax.broadcasted_iota(jnp.int32, (CHUNK, PW), 0).astype(F32)
    for pi in range(npairs):
        lg = _log_gammas(rd_ref, pair0 + pi)
        lg_q = [jnp.where(lane_q < DK_R, lg[d][0][:, :PW], lg[d][1][:, :PW])
                for d in range(2)]
        for hh in range(2):
            dec_scr[pi, hh] = (
                jnp.where(rel >= 0, jnp.exp(lg[0][hh][:, :1] * jnp.maximum(rel, 0.0)), 0.0)
                + jnp.where(rel <= 0, jnp.exp(lg[1][hh][:, :1] * jnp.maximum(-rel, 0.0)), 0.0))
        qdec_scr[pi] = jnp.concatenate([jnp.exp(lg_q[0] * (row_q + 1.0)),
                                        jnp.exp(lg_q[1] * (CHUNK - row_q))], axis=1)
        kdec_scr[pi] = jnp.concatenate([jnp.exp(lg_q[0] * (CHUNK - 1.0 - row_q)),
                                        jnp.exp(lg_q[1] * row_q)], axis=1)


def _ret_pairs(rd_ref, q_ref, k_ref, v_ref, z_ref, g_ref, s0_refs, o_ref, sout_refs,
               u_scr, s_scr, dec_scr, qdec_scr, kdec_scr, *, nc, npairs, pair0):
    cross = s0_refs is not None or nc > 1
    lane_q = lax.broadcasted_iota(jnp.int32, (1, PW), 1)
    lane_v = lax.broadcasted_iota(jnp.int32, (1, VW), 1)
    srow = lax.broadcasted_iota(jnp.int32, (2 * PW, VW), 0)
    scol = lax.broadcasted_iota(jnp.int32, (2 * PW, VW), 1)
    diag_blocks = ((srow % PW) // DK_R) == (scol // DV)

    def rows(c):
        return slice(c * CHUNK, (c + 1) * CHUNK)

    for pi in range(npairs):
        qs = slice(pi * PW, (pi + 1) * PW)
        vs = slice(pi * VW, (pi + 1) * VW)
        lg = _log_gammas(rd_ref, pair0 + pi)
        lg_v = [jnp.where(lane_v < DV, lg[d][0], lg[d][1]) for d in range(2)]
        chunk_decay = [jnp.exp(lg_v[d] * CHUNK) for d in range(2)]

        for c in range(nc):
            k = k_ref[rows(c), qs].astype(F32)
            kd = (jnp.concatenate([k, k], axis=1) * kdec_scr[pi]).astype(BF16)
            u_scr[pi, c] = jnp.where(diag_blocks, _dot_tn(kd, v_ref[rows(c), vs]), 0.0)

        for d, order in ((0, range(nc)), (1, reversed(range(nc)))):
            half = slice(d * PW, (d + 1) * PW)
            if s0_refs is None:
                s = jnp.zeros((PW, VW), F32)
            else:
                zero = jnp.zeros((DK_R, DV), F32)
                s = jnp.concatenate(
                    [jnp.concatenate([s0_refs[d][2 * pi], zero], axis=1),
                     jnp.concatenate([zero, s0_refs[d][2 * pi + 1]], axis=1)], axis=0)
            for c in order:
                if cross:
                    s_scr[pi, c, half, :] = s.astype(BF16)
                s = s * chunk_decay[d] + u_scr[pi, c, half, :]
            if sout_refs is not None:
                sout_refs[d][2 * pi] = s[:DK_R, :DV]
                sout_refs[d][2 * pi + 1] = s[DK_R:, DV:]

        g = g_ref[:, vs]
        for c in range(nc):
            q = q_ref[rows(c), qs]
            k = k_ref[rows(c), qs]
            v = v_ref[rows(c), vs]
            zero = jnp.zeros_like(q)
            q2 = jnp.concatenate([jnp.where(lane_q < DK_R, q, zero),
                                  jnp.where(lane_q >= DK_R, q, zero)], axis=0)
            a2 = _dot_nt(q2, k)
            o = jnp.concatenate(
                [_dot((a2[hh * CHUNK:(hh + 1) * CHUNK] * dec_scr[pi, hh]).astype(BF16),
                      v[:, hh * DV:(hh + 1) * DV]) for hh in range(2)], axis=1)
            if cross:
                qf = q.astype(F32)
                qq = (jnp.concatenate([qf, qf], axis=1) * qdec_scr[pi]).astype(BF16)
                o = o + _dot(qq, s_scr[pi, c])
            outs = []
            for hh in range(2):
                oh = o[:, hh * DV:(hh + 1) * DV]
                mu = jnp.mean(oh, axis=-1, keepdims=True)
                oc = oh - mu
                var = jnp.mean(oc * oc, axis=-1, keepdims=True)
                outs.append(oc * lax.rsqrt(var + LN_EPS))
            o = jnp.concatenate(outs, axis=1) * g * z_ref[rows(c), vs].astype(F32)
            o_ref[rows(c), vs] = o.astype(BF16)


def _ret_lat_kernel(rd_ref, q_ref, k_ref, v_ref, z_ref, g_ref, s0f_ref, s0b_ref, o_ref,
                    u_scr, s_scr, dec_scr, qdec_scr, kdec_scr, *, nc, npairs):
    pair0 = pl.program_id(0) * npairs

    @pl.when(pl.program_id(1) == 0)
    def _():
        _ret_build_tables(rd_ref, dec_scr, qdec_scr, kdec_scr, npairs=npairs, pair0=pair0)

    _ret_pairs(rd_ref, q_ref, k_ref, v_ref, z_ref, g_ref, (s0f_ref, s0b_ref), o_ref, None,
               u_scr, s_scr, dec_scr, qdec_scr, kdec_scr, nc=nc, npairs=npairs, pair0=pair0)


def _ret_lat_call(p, ret_decay, gn_g, s0f, s0b, *, batch, seq):
    nc = seq // CHUNK
    hps = 2
    npairs = hps // 2
    qw, vw = DK_R * hps, DV * hps
    col = lambda w, blk: pl.BlockSpec((seq, w), lambda j, b: (b, blk * LANES // w + j))
    state = pl.BlockSpec((None, hps, DK_R, DV), lambda j, b: (b, j, 0, 0))
    return pl.pallas_call(
        functools.partial(_ret_lat_kernel, nc=nc, npairs=npairs),
        grid=(N_HEADS // hps, batch),
        in_specs=[pl.BlockSpec(memory_space=pltpu.SMEM),
                  col(qw, QR_BLK), col(qw, KR_BLK), col(vw, VR_BLK), col(vw, ZR_BLK),
                  pl.BlockSpec((1, vw), lambda j, b: (0, j)), state, state],
        out_specs=pl.BlockSpec((seq, vw), lambda j, b: (b, j)),
        out_shape=jax.ShapeDtypeStruct((batch * seq, D_MODEL), BF16),
        scratch_shapes=_ret_scratch(npairs, nc),
        compiler_params=_params(("arbitrary", "arbitrary")),
        name="ret_lat",
    )(ret_decay, p, p, p, p, gn_g, s0f, s0b)


def _out_rows(oa_ref, or_ref, ga_ref, gr_ref, x_ref, gate, wpa_ref, wpr_ref, wo_ref,
              lng_ref, lnb_ref, y_ref, *, sub):
    for s in range(y_ref.shape[0] // sub):
        rows = slice(s * sub, (s + 1) * sub)
        a = _dot(oa_ref[rows, :], wpa_ref[...])
        r = _dot(or_ref[rows, :], wpr_ref[...])
        m = ga_ref[rows, :].astype(F32) * a + gr_ref[rows, :].astype(F32) * r
        out = _dot(m.astype(BF16), wo_ref[...])
        t = DEEPNORM_ALPHA * x_ref[rows, :] + gate * out
        mu = jnp.mean(t, axis=-1, keepdims=True)
        tc = t - mu
        var = jnp.mean(tc * tc, axis=-1, keepdims=True)
        y_ref[rows, :] = tc * lax.rsqrt(var + LN_EPS) * lng_ref[...] + lnb_ref[...]


def _out_lat_kernel(oa_ref, or_ref, ga_ref, gr_ref, x_ref, mod_ref, wpa_ref, wpr_ref, wo_ref,
                    lng_ref, lnb_ref, y_ref, *, sub):
    gate = mod_ref[0, :, 2 * D_MODEL:3 * D_MODEL]
    _out_rows(oa_ref, or_ref, ga_ref, gr_ref, x_ref, gate, wpa_ref, wpr_ref, wo_ref,
              lng_ref, lnb_ref, y_ref, sub=sub)


def _out_lat_call(oa, orr, p, x2d, mod3, w_pa, w_pr, w_out, ln_g, ln_b, *, seq, mod_row0):
    m = x2d.shape[0]
    tm, sub = 512, 512
    per_seq = seq // tm
    tile = lambda j: pl.BlockSpec((tm, D_MODEL), lambda i: (i, j))
    vec = pl.BlockSpec((1, D_MODEL), lambda i: (0, 0))
    return pl.pallas_call(
        functools.partial(_out_lat_kernel, sub=sub),
        grid=(m // tm,),
        in_specs=[tile(0), tile(0), tile(GA_BLK * LANES // D_MODEL), tile(GR_BLK * LANES // D_MODEL),
                  tile(0),
                  pl.BlockSpec((1, 1, 3 * D_MODEL), lambda i: (mod_row0 + i // per_seq, 0, 0)),
                  _resident((D_MODEL, D_MODEL)), _resident((D_MODEL, D_MODEL)),
                  _resident((D_MODEL, D_MODEL)), vec, vec],
        out_specs=tile(0),
        out_shape=jax.ShapeDtypeStruct((m, D_MODEL), F32),
        compiler_params=_params(("arbitrary",)),
        name="out_lat",
    )(oa, orr, p, p, x2d, mod3, w_pa, w_pr, w_out, ln_g, ln_b)


def _ctx_kernel(x_ref, mod_ref, win_ref, wg_ref, bg_ref, lp_ref, sg_ref, rd_ref, gg_ref,
                wpa_ref, wpr_ref, wo_ref, lng_ref, lnb_ref,
                y_ref, k_ref, v_ref, sf_ref, sb_ref,
                p_scr, oa_scr, or_scr, u_scr, s_scr, dec_scr, qdec_scr, kdec_scr, *, npairs):
    @pl.when(pl.program_id(0) == 0)
    def _():
        _ret_build_tables(rd_ref, dec_scr, qdec_scr, kdec_scr, npairs=npairs, pair0=0)

    cols = lambda blk, w: p_scr.at[:, blk * LANES:blk * LANES + w]
    _project(_modulated_ln(x_ref, mod_ref), win_ref, wg_ref, bg_ref, p_scr, kv_out=(k_ref, v_ref))
    _attn_ctx_heads(cols(QA_BLK, D_MODEL), cols(KA_BLK, D_MODEL), cols(VA_BLK, D_MODEL),
                    cols(ZA_BLK, D_MODEL), _lam(lp_ref), sg_ref[...], oa_scr)
    _ret_pairs(rd_ref, cols(QR_BLK, D_MODEL // 2), cols(KR_BLK, D_MODEL // 2),
               cols(VR_BLK, D_MODEL), cols(ZR_BLK, D_MODEL), gg_ref, None, or_scr,
               (sf_ref, sb_ref), u_scr, s_scr, dec_scr, qdec_scr, kdec_scr,
               nc=x_ref.shape[0] // CHUNK, npairs=npairs, pair0=0)
    gate = mod_ref[0, :, 2 * D_MODEL:3 * D_MODEL]
    _out_rows(oa_scr, or_scr, cols(GA_BLK, D_MODEL), cols(GR_BLK, D_MODEL), x_ref, gate,
              wpa_ref, wpr_ref, wo_ref, lng_ref, lnb_ref, y_ref, sub=x_ref.shape[0])


def _ctx_call(x2d, mod3, w_in, w_gate, b_gate, lam_params, subln_g, ret_decay, gn_g,
              w_pa, w_pr, w_out, ln_g, ln_b, *, batch, seq, mod_row):
    npairs = N_HEADS // 2
    nc = seq // CHUNK
    tile = pl.BlockSpec((seq, D_MODEL), lambda b: (b, 0))
    vec = lambda w: pl.BlockSpec((1, w), lambda b: (0, 0))
    state = pl.BlockSpec((None, N_HEADS, DK_R, DV), lambda b: (b, 0, 0, 0))
    heads = pl.BlockSpec((seq * N_HEADS, LANES), lambda b: (b, 0))
    tok = jax.ShapeDtypeStruct((batch * seq, D_MODEL), F32)
    tok_heads = jax.ShapeDtypeStruct((batch * seq * N_HEADS, LANES), F32)
    st = jax.ShapeDtypeStruct((batch, N_HEADS, DK_R, DV), F32)
    return pl.pallas_call(
        functools.partial(_ctx_kernel, npairs=npairs),
        grid=(batch,),
        in_specs=[tile,
                  pl.BlockSpec((1, 1, 3 * D_MODEL), lambda b: (mod_row, 0, 0)),
                  _resident((D_MODEL, IN_SEGS * D_MODEL)), _resident((D_MODEL, 2 * D_MODEL)),
                  vec(2 * D_MODEL),
                  pl.BlockSpec((4, DK_A), lambda b: (0, 0)), vec(DV),
                  pl.BlockSpec(memory_space=pltpu.SMEM), vec(D_MODEL),
                  _resident((D_MODEL, D_MODEL)), _resident((D_MODEL, D_MODEL)),
                  _resident((D_MODEL, D_MODEL)), vec(D_MODEL), vec(D_MODEL)],
        out_specs=[tile, heads, heads, state, state],
        out_shape=[tok, tok_heads, tok_heads, st, st],
        scratch_shapes=[pltpu.VMEM((seq, P_WIDTH), BF16),
                        pltpu.VMEM((seq, D_MODEL), BF16),
                        pltpu.VMEM((seq, D_MODEL), BF16)] + _ret_scratch(npairs, nc),
        compiler_params=_params(("arbitrary",)),
        name="ctx",
    )(x2d, mod3, w_in, w_gate, b_gate, lam_params, subln_g, ret_decay, gn_g,
      w_pa, w_pr, w_out, ln_g, ln_b)


def _rope_tables(n_tokens):
    rows = n_tokens // GRID_W
    r = np.repeat(np.arange(rows, dtype=np.float32), GRID_W)
    col = np.tile(np.arange(GRID_W, dtype=np.float32), rows)
    n_freq = DK_A // 4
    inv = np.float32(ROPE_BASE) ** (-np.arange(n_freq, dtype=np.float32) / np.float32(n_freq))
    ang = np.concatenate([r[:, None] * inv, col[:, None] * inv], axis=-1).astype(np.float32)
    cos = np.repeat(np.cos(ang), 2, axis=-1)
    sin = np.repeat(np.sin(ang), 2, axis=-1)
    even = (np.arange(DK_A) % 2 == 0)[None, :]
    sin_even = np.where(even, -sin, 0.0)
    sin_odd = np.where(even, 0.0, sin)
    two = lambda t: jnp.asarray(np.concatenate([t, t], axis=-1), F32)
    return two(cos), two(sin_even), two(sin_odd)


def kernel(x_prompt, x_sample, cache_attn_k, cache_attn_v, state_ret_fwd, state_ret_bwd,
           c, c_ctx, w_mod, b_mod, w_in, lam_params, subln_g, ret_decay, ret_gn_g,
           w_pa, w_pr, w_gate, b_gate, w_out, ln_g, ln_b):
    batch, seq, _ = x_prompt.shape
    dbatch, dseq, _ = x_sample.shape
    past = cache_attn_k.shape[2]
    l = 0

    cond8 = jnp.concatenate([c_ctx[None, :], c, jnp.zeros((8 - 1 - dbatch, D_MODEL), F32)], axis=0)
    mod3 = _mod_call(cond8, w_mod[l], b_mod[l][None, :]).reshape(8, 1, 3 * D_MODEL)

    win, wg = w_in[l].astype(BF16), w_gate[l].astype(BF16)
    bg = b_gate[l][None, :]
    wpa, wpr, wo = w_pa[l].astype(BF16), w_pr[l].astype(BF16), w_out[l].astype(BF16)
    lp, sg = lam_params[l], subln_g[l][None, :]
    rd, gg = ret_decay[l], ret_gn_g[l][None, :]
    lng, lnb = ln_g[l][None, :], ln_b[l][None, :]

    xc = x_prompt.reshape(batch * seq, D_MODEL)
    y_c, k_c, v_c, sf_c, sb_c = _ctx_call(xc, mod3, win, wg, bg, lp, sg, rd, gg, wpa, wpr, wo,
                                          lng, lnb, batch=batch, seq=seq, mod_row=0)

    xs = x_sample.reshape(dbatch * dseq, D_MODEL)
    p_s = _proj_lat_call(xs, mod3, win, wg, bg, _rope_tables(dseq), seq=dseq, mod_row0=1)
    ck = cache_attn_k[:, l].reshape(dbatch, past * N_HEADS, 2 * DK_A)
    cv = cache_attn_v[:, l].reshape(dbatch, past * N_HEADS, DV)
    oa_s = _attn_lat_call(p_s, ck, cv, lp, sg, batch=dbatch, seq=dseq, past=past)
    or_s = _ret_lat_call(p_s, rd, gg, state_ret_fwd[:, l], state_ret_bwd[:, l],
                         batch=dbatch, seq=dseq)
    y_s = _out_lat_call(oa_s, or_s, p_s, xs, mod3, wpa, wpr, wo, lng, lnb, seq=dseq, mod_row0=1)

    return (y_c.reshape(batch, seq, D_MODEL),
            y_s.reshape(dbatch, dseq, D_MODEL),
            k_c.reshape(batch, 1, seq, N_HEADS, 2 * DK_A),
            v_c.reshape(batch, 1, seq, N_HEADS, DV),
            sf_c.reshape(batch, 1, N_HEADS, DK_R, DV),
            sb_c.reshape(batch, 1, N_HEADS, DK_R, DV))
```

```python
import functools
import math

import jax
import jax.numpy as jnp
import numpy as np
from jax import lax
from jax.experimental import pallas as pl
from jax.experimental.pallas import tpu as pltpu

F32 = jnp.float32
BF16 = jnp.bfloat16

D_MODEL = 1024
N_HEADS = 8
DK_A = 64
DV = 128
DK_R = 64
CHUNK = 256
GRID_W = 64
ROPE_BASE = 10000.0
MOD_EPS = 1e-6
LN_EPS = 1e-5
DEPTH = 1
DEEPNORM_ALPHA = (2.0 * DEPTH) ** 0.25
LAM_INIT = 0.8 - 0.6 * math.exp(-0.3 * 0)
LOG2E = math.log2(math.e)

LANES = 128
QA_BLK, KA_BLK, VA_BLK, ZA_BLK = 0, 8, 16, 24
QR_BLK, KR_BLK, VR_BLK, ZR_BLK = 32, 36, 40, 48
GA_BLK, GR_BLK = 56, 64
IN_SEGS = 7
P_WIDTH = (IN_SEGS + 2) * D_MODEL
VMEM_LIMIT = 56 * 1024 * 1024


def _params(sem):
    return pltpu.CompilerParams(dimension_semantics=sem, vmem_limit_bytes=VMEM_LIMIT)


def _resident(shape):
    return pl.BlockSpec(shape, lambda *_: tuple(0 for _ in shape), pipeline_mode=pl.Buffered(1))


def _silu(z):
    return z * (1.0 / (1.0 + jnp.exp(-z)))


def _dot(a, b):
    return jnp.dot(a, b, preferred_element_type=F32)


def _dot_nt(a, b):
    return lax.dot_general(a, b, (((1,), (1,)), ((), ())), preferred_element_type=F32)


def _dot_tn(a, b):
    return lax.dot_general(a, b, (((0,), (0,)), ((), ())), preferred_element_type=F32)


def _mod_kernel(cond_ref, w_ref, b_ref, o_ref):
    s = _silu(cond_ref[...])
    o_ref[...] = _dot(s.astype(BF16), w_ref[...].astype(BF16)) + b_ref[...]


def _mod_call(cond8, w_mod, b_mod):
    tn = D_MODEL
    return pl.pallas_call(
        _mod_kernel,
        grid=(3 * D_MODEL // tn,),
        in_specs=[pl.BlockSpec((8, D_MODEL), lambda j: (0, 0)),
                  pl.BlockSpec((D_MODEL, tn), lambda j: (0, j)),
                  pl.BlockSpec((1, tn), lambda j: (0, j))],
        out_specs=pl.BlockSpec((8, tn), lambda j: (0, j)),
        out_shape=jax.ShapeDtypeStruct((8, 3 * D_MODEL), F32),
        compiler_params=_params(("arbitrary",)),
        name="mod",
    )(cond8, w_mod, b_mod)


def _rope(acc, cos, sin_even, sin_odd):
    outs = []
    for hh in range(N_HEADS):
        xs = acc[:, hh * LANES:(hh + 1) * LANES]
        nxt = pltpu.roll(xs, LANES - 1, 1)
        prv = pltpu.roll(xs, 1, 1)
        outs.append(xs * cos + nxt * sin_even + prv * sin_odd)
    return jnp.concatenate(outs, axis=1)


def _modulated_ln(x_ref, mod_ref):
    x = x_ref[...]
    mu = jnp.mean(x, axis=-1, keepdims=True)
    xc = x - mu
    var = jnp.mean(xc * xc, axis=-1, keepdims=True)
    shift = mod_ref[0, :, 0:D_MODEL]
    scale = mod_ref[0, :, D_MODEL:2 * D_MODEL]
    return (xc * lax.rsqrt(var + MOD_EPS) * (1.0 + scale) + shift).astype(BF16)


def _store_heads(ref, val):
    for h in range(N_HEADS):
        ref[pl.ds(h, val.shape[0], stride=N_HEADS), :] = val[:, h * LANES:(h + 1) * LANES]


def _project(h, win_ref, wg_ref, bg_ref, p_ref, rope=None, kv_out=None):
    def seg(j):
        if j < IN_SEGS:
            return _dot(h, win_ref[:, j * D_MODEL:(j + 1) * D_MODEL])
        return _dot(h, wg_ref[:, (j - IN_SEGS) * D_MODEL:(j - IN_SEGS + 1) * D_MODEL])

    def put(j, val):
        p_ref[:, j * D_MODEL:(j + 1) * D_MODEL] = val.astype(BF16)

    qa = seg(0)
    ka = seg(1)
    if rope is not None:
        cos, se, so = (t[...] for t in rope)
        qa = _rope(qa, cos, se, so)
        ka = _rope(ka, cos, se, so)
    if kv_out is not None:
        _store_heads(kv_out[0], ka)
    put(0, qa * (DK_A ** -0.5 * LOG2E))
    put(1, ka)
    va = seg(2)
    if kv_out is not None:
        _store_heads(kv_out[1], va)
    put(2, va)
    put(3, _silu(seg(3)))
    qk = seg(4)
    put(4, jnp.concatenate([qk[:, :D_MODEL // 2], qk[:, D_MODEL // 2:] * (DK_R ** -0.5)], axis=1))
    put(5, seg(5))
    put(6, _silu(seg(6)))
    for j in (IN_SEGS, IN_SEGS + 1):
        g = seg(j) + bg_ref[:, (j - IN_SEGS) * D_MODEL:(j - IN_SEGS + 1) * D_MODEL]
        put(j, 1.0 / (1.0 + jnp.exp(-g)))


def _proj_lat_kernel(x_ref, mod_ref, win_ref, wg_ref, bg_ref, cos_ref, se_ref, so_ref, p_ref):
    _project(_modulated_ln(x_ref, mod_ref), win_ref, wg_ref, bg_ref, p_ref,
             rope=(cos_ref, se_ref, so_ref))


def _proj_lat_call(x2d, mod3, w_in, w_gate, b_gate, rope, *, seq, mod_row0):
    m = x2d.shape[0]
    tm = 256
    per_seq = seq // tm
    return pl.pallas_call(
        _proj_lat_kernel,
        grid=(m // tm,),
        in_specs=[
            pl.BlockSpec((tm, D_MODEL), lambda i: (i, 0)),
            pl.BlockSpec((1, 1, 3 * D_MODEL), lambda i: (mod_row0 + i // per_seq, 0, 0)),
            _resident((D_MODEL, IN_SEGS * D_MODEL)),
            _resident((D_MODEL, 2 * D_MODEL)),
            pl.BlockSpec((1, 2 * D_MODEL), lambda i: (0, 0)),
        ] + [pl.BlockSpec((tm, LANES), lambda i: (i % per_seq, 0))] * 3,
        out_specs=pl.BlockSpec((tm, P_WIDTH), lambda i: (i, 0)),
        out_shape=jax.ShapeDtypeStruct((m, P_WIDTH), BF16),
        compiler_params=_params(("arbitrary",)),
        name="proj_lat",
    )(x2d, mod3, w_in, w_gate, b_gate, *rope)


def _lam(lp_ref):
    lp = lp_ref[...]
    a = jnp.sum(lp[0:1] * lp[1:2], axis=-1, keepdims=True)
    b = jnp.sum(lp[2:3] * lp[3:4], axis=-1, keepdims=True)
    return jnp.exp(a) - jnp.exp(b) + LAM_INIT


def _attn_rows(q, k_all, v_ones, z, lam, g):
    lane = lax.broadcasted_iota(jnp.int32, (1, LANES), 1)
    zero = jnp.zeros_like(q)
    outs = []
    for qm in (jnp.where(lane < DK_A, q, zero), jnp.where(lane >= DK_A, q, zero)):
        s = _dot_nt(qm, k_all)
        mx = jnp.max(s, axis=-1, keepdims=True)
        p = jnp.exp2(s - mx).astype(BF16)
        ol = _dot(p, v_ones)
        outs.append(ol[:, :DV] * (1.0 / ol[:, DV:]))
    oa = outs[0] - lam * outs[1]
    oa = oa * lax.rsqrt(jnp.mean(oa * oa, axis=-1, keepdims=True) + LN_EPS)
    oa = oa * g * (1.0 - LAM_INIT)
    return oa * z.astype(F32)


def _attn_ctx_heads(q_ref, k_ref, v_ref, z_ref, lam, g, o_ref):
    ones = jnp.ones((q_ref.shape[0], DV), BF16)
    for h in range(N_HEADS):
        sl = slice(h * LANES, (h + 1) * LANES)
        v_ones = jnp.concatenate([v_ref[:, sl], ones], axis=1)
        o_ref[:, sl] = _attn_rows(q_ref[:, sl], k_ref[:, sl], v_ones, z_ref[:, sl],
                                  lam, g).astype(BF16)


def _attn_lat_kernel(q_ref, k_ref, v_ref, z_ref, kc_ref, vc_ref, lp_ref, g_ref, o_ref,
                     k_all, v_all, *, past, tc):
    @pl.when(pl.program_id(2) == 0)
    def _():
        head_rows = pl.ds(pl.program_id(1), past, stride=N_HEADS)
        k_all[0:past, :] = kc_ref[head_rows, :].astype(BF16)
        v_all[0:past, 0:DV] = vc_ref[head_rows, :].astype(BF16)
        k_all[past:, :] = k_ref[...]
        v_all[past:, 0:DV] = v_ref[...]
        v_all[:, DV:] = jnp.ones((v_all.shape[0], DV), BF16)

    lam = _lam(lp_ref)
    g = g_ref[...]
    for lo in range(0, q_ref.shape[0], tc):
        o_ref[lo:lo + tc, :] = _attn_rows(q_ref[lo:lo + tc, :], k_all[...], v_all[...],
                                          z_ref[lo:lo + tc, :], lam, g).astype(BF16)


def _attn_lat_call(p, cache_k, cache_v, lam_params, subln_g, *, batch, seq, past):
    tq, tc = 2048, 128
    nq = seq // tq
    return pl.pallas_call(
        functools.partial(_attn_lat_kernel, past=past, tc=tc),
        grid=(batch, N_HEADS, nq),
        in_specs=[
            pl.BlockSpec((tq, LANES), lambda b, h, i: (b * nq + i, QA_BLK + h)),
            pl.BlockSpec((seq, LANES), lambda b, h, i: (b, KA_BLK + h)),
            pl.BlockSpec((seq, LANES), lambda b, h, i: (b, VA_BLK + h)),
            pl.BlockSpec((tq, LANES), lambda b, h, i: (b * nq + i, ZA_BLK + h)),
            pl.BlockSpec((None, past * N_HEADS, LANES), lambda b, h, i: (b, 0, 0)),
            pl.BlockSpec((None, past * N_HEADS, LANES), lambda b, h, i: (b, 0, 0)),
            pl.BlockSpec((4, DK_A), lambda b, h, i: (0, 0)),
            pl.BlockSpec((1, DV), lambda b, h, i: (0, 0)),
        ],
        out_specs=pl.BlockSpec((tq, LANES), lambda b, h, i: (b * nq + i, h)),
        out_shape=jax.ShapeDtypeStruct((batch * seq, D_MODEL), BF16),
        scratch_shapes=[pltpu.VMEM((past + seq, LANES), BF16),
                        pltpu.VMEM((past + seq, 2 * DV), BF16)],
        compiler_params=_params(("arbitrary", "arbitrary", "arbitrary")),
        name="attn_lat",
    )(p, p, p, p, cache_k, cache_v, lam_params, subln_g)


PW, VW = 2 * DK_R, 2 * DV


def _ret_scratch(npairs, nc):
    stacked_state = (npairs, nc, 2 * PW, VW)
    return [pltpu.VMEM(stacked_state, F32),
            pltpu.VMEM(stacked_state, BF16),
            pltpu.VMEM((npairs, 2, CHUNK, CHUNK), F32),
            pltpu.VMEM((npairs, CHUNK, 2 * PW), F32),
            pltpu.VMEM((npairs, CHUNK, 2 * PW), F32)]


def _log_gammas(rd_ref, pair):
    def one(d, hh):
        r = jnp.full((1, VW), rd_ref[d, 2 * pair + hh], F32)
        return jnp.log1p(-jnp.exp2(r))
    return [[one(d, hh) for hh in range(2)] for d in range(2)]


def _ret_build_tables(rd_ref, dec_scr, qdec_scr, kdec_scr, *, npairs, pair0):
    lane_q = lax.broadcasted_iota(jnp.int32, (1, PW), 1)
    rel = (lax.broadcasted_iota(jnp.int32, (CHUNK, CHUNK), 0)
           - lax.broadcasted_iota(jnp.int32, (CHUNK, CHUNK), 1)).astype(F32)
    row_q = lax.broadcasted_iota(jnp.int32, (CHUNK, PW), 0).astype(F32)
    for pi in range(npairs):
        lg = _log_gammas(rd_ref, pair0 + pi)
        lg_q = [jnp.where(lane_q < DK_R, lg[d][0][:, :PW], lg[d][1][:, :PW])
                for d in range(2)]
        for hh in range(2):
            dec_scr[pi, hh] = (
                jnp.where(rel >= 0, jnp.exp(lg[0][hh][:, :1] * jnp.maximum(rel, 0.0)), 0.0)
                + jnp.where(rel <= 0, jnp.exp(lg[1][hh][:, :1] * jnp.maximum(-rel, 0.0)), 0.0))
        qdec_scr[pi] = jnp.concatenate([jnp.exp(lg_q[0] * (row_q + 1.0)),
                                        jnp.exp(lg_q[1] * (CHUNK - row_q))], axis=1)
        kdec_scr[pi] = jnp.concatenate([jnp.exp(lg_q[0] * (CHUNK - 1.0 - row_q)),
                                        jnp.exp(lg_q[1] * row_q)], axis=1)


def _ret_pairs(rd_ref, q_ref, k_ref, v_ref, z_ref, g_ref, s0_refs, o_ref, sout_refs,
               u_scr, s_scr, dec_scr, qdec_scr, kdec_scr, *, nc, npairs, pair0):
    cross = s0_refs is not None or nc > 1
    lane_q = lax.broadcasted_iota(jnp.int32, (1, PW), 1)
    lane_v = lax.broadcasted_iota(jnp.int32, (1, VW), 1)
    srow = lax.broadcasted_iota(jnp.int32, (2 * PW, VW), 0)
    scol = lax.broadcasted_iota(jnp.int32, (2 * PW, VW), 1)
    diag_blocks = ((srow % PW) // DK_R) == (scol // DV)

    def rows(c):
        return slice(c * CHUNK, (c + 1) * CHUNK)

    for pi in range(npairs):
        qs = slice(pi * PW, (pi + 1) * PW)
        vs = slice(pi * VW, (pi + 1) * VW)
        lg = _log_gammas(rd_ref, pair0 + pi)
        lg_v = [jnp.where(lane_v < DV, lg[d][0], lg[d][1]) for d in range(2)]
        chunk_decay = [jnp.exp(lg_v[d] * CHUNK) for d in range(2)]

        for c in range(nc):
            k = k_ref[rows(c), qs].astype(F32)
            kd = (jnp.concatenate([k, k], axis=1) * kdec_scr[pi]).astype(BF16)
            u_scr[pi, c] = jnp.where(diag_blocks, _dot_tn(kd, v_ref[rows(c), vs]), 0.0)

        for d, order in ((0, range(nc)), (1, reversed(range(nc)))):
            half = slice(d * PW, (d + 1) * PW)
            if s0_refs is None:
                s = jnp.zeros((PW, VW), F32)
            else:
                zero = jnp.zeros((DK_R, DV), F32)
                s = jnp.concatenate(
                    [jnp.concatenate([s0_refs[d][2 * pi], zero], axis=1),
                     jnp.concatenate([zero, s0_refs[d][2 * pi + 1]], axis=1)], axis=0)
            for c in order:
                if cross:
                    s_scr[pi, c, half, :] = s.astype(BF16)
                s = s * chunk_decay[d] + u_scr[pi, c, half, :]
            if sout_refs is not None:
                sout_refs[d][2 * pi] = s[:DK_R, :DV]
                sout_refs[d][2 * pi + 1] = s[DK_R:, DV:]

        g = g_ref[:, vs]
        for c in range(nc):
            q = q_ref[rows(c), qs]
            k = k_ref[rows(c), qs]
            v = v_ref[rows(c), vs]
            zero = jnp.zeros_like(q)
            q2 = jnp.concatenate([jnp.where(lane_q < DK_R, q, zero),
                                  jnp.where(lane_q >= DK_R, q, zero)], axis=0)
            a2 = _dot_nt(q2, k)
            o = jnp.concatenate(
                [_dot((a2[hh * CHUNK:(hh + 1) * CHUNK] * dec_scr[pi, hh]).astype(BF16),
                      v[:, hh * DV:(hh + 1) * DV]) for hh in range(2)], axis=1)
            if cross:
                qf = q.astype(F32)
                qq = (jnp.concatenate([qf, qf], axis=1) * qdec_scr[pi]).astype(BF16)
                o = o + _dot(qq, s_scr[pi, c])
            outs = []
            for hh in range(2):
                oh = o[:, hh * DV:(hh + 1) * DV]
                mu = jnp.mean(oh, axis=-1, keepdims=True)
                oc = oh - mu
                var = jnp.mean(oc * oc, axis=-1, keepdims=True)
                outs.append(oc * lax.rsqrt(var + LN_EPS))
            o = jnp.concatenate(outs, axis=1) * g * z_ref[rows(c), vs].astype(F32)
            o_ref[rows(c), vs] = o.astype(BF16)


def _ret_lat_kernel(rd_ref, q_ref, k_ref, v_ref, z_ref, g_ref, s0f_ref, s0b_ref, o_ref,
                    u_scr, s_scr, dec_scr, qdec_scr, kdec_scr, *, nc, npairs):
    pair0 = pl.program_id(0) * npairs

    @pl.when(pl.program_id(1) == 0)
    def _():
        _ret_build_tables(rd_ref, dec_scr, qdec_scr, kdec_scr, npairs=npairs, pair0=pair0)

    _ret_pairs(rd_ref, q_ref, k_ref, v_ref, z_ref, g_ref, (s0f_ref, s0b_ref), o_ref, None,
               u_scr, s_scr, dec_scr, qdec_scr, kdec_scr, nc=nc, npairs=npairs, pair0=pair0)


def _ret_lat_call(p, ret_decay, gn_g, s0f, s0b, *, batch, seq):
    nc = seq // CHUNK
    hps = 2
    npairs = hps // 2
    qw, vw = DK_R * hps, DV * hps
    col = lambda w, blk: pl.BlockSpec((seq, w), lambda j, b: (b, blk * LANES // w + j))
    state = pl.BlockSpec((None, hps, DK_R, DV), lambda j, b: (b, j, 0, 0))
    return pl.pallas_call(
        functools.partial(_ret_lat_kernel, nc=nc, npairs=npairs),
        grid=(N_HEADS // hps, batch),
        in_specs=[pl.BlockSpec(memory_space=pltpu.SMEM),
                  col(qw, QR_BLK), col(qw, KR_BLK), col(vw, VR_BLK), col(vw, ZR_BLK),
                  pl.BlockSpec((1, vw), lambda j, b: (0, j)), state, state],
        out_specs=pl.BlockSpec((seq, vw), lambda j, b: (b, j)),
        out_shape=jax.ShapeDtypeStruct((batch * seq, D_MODEL), BF16),
        scratch_shapes=_ret_scratch(npairs, nc),
        compiler_params=_params(("arbitrary", "arbitrary")),
        name="ret_lat",
    )(ret_decay, p, p, p, p, gn_g, s0f, s0b)


def _out_rows(oa_ref, or_ref, ga_ref, gr_ref, x_ref, gate, wpa_ref, wpr_ref, wo_ref,
              lng_ref, lnb_ref, y_ref, *, sub):
    for s in range(y_ref.shape[0] // sub):
        rows = slice(s * sub, (s + 1) * sub)
        a = _dot(oa_ref[rows, :], wpa_ref[...])
        r = _dot(or_ref[rows, :], wpr_ref[...])
        m = ga_ref[rows, :].astype(F32) * a + gr_ref[rows, :].astype(F32) * r
        out = _dot(m.astype(BF16), wo_ref[...])
        t = DEEPNORM_ALPHA * x_ref[rows, :] + gate * out
        mu = jnp.mean(t, axis=-1, keepdims=True)
        tc = t - mu
        var = jnp.mean(tc * tc, axis=-1, keepdims=True)
        y_ref[rows, :] = tc * lax.rsqrt(var + LN_EPS) * lng_ref[...] + lnb_ref[...]


def _out_lat_kernel(oa_ref, or_ref, ga_ref, gr_ref, x_ref, mod_ref, wpa_ref, wpr_ref, wo_ref,
                    lng_ref, lnb_ref, y_ref, *, sub):
    gate = mod_ref[0, :, 2 * D_MODEL:3 * D_MODEL]
    _out_rows(oa_ref, or_ref, ga_ref, gr_ref, x_ref, gate, wpa_ref, wpr_ref, wo_ref,
              lng_ref, lnb_ref, y_ref, sub=sub)


def _out_lat_call(oa, orr, p, x2d, mod3, w_pa, w_pr, w_out, ln_g, ln_b, *, seq, mod_row0):
    m = x2d.shape[0]
    tm, sub = 512, 512
    per_seq = seq // tm
    tile = lambda j: pl.BlockSpec((tm, D_MODEL), lambda i: (i, j))
    vec = pl.BlockSpec((1, D_MODEL), lambda i: (0, 0))
    return pl.pallas_call(
        functools.partial(_out_lat_kernel, sub=sub),
        grid=(m // tm,),
        in_specs=[tile(0), tile(0), tile(GA_BLK * LANES // D_MODEL), tile(GR_BLK * LANES // D_MODEL),
                  tile(0),
                  pl.BlockSpec((1, 1, 3 * D_MODEL), lambda i: (mod_row0 + i // per_seq, 0, 0)),
                  _resident((D_MODEL, D_MODEL)), _resident((D_MODEL, D_MODEL)),
                  _resident((D_MODEL, D_MODEL)), vec, vec],
        out_specs=tile(0),
        out_shape=jax.ShapeDtypeStruct((m, D_MODEL), F32),
        compiler_params=_params(("arbitrary",)),
        name="out_lat",
    )(oa, orr, p, p, x2d, mod3, w_pa, w_pr, w_out, ln_g, ln_b)


W_CHUNK = D_MODEL


def _weight_chunks(w_hbm):
    return [(k, slice(j * W_CHUNK, (j + 1) * W_CHUNK))
            for k, w in enumerate(w_hbm) for j in range(w.shape[1] // W_CHUNK)]


def _weight_store(n, chunks, w_scr, w_bf_hbm, out_sem):
    k, cols = chunks[n]
    return pltpu.make_async_copy(w_scr[k].at[:, cols], w_bf_hbm[k].at[:, cols], out_sem.at[n])


def _cast_weights(w_hbm, w_scr, w_bf_hbm, stage, in_sem, out_sem):
    chunks = _weight_chunks(w_hbm)

    def load(n):
        k, cols = chunks[n]
        return pltpu.make_async_copy(w_hbm[k].at[:, cols], stage.at[n % 2], in_sem.at[n % 2])

    load(0).start()
    for n, (k, cols) in enumerate(chunks):
        if n + 1 < len(chunks):
            load(n + 1).start()
        load(n).wait()
        w_scr[k][:, cols] = stage[n % 2].astype(BF16)
        _weight_store(n, chunks, w_scr, w_bf_hbm, out_sem).start()


def _ctx_kernel(x_ref, mod_ref, win_hbm, wg_hbm, bg_ref, lp_ref, sg_ref, rd_ref, gg_ref,
                wpa_hbm, wpr_hbm, wo_hbm, lng_ref, lnb_ref,
                y_ref, k_ref, v_ref, sf_ref, sb_ref, win_bf, wg_bf, wpa_bf, wpr_bf, wo_bf,
                p_scr, oa_scr, or_scr, u_scr, s_scr, dec_scr, qdec_scr, kdec_scr,
                win_ref, wg_ref, wpa_ref, wpr_ref, wo_ref, stage, in_sem, out_sem, *, npairs):
    w_hbm = (win_hbm, wg_hbm, wpa_hbm, wpr_hbm, wo_hbm)
    w_scr = (win_ref, wg_ref, wpa_ref, wpr_ref, wo_ref)
    w_bf = (win_bf, wg_bf, wpa_bf, wpr_bf, wo_bf)

    @pl.when(pl.program_id(0) == 0)
    def _():
        _cast_weights(w_hbm, w_scr, w_bf, stage, in_sem, out_sem)
        _ret_build_tables(rd_ref, dec_scr, qdec_scr, kdec_scr, npairs=npairs, pair0=0)

    cols = lambda blk, w: p_scr.at[:, blk * LANES:blk * LANES + w]
    _project(_modulated_ln(x_ref, mod_ref), win_ref, wg_ref, bg_ref, p_scr, kv_out=(k_ref, v_ref))
    _attn_ctx_heads(cols(QA_BLK, D_MODEL), cols(KA_BLK, D_MODEL), cols(VA_BLK, D_MODEL),
                    cols(ZA_BLK, D_MODEL), _lam(lp_ref), sg_ref[...], oa_scr)
    _ret_pairs(rd_ref, cols(QR_BLK, D_MODEL // 2), cols(KR_BLK, D_MODEL // 2),
               cols(VR_BLK, D_MODEL), cols(ZR_BLK, D_MODEL), gg_ref, None, or_scr,
               (sf_ref, sb_ref), u_scr, s_scr, dec_scr, qdec_scr, kdec_scr,
               nc=x_ref.shape[0] // CHUNK, npairs=npairs, pair0=0)
    gate = mod_ref[0, :, 2 * D_MODEL:3 * D_MODEL]
    _out_rows(oa_scr, or_scr, cols(GA_BLK, D_MODEL), cols(GR_BLK, D_MODEL), x_ref, gate,
              wpa_ref, wpr_ref, wo_ref, lng_ref, lnb_ref, y_ref, sub=x_ref.shape[0])

    @pl.when(pl.program_id(0) == pl.num_programs(0) - 1)
    def _():
        chunks = _weight_chunks(w_hbm)
        for n in range(len(chunks)):
            _weight_store(n, chunks, w_scr, w_bf, out_sem).wait()


def _ctx_call(x2d, mod3, w_in, w_gate, b_gate, lam_params, subln_g, ret_decay, gn_g,
              w_pa, w_pr, w_out, ln_g, ln_b, *, batch, seq, mod_row):
    npairs = N_HEADS // 2
    nc = seq // CHUNK
    tile = pl.BlockSpec((seq, D_MODEL), lambda b: (b, 0))
    vec = lambda w: pl.BlockSpec((1, w), lambda b: (0, 0))
    state = pl.BlockSpec((None, N_HEADS, DK_R, DV), lambda b: (b, 0, 0, 0))
    heads = pl.BlockSpec((seq * N_HEADS, LANES), lambda b: (b, 0))
    tok = jax.ShapeDtypeStruct((batch * seq, D_MODEL), F32)
    tok_heads = jax.ShapeDtypeStruct((batch * seq * N_HEADS, LANES), F32)
    st = jax.ShapeDtypeStruct((batch, N_HEADS, DK_R, DV), F32)
    hbm = pl.BlockSpec(memory_space=pl.ANY)
    weights = (w_in, w_gate, w_pa, w_pr, w_out)
    n_chunks = sum(w.shape[1] // W_CHUNK for w in weights)
    return pl.pallas_call(
        functools.partial(_ctx_kernel, npairs=npairs),
        grid=(batch,),
        in_specs=[tile,
                  pl.BlockSpec((1, 1, 3 * D_MODEL), lambda b: (mod_row, 0, 0)),
                  hbm, hbm, vec(2 * D_MODEL),
                  pl.BlockSpec((4, DK_A), lambda b: (0, 0)), vec(DV),
                  pl.BlockSpec(memory_space=pltpu.SMEM), vec(D_MODEL),
                  hbm, hbm, hbm, vec(D_MODEL), vec(D_MODEL)],
        out_specs=[tile, heads, heads, state, state] + [hbm] * len(weights),
        out_shape=[tok, tok_heads, tok_heads, st, st]
                  + [jax.ShapeDtypeStruct(w.shape, BF16) for w in weights],
        scratch_shapes=[pltpu.VMEM((seq, P_WIDTH), BF16),
                        pltpu.VMEM((seq, D_MODEL), BF16),
                        pltpu.VMEM((seq, D_MODEL), BF16)] + _ret_scratch(npairs, nc)
                       + [pltpu.VMEM(w.shape, BF16) for w in weights]
                       + [pltpu.VMEM((2, D_MODEL, W_CHUNK), F32),
                          pltpu.SemaphoreType.DMA((2,)),
                          pltpu.SemaphoreType.DMA((n_chunks,))],
        compiler_params=_params(("arbitrary",)),
        name="ctx",
    )(x2d, mod3, w_in, w_gate, b_gate, lam_params, subln_g, ret_decay, gn_g,
      w_pa, w_pr, w_out, ln_g, ln_b)


def _rope_tables(n_tokens):
    rows = n_tokens // GRID_W
    r = np.repeat(np.arange(rows, dtype=np.float32), GRID_W)
    col = np.tile(np.arange(GRID_W, dtype=np.float32), rows)
    n_freq = DK_A // 4
    inv = np.float32(ROPE_BASE) ** (-np.arange(n_freq, dtype=np.float32) / np.float32(n_freq))
    ang = np.concatenate([r[:, None] * inv, col[:, None] * inv], axis=-1).astype(np.float32)
    cos = np.repeat(np.cos(ang), 2, axis=-1)
    sin = np.repeat(np.sin(ang), 2, axis=-1)
    even = (np.arange(DK_A) % 2 == 0)[None, :]
    sin_even = np.where(even, -sin, 0.0)
    sin_odd = np.where(even, 0.0, sin)
    two = lambda t: jnp.asarray(np.concatenate([t, t], axis=-1), F32)
    return two(cos), two(sin_even), two(sin_odd)


def kernel(x_prompt, x_sample, cache_attn_k, cache_attn_v, state_ret_fwd, state_ret_bwd,
           c, c_ctx, w_mod, b_mod, w_in, lam_params, subln_g, ret_decay, ret_gn_g,
           w_pa, w_pr, w_gate, b_gate, w_out, ln_g, ln_b):
    batch, seq, _ = x_prompt.shape
    dbatch, dseq, _ = x_sample.shape
    past = cache_attn_k.shape[2]
    l = 0

    cond8 = jnp.concatenate([c_ctx[None, :], c, jnp.zeros((8 - 1 - dbatch, D_MODEL), F32)], axis=0)
    mod3 = _mod_call(cond8, w_mod[l], b_mod[l][None, :]).reshape(8, 1, 3 * D_MODEL)

    bg = b_gate[l][None, :]
    lp, sg = lam_params[l], subln_g[l][None, :]
    rd, gg = ret_decay[l], ret_gn_g[l][None, :]
    lng, lnb = ln_g[l][None, :], ln_b[l][None, :]

    xc = x_prompt.reshape(batch * seq, D_MODEL)
    y_c, k_c, v_c, sf_c, sb_c, win, wg, wpa, wpr, wo = _ctx_call(
        xc, mod3, w_in[l], w_gate[l], bg, lp, sg, rd, gg, w_pa[l], w_pr[l], w_out[l], lng, lnb,
        batch=batch, seq=seq, mod_row=0)

    xs = x_sample.reshape(dbatch * dseq, D_MODEL)
    p_s = _proj_lat_call(xs, mod3, win, wg, bg, _rope_tables(dseq), seq=dseq, mod_row0=1)
    ck = cache_attn_k[:, l].reshape(dbatch, past * N_HEADS, 2 * DK_A)
    cv = cache_attn_v[:, l].reshape(dbatch, past * N_HEADS, DV)
    oa_s = _attn_lat_call(p_s, ck, cv, lp, sg, batch=dbatch, seq=dseq, past=past)
    or_s = _ret_lat_call(p_s, rd, gg, state_ret_fwd[:, l], state_ret_bwd[:, l],
                         batch=dbatch, seq=dseq)
    y_s = _out_lat_call(oa_s, or_s, p_s, xs, mod3, wpa, wpr, wo, lng, lnb, seq=dseq, mod_row0=1)

    return (y_c.reshape(batch, seq, D_MODEL),
            y_s.reshape(dbatch, dseq, D_MODEL),
            k_c.reshape(batch, 1, seq, N_HEADS, 2 * DK_A),
            v_c.reshape(batch, 1, seq, N_HEADS, DV),
            sf_c.reshape(batch, 1, N_HEADS, DK_R, DV),
            sb_c.reshape(batch, 1, N_HEADS, DK_R, DV))
```

```python
import functools
import math

import jax
import jax.numpy as jnp
import numpy as np
from jax import lax
from jax.experimental import pallas as pl
from jax.experimental.pallas import tpu as pltpu

F32 = jnp.float32
BF16 = jnp.bfloat16

D_MODEL = 1024
N_HEADS = 8
DK_A = 64
DV = 128
DK_R = 64
CHUNK = 256
GRID_W = 64
ROPE_BASE = 10000.0
MOD_EPS = 1e-6
LN_EPS = 1e-5
DEPTH = 1
DEEPNORM_ALPHA = (2.0 * DEPTH) ** 0.25
LAM_INIT = 0.8 - 0.6 * math.exp(-0.3 * 0)
LOG2E = math.log2(math.e)

LANES = 128
QA_BLK, KA_BLK, VA_BLK, ZA_BLK = 0, 8, 16, 24
QR_BLK, KR_BLK, VR_BLK, ZR_BLK = 32, 36, 40, 48
GA_BLK, GR_BLK = 56, 64
IN_SEGS = 7
P_WIDTH = (IN_SEGS + 2) * D_MODEL
VMEM_LIMIT = 56 * 1024 * 1024


def _params(sem):
    return pltpu.CompilerParams(dimension_semantics=sem, vmem_limit_bytes=VMEM_LIMIT)


def _resident(shape):
    return pl.BlockSpec(shape, lambda *_: tuple(0 for _ in shape), pipeline_mode=pl.Buffered(1))


def _silu(z):
    return z * (1.0 / (1.0 + jnp.exp(-z)))


def _dot(a, b):
    return jnp.dot(a, b, preferred_element_type=F32)


def _dot_nt(a, b):
    return lax.dot_general(a, b, (((1,), (1,)), ((), ())), preferred_element_type=F32)


def _dot_tn(a, b):
    return lax.dot_general(a, b, (((0,), (0,)), ((), ())), preferred_element_type=F32)


def _mod_kernel(cond_ref, w_ref, b_ref, o_ref):
    s = _silu(cond_ref[...])
    o_ref[...] = _dot(s.astype(BF16), w_ref[...].astype(BF16)) + b_ref[...]


def _mod_call(cond8, w_mod, b_mod):
    tn = D_MODEL
    return pl.pallas_call(
        _mod_kernel,
        grid=(3 * D_MODEL // tn,),
        in_specs=[pl.BlockSpec((8, D_MODEL), lambda j: (0, 0)),
                  pl.BlockSpec((D_MODEL, tn), lambda j: (0, j)),
                  pl.BlockSpec((1, tn), lambda j: (0, j))],
        out_specs=pl.BlockSpec((8, tn), lambda j: (0, j)),
        out_shape=jax.ShapeDtypeStruct((8, 3 * D_MODEL), F32),
        compiler_params=_params(("arbitrary",)),
        name="mod",
    )(cond8, w_mod, b_mod)


def _rope(acc, cos, sin_even, sin_odd):
    outs = []
    for hh in range(N_HEADS):
        xs = acc[:, hh * LANES:(hh + 1) * LANES]
        nxt = pltpu.roll(xs, LANES - 1, 1)
        prv = pltpu.roll(xs, 1, 1)
        outs.append(xs * cos + nxt * sin_even + prv * sin_odd)
    return jnp.concatenate(outs, axis=1)


def _modulated_ln(x_ref, mod_ref):
    x = x_ref[...]
    mu = jnp.mean(x, axis=-1, keepdims=True)
    xc = x - mu
    var = jnp.mean(xc * xc, axis=-1, keepdims=True)
    shift = mod_ref[0, :, 0:D_MODEL]
    scale = mod_ref[0, :, D_MODEL:2 * D_MODEL]
    return (xc * lax.rsqrt(var + MOD_EPS) * (1.0 + scale) + shift).astype(BF16)


def _store_heads(ref, val):
    for h in range(N_HEADS):
        ref[pl.ds(h, val.shape[0], stride=N_HEADS), :] = val[:, h * LANES:(h + 1) * LANES]


def _project(h, win_ref, wg_ref, bg_ref, p_ref, rope=None, kv_out=None):
    def seg(j):
        if j < IN_SEGS:
            return _dot(h, win_ref[:, j * D_MODEL:(j + 1) * D_MODEL])
        return _dot(h, wg_ref[:, (j - IN_SEGS) * D_MODEL:(j - IN_SEGS + 1) * D_MODEL])

    def put(j, val):
        p_ref[:, j * D_MODEL:(j + 1) * D_MODEL] = val.astype(BF16)

    qa = seg(0)
    ka = seg(1)
    if rope is not None:
        cos, se, so = (t[...] for t in rope)
        qa = _rope(qa, cos, se, so)
        ka = _rope(ka, cos, se, so)
    if kv_out is not None:
        _store_heads(kv_out[0], ka)
    put(0, qa * (DK_A ** -0.5 * LOG2E))
    put(1, ka)
    va = seg(2)
    if kv_out is not None:
        _store_heads(kv_out[1], va)
    put(2, va)
    put(3, _silu(seg(3)))
    qk = seg(4)
    put(4, jnp.concatenate([qk[:, :D_MODEL // 2], qk[:, D_MODEL // 2:] * (DK_R ** -0.5)], axis=1))
    put(5, seg(5))
    put(6, _silu(seg(6)))
    for j in (IN_SEGS, IN_SEGS + 1):
        g = seg(j) + bg_ref[:, (j - IN_SEGS) * D_MODEL:(j - IN_SEGS + 1) * D_MODEL]
        put(j, 1.0 / (1.0 + jnp.exp(-g)))


def _proj_lat_kernel(x_ref, mod_ref, win_ref, wg_ref, bg_ref, cos_ref, se_ref, so_ref, p_ref):
    _project(_modulated_ln(x_ref, mod_ref), win_ref, wg_ref, bg_ref, p_ref,
             rope=(cos_ref, se_ref, so_ref))


def _proj_lat_call(x2d, mod3, w_in, w_gate, b_gate, rope, *, seq, mod_row0):
    m = x2d.shape[0]
    tm = 256
    per_seq = seq // tm
    return pl.pallas_call(
        _proj_lat_kernel,
        grid=(m // tm,),
        in_specs=[
            pl.BlockSpec((tm, D_MODEL), lambda i: (i, 0)),
            pl.BlockSpec((1, 1, 3 * D_MODEL), lambda i: (mod_row0 + i // per_seq, 0, 0)),
            _resident((D_MODEL, IN_SEGS * D_MODEL)),
            _resident((D_MODEL, 2 * D_MODEL)),
            pl.BlockSpec((1, 2 * D_MODEL), lambda i: (0, 0)),
        ] + [pl.BlockSpec((tm, LANES), lambda i: (i % per_seq, 0))] * 3,
        out_specs=pl.BlockSpec((tm, P_WIDTH), lambda i: (i, 0)),
        out_shape=jax.ShapeDtypeStruct((m, P_WIDTH), BF16),
        compiler_params=_params(("arbitrary",)),
        name="proj_lat",
    )(x2d, mod3, w_in, w_gate, b_gate, *rope)


def _lam(lp_ref):
    lp = lp_ref[...]
    a = jnp.sum(lp[0:1] * lp[1:2], axis=-1, keepdims=True)
    b = jnp.sum(lp[2:3] * lp[3:4], axis=-1, keepdims=True)
    return jnp.exp(a) - jnp.exp(b) + LAM_INIT


def _attn_rows(q, k_all, v_ones, z, lam, g):
    lane = lax.broadcasted_iota(jnp.int32, (1, LANES), 1)
    zero = jnp.zeros_like(q)
    outs = []
    for qm in (jnp.where(lane < DK_A, q, zero), jnp.where(lane >= DK_A, q, zero)):
        s = _dot_nt(qm, k_all)
        mx = jnp.max(s, axis=-1, keepdims=True)
        p = jnp.exp2(s - mx).astype(BF16)
        ol = _dot(p, v_ones)
        outs.append(ol[:, :DV] * (1.0 / ol[:, DV:]))
    oa = outs[0] - lam * outs[1]
    oa = oa * lax.rsqrt(jnp.mean(oa * oa, axis=-1, keepdims=True) + LN_EPS)
    oa = oa * g * (1.0 - LAM_INIT)
    return oa * z.astype(F32)


def _attn_ctx_heads(q_ref, k_ref, v_ref, z_ref, lam, g, o_ref):
    ones = jnp.ones((q_ref.shape[0], DV), BF16)
    for h in range(N_HEADS):
        sl = slice(h * LANES, (h + 1) * LANES)
        v_ones = jnp.concatenate([v_ref[:, sl], ones], axis=1)
        o_ref[:, sl] = _attn_rows(q_ref[:, sl], k_ref[:, sl], v_ones, z_ref[:, sl],
                                  lam, g).astype(BF16)


def _attn_lat_kernel(q_ref, k_ref, v_ref, z_ref, kc_ref, vc_ref, lp_ref, g_ref, o_ref,
                     k_all, v_all, *, past, tc):
    @pl.when(pl.program_id(2) == 0)
    def _():
        head_rows = pl.ds(pl.program_id(1), past, stride=N_HEADS)
        k_all[0:past, :] = kc_ref[head_rows, :].astype(BF16)
        v_all[0:past, 0:DV] = vc_ref[head_rows, :].astype(BF16)
        k_all[past:, :] = k_ref[...]
        v_all[past:, 0:DV] = v_ref[...]
        v_all[:, DV:] = jnp.ones((v_all.shape[0], DV), BF16)

    lam = _lam(lp_ref)
    g = g_ref[...]
    for lo in range(0, q_ref.shape[0], tc):
        o_ref[lo:lo + tc, :] = _attn_rows(q_ref[lo:lo + tc, :], k_all[...], v_all[...],
                                          z_ref[lo:lo + tc, :], lam, g).astype(BF16)


def _attn_lat_call(p, cache_k, cache_v, lam_params, subln_g, *, batch, seq, past):
    tq, tc = 2048, 128
    nq = seq // tq
    return pl.pallas_call(
        functools.partial(_attn_lat_kernel, past=past, tc=tc),
        grid=(batch, N_HEADS, nq),
        in_specs=[
            pl.BlockSpec((tq, LANES), lambda b, h, i: (b * nq + i, QA_BLK + h)),
            pl.BlockSpec((seq, LANES), lambda b, h, i: (b, KA_BLK + h)),
            pl.BlockSpec((seq, LANES), lambda b, h, i: (b, VA_BLK + h)),
            pl.BlockSpec((tq, LANES), lambda b, h, i: (b * nq + i, ZA_BLK + h)),
            pl.BlockSpec((None, past * N_HEADS, LANES), lambda b, h, i: (b, 0, 0)),
            pl.BlockSpec((None, past * N_HEADS, LANES), lambda b, h, i: (b, 0, 0)),
            pl.BlockSpec((4, DK_A), lambda b, h, i: (0, 0)),
            pl.BlockSpec((1, DV), lambda b, h, i: (0, 0)),
        ],
        out_specs=pl.BlockSpec((tq, LANES), lambda b, h, i: (b * nq + i, h)),
        out_shape=jax.ShapeDtypeStruct((batch * seq, D_MODEL), BF16),
        scratch_shapes=[pltpu.VMEM((past + seq, LANES), BF16),
                        pltpu.VMEM((past + seq, 2 * DV), BF16)],
        compiler_params=_params(("arbitrary", "arbitrary", "arbitrary")),
        name="attn_lat",
    )(p, p, p, p, cache_k, cache_v, lam_params, subln_g)


PW, VW = 2 * DK_R, 2 * DV


def _ret_scratch(npairs, nc):
    stacked_state = (npairs, nc, 2 * PW, VW)
    return [pltpu.VMEM(stacked_state, F32),
            pltpu.VMEM(stacked_state, BF16),
            pltpu.VMEM((npairs, 2, CHUNK, CHUNK), F32),
            pltpu.VMEM((npairs, CHUNK, 2 * PW), F32),
            pltpu.VMEM((npairs, CHUNK, 2 * PW), F32)]


def _log_gammas(rd_ref, pair):
    def one(d, hh):
        r = jnp.full((1, VW), rd_ref[d, 2 * pair + hh], F32)
        return jnp.log1p(-jnp.exp2(r))
    return [[one(d, hh) for hh in range(2)] for d in range(2)]


def _ret_build_tables(rd_ref, dec_scr, qdec_scr, kdec_scr, *, npairs, pair0):
    lane_q = lax.broadcasted_iota(jnp.int32, (1, PW), 1)
    rel = (lax.broadcasted_iota(jnp.int32, (CHUNK, CHUNK), 0)
           - lax.broadcasted_iota(jnp.int32, (CHUNK, CHUNK), 1)).astype(F32)
    row_q = lax.broadcasted_iota(jnp.int32, (CHUNK, PW), 0).astype(F32)
    for pi in range(npairs):
        lg = _log_gammas(rd_ref, pair0 + pi)
        lg_q = [jnp.where(lane_q < DK_R, lg[d][0][:, :PW], lg[d][1][:, :PW])
                for d in range(2)]
        for hh in range(2):
            dec_scr[pi, hh] = (
                jnp.where(rel >= 0, jnp.exp(lg[0][hh][:, :1] * jnp.maximum(rel, 0.0)), 0.0)
                + jnp.where(rel <= 0, jnp.exp(lg[1][hh][:, :1] * jnp.maximum(-rel, 0.0)), 0.0))
        qdec_scr[pi] = jnp.concatenate([jnp.exp(lg_q[0] * (row_q + 1.0)),
                                        jnp.exp(lg_q[1] * (CHUNK - row_q))], axis=1)
        kdec_scr[pi] = jnp.concatenate([jnp.exp(lg_q[0] * (CHUNK - 1.0 - row_q)),
                                        jnp.exp(lg_q[1] * row_q)], axis=1)


def _ret_pairs(rd_ref, q_ref, k_ref, v_ref, z_ref, g_ref, s0_refs, o_ref, sout_refs,
               u_scr, s_scr, dec_scr, qdec_scr, kdec_scr, *, nc, npairs, pair0):
    cross = s0_refs is not None or nc > 1
    lane_q = lax.broadcasted_iota(jnp.int32, (1, PW), 1)
    lane_v = lax.broadcasted_iota(jnp.int32, (1, VW), 1)
    srow = lax.broadcasted_iota(jnp.int32, (2 * PW, VW), 0)
    scol = lax.broadcasted_iota(jnp.int32, (2 * PW, VW), 1)
    diag_blocks = ((srow % PW) // DK_R) == (scol // DV)

    def rows(c):
        return slice(c * CHUNK, (c + 1) * CHUNK)

    for pi in range(npairs):
        qs = slice(pi * PW, (pi + 1) * PW)
        vs = slice(pi * VW, (pi + 1) * VW)
        lg = _log_gammas(rd_ref, pair0 + pi)
        lg_v = [jnp.where(lane_v < DV, lg[d][0], lg[d][1]) for d in range(2)]
        chunk_decay = [jnp.exp(lg_v[d] * CHUNK) for d in range(2)]

        for c in range(nc):
            k = k_ref[rows(c), qs].astype(F32)
            kd = (jnp.concatenate([k, k], axis=1) * kdec_scr[pi]).astype(BF16)
            u_scr[pi, c] = jnp.where(diag_blocks, _dot_tn(kd, v_ref[rows(c), vs]), 0.0)

        for d, order in ((0, range(nc)), (1, reversed(range(nc)))):
            half = slice(d * PW, (d + 1) * PW)
            if s0_refs is None:
                s = jnp.zeros((PW, VW), F32)
            else:
                zero = jnp.zeros((DK_R, DV), F32)
                s = jnp.concatenate(
                    [jnp.concatenate([s0_refs[d][2 * pi], zero], axis=1),
                     jnp.concatenate([zero, s0_refs[d][2 * pi + 1]], axis=1)], axis=0)
            for c in order:
                if cross:
                    s_scr[pi, c, half, :] = s.astype(BF16)
                s = s * chunk_decay[d] + u_scr[pi, c, half, :]
            if sout_refs is not None:
                sout_refs[d][2 * pi] = s[:DK_R, :DV]
                sout_refs[d][2 * pi + 1] = s[DK_R:, DV:]

        g = g_ref[:, vs]
        for c in range(nc):
            q = q_ref[rows(c), qs]
            k = k_ref[rows(c), qs]
            v = v_ref[rows(c), vs]
            zero = jnp.zeros_like(q)
            q2 = jnp.concatenate([jnp.where(lane_q < DK_R, q, zero),
                                  jnp.where(lane_q >= DK_R, q, zero)], axis=0)
            a2 = _dot_nt(q2, k)
            o = jnp.concatenate(
                [_dot((a2[hh * CHUNK:(hh + 1) * CHUNK] * dec_scr[pi, hh]).astype(BF16),
                      v[:, hh * DV:(hh + 1) * DV]) for hh in range(2)], axis=1)
            if cross:
                qf = q.astype(F32)
                qq = (jnp.concatenate([qf, qf], axis=1) * qdec_scr[pi]).astype(BF16)
                o = o + _dot(qq, s_scr[pi, c])
            outs = []
            for hh in range(2):
                oh = o[:, hh * DV:(hh + 1) * DV]
                mu = jnp.mean(oh, axis=-1, keepdims=True)
                oc = oh - mu
                var = jnp.mean(oc * oc, axis=-1, keepdims=True)
                outs.append(oc * lax.rsqrt(var + LN_EPS))
            o = jnp.concatenate(outs, axis=1) * g * z_ref[rows(c), vs].astype(F32)
            o_ref[rows(c), vs] = o.astype(BF16)


def _ret_lat_kernel(rd_ref, q_ref, k_ref, v_ref, z_ref, g_ref, s0f_ref, s0b_ref, o_ref,
                    u_scr, s_scr, dec_scr, qdec_scr, kdec_scr, *, nc, npairs):
    pair0 = pl.program_id(0) * npairs

    @pl.when(pl.program_id(1) == 0)
    def _():
        _ret_build_tables(rd_ref, dec_scr, qdec_scr, kdec_scr, npairs=npairs, pair0=pair0)

    _ret_pairs(rd_ref, q_ref, k_ref, v_ref, z_ref, g_ref, (s0f_ref, s0b_ref), o_ref, None,
               u_scr, s_scr, dec_scr, qdec_scr, kdec_scr, nc=nc, npairs=npairs, pair0=pair0)


def _ret_lat_call(p, ret_decay, gn_g, s0f, s0b, *, batch, seq):
    nc = seq // CHUNK
    hps = 2
    npairs = hps // 2
    qw, vw = DK_R * hps, DV * hps
    col = lambda w, blk: pl.BlockSpec((seq, w), lambda j, b: (b, blk * LANES // w + j))
    state = pl.BlockSpec((None, hps, DK_R, DV), lambda j, b: (b, j, 0, 0))
    return pl.pallas_call(
        functools.partial(_ret_lat_kernel, nc=nc, npairs=npairs),
        grid=(N_HEADS // hps, batch),
        in_specs=[pl.BlockSpec(memory_space=pltpu.SMEM),
                  col(qw, QR_BLK), col(qw, KR_BLK), col(vw, VR_BLK), col(vw, ZR_BLK),
                  pl.BlockSpec((1, vw), lambda j, b: (0, j)), state, state],
        out_specs=pl.BlockSpec((seq, vw), lambda j, b: (b, j)),
        out_shape=jax.ShapeDtypeStruct((batch * seq, D_MODEL), BF16),
        scratch_shapes=_ret_scratch(npairs, nc),
        compiler_params=_params(("arbitrary", "arbitrary")),
        name="ret_lat",
    )(ret_decay, p, p, p, p, gn_g, s0f, s0b)


def _out_rows(oa_ref, or_ref, ga_ref, gr_ref, x_ref, gate, wpa_ref, wpr_ref, wo_ref,
              lng_ref, lnb_ref, y_ref, *, sub):
    for s in range(y_ref.shape[0] // sub):
        rows = slice(s * sub, (s + 1) * sub)
        a = _dot(oa_ref[rows, :], wpa_ref[...])
        r = _dot(or_ref[rows, :], wpr_ref[...])
        m = ga_ref[rows, :].astype(F32) * a + gr_ref[rows, :].astype(F32) * r
        out = _dot(m.astype(BF16), wo_ref[...])
        t = DEEPNORM_ALPHA * x_ref[rows, :] + gate * out
        mu = jnp.mean(t, axis=-1, keepdims=True)
        tc = t - mu
        var = jnp.mean(tc * tc, axis=-1, keepdims=True)
        y_ref[rows, :] = tc * lax.rsqrt(var + LN_EPS) * lng_ref[...] + lnb_ref[...]


def _out_lat_kernel(oa_ref, or_ref, ga_ref, gr_ref, x_ref, mod_ref, wpa_ref, wpr_ref, wo_ref,
                    lng_ref, lnb_ref, y_ref, *, sub):
    gate = mod_ref[0, :, 2 * D_MODEL:3 * D_MODEL]
    _out_rows(oa_ref, or_ref, ga_ref, gr_ref, x_ref, gate, wpa_ref, wpr_ref, wo_ref,
              lng_ref, lnb_ref, y_ref, sub=sub)


def _out_lat_call(oa, orr, p, x2d, mod3, w_pa, w_pr, w_out, ln_g, ln_b, *, seq, mod_row0):
    m = x2d.shape[0]
    tm, sub = 512, 512
    per_seq = seq // tm
    tile = lambda j: pl.BlockSpec((tm, D_MODEL), lambda i: (i, j))
    vec = pl.BlockSpec((1, D_MODEL), lambda i: (0, 0))
    return pl.pallas_call(
        functools.partial(_out_lat_kernel, sub=sub),
        grid=(m // tm,),
        in_specs=[tile(0), tile(0), tile(GA_BLK * LANES // D_MODEL), tile(GR_BLK * LANES // D_MODEL),
                  tile(0),
                  pl.BlockSpec((1, 1, 3 * D_MODEL), lambda i: (mod_row0 + i // per_seq, 0, 0)),
                  _resident((D_MODEL, D_MODEL)), _resident((D_MODEL, D_MODEL)),
                  _resident((D_MODEL, D_MODEL)), vec, vec],
        out_specs=tile(0),
        out_shape=jax.ShapeDtypeStruct((m, D_MODEL), F32),
        compiler_params=_params(("arbitrary",)),
        name="out_lat",
    )(oa, orr, p, p, x2d, mod3, w_pa, w_pr, w_out, ln_g, ln_b)


W_CHUNK = D_MODEL


def _weight_chunks(w_hbm):
    return [(k, slice(j * W_CHUNK, (j + 1) * W_CHUNK))
            for k, w in enumerate(w_hbm) for j in range(w.shape[1] // W_CHUNK)]


def _weight_store(n, chunks, w_scr, w_bf_hbm, out_sem):
    k, cols = chunks[n]
    return pltpu.make_async_copy(w_scr[k].at[:, cols], w_bf_hbm[k].at[:, cols], out_sem.at[n])


def _cast_weights(w_hbm, w_scr, w_bf_hbm, stage, in_sem, out_sem):
    chunks = _weight_chunks(w_hbm)

    def load(n):
        k, cols = chunks[n]
        return pltpu.make_async_copy(w_hbm[k].at[:, cols], stage.at[n % 2], in_sem.at[n % 2])

    load(0).start()
    for n, (k, cols) in enumerate(chunks):
        if n + 1 < len(chunks):
            load(n + 1).start()
        load(n).wait()
        w_scr[k][:, cols] = stage[n % 2].astype(BF16)
    for n in range(len(chunks)):
        _weight_store(n, chunks, w_scr, w_bf_hbm, out_sem).start()


def _ctx_kernel(x_ref, mod_ref, win_hbm, wg_hbm, bg_ref, lp_ref, sg_ref, rd_ref, gg_ref,
                wpa_hbm, wpr_hbm, wo_hbm, lng_ref, lnb_ref,
                y_ref, k_ref, v_ref, sf_ref, sb_ref, win_bf, wg_bf, wpa_bf, wpr_bf, wo_bf,
                p_scr, oa_scr, or_scr, u_scr, s_scr, dec_scr, qdec_scr, kdec_scr,
                win_ref, wg_ref, wpa_ref, wpr_ref, wo_ref, stage, in_sem, out_sem, *, npairs):
    w_hbm = (win_hbm, wg_hbm, wpa_hbm, wpr_hbm, wo_hbm)
    w_scr = (win_ref, wg_ref, wpa_ref, wpr_ref, wo_ref)
    w_bf = (win_bf, wg_bf, wpa_bf, wpr_bf, wo_bf)

    @pl.when(pl.program_id(0) == 0)
    def _():
        _cast_weights(w_hbm, w_scr, w_bf, stage, in_sem, out_sem)
        _ret_build_tables(rd_ref, dec_scr, qdec_scr, kdec_scr, npairs=npairs, pair0=0)

    cols = lambda blk, w: p_scr.at[:, blk * LANES:blk * LANES + w]
    _project(_modulated_ln(x_ref, mod_ref), win_ref, wg_ref, bg_ref, p_scr, kv_out=(k_ref, v_ref))
    _attn_ctx_heads(cols(QA_BLK, D_MODEL), cols(KA_BLK, D_MODEL), cols(VA_BLK, D_MODEL),
                    cols(ZA_BLK, D_MODEL), _lam(lp_ref), sg_ref[...], oa_scr)
    _ret_pairs(rd_ref, cols(QR_BLK, D_MODEL // 2), cols(KR_BLK, D_MODEL // 2),
               cols(VR_BLK, D_MODEL), cols(ZR_BLK, D_MODEL), gg_ref, None, or_scr,
               (sf_ref, sb_ref), u_scr, s_scr, dec_scr, qdec_scr, kdec_scr,
               nc=x_ref.shape[0] // CHUNK, npairs=npairs, pair0=0)
    gate = mod_ref[0, :, 2 * D_MODEL:3 * D_MODEL]
    _out_rows(oa_scr, or_scr, cols(GA_BLK, D_MODEL), cols(GR_BLK, D_MODEL), x_ref, gate,
              wpa_ref, wpr_ref, wo_ref, lng_ref, lnb_ref, y_ref, sub=x_ref.shape[0])

    @pl.when(pl.program_id(0) == pl.num_programs(0) - 1)
    def _():
        chunks = _weight_chunks(w_hbm)
        for n in range(len(chunks)):
            _weight_store(n, chunks, w_scr, w_bf, out_sem).wait()


def _ctx_call(x2d, mod3, w_in, w_gate, b_gate, lam_params, subln_g, ret_decay, gn_g,
              w_pa, w_pr, w_out, ln_g, ln_b, *, batch, seq, mod_row):
    npairs = N_HEADS // 2
    nc = seq // CHUNK
    tile = pl.BlockSpec((seq, D_MODEL), lambda b: (b, 0))
    vec = lambda w: pl.BlockSpec((1, w), lambda b: (0, 0))
    state = pl.BlockSpec((None, N_HEADS, DK_R, DV), lambda b: (b, 0, 0, 0))
    heads = pl.BlockSpec((seq * N_HEADS, LANES), lambda b: (b, 0))
    tok = jax.ShapeDtypeStruct((batch * seq, D_MODEL), F32)
    tok_heads = jax.ShapeDtypeStruct((batch * seq * N_HEADS, LANES), F32)
    st = jax.ShapeDtypeStruct((batch, N_HEADS, DK_R, DV), F32)
    hbm = pl.BlockSpec(memory_space=pl.ANY)
    weights = (w_in, w_gate, w_pa, w_pr, w_out)
    n_chunks = sum(w.shape[1] // W_CHUNK for w in weights)
    return pl.pallas_call(
        functools.partial(_ctx_kernel, npairs=npairs),
        grid=(batch,),
        in_specs=[tile,
                  pl.BlockSpec((1, 1, 3 * D_MODEL), lambda b: (mod_row, 0, 0)),
                  hbm, hbm, vec(2 * D_MODEL),
                  pl.BlockSpec((4, DK_A), lambda b: (0, 0)), vec(DV),
                  pl.BlockSpec(memory_space=pltpu.SMEM), vec(D_MODEL),
                  hbm, hbm, hbm, vec(D_MODEL), vec(D_MODEL)],
        out_specs=[tile, heads, heads, state, state] + [hbm] * len(weights),
        out_shape=[tok, tok_heads, tok_heads, st, st]
                  + [jax.ShapeDtypeStruct(w.shape, BF16) for w in weights],
        scratch_shapes=[pltpu.VMEM((seq, P_WIDTH), BF16),
                        pltpu.VMEM((seq, D_MODEL), BF16),
                        pltpu.VMEM((seq, D_MODEL), BF16)] + _ret_scratch(npairs, nc)
                       + [pltpu.VMEM(w.shape, BF16) for w in weights]
                       + [pltpu.VMEM((2, D_MODEL, W_CHUNK), F32),
                          pltpu.SemaphoreType.DMA((2,)),
                          pltpu.SemaphoreType.DMA((n_chunks,))],
        compiler_params=_params(("arbitrary",)),
        name="ctx",
    )(x2d, mod3, w_in, w_gate, b_gate, lam_params, subln_g, ret_decay, gn_g,
      w_pa, w_pr, w_out, ln_g, ln_b)


def _rope_tables(n_tokens):
    rows = n_tokens // GRID_W
    r = np.repeat(np.arange(rows, dtype=np.float32), GRID_W)
    col = np.tile(np.arange(GRID_W, dtype=np.float32), rows)
    n_freq = DK_A // 4
    inv = np.float32(ROPE_BASE) ** (-np.arange(n_freq, dtype=np.float32) / np.float32(n_freq))
    ang = np.concatenate([r[:, None] * inv, col[:, None] * inv], axis=-1).astype(np.float32)
    cos = np.repeat(np.cos(ang), 2, axis=-1)
    sin = np.repeat(np.sin(ang), 2, axis=-1)
    even = (np.arange(DK_A) % 2 == 0)[None, :]
    sin_even = np.where(even, -sin, 0.0)
    sin_odd = np.where(even, 0.0, sin)
    two = lambda t: jnp.asarray(np.concatenate([t, t], axis=-1), F32)
    return two(cos), two(sin_even), two(sin_odd)


def kernel(x_prompt, x_sample, cache_attn_k, cache_attn_v, state_ret_fwd, state_ret_bwd,
           c, c_ctx, w_mod, b_mod, w_in, lam_params, subln_g, ret_decay, ret_gn_g,
           w_pa, w_pr, w_gate, b_gate, w_out, ln_g, ln_b):
    batch, seq, _ = x_prompt.shape
    dbatch, dseq, _ = x_sample.shape
    past = cache_attn_k.shape[2]
    l = 0

    cond8 = jnp.concatenate([c_ctx[None, :], c, jnp.zeros((8 - 1 - dbatch, D_MODEL), F32)], axis=0)
    mod3 = _mod_call(cond8, w_mod[l], b_mod[l][None, :]).reshape(8, 1, 3 * D_MODEL)

    bg = b_gate[l][None, :]
    lp, sg = lam_params[l], subln_g[l][None, :]
    rd, gg = ret_decay[l], ret_gn_g[l][None, :]
    lng, lnb = ln_g[l][None, :], ln_b[l][None, :]

    xc = x_prompt.reshape(batch * seq, D_MODEL)
    y_c, k_c, v_c, sf_c, sb_c, win, wg, wpa, wpr, wo = _ctx_call(
        xc, mod3, w_in[l], w_gate[l], bg, lp, sg, rd, gg, w_pa[l], w_pr[l], w_out[l], lng, lnb,
        batch=batch, seq=seq, mod_row=0)

    xs = x_sample.reshape(dbatch * dseq, D_MODEL)
    p_s = _proj_lat_call(xs, mod3, win, wg, bg, _rope_tables(dseq), seq=dseq, mod_row0=1)
    ck = cache_attn_k[:, l].reshape(dbatch, past * N_HEADS, 2 * DK_A)
    cv = cache_attn_v[:, l].reshape(dbatch, past * N_HEADS, DV)
    oa_s = _attn_lat_call(p_s, ck, cv, lp, sg, batch=dbatch, seq=dseq, past=past)
    or_s = _ret_lat_call(p_s, rd, gg, state_ret_fwd[:, l], state_ret_bwd[:, l],
                         batch=dbatch, seq=dseq)
    y_s = _out_lat_call(oa_s, or_s, p_s, xs, mod3, wpa, wpr, wo, lng, lnb, seq=dseq, mod_row0=1)

    return (y_c.reshape(batch, seq, D_MODEL),
            y_s.reshape(dbatch, dseq, D_MODEL),
            k_c.reshape(batch, 1, seq, N_HEADS, 2 * DK_A),
            v_c.reshape(batch, 1, seq, N_HEADS, DV),
            sf_c.reshape(batch, 1, N_HEADS, DK_R, DV),
            sb_c.reshape(batch, 1, N_HEADS, DK_R, DV))
```

```python
import functools
import math

import jax
import jax.numpy as jnp
import numpy as np
from jax import lax
from jax.experimental import pallas as pl
from jax.experimental.pallas import tpu as pltpu

F32 = jnp.float32
BF16 = jnp.bfloat16

D_MODEL = 1024
N_HEADS = 8
DK_A = 64
DV = 128
DK_R = 64
CHUNK = 256
GRID_W = 64
ROPE_BASE = 10000.0
MOD_EPS = 1e-6
LN_EPS = 1e-5
DEPTH = 1
DEEPNORM_ALPHA = (2.0 * DEPTH) ** 0.25
LAM_INIT = 0.8 - 0.6 * math.exp(-0.3 * 0)
LOG2E = math.log2(math.e)

LANES = 128
QA_BLK, KA_BLK, VA_BLK, ZA_BLK = 0, 8, 16, 24
QR_BLK, KR_BLK, VR_BLK, ZR_BLK = 32, 36, 40, 48
GA_BLK, GR_BLK = 56, 64
IN_SEGS = 7
P_WIDTH = (IN_SEGS + 2) * D_MODEL
VMEM_LIMIT = 56 * 1024 * 1024


def _params(sem):
    return pltpu.CompilerParams(dimension_semantics=sem, vmem_limit_bytes=VMEM_LIMIT)


def _resident(shape):
    return pl.BlockSpec(shape, lambda *_: tuple(0 for _ in shape), pipeline_mode=pl.Buffered(1))


def _silu(z):
    return z * (1.0 / (1.0 + jnp.exp(-z)))


def _dot(a, b):
    return jnp.dot(a, b, preferred_element_type=F32)


def _dot_nt(a, b):
    return lax.dot_general(a, b, (((1,), (1,)), ((), ())), preferred_element_type=F32)


def _dot_tn(a, b):
    return lax.dot_general(a, b, (((0,), (0,)), ((), ())), preferred_element_type=F32)


def _rope(acc, cos, sin_even, sin_odd):
    outs = []
    for hh in range(N_HEADS):
        xs = acc[:, hh * LANES:(hh + 1) * LANES]
        nxt = pltpu.roll(xs, LANES - 1, 1)
        prv = pltpu.roll(xs, 1, 1)
        outs.append(xs * cos + nxt * sin_even + prv * sin_odd)
    return jnp.concatenate(outs, axis=1)


def _mod_row(mod_ref, row):
    r = pl.ds(row, 1)
    return (mod_ref[r, 0:D_MODEL], mod_ref[r, D_MODEL:2 * D_MODEL], mod_ref[r, 2 * D_MODEL:3 * D_MODEL])


def _modulated_ln(x_ref, shift, scale):
    x = x_ref[...]
    mu = jnp.mean(x, axis=-1, keepdims=True)
    xc = x - mu
    var = jnp.mean(xc * xc, axis=-1, keepdims=True)
    return (xc * lax.rsqrt(var + MOD_EPS) * (1.0 + scale) + shift).astype(BF16)


def _store_heads(ref, val):
    for h in range(N_HEADS):
        ref[pl.ds(h, val.shape[0], stride=N_HEADS), :] = val[:, h * LANES:(h + 1) * LANES]


def _project(h, win_ref, wg_ref, bg_ref, p_ref, rope=None, kv_out=None):
    def seg(j):
        if j < IN_SEGS:
            return _dot(h, win_ref[:, j * D_MODEL:(j + 1) * D_MODEL])
        return _dot(h, wg_ref[:, (j - IN_SEGS) * D_MODEL:(j - IN_SEGS + 1) * D_MODEL])

    def put(j, val):
        p_ref[:, j * D_MODEL:(j + 1) * D_MODEL] = val.astype(BF16)

    qa = seg(0)
    ka = seg(1)
    if rope is not None:
        cos, se, so = (t[...] for t in rope)
        qa = _rope(qa, cos, se, so)
        ka = _rope(ka, cos, se, so)
    if kv_out is not None:
        _store_heads(kv_out[0], ka)
    put(0, qa * (DK_A ** -0.5 * LOG2E))
    put(1, ka)
    va = seg(2)
    if kv_out is not None:
        _store_heads(kv_out[1], va)
    put(2, va)
    put(3, _silu(seg(3)))
    qk = seg(4)
    put(4, jnp.concatenate([qk[:, :D_MODEL // 2], qk[:, D_MODEL // 2:] * (DK_R ** -0.5)], axis=1))
    put(5, seg(5))
    put(6, _silu(seg(6)))
    for j in (IN_SEGS, IN_SEGS + 1):
        g = seg(j) + bg_ref[:, (j - IN_SEGS) * D_MODEL:(j - IN_SEGS + 1) * D_MODEL]
        put(j, 1.0 / (1.0 + jnp.exp(-g)))


def _proj_lat_kernel(x_ref, mod_ref, win_ref, wg_ref, bg_ref, cos_ref, se_ref, so_ref, p_ref,
                     *, per_seq, mod_row0):
    shift, scale, _ = _mod_row(mod_ref, mod_row0 + pl.program_id(0) // per_seq)
    _project(_modulated_ln(x_ref, shift, scale), win_ref, wg_ref, bg_ref, p_ref,
             rope=(cos_ref, se_ref, so_ref))


def _proj_lat_call(x2d, mod, w_in, w_gate, b_gate, rope, *, seq, mod_row0):
    m = x2d.shape[0]
    tm = 256
    per_seq = seq // tm
    return pl.pallas_call(
        functools.partial(_proj_lat_kernel, per_seq=per_seq, mod_row0=mod_row0),
        grid=(m // tm,),
        in_specs=[
            pl.BlockSpec((tm, D_MODEL), lambda i: (i, 0)),
            pl.BlockSpec((8, 3 * D_MODEL), lambda i: (0, 0)),
            _resident((D_MODEL, IN_SEGS * D_MODEL)),
            _resident((D_MODEL, 2 * D_MODEL)),
            pl.BlockSpec((1, 2 * D_MODEL), lambda i: (0, 0)),
        ] + [pl.BlockSpec((tm, LANES), lambda i: (i % per_seq, 0))] * 3,
        out_specs=pl.BlockSpec((tm, P_WIDTH), lambda i: (i, 0)),
        out_shape=jax.ShapeDtypeStruct((m, P_WIDTH), BF16),
        compiler_params=_params(("arbitrary",)),
        name="proj_lat",
    )(x2d, mod, w_in, w_gate, b_gate, *rope)


def _lam(lp_ref):
    lp = lp_ref[...]
    a = jnp.sum(lp[0:1] * lp[1:2], axis=-1, keepdims=True)
    b = jnp.sum(lp[2:3] * lp[3:4], axis=-1, keepdims=True)
    return jnp.exp(a) - jnp.exp(b) + LAM_INIT


def _attn_rows(q, k_all, v_ones, z, lam, g):
    lane = lax.broadcasted_iota(jnp.int32, (1, LANES), 1)
    zero = jnp.zeros_like(q)
    outs = []
    for qm in (jnp.where(lane < DK_A, q, zero), jnp.where(lane >= DK_A, q, zero)):
        s = _dot_nt(qm, k_all)
        mx = jnp.max(s, axis=-1, keepdims=True)
        p = jnp.exp2(s - mx).astype(BF16)
        ol = _dot(p, v_ones)
        outs.append(ol[:, :DV] * (1.0 / ol[:, DV:]))
    oa = outs[0] - lam * outs[1]
    oa = oa * lax.rsqrt(jnp.mean(oa * oa, axis=-1, keepdims=True) + LN_EPS)
    oa = oa * g * (1.0 - LAM_INIT)
    return oa * z.astype(F32)


def _attn_ctx_heads(q_ref, k_ref, v_ref, z_ref, lam, g, o_ref):
    ones = jnp.ones((q_ref.shape[0], DV), BF16)
    for h in range(N_HEADS):
        sl = slice(h * LANES, (h + 1) * LANES)
        v_ones = jnp.concatenate([v_ref[:, sl], ones], axis=1)
        o_ref[:, sl] = _attn_rows(q_ref[:, sl], k_ref[:, sl], v_ones, z_ref[:, sl],
                                  lam, g).astype(BF16)


def _attn_lat_kernel(q_ref, k_ref, v_ref, z_ref, kc_ref, vc_ref, lp_ref, g_ref, o_ref,
                     k_all, v_all, *, past, tc):
    @pl.when(pl.program_id(2) == 0)
    def _():
        head_rows = pl.ds(pl.program_id(1), past, stride=N_HEADS)
        k_all[0:past, :] = kc_ref[head_rows, :].astype(BF16)
        v_all[0:past, 0:DV] = vc_ref[head_rows, :].astype(BF16)
        k_all[past:, :] = k_ref[...]
        v_all[past:, 0:DV] = v_ref[...]
        v_all[:, DV:] = jnp.ones((v_all.shape[0], DV), BF16)

    lam = _lam(lp_ref)
    g = g_ref[...]
    for lo in range(0, q_ref.shape[0], tc):
        o_ref[lo:lo + tc, :] = _attn_rows(q_ref[lo:lo + tc, :], k_all[...], v_all[...],
                                          z_ref[lo:lo + tc, :], lam, g).astype(BF16)


def _attn_lat_call(p, cache_k, cache_v, lam_params, subln_g, *, batch, seq, past):
    tq, tc = 2048, 128
    nq = seq // tq
    return pl.pallas_call(
        functools.partial(_attn_lat_kernel, past=past, tc=tc),
        grid=(batch, N_HEADS, nq),
        in_specs=[
            pl.BlockSpec((tq, LANES), lambda b, h, i: (b * nq + i, QA_BLK + h)),
            pl.BlockSpec((seq, LANES), lambda b, h, i: (b, KA_BLK + h)),
            pl.BlockSpec((seq, LANES), lambda b, h, i: (b, VA_BLK + h)),
            pl.BlockSpec((tq, LANES), lambda b, h, i: (b * nq + i, ZA_BLK + h)),
            pl.BlockSpec((None, past * N_HEADS, LANES), lambda b, h, i: (b, 0, 0)),
            pl.BlockSpec((None, past * N_HEADS, LANES), lambda b, h, i: (b, 0, 0)),
            pl.BlockSpec((4, DK_A), lambda b, h, i: (0, 0)),
            pl.BlockSpec((1, DV), lambda b, h, i: (0, 0)),
        ],
        out_specs=pl.BlockSpec((tq, LANES), lambda b, h, i: (b * nq + i, h)),
        out_shape=jax.ShapeDtypeStruct((batch * seq, D_MODEL), BF16),
        scratch_shapes=[pltpu.VMEM((past + seq, LANES), BF16),
                        pltpu.VMEM((past + seq, 2 * DV), BF16)],
        compiler_params=_params(("arbitrary", "arbitrary", "arbitrary")),
        name="attn_lat",
    )(p, p, p, p, cache_k, cache_v, lam_params, subln_g)


PW, VW = 2 * DK_R, 2 * DV


def _ret_scratch(npairs, nc):
    stacked_state = (npairs, nc, 2 * PW, VW)
    return [pltpu.VMEM(stacked_state, F32),
            pltpu.VMEM(stacked_state, BF16),
            pltpu.VMEM((npairs, 2, CHUNK, CHUNK), F32),
            pltpu.VMEM((npairs, CHUNK, 2 * PW), F32),
            pltpu.VMEM((npairs, CHUNK, 2 * PW), F32)]


def _log_gammas(rd_ref, pair):
    def one(d, hh):
        r = jnp.full((1, VW), rd_ref[d, 2 * pair + hh], F32)
        return jnp.log1p(-jnp.exp2(r))
    return [[one(d, hh) for hh in range(2)] for d in range(2)]


def _ret_build_tables(rd_ref, dec_scr, qdec_scr, kdec_scr, *, npairs, pair0):
    lane_q = lax.broadcasted_iota(jnp.int32, (1, PW), 1)
    rel = (lax.broadcasted_iota(jnp.int32, (CHUNK, CHUNK), 0)
           - lax.broadcasted_iota(jnp.int32, (CHUNK, CHUNK), 1)).astype(F32)
    row_q = lax.broadcasted_iota(jnp.int32, (CHUNK, PW), 0).astype(F32)
    for pi in range(npairs):
        lg = _log_gammas(rd_ref, pair0 + pi)
        lg_q = [jnp.where(lane_q < DK_R, lg[d][0][:, :PW], lg[d][1][:, :PW])
                for d in range(2)]
        for hh in range(2):
            dec_scr[pi, hh] = (
                jnp.where(rel >= 0, jnp.exp(lg[0][hh][:, :1] * jnp.maximum(rel, 0.0)), 0.0)
                + jnp.where(rel <= 0, jnp.exp(lg[1][hh][:, :1] * jnp.maximum(-rel, 0.0)), 0.0))
        qdec_scr[pi] = jnp.concatenate([jnp.exp(lg_q[0] * (row_q + 1.0)),
                                        jnp.exp(lg_q[1] * (CHUNK - row_q))], axis=1)
        kdec_scr[pi] = jnp.concatenate([jnp.exp(lg_q[0] * (CHUNK - 1.0 - row_q)),
                                        jnp.exp(lg_q[1] * row_q)], axis=1)


def _ret_pairs(rd_ref, q_ref, k_ref, v_ref, z_ref, g_ref, s0_refs, o_ref, sout_refs,
               u_scr, s_scr, dec_scr, qdec_scr, kdec_scr, *, nc, npairs, pair0):
    cross = s0_refs is not None or nc > 1
    lane_q = lax.broadcasted_iota(jnp.int32, (1, PW), 1)
    lane_v = lax.broadcasted_iota(jnp.int32, (1, VW), 1)
    srow = lax.broadcasted_iota(jnp.int32, (2 * PW, VW), 0)
    scol = lax.broadcasted_iota(jnp.int32, (2 * PW, VW), 1)
    diag_blocks = ((srow % PW) // DK_R) == (scol // DV)

    def rows(c):
        return slice(c * CHUNK, (c + 1) * CHUNK)

    for pi in range(npairs):
        qs = slice(pi * PW, (pi + 1) * PW)
        vs = slice(pi * VW, (pi + 1) * VW)
        lg = _log_gammas(rd_ref, pair0 + pi)
        lg_v = [jnp.where(lane_v < DV, lg[d][0], lg[d][1]) for d in range(2)]
        chunk_decay = [jnp.exp(lg_v[d] * CHUNK) for d in range(2)]

        for c in range(nc):
            k = k_ref[rows(c), qs].astype(F32)
            kd = (jnp.concatenate([k, k], axis=1) * kdec_scr[pi]).astype(BF16)
            u_scr[pi, c] = jnp.where(diag_blocks, _dot_tn(kd, v_ref[rows(c), vs]), 0.0)

        for d, order in ((0, range(nc)), (1, reversed(range(nc)))):
            half = slice(d * PW, (d + 1) * PW)
            if s0_refs is None:
                s = jnp.zeros((PW, VW), F32)
            else:
                zero = jnp.zeros((DK_R, DV), F32)
                s = jnp.concatenate(
                    [jnp.concatenate([s0_refs[d][2 * pi], zero], axis=1),
                     jnp.concatenate([zero, s0_refs[d][2 * pi + 1]], axis=1)], axis=0)
            for c in order:
                if cross:
                    s_scr[pi, c, half, :] = s.astype(BF16)
                s = s * chunk_decay[d] + u_scr[pi, c, half, :]
            if sout_refs is not None:
                sout_refs[d][2 * pi] = s[:DK_R, :DV]
                sout_refs[d][2 * pi + 1] = s[DK_R:, DV:]

        g = g_ref[:, vs]
        for c in range(nc):
            q = q_ref[rows(c), qs]
            k = k_ref[rows(c), qs]
            v = v_ref[rows(c), vs]
            zero = jnp.zeros_like(q)
            q2 = jnp.concatenate([jnp.where(lane_q < DK_R, q, zero),
                                  jnp.where(lane_q >= DK_R, q, zero)], axis=0)
            a2 = _dot_nt(q2, k)
            o = jnp.concatenate(
                [_dot((a2[hh * CHUNK:(hh + 1) * CHUNK] * dec_scr[pi, hh]).astype(BF16),
                      v[:, hh * DV:(hh + 1) * DV]) for hh in range(2)], axis=1)
            if cross:
                qf = q.astype(F32)
                qq = (jnp.concatenate([qf, qf], axis=1) * qdec_scr[pi]).astype(BF16)
                o = o + _dot(qq, s_scr[pi, c])
            outs = []
            for hh in range(2):
                oh = o[:, hh * DV:(hh + 1) * DV]
                mu = jnp.mean(oh, axis=-1, keepdims=True)
                oc = oh - mu
                var = jnp.mean(oc * oc, axis=-1, keepdims=True)
                outs.append(oc * lax.rsqrt(var + LN_EPS))
            o = jnp.concatenate(outs, axis=1) * g * z_ref[rows(c), vs].astype(F32)
            o_ref[rows(c), vs] = o.astype(BF16)


def _ret_lat_kernel(rd_ref, q_ref, k_ref, v_ref, z_ref, g_ref, s0f_ref, s0b_ref, o_ref,
                    u_scr, s_scr, dec_scr, qdec_scr, kdec_scr, *, nc, npairs):
    pair0 = pl.program_id(0) * npairs

    @pl.when(pl.program_id(1) == 0)
    def _():
        _ret_build_tables(rd_ref, dec_scr, qdec_scr, kdec_scr, npairs=npairs, pair0=pair0)

    _ret_pairs(rd_ref, q_ref, k_ref, v_ref, z_ref, g_ref, (s0f_ref, s0b_ref), o_ref, None,
               u_scr, s_scr, dec_scr, qdec_scr, kdec_scr, nc=nc, npairs=npairs, pair0=pair0)


def _ret_lat_call(p, ret_decay, gn_g, s0f, s0b, *, batch, seq):
    nc = seq // CHUNK
    hps = 2
    npairs = hps // 2
    qw, vw = DK_R * hps, DV * hps
    col = lambda w, blk: pl.BlockSpec((seq, w), lambda j, b: (b, blk * LANES // w + j))
    state = pl.BlockSpec((None, hps, DK_R, DV), lambda j, b: (b, j, 0, 0))
    return pl.pallas_call(
        functools.partial(_ret_lat_kernel, nc=nc, npairs=npairs),
        grid=(N_HEADS // hps, batch),
        in_specs=[pl.BlockSpec(memory_space=pltpu.SMEM),
                  col(qw, QR_BLK), col(qw, KR_BLK), col(vw, VR_BLK), col(vw, ZR_BLK),
                  pl.BlockSpec((1, vw), lambda j, b: (0, j)), state, state],
        out_specs=pl.BlockSpec((seq, vw), lambda j, b: (b, j)),
        out_shape=jax.ShapeDtypeStruct((batch * seq, D_MODEL), BF16),
        scratch_shapes=_ret_scratch(npairs, nc),
        compiler_params=_params(("arbitrary", "arbitrary")),
        name="ret_lat",
    )(ret_decay, p, p, p, p, gn_g, s0f, s0b)


def _out_rows(oa_ref, or_ref, ga_ref, gr_ref, x_ref, gate, wpa_ref, wpr_ref, wo_ref,
              lng_ref, lnb_ref, y_ref, *, sub):
    for s in range(y_ref.shape[0] // sub):
        rows = slice(s * sub, (s + 1) * sub)
        a = _dot(oa_ref[rows, :], wpa_ref[...])
        r = _dot(or_ref[rows, :], wpr_ref[...])
        m = ga_ref[rows, :].astype(F32) * a + gr_ref[rows, :].astype(F32) * r
        out = _dot(m.astype(BF16), wo_ref[...])
        t = DEEPNORM_ALPHA * x_ref[rows, :] + gate * out
        mu = jnp.mean(t, axis=-1, keepdims=True)
        tc = t - mu
        var = jnp.mean(tc * tc, axis=-1, keepdims=True)
        y_ref[rows, :] = tc * lax.rsqrt(var + LN_EPS) * lng_ref[...] + lnb_ref[...]


def _out_lat_kernel(oa_ref, or_ref, ga_ref, gr_ref, x_ref, mod_ref, wpa_ref, wpr_ref, wo_ref,
                    lng_ref, lnb_ref, y_ref, *, sub, per_seq, mod_row0):
    _, _, gate = _mod_row(mod_ref, mod_row0 + pl.program_id(0) // per_seq)
    _out_rows(oa_ref, or_ref, ga_ref, gr_ref, x_ref, gate, wpa_ref, wpr_ref, wo_ref,
              lng_ref, lnb_ref, y_ref, sub=sub)


def _out_lat_call(oa, orr, p, x2d, mod, w_pa, w_pr, w_out, ln_g, ln_b, *, seq, mod_row0):
    m = x2d.shape[0]
    tm, sub = 512, 512
    per_seq = seq // tm
    tile = lambda j: pl.BlockSpec((tm, D_MODEL), lambda i: (i, j))
    vec = pl.BlockSpec((1, D_MODEL), lambda i: (0, 0))
    return pl.pallas_call(
        functools.partial(_out_lat_kernel, sub=sub, per_seq=per_seq, mod_row0=mod_row0),
        grid=(m // tm,),
        in_specs=[tile(0), tile(0), tile(GA_BLK * LANES // D_MODEL), tile(GR_BLK * LANES // D_MODEL),
                  tile(0),
                  pl.BlockSpec((8, 3 * D_MODEL), lambda i: (0, 0)),
                  _resident((D_MODEL, D_MODEL)), _resident((D_MODEL, D_MODEL)),
                  _resident((D_MODEL, D_MODEL)), vec, vec],
        out_specs=tile(0),
        out_shape=jax.ShapeDtypeStruct((m, D_MODEL), F32),
        compiler_params=_params(("arbitrary",)),
        name="out_lat",
    )(oa, orr, p, p, x2d, mod, w_pa, w_pr, w_out, ln_g, ln_b)


W_CHUNK = D_MODEL


def _weight_chunks(w_hbm):
    return [(k, slice(j * W_CHUNK, (j + 1) * W_CHUNK))
            for k, w in enumerate(w_hbm) for j in range(w.shape[1] // W_CHUNK)]


def _weight_store(n, chunks, w_scr, w_bf_hbm, out_sem):
    k, cols = chunks[n]
    return pltpu.make_async_copy(w_scr[k].at[:, cols], w_bf_hbm[k].at[:, cols], out_sem.at[n])


def _stream_f32_chunks(jobs, stage, in_sem):
    def load(n):
        return pltpu.make_async_copy(jobs[n][0], stage.at[n % 2], in_sem.at[n % 2])

    load(0).start()
    for n, (_, sink) in enumerate(jobs):
        if n + 1 < len(jobs):
            load(n + 1).start()
        load(n).wait()
        sink(stage.at[n % 2])


def _ctx_kernel(x_ref, cctx_ref, c_ref, wmod_hbm, bmod_ref, win_hbm, wg_hbm, bg_ref, lp_ref, sg_ref,
                rd_ref, gg_ref, wpa_hbm, wpr_hbm, wo_hbm, lng_ref, lnb_ref,
                y_ref, k_ref, v_ref, sf_ref, sb_ref, mod_ref, win_bf, wg_bf, wpa_bf, wpr_bf, wo_bf,
                p_scr, oa_scr, or_scr, u_scr, s_scr, dec_scr, qdec_scr, kdec_scr,
                win_ref, wg_ref, wpa_ref, wpr_ref, wo_ref, cond_scr, stage, in_sem, out_sem,
                *, npairs):
    w_hbm = (win_hbm, wg_hbm, wpa_hbm, wpr_hbm, wo_hbm)
    w_scr = (win_ref, wg_ref, wpa_ref, wpr_ref, wo_ref)
    w_bf = (win_bf, wg_bf, wpa_bf, wpr_bf, wo_bf)

    @pl.when(pl.program_id(0) == 0)
    def _():
        cond_scr[...] = jnp.zeros(cond_scr.shape, F32)
        cond_scr[0:1, :] = cctx_ref[...]
        cond_scr[1:1 + c_ref.shape[0], :] = c_ref[...]
        cond = _silu(cond_scr[...]).astype(BF16)
        jobs = []
        for j in range(wmod_hbm.shape[1] // W_CHUNK):
            cols = slice(j * W_CHUNK, (j + 1) * W_CHUNK)

            def mod_sink(staged, cols=cols):
                mod_ref[:, cols] = _dot(cond, staged[...].astype(BF16)) + bmod_ref[:, cols]
            jobs.append((wmod_hbm.at[:, cols], mod_sink))
        chunks = _weight_chunks(w_hbm)
        for k, cols in chunks:
            def cast_sink(staged, k=k, cols=cols):
                w_scr[k][:, cols] = staged[...].astype(BF16)
            jobs.append((w_hbm[k].at[:, cols], cast_sink))
        _stream_f32_chunks(jobs, stage, in_sem)
        for n in range(len(chunks)):
            _weight_store(n, chunks, w_scr, w_bf, out_sem).start()
        _ret_build_tables(rd_ref, dec_scr, qdec_scr, kdec_scr, npairs=npairs, pair0=0)

    cols = lambda blk, w: p_scr.at[:, blk * LANES:blk * LANES + w]
    shift, scale, gate = _mod_row(mod_ref, 0)
    _project(_modulated_ln(x_ref, shift, scale), win_ref, wg_ref, bg_ref, p_scr,
             kv_out=(k_ref, v_ref))
    _attn_ctx_heads(cols(QA_BLK, D_MODEL), cols(KA_BLK, D_MODEL), cols(VA_BLK, D_MODEL),
                    cols(ZA_BLK, D_MODEL), _lam(lp_ref), sg_ref[...], oa_scr)
    _ret_pairs(rd_ref, cols(QR_BLK, D_MODEL // 2), cols(KR_BLK, D_MODEL // 2),
               cols(VR_BLK, D_MODEL), cols(ZR_BLK, D_MODEL), gg_ref, None, or_scr,
               (sf_ref, sb_ref), u_scr, s_scr, dec_scr, qdec_scr, kdec_scr,
               nc=x_ref.shape[0] // CHUNK, npairs=npairs, pair0=0)
    _out_rows(oa_scr, or_scr, cols(GA_BLK, D_MODEL), cols(GR_BLK, D_MODEL), x_ref, gate,
              wpa_ref, wpr_ref, wo_ref, lng_ref, lnb_ref, y_ref, sub=x_ref.shape[0])

    @pl.when(pl.program_id(0) == pl.num_programs(0) - 1)
    def _():
        chunks = _weight_chunks(w_hbm)
        for n in range(len(chunks)):
            _weight_store(n, chunks, w_scr, w_bf, out_sem).wait()


def _ctx_call(x2d, c_ctx, c, w_mod, b_mod, w_in, w_gate, b_gate, lam_params, subln_g, ret_decay,
              gn_g, w_pa, w_pr, w_out, ln_g, ln_b, *, batch, seq):
    npairs = N_HEADS // 2
    nc = seq // CHUNK
    tile = pl.BlockSpec((seq, D_MODEL), lambda b: (b, 0))
    vec = lambda w: pl.BlockSpec((1, w), lambda b: (0, 0))
    state = pl.BlockSpec((None, N_HEADS, DK_R, DV), lambda b: (b, 0, 0, 0))
    heads = pl.BlockSpec((seq * N_HEADS, LANES), lambda b: (b, 0))
    tok = jax.ShapeDtypeStruct((batch * seq, D_MODEL), F32)
    tok_heads = jax.ShapeDtypeStruct((batch * seq * N_HEADS, LANES), F32)
    st = jax.ShapeDtypeStruct((batch, N_HEADS, DK_R, DV), F32)
    hbm = pl.BlockSpec(memory_space=pl.ANY)
    weights = (w_in, w_gate, w_pa, w_pr, w_out)
    n_chunks = sum(w.shape[1] // W_CHUNK for w in weights)
    return pl.pallas_call(
        functools.partial(_ctx_kernel, npairs=npairs),
        grid=(batch,),
        in_specs=[tile, vec(D_MODEL), pl.BlockSpec(c.shape, lambda b: (0, 0)), hbm,
                  vec(3 * D_MODEL), hbm, hbm, vec(2 * D_MODEL),
                  pl.BlockSpec((4, DK_A), lambda b: (0, 0)), vec(DV),
                  pl.BlockSpec(memory_space=pltpu.SMEM), vec(D_MODEL),
                  hbm, hbm, hbm, vec(D_MODEL), vec(D_MODEL)],
        out_specs=[tile, heads, heads, state, state,
                   pl.BlockSpec((8, 3 * D_MODEL), lambda b: (0, 0))] + [hbm] * len(weights),
        out_shape=[tok, tok_heads, tok_heads, st, st, jax.ShapeDtypeStruct((8, 3 * D_MODEL), F32)]
                  + [jax.ShapeDtypeStruct(w.shape, BF16) for w in weights],
        scratch_shapes=[pltpu.VMEM((seq, P_WIDTH), BF16),
                        pltpu.VMEM((seq, D_MODEL), BF16),
                        pltpu.VMEM((seq, D_MODEL), BF16)] + _ret_scratch(npairs, nc)
                       + [pltpu.VMEM(w.shape, BF16) for w in weights]
                       + [pltpu.VMEM((8, D_MODEL), F32),
                          pltpu.VMEM((2, D_MODEL, W_CHUNK), F32),
                          pltpu.SemaphoreType.DMA((2,)),
                          pltpu.SemaphoreType.DMA((n_chunks,))],
        compiler_params=_params(("arbitrary",)),
        name="ctx",
    )(x2d, c_ctx, c, w_mod, b_mod, w_in, w_gate, b_gate, lam_params, subln_g, ret_decay, gn_g,
      w_pa, w_pr, w_out, ln_g, ln_b)


def _rope_tables(n_tokens):
    rows = n_tokens // GRID_W
    r = np.repeat(np.arange(rows, dtype=np.float32), GRID_W)
    col = np.tile(np.arange(GRID_W, dtype=np.float32), rows)
    n_freq = DK_A // 4
    inv = np.float32(ROPE_BASE) ** (-np.arange(n_freq, dtype=np.float32) / np.float32(n_freq))
    ang = np.concatenate([r[:, None] * inv, col[:, None] * inv], axis=-1).astype(np.float32)
    cos = np.repeat(np.cos(ang), 2, axis=-1)
    sin = np.repeat(np.sin(ang), 2, axis=-1)
    even = (np.arange(DK_A) % 2 == 0)[None, :]
    sin_even = np.where(even, -sin, 0.0)
    sin_odd = np.where(even, 0.0, sin)
    two = lambda t: jnp.asarray(np.concatenate([t, t], axis=-1), F32)
    return two(cos), two(sin_even), two(sin_odd)


def kernel(x_prompt, x_sample, cache_attn_k, cache_attn_v, state_ret_fwd, state_ret_bwd,
           c, c_ctx, w_mod, b_mod, w_in, lam_params, subln_g, ret_decay, ret_gn_g,
           w_pa, w_pr, w_gate, b_gate, w_out, ln_g, ln_b):
    batch, seq, _ = x_prompt.shape
    dbatch, dseq, _ = x_sample.shape
    past = cache_attn_k.shape[2]
    l = 0

    bg = b_gate[l][None, :]
    lp, sg = lam_params[l], subln_g[l][None, :]
    rd, gg = ret_decay[l], ret_gn_g[l][None, :]
    lng, lnb = ln_g[l][None, :], ln_b[l][None, :]

    xc = x_prompt.reshape(batch * seq, D_MODEL)
    y_c, k_c, v_c, sf_c, sb_c, mod, win, wg, wpa, wpr, wo = _ctx_call(
        xc, c_ctx[None, :], c, w_mod[l], b_mod[l][None, :], w_in[l], w_gate[l], bg, lp, sg, rd, gg,
        w_pa[l], w_pr[l], w_out[l], lng, lnb, batch=batch, seq=seq)

    xs = x_sample.reshape(dbatch * dseq, D_MODEL)
    p_s = _proj_lat_call(xs, mod, win, wg, bg, _rope_tables(dseq), seq=dseq, mod_row0=1)
    ck = cache_attn_k[:, l].reshape(dbatch, past * N_HEADS, 2 * DK_A)
    cv = cache_attn_v[:, l].reshape(dbatch, past * N_HEADS, DV)
    oa_s = _attn_lat_call(p_s, ck, cv, lp, sg, batch=dbatch, seq=dseq, past=past)
    or_s = _ret_lat_call(p_s, rd, gg, state_ret_fwd[:, l], state_ret_bwd[:, l],
                         batch=dbatch, seq=dseq)
    y_s = _out_lat_call(oa_s, or_s, p_s, xs, mod, wpa, wpr, wo, lng, lnb, seq=dseq, mod_row0=1)

    return (y_c.reshape(batch, seq, D_MODEL),
            y_s.reshape(dbatch, dseq, D_MODEL),
            k_c.reshape(batch, 1, seq, N_HEADS, 2 * DK_A),
            v_c.reshape(batch, 1, seq, N_HEADS, DV),
            sf_c.reshape(batch, 1, N_HEADS, DK_R, DV),
            sb_c.reshape(batch, 1, N_HEADS, DK_R, DV))
```

```python
import functools
import math

import jax
import jax.numpy as jnp
import numpy as np
from jax import lax
from jax.experimental import pallas as pl
from jax.experimental.pallas import tpu as pltpu

F32 = jnp.float32
BF16 = jnp.bfloat16

D_MODEL = 1024
N_HEADS = 8
DK_A = 64
DV = 128
DK_R = 64
CHUNK = 256
GRID_W = 64
ROPE_BASE = 10000.0
MOD_EPS = 1e-6
LN_EPS = 1e-5
DEPTH = 1
DEEPNORM_ALPHA = (2.0 * DEPTH) ** 0.25
LAM_INIT = 0.8 - 0.6 * math.exp(-0.3 * 0)
LOG2E = math.log2(math.e)

LANES = 128
QA_BLK, KA_BLK, VA_BLK, ZA_BLK = 0, 8, 16, 24
QR_BLK, KR_BLK, VR_BLK, ZR_BLK = 32, 36, 40, 48
GA_BLK, GR_BLK = 56, 64
IN_SEGS = 7
P_WIDTH = (IN_SEGS + 2) * D_MODEL
VMEM_LIMIT = 56 * 1024 * 1024


def _params(sem):
    return pltpu.CompilerParams(dimension_semantics=sem, vmem_limit_bytes=VMEM_LIMIT)


def _resident(shape):
    return pl.BlockSpec(shape, lambda *_: tuple(0 for _ in shape), pipeline_mode=pl.Buffered(1))


def _silu(z):
    return z * (1.0 / (1.0 + jnp.exp(-z)))


def _dot(a, b):
    return jnp.dot(a, b, preferred_element_type=F32)


def _dot_nt(a, b):
    return lax.dot_general(a, b, (((1,), (1,)), ((), ())), preferred_element_type=F32)


def _dot_tn(a, b):
    return lax.dot_general(a, b, (((0,), (0,)), ((), ())), preferred_element_type=F32)


def _rope(acc, cos, sin_even, sin_odd):
    outs = []
    for hh in range(N_HEADS):
        xs = acc[:, hh * LANES:(hh + 1) * LANES]
        nxt = pltpu.roll(xs, LANES - 1, 1)
        prv = pltpu.roll(xs, 1, 1)
        outs.append(xs * cos + nxt * sin_even + prv * sin_odd)
    return jnp.concatenate(outs, axis=1)


def _mod_row(mod_ref, row):
    r = pl.ds(row, 1)
    return (mod_ref[r, 0:D_MODEL], mod_ref[r, D_MODEL:2 * D_MODEL], mod_ref[r, 2 * D_MODEL:3 * D_MODEL])


def _modulated_ln(x_ref, shift, scale):
    x = x_ref[...]
    mu = jnp.mean(x, axis=-1, keepdims=True)
    xc = x - mu
    var = jnp.mean(xc * xc, axis=-1, keepdims=True)
    return (xc * lax.rsqrt(var + MOD_EPS) * (1.0 + scale) + shift).astype(BF16)


def _store_heads(ref, val):
    for h in range(N_HEADS):
        ref[pl.ds(h, val.shape[0], stride=N_HEADS), :] = val[:, h * LANES:(h + 1) * LANES]


def _project(h, win_ref, wg_ref, bg_ref, p_ref, rope=None, kv_out=None):
    def seg(j):
        if j < IN_SEGS:
            return _dot(h, win_ref[:, j * D_MODEL:(j + 1) * D_MODEL])
        return _dot(h, wg_ref[:, (j - IN_SEGS) * D_MODEL:(j - IN_SEGS + 1) * D_MODEL])

    def put(j, val):
        p_ref[:, j * D_MODEL:(j + 1) * D_MODEL] = val.astype(BF16)

    qa = seg(0)
    ka = seg(1)
    if rope is not None:
        cos, se, so = (t[...] for t in rope)
        qa = _rope(qa, cos, se, so)
        ka = _rope(ka, cos, se, so)
    if kv_out is not None:
        _store_heads(kv_out[0], ka)
    put(0, qa * (DK_A ** -0.5 * LOG2E))
    put(1, ka)
    va = seg(2)
    if kv_out is not None:
        _store_heads(kv_out[1], va)
    put(2, va)
    put(3, _silu(seg(3)))
    qk = seg(4)
    put(4, jnp.concatenate([qk[:, :D_MODEL // 2], qk[:, D_MODEL // 2:] * (DK_R ** -0.5)], axis=1))
    put(5, seg(5))
    put(6, _silu(seg(6)))
    for j in (IN_SEGS, IN_SEGS + 1):
        g = seg(j) + bg_ref[:, (j - IN_SEGS) * D_MODEL:(j - IN_SEGS + 1) * D_MODEL]
        put(j, 1.0 / (1.0 + jnp.exp(-g)))


def _proj_lat_kernel(x_ref, mod_ref, win_ref, wg_ref, bg_ref, cos_ref, se_ref, so_ref, p_ref,
                     *, per_seq, mod_row0):
    shift, scale, _ = _mod_row(mod_ref, mod_row0 + pl.program_id(0) // per_seq)
    _project(_modulated_ln(x_ref, shift, scale), win_ref, wg_ref, bg_ref, p_ref,
             rope=(cos_ref, se_ref, so_ref))


def _proj_lat_call(x2d, mod, w_in, w_gate, b_gate, rope, *, seq, mod_row0):
    m = x2d.shape[0]
    tm = 256
    per_seq = seq // tm
    return pl.pallas_call(
        functools.partial(_proj_lat_kernel, per_seq=per_seq, mod_row0=mod_row0),
        grid=(m // tm,),
        in_specs=[
            pl.BlockSpec((tm, D_MODEL), lambda i: (i, 0)),
            pl.BlockSpec((8, 3 * D_MODEL), lambda i: (0, 0)),
            _resident((D_MODEL, IN_SEGS * D_MODEL)),
            _resident((D_MODEL, 2 * D_MODEL)),
            pl.BlockSpec((1, 2 * D_MODEL), lambda i: (0, 0)),
        ] + [pl.BlockSpec((tm, LANES), lambda i: (i % per_seq, 0))] * 3,
        out_specs=pl.BlockSpec((tm, P_WIDTH), lambda i: (i, 0)),
        out_shape=jax.ShapeDtypeStruct((m, P_WIDTH), BF16),
        compiler_params=_params(("arbitrary",)),
        name="proj_lat",
    )(x2d, mod, w_in, w_gate, b_gate, *rope)


def _lam(lp_ref):
    lp = lp_ref[...]
    a = jnp.sum(lp[0:1] * lp[1:2], axis=-1, keepdims=True)
    b = jnp.sum(lp[2:3] * lp[3:4], axis=-1, keepdims=True)
    return jnp.exp(a) - jnp.exp(b) + LAM_INIT


def _attn_rows(q, k_all, v_ones, z, lam, g):
    lane = lax.broadcasted_iota(jnp.int32, (1, LANES), 1)
    zero = jnp.zeros_like(q)
    outs = []
    for qm in (jnp.where(lane < DK_A, q, zero), jnp.where(lane >= DK_A, q, zero)):
        s = _dot_nt(qm, k_all)
        mx = jnp.max(s, axis=-1, keepdims=True)
        p = jnp.exp2(s - mx).astype(BF16)
        ol = _dot(p, v_ones)
        outs.append(ol[:, :DV] * (1.0 / ol[:, DV:]))
    oa = outs[0] - lam * outs[1]
    oa = oa * lax.rsqrt(jnp.mean(oa * oa, axis=-1, keepdims=True) + LN_EPS)
    oa = oa * g * (1.0 - LAM_INIT)
    return oa * z.astype(F32)


def _attn_ctx_heads(q_ref, k_ref, v_ref, z_ref, lam, g, o_ref):
    ones = jnp.ones((q_ref.shape[0], DV), BF16)
    for h in range(N_HEADS):
        sl = slice(h * LANES, (h + 1) * LANES)
        v_ones = jnp.concatenate([v_ref[:, sl], ones], axis=1)
        o_ref[:, sl] = _attn_rows(q_ref[:, sl], k_ref[:, sl], v_ones, z_ref[:, sl],
                                  lam, g).astype(BF16)


def _attn_lat_kernel(q_ref, k_ref, v_ref, z_ref, kc_ref, vc_ref, lp_ref, g_ref, o_ref,
                     k_all, v_all, *, past, tc):
    @pl.when(pl.program_id(2) == 0)
    def _():
        head_rows = pl.ds(pl.program_id(1), past, stride=N_HEADS)
        k_all[0:past, :] = kc_ref[head_rows, :].astype(BF16)
        v_all[0:past, 0:DV] = vc_ref[head_rows, :].astype(BF16)
        k_all[past:, :] = k_ref[...]
        v_all[past:, 0:DV] = v_ref[...]
        v_all[:, DV:] = jnp.ones((v_all.shape[0], DV), BF16)

    lam = _lam(lp_ref)
    g = g_ref[...]
    for lo in range(0, q_ref.shape[0], tc):
        o_ref[lo:lo + tc, :] = _attn_rows(q_ref[lo:lo + tc, :], k_all[...], v_all[...],
                                          z_ref[lo:lo + tc, :], lam, g).astype(BF16)


def _attn_lat_call(p, cache_k, cache_v, lam_params, subln_g, *, batch, seq, past):
    tq, tc = 2048, 128
    nq = seq // tq
    return pl.pallas_call(
        functools.partial(_attn_lat_kernel, past=past, tc=tc),
        grid=(batch, N_HEADS, nq),
        in_specs=[
            pl.BlockSpec((tq, LANES), lambda b, h, i: (b * nq + i, QA_BLK + h)),
            pl.BlockSpec((seq, LANES), lambda b, h, i: (b, KA_BLK + h)),
            pl.BlockSpec((seq, LANES), lambda b, h, i: (b, VA_BLK + h)),
            pl.BlockSpec((tq, LANES), lambda b, h, i: (b * nq + i, ZA_BLK + h)),
            pl.BlockSpec((None, past * N_HEADS, LANES), lambda b, h, i: (b, 0, 0)),
            pl.BlockSpec((None, past * N_HEADS, LANES), lambda b, h, i: (b, 0, 0)),
            pl.BlockSpec((4, DK_A), lambda b, h, i: (0, 0)),
            pl.BlockSpec((1, DV), lambda b, h, i: (0, 0)),
        ],
        out_specs=pl.BlockSpec((tq, LANES), lambda b, h, i: (b * nq + i, h)),
        out_shape=jax.ShapeDtypeStruct((batch * seq, D_MODEL), BF16),
        scratch_shapes=[pltpu.VMEM((past + seq, LANES), BF16),
                        pltpu.VMEM((past + seq, 2 * DV), BF16)],
        compiler_params=_params(("arbitrary", "arbitrary", "arbitrary")),
        name="attn_lat",
    )(p, p, p, p, cache_k, cache_v, lam_params, subln_g)


PW, VW = 2 * DK_R, 2 * DV


def _ret_scratch(npairs, nc):
    stacked_state = (npairs, nc, 2 * PW, VW)
    return [pltpu.VMEM(stacked_state, F32),
            pltpu.VMEM(stacked_state, BF16),
            pltpu.VMEM((npairs, 2, CHUNK, CHUNK), F32),
            pltpu.VMEM((npairs, CHUNK, 2 * PW), F32),
            pltpu.VMEM((npairs, CHUNK, 2 * PW), F32)]


def _log_gammas(rd_ref, pair):
    def one(d, hh):
        r = jnp.full((1, VW), rd_ref[d, 2 * pair + hh], F32)
        return jnp.log1p(-jnp.exp2(r))
    return [[one(d, hh) for hh in range(2)] for d in range(2)]


def _ret_build_tables(rd_ref, dec_scr, qdec_scr, kdec_scr, *, npairs, pair0):
    lane_q = lax.broadcasted_iota(jnp.int32, (1, PW), 1)
    rel = (lax.broadcasted_iota(jnp.int32, (CHUNK, CHUNK), 0)
           - lax.broadcasted_iota(jnp.int32, (CHUNK, CHUNK), 1)).astype(F32)
    row_q = lax.broadcasted_iota(jnp.int32, (CHUNK, PW), 0).astype(F32)
    for pi in range(npairs):
        lg = _log_gammas(rd_ref, pair0 + pi)
        lg_q = [jnp.where(lane_q < DK_R, lg[d][0][:, :PW], lg[d][1][:, :PW])
                for d in range(2)]
        for hh in range(2):
            dec_scr[pi, hh] = (
                jnp.where(rel >= 0, jnp.exp(lg[0][hh][:, :1] * jnp.maximum(rel, 0.0)), 0.0)
                + jnp.where(rel <= 0, jnp.exp(lg[1][hh][:, :1] * jnp.maximum(-rel, 0.0)), 0.0))
        qdec_scr[pi] = jnp.concatenate([jnp.exp(lg_q[0] * (row_q + 1.0)),
                                        jnp.exp(lg_q[1] * (CHUNK - row_q))], axis=1)
        kdec_scr[pi] = jnp.concatenate([jnp.exp(lg_q[0] * (CHUNK - 1.0 - row_q)),
                                        jnp.exp(lg_q[1] * row_q)], axis=1)


def _ret_pairs(rd_ref, q_ref, k_ref, v_ref, z_ref, g_ref, s0_refs, o_ref, sout_refs,
               u_scr, s_scr, dec_scr, qdec_scr, kdec_scr, *, nc, npairs, pair0):
    cross = s0_refs is not None or nc > 1
    lane_q = lax.broadcasted_iota(jnp.int32, (1, PW), 1)
    lane_v = lax.broadcasted_iota(jnp.int32, (1, VW), 1)
    srow = lax.broadcasted_iota(jnp.int32, (2 * PW, VW), 0)
    scol = lax.broadcasted_iota(jnp.int32, (2 * PW, VW), 1)
    diag_blocks = ((srow % PW) // DK_R) == (scol // DV)

    def rows(c):
        return slice(c * CHUNK, (c + 1) * CHUNK)

    for pi in range(npairs):
        qs = slice(pi * PW, (pi + 1) * PW)
        vs = slice(pi * VW, (pi + 1) * VW)
        lg = _log_gammas(rd_ref, pair0 + pi)
        lg_v = [jnp.where(lane_v < DV, lg[d][0], lg[d][1]) for d in range(2)]
        chunk_decay = [jnp.exp(lg_v[d] * CHUNK) for d in range(2)]

        for c in range(nc):
            k = k_ref[rows(c), qs].astype(F32)
            kd = (jnp.concatenate([k, k], axis=1) * kdec_scr[pi]).astype(BF16)
            u_scr[pi, c] = jnp.where(diag_blocks, _dot_tn(kd, v_ref[rows(c), vs]), 0.0)

        for d, order in ((0, range(nc)), (1, reversed(range(nc)))):
            half = slice(d * PW, (d + 1) * PW)
            if s0_refs is None:
                s = jnp.zeros((PW, VW), F32)
            else:
                zero = jnp.zeros((DK_R, DV), F32)
                s = jnp.concatenate(
                    [jnp.concatenate([s0_refs[d][2 * pi], zero], axis=1),
                     jnp.concatenate([zero, s0_refs[d][2 * pi + 1]], axis=1)], axis=0)
            for c in order:
                if cross:
                    s_scr[pi, c, half, :] = s.astype(BF16)
                s = s * chunk_decay[d] + u_scr[pi, c, half, :]
            if sout_refs is not None:
                sout_refs[d][2 * pi] = s[:DK_R, :DV]
                sout_refs[d][2 * pi + 1] = s[DK_R:, DV:]

        g = g_ref[:, vs]
        for c in range(nc):
            q = q_ref[rows(c), qs]
            k = k_ref[rows(c), qs]
            v = v_ref[rows(c), vs]
            zero = jnp.zeros_like(q)
            q2 = jnp.concatenate([jnp.where(lane_q < DK_R, q, zero),
                                  jnp.where(lane_q >= DK_R, q, zero)], axis=0)
            a2 = _dot_nt(q2, k)
            o = jnp.concatenate(
                [_dot((a2[hh * CHUNK:(hh + 1) * CHUNK] * dec_scr[pi, hh]).astype(BF16),
                      v[:, hh * DV:(hh + 1) * DV]) for hh in range(2)], axis=1)
            if cross:
                qf = q.astype(F32)
                qq = (jnp.concatenate([qf, qf], axis=1) * qdec_scr[pi]).astype(BF16)
                o = o + _dot(qq, s_scr[pi, c])
            outs = []
            for hh in range(2):
                oh = o[:, hh * DV:(hh + 1) * DV]
                mu = jnp.mean(oh, axis=-1, keepdims=True)
                oc = oh - mu
                var = jnp.mean(oc * oc, axis=-1, keepdims=True)
                outs.append(oc * lax.rsqrt(var + LN_EPS))
            o = jnp.concatenate(outs, axis=1) * g * z_ref[rows(c), vs].astype(F32)
            o_ref[rows(c), vs] = o.astype(BF16)


def _ret_lat_kernel(rd_ref, q_ref, k_ref, v_ref, z_ref, g_ref, s0f_ref, s0b_ref, o_ref,
                    u_scr, s_scr, dec_scr, qdec_scr, kdec_scr, *, nc, npairs):
    pair0 = pl.program_id(0) * npairs

    @pl.when(pl.program_id(1) == 0)
    def _():
        _ret_build_tables(rd_ref, dec_scr, qdec_scr, kdec_scr, npairs=npairs, pair0=pair0)

    _ret_pairs(rd_ref, q_ref, k_ref, v_ref, z_ref, g_ref, (s0f_ref, s0b_ref), o_ref, None,
               u_scr, s_scr, dec_scr, qdec_scr, kdec_scr, nc=nc, npairs=npairs, pair0=pair0)


def _ret_lat_call(p, ret_decay, gn_g, s0f, s0b, *, batch, seq):
    nc = seq // CHUNK
    hps = 2
    npairs = hps // 2
    qw, vw = DK_R * hps, DV * hps
    col = lambda w, blk: pl.BlockSpec((seq, w), lambda j, b: (b, blk * LANES // w + j))
    state = pl.BlockSpec((None, hps, DK_R, DV), lambda j, b: (b, j, 0, 0))
    return pl.pallas_call(
        functools.partial(_ret_lat_kernel, nc=nc, npairs=npairs),
        grid=(N_HEADS // hps, batch),
        in_specs=[pl.BlockSpec(memory_space=pltpu.SMEM),
                  col(qw, QR_BLK), col(qw, KR_BLK), col(vw, VR_BLK), col(vw, ZR_BLK),
                  pl.BlockSpec((1, vw), lambda j, b: (0, j)), state, state],
        out_specs=pl.BlockSpec((seq, vw), lambda j, b: (b, j)),
        out_shape=jax.ShapeDtypeStruct((batch * seq, D_MODEL), BF16),
        scratch_shapes=_ret_scratch(npairs, nc),
        compiler_params=_params(("arbitrary", "arbitrary")),
        name="ret_lat",
    )(ret_decay, p, p, p, p, gn_g, s0f, s0b)


def _out_rows(oa_ref, or_ref, ga_ref, gr_ref, x_ref, gate, wpa_ref, wpr_ref, wo_ref,
              lng_ref, lnb_ref, y_ref, *, sub):
    for s in range(y_ref.shape[0] // sub):
        rows = slice(s * sub, (s + 1) * sub)
        a = _dot(oa_ref[rows, :], wpa_ref[...])
        r = _dot(or_ref[rows, :], wpr_ref[...])
        m = ga_ref[rows, :].astype(F32) * a + gr_ref[rows, :].astype(F32) * r
        out = _dot(m.astype(BF16), wo_ref[...])
        t = DEEPNORM_ALPHA * x_ref[rows, :] + gate * out
        mu = jnp.mean(t, axis=-1, keepdims=True)
        tc = t - mu
        var = jnp.mean(tc * tc, axis=-1, keepdims=True)
        y_ref[rows, :] = tc * lax.rsqrt(var + LN_EPS) * lng_ref[...] + lnb_ref[...]


def _out_lat_kernel(oa_ref, or_ref, ga_ref, gr_ref, x_ref, mod_ref, wpa_ref, wpr_ref, wo_ref,
                    lng_ref, lnb_ref, y_ref, *, sub, per_seq, mod_row0):
    _, _, gate = _mod_row(mod_ref, mod_row0 + pl.program_id(0) // per_seq)
    _out_rows(oa_ref, or_ref, ga_ref, gr_ref, x_ref, gate, wpa_ref, wpr_ref, wo_ref,
              lng_ref, lnb_ref, y_ref, sub=sub)


def _out_lat_call(oa, orr, p, x2d, mod, w_pa, w_pr, w_out, ln_g, ln_b, *, seq, mod_row0):
    m = x2d.shape[0]
    tm, sub = 512, 256
    per_seq = seq // tm
    tile = lambda j: pl.BlockSpec((tm, D_MODEL), lambda i: (i, j))
    vec = pl.BlockSpec((1, D_MODEL), lambda i: (0, 0))
    return pl.pallas_call(
        functools.partial(_out_lat_kernel, sub=sub, per_seq=per_seq, mod_row0=mod_row0),
        grid=(m // tm,),
        in_specs=[tile(0), tile(0), tile(GA_BLK * LANES // D_MODEL), tile(GR_BLK * LANES // D_MODEL),
                  tile(0),
                  pl.BlockSpec((8, 3 * D_MODEL), lambda i: (0, 0)),
                  _resident((D_MODEL, D_MODEL)), _resident((D_MODEL, D_MODEL)),
                  _resident((D_MODEL, D_MODEL)), vec, vec],
        out_specs=tile(0),
        out_shape=jax.ShapeDtypeStruct((m, D_MODEL), F32),
        compiler_params=_params(("arbitrary",)),
        name="out_lat",
    )(oa, orr, p, p, x2d, mod, w_pa, w_pr, w_out, ln_g, ln_b)


W_CHUNK = D_MODEL // 2
W_RING = 3


def _weight_chunks(w_hbm):
    return [(k, slice(j * W_CHUNK, (j + 1) * W_CHUNK))
            for k, w in enumerate(w_hbm) for j in range(w.shape[1] // W_CHUNK)]


def _weight_store(n, chunks, w_scr, w_bf_hbm, out_sem):
    k, cols = chunks[n]
    return pltpu.make_async_copy(w_scr[k].at[:, cols], w_bf_hbm[k].at[:, cols], out_sem.at[n])


def _stream_f32_chunks(jobs, stage, in_sem):
    depth = stage.shape[0]

    def load(n):
        return pltpu.make_async_copy(jobs[n][0], stage.at[n % depth], in_sem.at[n % depth])

    for n in range(min(depth - 1, len(jobs))):
        load(n).start()
    for n, (_, sink) in enumerate(jobs):
        if n + depth - 1 < len(jobs):
            load(n + depth - 1).start()
        load(n).wait()
        sink(stage.at[n % depth])


def _ctx_kernel(x_ref, cctx_ref, c_ref, wmod_hbm, bmod_ref, win_hbm, wg_hbm, bg_ref, lp_ref, sg_ref,
                rd_ref, gg_ref, wpa_hbm, wpr_hbm, wo_hbm, lng_ref, lnb_ref,
                y_ref, k_ref, v_ref, sf_ref, sb_ref, mod_ref, win_bf, wg_bf, wpa_bf, wpr_bf, wo_bf,
                p_scr, oa_scr, or_scr, u_scr, s_scr, dec_scr, qdec_scr, kdec_scr,
                win_ref, wg_ref, wpa_ref, wpr_ref, wo_ref, cond_scr, stage, in_sem, out_sem,
                *, npairs):
    w_hbm = (win_hbm, wg_hbm, wpa_hbm, wpr_hbm, wo_hbm)
    w_scr = (win_ref, wg_ref, wpa_ref, wpr_ref, wo_ref)
    w_bf = (win_bf, wg_bf, wpa_bf, wpr_bf, wo_bf)

    @pl.when(pl.program_id(0) == 0)
    def _():
        cond_scr[...] = jnp.zeros(cond_scr.shape, F32)
        cond_scr[0:1, :] = cctx_ref[...]
        cond_scr[1:1 + c_ref.shape[0], :] = c_ref[...]
        cond = _silu(cond_scr[...]).astype(BF16)
        jobs = []
        for j in range(wmod_hbm.shape[1] // W_CHUNK):
            cols = slice(j * W_CHUNK, (j + 1) * W_CHUNK)

            def mod_sink(staged, cols=cols):
                mod_ref[:, cols] = _dot(cond, staged[...].astype(BF16)) + bmod_ref[:, cols]
            jobs.append((wmod_hbm.at[:, cols], mod_sink))
        chunks = _weight_chunks(w_hbm)
        for k, cols in chunks:
            def cast_sink(staged, k=k, cols=cols):
                w_scr[k][:, cols] = staged[...].astype(BF16)
            jobs.append((w_hbm[k].at[:, cols], cast_sink))
        _stream_f32_chunks(jobs, stage, in_sem)
        for n in range(len(chunks)):
            _weight_store(n, chunks, w_scr, w_bf, out_sem).start()
        _ret_build_tables(rd_ref, dec_scr, qdec_scr, kdec_scr, npairs=npairs, pair0=0)

    cols = lambda blk, w: p_scr.at[:, blk * LANES:blk * LANES + w]
    shift, scale, gate = _mod_row(mod_ref, 0)
    _project(_modulated_ln(x_ref, shift, scale), win_ref, wg_ref, bg_ref, p_scr,
             kv_out=(k_ref, v_ref))
    _attn_ctx_heads(cols(QA_BLK, D_MODEL), cols(KA_BLK, D_MODEL), cols(VA_BLK, D_MODEL),
                    cols(ZA_BLK, D_MODEL), _lam(lp_ref), sg_ref[...], oa_scr)
    _ret_pairs(rd_ref, cols(QR_BLK, D_MODEL // 2), cols(KR_BLK, D_MODEL // 2),
               cols(VR_BLK, D_MODEL), cols(ZR_BLK, D_MODEL), gg_ref, None, or_scr,
               (sf_ref, sb_ref), u_scr, s_scr, dec_scr, qdec_scr, kdec_scr,
               nc=x_ref.shape[0] // CHUNK, npairs=npairs, pair0=0)
    _out_rows(oa_scr, or_scr, cols(GA_BLK, D_MODEL), cols(GR_BLK, D_MODEL), x_ref, gate,
              wpa_ref, wpr_ref, wo_ref, lng_ref, lnb_ref, y_ref, sub=x_ref.shape[0])

    @pl.when(pl.program_id(0) == pl.num_programs(0) - 1)
    def _():
        chunks = _weight_chunks(w_hbm)
        for n in range(len(chunks)):
            _weight_store(n, chunks, w_scr, w_bf, out_sem).wait()


def _ctx_call(x2d, c_ctx, c, w_mod, b_mod, w_in, w_gate, b_gate, lam_params, subln_g, ret_decay,
              gn_g, w_pa, w_pr, w_out, ln_g, ln_b, *, batch, seq):
    npairs = N_HEADS // 2
    nc = seq // CHUNK
    tile = pl.BlockSpec((seq, D_MODEL), lambda b: (b, 0))
    vec = lambda w: pl.BlockSpec((1, w), lambda b: (0, 0))
    state = pl.BlockSpec((None, N_HEADS, DK_R, DV), lambda b: (b, 0, 0, 0))
    heads = pl.BlockSpec((seq * N_HEADS, LANES), lambda b: (b, 0))
    tok = jax.ShapeDtypeStruct((batch * seq, D_MODEL), F32)
    tok_heads = jax.ShapeDtypeStruct((batch * seq * N_HEADS, LANES), F32)
    st = jax.ShapeDtypeStruct((batch, N_HEADS, DK_R, DV), F32)
    hbm = pl.BlockSpec(memory_space=pl.ANY)
    weights = (w_in, w_gate, w_pa, w_pr, w_out)
    n_chunks = sum(w.shape[1] // W_CHUNK for w in weights)
    return pl.pallas_call(
        functools.partial(_ctx_kernel, npairs=npairs),
        grid=(batch,),
        in_specs=[tile, vec(D_MODEL), pl.BlockSpec(c.shape, lambda b: (0, 0)), hbm,
                  vec(3 * D_MODEL), hbm, hbm, vec(2 * D_MODEL),
                  pl.BlockSpec((4, DK_A), lambda b: (0, 0)), vec(DV),
                  pl.BlockSpec(memory_space=pltpu.SMEM), vec(D_MODEL),
                  hbm, hbm, hbm, vec(D_MODEL), vec(D_MODEL)],
        out_specs=[tile, heads, heads, state, state,
                   pl.BlockSpec((8, 3 * D_MODEL), lambda b: (0, 0))] + [hbm] * len(weights),
        out_shape=[tok, tok_heads, tok_heads, st, st, jax.ShapeDtypeStruct((8, 3 * D_MODEL), F32)]
                  + [jax.ShapeDtypeStruct(w.shape, BF16) for w in weights],
        scratch_shapes=[pltpu.VMEM((seq, P_WIDTH), BF16),
                        pltpu.VMEM((seq, D_MODEL), BF16),
                        pltpu.VMEM((seq, D_MODEL), BF16)] + _ret_scratch(npairs, nc)
                       + [pltpu.VMEM(w.shape, BF16) for w in weights]
                       + [pltpu.VMEM((8, D_MODEL), F32),
                          pltpu.VMEM((W_RING, D_MODEL, W_CHUNK), F32),
                          pltpu.SemaphoreType.DMA((W_RING,)),
                          pltpu.SemaphoreType.DMA((n_chunks,))],
        compiler_params=_params(("arbitrary",)),
        name="ctx",
    )(x2d, c_ctx, c, w_mod, b_mod, w_in, w_gate, b_gate, lam_params, subln_g, ret_decay, gn_g,
      w_pa, w_pr, w_out, ln_g, ln_b)


def _rope_tables(n_tokens):
    rows = n_tokens // GRID_W
    r = np.repeat(np.arange(rows, dtype=np.float32), GRID_W)
    col = np.tile(np.arange(GRID_W, dtype=np.float32), rows)
    n_freq = DK_A // 4
    inv = np.float32(ROPE_BASE) ** (-np.arange(n_freq, dtype=np.float32) / np.float32(n_freq))
    ang = np.concatenate([r[:, None] * inv, col[:, None] * inv], axis=-1).astype(np.float32)
    cos = np.repeat(np.cos(ang), 2, axis=-1)
    sin = np.repeat(np.sin(ang), 2, axis=-1)
    even = (np.arange(DK_A) % 2 == 0)[None, :]
    sin_even = np.where(even, -sin, 0.0)
    sin_odd = np.where(even, 0.0, sin)
    two = lambda t: jnp.asarray(np.concatenate([t, t], axis=-1), F32)
    return two(cos), two(sin_even), two(sin_odd)


def kernel(x_prompt, x_sample, cache_attn_k, cache_attn_v, state_ret_fwd, state_ret_bwd,
           c, c_ctx, w_mod, b_mod, w_in, lam_params, subln_g, ret_decay, ret_gn_g,
           w_pa, w_pr, w_gate, b_gate, w_out, ln_g, ln_b):
    batch, seq, _ = x_prompt.shape
    dbatch, dseq, _ = x_sample.shape
    past = cache_attn_k.shape[2]
    l = 0

    bg = b_gate[l][None, :]
    lp, sg = lam_params[l], subln_g[l][None, :]
    rd, gg = ret_decay[l], ret_gn_g[l][None, :]
    lng, lnb = ln_g[l][None, :], ln_b[l][None, :]

    xc = x_prompt.reshape(batch * seq, D_MODEL)
    y_c, k_c, v_c, sf_c, sb_c, mod, win, wg, wpa, wpr, wo = _ctx_call(
        xc, c_ctx[None, :], c, w_mod[l], b_mod[l][None, :], w_in[l], w_gate[l], bg, lp, sg, rd, gg,
        w_pa[l], w_pr[l], w_out[l], lng, lnb, batch=batch, seq=seq)

    xs = x_sample.reshape(dbatch * dseq, D_MODEL)
    p_s = _proj_lat_call(xs, mod, win, wg, bg, _rope_tables(dseq), seq=dseq, mod_row0=1)
    ck = cache_attn_k[:, l].reshape(dbatch, past * N_HEADS, 2 * DK_A)
    cv = cache_attn_v[:, l].reshape(dbatch, past * N_HEADS, DV)
    oa_s = _attn_lat_call(p_s, ck, cv, lp, sg, batch=dbatch, seq=dseq, past=past)
    or_s = _ret_lat_call(p_s, rd, gg, state_ret_fwd[:, l], state_ret_bwd[:, l],
                         batch=dbatch, seq=dseq)
    y_s = _out_lat_call(oa_s, or_s, p_s, xs, mod, wpa, wpr, wo, lng, lnb, seq=dseq, mod_row0=1)

    return (y_c.reshape(batch, seq, D_MODEL),
            y_s.reshape(dbatch, dseq, D_MODEL),
            k_c.reshape(batch, 1, seq, N_HEADS, 2 * DK_A),
            v_c.reshape(batch, 1, seq, N_HEADS, DV),
            sf_c.reshape(batch, 1, N_HEADS, DK_R, DV),
            sb_c.reshape(batch, 1, N_HEADS, DK_R, DV))
```

```python
import functools
import math

import jax
import jax.numpy as jnp
import numpy as np
from jax import lax
from jax.experimental import pallas as pl
from jax.experimental.pallas import tpu as pltpu

F32 = jnp.float32
BF16 = jnp.bfloat16

D_MODEL = 1024
N_HEADS = 8
DK_A = 64
DV = 128
DK_R = 64
CHUNK = 256
GRID_W = 64
ROPE_BASE = 10000.0
MOD_EPS = 1e-6
LN_EPS = 1e-5
DEPTH = 1
DEEPNORM_ALPHA = (2.0 * DEPTH) ** 0.25
LAM_INIT = 0.8 - 0.6 * math.exp(-0.3 * 0)
LOG2E = math.log2(math.e)

LANES = 128
MOD_ROWS = 8
QA_BLK, KA_BLK, VA_BLK, ZA_BLK = 0, 8, 16, 24
QR_BLK, KR_BLK, VR_BLK, ZR_BLK = 32, 36, 40, 48
GA_BLK, GR_BLK = 56, 64
IN_SEGS = 7
P_WIDTH = (IN_SEGS + 2) * D_MODEL
VMEM_LIMIT = 56 * 1024 * 1024


def _params(sem):
    return pltpu.CompilerParams(dimension_semantics=sem, vmem_limit_bytes=VMEM_LIMIT)


def _resident(shape):
    return pl.BlockSpec(shape, lambda *_: tuple(0 for _ in shape), pipeline_mode=pl.Buffered(1))


def _silu(z):
    return z * (1.0 / (1.0 + jnp.exp(-z)))


def _dot(a, b):
    return jnp.dot(a, b, preferred_element_type=F32)


def _dot_nt(a, b):
    return lax.dot_general(a, b, (((1,), (1,)), ((), ())), preferred_element_type=F32)


def _dot_tn(a, b):
    return lax.dot_general(a, b, (((0,), (0,)), ((), ())), preferred_element_type=F32)


def _rope(acc, cos, sin_even, sin_odd):
    outs = []
    for hh in range(N_HEADS):
        xs = acc[:, hh * LANES:(hh + 1) * LANES]
        nxt = pltpu.roll(xs, LANES - 1, 1)
        prv = pltpu.roll(xs, 1, 1)
        outs.append(xs * cos + nxt * sin_even + prv * sin_odd)
    return jnp.concatenate(outs, axis=1)


def _mod_row(mod_ref, row):
    r = pl.ds(row, 1)
    return (mod_ref[r, 0:D_MODEL], mod_ref[r, D_MODEL:2 * D_MODEL], mod_ref[r, 2 * D_MODEL:3 * D_MODEL])


def _modulated_ln(x_ref, shift, scale):
    x = x_ref[...]
    mu = jnp.mean(x, axis=-1, keepdims=True)
    xc = x - mu
    var = jnp.mean(xc * xc, axis=-1, keepdims=True)
    return (xc * lax.rsqrt(var + MOD_EPS) * (1.0 + scale) + shift).astype(BF16)


def _store_heads(ref, val):
    for h in range(N_HEADS):
        ref[pl.ds(h, val.shape[0], stride=N_HEADS), :] = val[:, h * LANES:(h + 1) * LANES]


def _project(h, win_ref, wg_ref, bg_ref, p_ref, rope=None, kv_out=None):
    def seg(j):
        if j < IN_SEGS:
            return _dot(h, win_ref[:, j * D_MODEL:(j + 1) * D_MODEL])
        return _dot(h, wg_ref[:, (j - IN_SEGS) * D_MODEL:(j - IN_SEGS + 1) * D_MODEL])

    def put(j, val):
        p_ref[:, j * D_MODEL:(j + 1) * D_MODEL] = val.astype(BF16)

    qa = seg(0)
    ka = seg(1)
    if rope is not None:
        cos, se, so = (t[...] for t in rope)
        qa = _rope(qa, cos, se, so)
        ka = _rope(ka, cos, se, so)
    if kv_out is not None:
        _store_heads(kv_out[0], ka)
    put(0, qa * (DK_A ** -0.5 * LOG2E))
    put(1, ka)
    va = seg(2)
    if kv_out is not None:
        _store_heads(kv_out[1], va)
    put(2, va)
    put(3, _silu(seg(3)))
    qk = seg(4)
    put(4, jnp.concatenate([qk[:, :D_MODEL // 2], qk[:, D_MODEL // 2:] * (DK_R ** -0.5)], axis=1))
    put(5, seg(5))
    put(6, _silu(seg(6)))
    for j in (IN_SEGS, IN_SEGS + 1):
        g = seg(j) + bg_ref[:, (j - IN_SEGS) * D_MODEL:(j - IN_SEGS + 1) * D_MODEL]
        put(j, 1.0 / (1.0 + jnp.exp(-g)))


def _proj_lat_kernel(x_ref, mod_ref, win_ref, wg_ref, bg_ref, cos_ref, se_ref, so_ref, p_ref,
                     *, per_seq, mod_row0):
    shift, scale, _ = _mod_row(mod_ref, mod_row0 + pl.program_id(0) // per_seq)
    _project(_modulated_ln(x_ref, shift, scale), win_ref, wg_ref, bg_ref, p_ref,
             rope=(cos_ref, se_ref, so_ref))


def _proj_lat_call(x2d, mod, w_in, w_gate, b_gate, rope, *, seq, mod_row0):
    m = x2d.shape[0]
    tm = 256
    per_seq = seq // tm
    return pl.pallas_call(
        functools.partial(_proj_lat_kernel, per_seq=per_seq, mod_row0=mod_row0),
        grid=(m // tm,),
        in_specs=[
            pl.BlockSpec((tm, D_MODEL), lambda i: (i, 0)),
            pl.BlockSpec((MOD_ROWS, 3 * D_MODEL), lambda i: (0, 0)),
            _resident((D_MODEL, IN_SEGS * D_MODEL)),
            _resident((D_MODEL, 2 * D_MODEL)),
            pl.BlockSpec((1, 2 * D_MODEL), lambda i: (0, 0)),
        ] + [pl.BlockSpec((tm, LANES), lambda i: (i % per_seq, 0))] * 3,
        out_specs=pl.BlockSpec((tm, P_WIDTH), lambda i: (i, 0)),
        out_shape=jax.ShapeDtypeStruct((m, P_WIDTH), BF16),
        compiler_params=_params(("arbitrary",)),
        name="proj_lat",
    )(x2d, mod, w_in, w_gate, b_gate, *rope)


def _lam(lp_ref):
    lp = lp_ref[...]
    a = jnp.sum(lp[0:1] * lp[1:2], axis=-1, keepdims=True)
    b = jnp.sum(lp[2:3] * lp[3:4], axis=-1, keepdims=True)
    return jnp.exp(a) - jnp.exp(b) + LAM_INIT


def _attn_rows(q, k_all, v_ones, z, lam, g):
    lane = lax.broadcasted_iota(jnp.int32, (1, LANES), 1)
    zero = jnp.zeros_like(q)
    outs = []
    for qm in (jnp.where(lane < DK_A, q, zero), jnp.where(lane >= DK_A, q, zero)):
        s = _dot_nt(qm, k_all)
        mx = jnp.max(s, axis=-1, keepdims=True)
        p = jnp.exp2(s - mx).astype(BF16)
        ol = _dot(p, v_ones)
        outs.append(ol[:, :DV] * (1.0 / ol[:, DV:]))
    oa = outs[0] - lam * outs[1]
    oa = oa * lax.rsqrt(jnp.mean(oa * oa, axis=-1, keepdims=True) + LN_EPS)
    oa = oa * g * (1.0 - LAM_INIT)
    return oa * z.astype(F32)


def _attn_ctx_heads(q_ref, k_ref, v_ref, z_ref, lam, g, o_ref):
    ones = jnp.ones((q_ref.shape[0], DV), BF16)
    for h in range(N_HEADS):
        sl = slice(h * LANES, (h + 1) * LANES)
        v_ones = jnp.concatenate([v_ref[:, sl], ones], axis=1)
        o_ref[:, sl] = _attn_rows(q_ref[:, sl], k_ref[:, sl], v_ones, z_ref[:, sl],
                                  lam, g).astype(BF16)


def _attn_lat_kernel(q_ref, k_ref, v_ref, z_ref, kc_ref, vc_ref, lp_ref, g_ref, o_ref,
                     k_all, v_all, *, past, tc):
    @pl.when(pl.program_id(2) == 0)
    def _():
        head_rows = pl.ds(pl.program_id(1), past, stride=N_HEADS)
        k_all[0:past, :] = kc_ref[head_rows, :].astype(BF16)
        v_all[0:past, 0:DV] = vc_ref[head_rows, :].astype(BF16)
        k_all[past:, :] = k_ref[...]
        v_all[past:, 0:DV] = v_ref[...]
        v_all[:, DV:] = jnp.ones((v_all.shape[0], DV), BF16)

    lam = _lam(lp_ref)
    g = g_ref[...]
    for lo in range(0, q_ref.shape[0], tc):
        o_ref[lo:lo + tc, :] = _attn_rows(q_ref[lo:lo + tc, :], k_all[...], v_all[...],
                                          z_ref[lo:lo + tc, :], lam, g).astype(BF16)


def _attn_lat_call(p, cache_k, cache_v, lam_params, subln_g, *, batch, seq, past):
    tq, tc = 2048, 128
    nq = seq // tq
    return pl.pallas_call(
        functools.partial(_attn_lat_kernel, past=past, tc=tc),
        grid=(batch, N_HEADS, nq),
        in_specs=[
            pl.BlockSpec((tq, LANES), lambda b, h, i: (b * nq + i, QA_BLK + h)),
            pl.BlockSpec((seq, LANES), lambda b, h, i: (b, KA_BLK + h)),
            pl.BlockSpec((seq, LANES), lambda b, h, i: (b, VA_BLK + h)),
            pl.BlockSpec((tq, LANES), lambda b, h, i: (b * nq + i, ZA_BLK + h)),
            pl.BlockSpec((None, past * N_HEADS, LANES), lambda b, h, i: (b, 0, 0)),
            pl.BlockSpec((None, past * N_HEADS, LANES), lambda b, h, i: (b, 0, 0)),
            pl.BlockSpec((4, DK_A), lambda b, h, i: (0, 0)),
            pl.BlockSpec((1, DV), lambda b, h, i: (0, 0)),
        ],
        out_specs=pl.BlockSpec((tq, LANES), lambda b, h, i: (b * nq + i, h)),
        out_shape=jax.ShapeDtypeStruct((batch * seq, D_MODEL), BF16),
        scratch_shapes=[pltpu.VMEM((past + seq, LANES), BF16),
                        pltpu.VMEM((past + seq, 2 * DV), BF16)],
        compiler_params=_params(("arbitrary", "arbitrary", "arbitrary")),
        name="attn_lat",
    )(p, p, p, p, cache_k, cache_v, lam_params, subln_g)


PW, VW = 2 * DK_R, 2 * DV


def _ret_scratch(npairs, nc):
    stacked_state = (npairs, nc, 2 * PW, VW)
    return [pltpu.VMEM(stacked_state, F32),
            pltpu.VMEM(stacked_state, BF16),
            pltpu.VMEM((npairs, 2, CHUNK, CHUNK), F32),
            pltpu.VMEM((npairs, CHUNK, 2 * PW), F32),
            pltpu.VMEM((npairs, CHUNK, 2 * PW), F32)]


def _log_gammas(rd_ref, pair):
    def one(d, hh):
        r = jnp.full((1, VW), rd_ref[d, 2 * pair + hh], F32)
        return jnp.log1p(-jnp.exp2(r))
    return [[one(d, hh) for hh in range(2)] for d in range(2)]


def _ret_build_tables(rd_ref, dec_scr, qdec_scr, kdec_scr, *, npairs, pair0):
    lane_q = lax.broadcasted_iota(jnp.int32, (1, PW), 1)
    rel = (lax.broadcasted_iota(jnp.int32, (CHUNK, CHUNK), 0)
           - lax.broadcasted_iota(jnp.int32, (CHUNK, CHUNK), 1)).astype(F32)
    row_q = lax.broadcasted_iota(jnp.int32, (CHUNK, PW), 0).astype(F32)
    for pi in range(npairs):
        lg = _log_gammas(rd_ref, pair0 + pi)
        lg_q = [jnp.where(lane_q < DK_R, lg[d][0][:, :PW], lg[d][1][:, :PW])
                for d in range(2)]
        for hh in range(2):
            dec_scr[pi, hh] = (
                jnp.where(rel >= 0, jnp.exp(lg[0][hh][:, :1] * jnp.maximum(rel, 0.0)), 0.0)
                + jnp.where(rel <= 0, jnp.exp(lg[1][hh][:, :1] * jnp.maximum(-rel, 0.0)), 0.0))
        qdec_scr[pi] = jnp.concatenate([jnp.exp(lg_q[0] * (row_q + 1.0)),
                                        jnp.exp(lg_q[1] * (CHUNK - row_q))], axis=1)
        kdec_scr[pi] = jnp.concatenate([jnp.exp(lg_q[0] * (CHUNK - 1.0 - row_q)),
                                        jnp.exp(lg_q[1] * row_q)], axis=1)


def _ret_pairs(rd_ref, q_ref, k_ref, v_ref, z_ref, g_ref, s0_refs, o_ref, sout_refs,
               u_scr, s_scr, dec_scr, qdec_scr, kdec_scr, *, nc, npairs, pair0):
    cross = s0_refs is not None or nc > 1
    lane_q = lax.broadcasted_iota(jnp.int32, (1, PW), 1)
    lane_v = lax.broadcasted_iota(jnp.int32, (1, VW), 1)
    srow = lax.broadcasted_iota(jnp.int32, (2 * PW, VW), 0)
    scol = lax.broadcasted_iota(jnp.int32, (2 * PW, VW), 1)
    diag_blocks = ((srow % PW) // DK_R) == (scol // DV)

    def rows(c):
        return slice(c * CHUNK, (c + 1) * CHUNK)

    for pi in range(npairs):
        qs = slice(pi * PW, (pi + 1) * PW)
        vs = slice(pi * VW, (pi + 1) * VW)
        lg = _log_gammas(rd_ref, pair0 + pi)
        lg_v = [jnp.where(lane_v < DV, lg[d][0], lg[d][1]) for d in range(2)]
        chunk_decay = [jnp.exp(lg_v[d] * CHUNK) for d in range(2)]

        for c in range(nc):
            k = k_ref[rows(c), qs].astype(F32)
            kd = (jnp.concatenate([k, k], axis=1) * kdec_scr[pi]).astype(BF16)
            u_scr[pi, c] = jnp.where(diag_blocks, _dot_tn(kd, v_ref[rows(c), vs]), 0.0)

        for d, order in ((0, range(nc)), (1, reversed(range(nc)))):
            half = slice(d * PW, (d + 1) * PW)
            if s0_refs is None:
                s = jnp.zeros((PW, VW), F32)
            else:
                zero = jnp.zeros((DK_R, DV), F32)
                s = jnp.concatenate(
                    [jnp.concatenate([s0_refs[d][2 * pi], zero], axis=1),
                     jnp.concatenate([zero, s0_refs[d][2 * pi + 1]], axis=1)], axis=0)
            for c in order:
                if cross:
                    s_scr[pi, c, half, :] = s.astype(BF16)
                s = s * chunk_decay[d] + u_scr[pi, c, half, :]
            if sout_refs is not None:
                sout_refs[d][2 * pi] = s[:DK_R, :DV]
                sout_refs[d][2 * pi + 1] = s[DK_R:, DV:]

        g = g_ref[:, vs]
        for c in range(nc):
            q = q_ref[rows(c), qs]
            k = k_ref[rows(c), qs]
            v = v_ref[rows(c), vs]
            zero = jnp.zeros_like(q)
            q2 = jnp.concatenate([jnp.where(lane_q < DK_R, q, zero),
                                  jnp.where(lane_q >= DK_R, q, zero)], axis=0)
            a2 = _dot_nt(q2, k)
            o = jnp.concatenate(
                [_dot((a2[hh * CHUNK:(hh + 1) * CHUNK] * dec_scr[pi, hh]).astype(BF16),
                      v[:, hh * DV:(hh + 1) * DV]) for hh in range(2)], axis=1)
            if cross:
                qf = q.astype(F32)
                qq = (jnp.concatenate([qf, qf], axis=1) * qdec_scr[pi]).astype(BF16)
                o = o + _dot(qq, s_scr[pi, c])
            outs = []
            for hh in range(2):
                oh = o[:, hh * DV:(hh + 1) * DV]
                mu = jnp.mean(oh, axis=-1, keepdims=True)
                oc = oh - mu
                var = jnp.mean(oc * oc, axis=-1, keepdims=True)
                outs.append(oc * lax.rsqrt(var + LN_EPS))
            o = jnp.concatenate(outs, axis=1) * g * z_ref[rows(c), vs].astype(F32)
            o_ref[rows(c), vs] = o.astype(BF16)


def _ret_lat_kernel(rd_ref, q_ref, k_ref, v_ref, z_ref, g_ref, s0f_ref, s0b_ref, o_ref,
                    u_scr, s_scr, dec_scr, qdec_scr, kdec_scr, *, nc, npairs):
    pair0 = pl.program_id(0) * npairs

    @pl.when(pl.program_id(1) == 0)
    def _():
        _ret_build_tables(rd_ref, dec_scr, qdec_scr, kdec_scr, npairs=npairs, pair0=pair0)

    _ret_pairs(rd_ref, q_ref, k_ref, v_ref, z_ref, g_ref, (s0f_ref, s0b_ref), o_ref, None,
               u_scr, s_scr, dec_scr, qdec_scr, kdec_scr, nc=nc, npairs=npairs, pair0=pair0)


def _ret_lat_call(p, ret_decay, gn_g, s0f, s0b, *, batch, seq):
    nc = seq // CHUNK
    hps = 2
    npairs = hps // 2
    qw, vw = DK_R * hps, DV * hps
    col = lambda w, blk: pl.BlockSpec((seq, w), lambda j, b: (b, blk * LANES // w + j))
    state = pl.BlockSpec((None, hps, DK_R, DV), lambda j, b: (b, j, 0, 0))
    return pl.pallas_call(
        functools.partial(_ret_lat_kernel, nc=nc, npairs=npairs),
        grid=(N_HEADS // hps, batch),
        in_specs=[pl.BlockSpec(memory_space=pltpu.SMEM),
                  col(qw, QR_BLK), col(qw, KR_BLK), col(vw, VR_BLK), col(vw, ZR_BLK),
                  pl.BlockSpec((1, vw), lambda j, b: (0, j)), state, state],
        out_specs=pl.BlockSpec((seq, vw), lambda j, b: (b, j)),
        out_shape=jax.ShapeDtypeStruct((batch * seq, D_MODEL), BF16),
        scratch_shapes=_ret_scratch(npairs, nc),
        compiler_params=_params(("arbitrary", "arbitrary")),
        name="ret_lat",
    )(ret_decay, p, p, p, p, gn_g, s0f, s0b)


def _out_rows(oa_ref, or_ref, ga_ref, gr_ref, x_ref, gate, wpa_ref, wpr_ref, wo_ref,
              lng_ref, lnb_ref, y_ref):
    a = _dot(oa_ref[...], wpa_ref[...])
    r = _dot(or_ref[...], wpr_ref[...])
    m = ga_ref[...].astype(F32) * a + gr_ref[...].astype(F32) * r
    out = _dot(m.astype(BF16), wo_ref[...])
    t = DEEPNORM_ALPHA * x_ref[...] + gate * out
    mu = jnp.mean(t, axis=-1, keepdims=True)
    tc = t - mu
    var = jnp.mean(tc * tc, axis=-1, keepdims=True)
    y_ref[...] = tc * lax.rsqrt(var + LN_EPS) * lng_ref[...] + lnb_ref[...]


def _out_lat_kernel(oa_ref, or_ref, ga_ref, gr_ref, x_ref, mod_ref, wpa_ref, wpr_ref, wo_ref,
                    lng_ref, lnb_ref, y_ref, *, per_seq, mod_row0):
    _, _, gate = _mod_row(mod_ref, mod_row0 + pl.program_id(0) // per_seq)
    _out_rows(oa_ref, or_ref, ga_ref, gr_ref, x_ref, gate, wpa_ref, wpr_ref, wo_ref,
              lng_ref, lnb_ref, y_ref)


def _out_lat_call(oa, orr, p, x2d, mod, w_pa, w_pr, w_out, ln_g, ln_b, *, seq, mod_row0):
    m = x2d.shape[0]
    tm = 512
    per_seq = seq // tm
    tile = lambda j: pl.BlockSpec((tm, D_MODEL), lambda i: (i, j))
    vec = pl.BlockSpec((1, D_MODEL), lambda i: (0, 0))
    return pl.pallas_call(
        functools.partial(_out_lat_kernel, per_seq=per_seq, mod_row0=mod_row0),
        grid=(m // tm,),
        in_specs=[tile(0), tile(0), tile(GA_BLK * LANES // D_MODEL), tile(GR_BLK * LANES // D_MODEL),
                  tile(0),
                  pl.BlockSpec((MOD_ROWS, 3 * D_MODEL), lambda i: (0, 0)),
                  _resident((D_MODEL, D_MODEL)), _resident((D_MODEL, D_MODEL)),
                  _resident((D_MODEL, D_MODEL)), vec, vec],
        out_specs=tile(0),
        out_shape=jax.ShapeDtypeStruct((m, D_MODEL), F32),
        compiler_params=_params(("arbitrary",)),
        name="out_lat",
    )(oa, orr, p, p, x2d, mod, w_pa, w_pr, w_out, ln_g, ln_b)


W_CHUNK = D_MODEL


def _weight_chunks(w_hbm):
    return [(k, slice(j * W_CHUNK, (j + 1) * W_CHUNK))
            for k, w in enumerate(w_hbm) for j in range(w.shape[1] // W_CHUNK)]


def _weight_store(n, chunks, w_scr, w_bf_hbm, out_sem):
    k, cols = chunks[n]
    return pltpu.make_async_copy(w_scr[k].at[:, cols], w_bf_hbm[k].at[:, cols], out_sem.at[n])


def _stream_f32_chunks(jobs, stage, in_sem):
    def load(n):
        return pltpu.make_async_copy(jobs[n][0], stage.at[n % 2], in_sem.at[n % 2])

    load(0).start()
    for n, (_, sink) in enumerate(jobs):
        if n + 1 < len(jobs):
            load(n + 1).start()
        load(n).wait()
        sink(stage.at[n % 2])


def _ctx_kernel(x_ref, cctx_ref, c_ref, wmod_hbm, bmod_ref, win_hbm, wg_hbm, bg_ref, lp_ref, sg_ref,
                rd_ref, gg_ref, wpa_hbm, wpr_hbm, wo_hbm, lng_ref, lnb_ref,
                y_ref, k_ref, v_ref, sf_ref, sb_ref, mod_ref, win_bf, wg_bf, wpa_bf, wpr_bf, wo_bf,
                p_scr, oa_scr, or_scr, u_scr, s_scr, dec_scr, qdec_scr, kdec_scr,
                win_ref, wg_ref, wpa_ref, wpr_ref, wo_ref, cond_scr, stage, in_sem, out_sem,
                *, npairs):
    w_hbm = (win_hbm, wg_hbm, wpa_hbm, wpr_hbm, wo_hbm)
    w_scr = (win_ref, wg_ref, wpa_ref, wpr_ref, wo_ref)
    w_bf = (win_bf, wg_bf, wpa_bf, wpr_bf, wo_bf)

    @pl.when(pl.program_id(0) == 0)
    def _():
        cond_scr[...] = jnp.zeros(cond_scr.shape, F32)
        cond_scr[0:1, :] = cctx_ref[...]
        cond_scr[1:1 + c_ref.shape[0], :] = c_ref[...]
        cond = _silu(cond_scr[...]).astype(BF16)
        jobs = []
        for j in range(wmod_hbm.shape[1] // W_CHUNK):
            cols = slice(j * W_CHUNK, (j + 1) * W_CHUNK)

            def mod_sink(staged, cols=cols):
                mod_ref[:, cols] = _dot(cond, staged[...].astype(BF16)) + bmod_ref[:, cols]
            jobs.append((wmod_hbm.at[:, cols], mod_sink))
        chunks = _weight_chunks(w_hbm)
        for k, cols in chunks:
            def cast_sink(staged, k=k, cols=cols):
                w_scr[k][:, cols] = staged[...].astype(BF16)
            jobs.append((w_hbm[k].at[:, cols], cast_sink))
        _stream_f32_chunks(jobs, stage, in_sem)
        for n in range(len(chunks)):
            _weight_store(n, chunks, w_scr, w_bf, out_sem).start()
        _ret_build_tables(rd_ref, dec_scr, qdec_scr, kdec_scr, npairs=npairs, pair0=0)

    cols = lambda blk, w: p_scr.at[:, blk * LANES:blk * LANES + w]
    shift, scale, gate = _mod_row(mod_ref, 0)
    _project(_modulated_ln(x_ref, shift, scale), win_ref, wg_ref, bg_ref, p_scr,
             kv_out=(k_ref, v_ref))
    _attn_ctx_heads(cols(QA_BLK, D_MODEL), cols(KA_BLK, D_MODEL), cols(VA_BLK, D_MODEL),
                    cols(ZA_BLK, D_MODEL), _lam(lp_ref), sg_ref[...], oa_scr)
    _ret_pairs(rd_ref, cols(QR_BLK, D_MODEL // 2), cols(KR_BLK, D_MODEL // 2),
               cols(VR_BLK, D_MODEL), cols(ZR_BLK, D_MODEL), gg_ref, None, or_scr,
               (sf_ref, sb_ref), u_scr, s_scr, dec_scr, qdec_scr, kdec_scr,
               nc=x_ref.shape[0] // CHUNK, npairs=npairs, pair0=0)
    _out_rows(oa_scr, or_scr, cols(GA_BLK, D_MODEL), cols(GR_BLK, D_MODEL), x_ref, gate,
              wpa_ref, wpr_ref, wo_ref, lng_ref, lnb_ref, y_ref)

    @pl.when(pl.program_id(0) == pl.num_programs(0) - 1)
    def _():
        chunks = _weight_chunks(w_hbm)
        for n in range(len(chunks)):
            _weight_store(n, chunks, w_scr, w_bf, out_sem).wait()


def _ctx_call(x2d, c_ctx, c, w_mod, b_mod, w_in, w_gate, b_gate, lam_params, subln_g, ret_decay,
              gn_g, w_pa, w_pr, w_out, ln_g, ln_b, *, batch, seq):
    npairs = N_HEADS // 2
    nc = seq // CHUNK
    tile = pl.BlockSpec((seq, D_MODEL), lambda b: (b, 0))
    vec = lambda w: pl.BlockSpec((1, w), lambda b: (0, 0))
    state = pl.BlockSpec((None, N_HEADS, DK_R, DV), lambda b: (b, 0, 0, 0))
    heads = pl.BlockSpec((seq * N_HEADS, LANES), lambda b: (b, 0))
    tok = jax.ShapeDtypeStruct((batch * seq, D_MODEL), F32)
    tok_heads = jax.ShapeDtypeStruct((batch * seq * N_HEADS, LANES), F32)
    st = jax.ShapeDtypeStruct((batch, N_HEADS, DK_R, DV), F32)
    hbm = pl.BlockSpec(memory_space=pl.ANY)
    weights = (w_in, w_gate, w_pa, w_pr, w_out)
    n_chunks = sum(w.shape[1] // W_CHUNK for w in weights)
    return pl.pallas_call(
        functools.partial(_ctx_kernel, npairs=npairs),
        grid=(batch,),
        in_specs=[tile, vec(D_MODEL), pl.BlockSpec(c.shape, lambda b: (0, 0)), hbm,
                  vec(3 * D_MODEL), hbm, hbm, vec(2 * D_MODEL),
                  pl.BlockSpec((4, DK_A), lambda b: (0, 0)), vec(DV),
                  pl.BlockSpec(memory_space=pltpu.SMEM), vec(D_MODEL),
                  hbm, hbm, hbm, vec(D_MODEL), vec(D_MODEL)],
        out_specs=[tile, heads, heads, state, state,
                   pl.BlockSpec((MOD_ROWS, 3 * D_MODEL), lambda b: (0, 0))] + [hbm] * len(weights),
        out_shape=[tok, tok_heads, tok_heads, st, st, jax.ShapeDtypeStruct((MOD_ROWS, 3 * D_MODEL), F32)]
                  + [jax.ShapeDtypeStruct(w.shape, BF16) for w in weights],
        scratch_shapes=[pltpu.VMEM((seq, P_WIDTH), BF16),
                        pltpu.VMEM((seq, D_MODEL), BF16),
                        pltpu.VMEM((seq, D_MODEL), BF16)] + _ret_scratch(npairs, nc)
                       + [pltpu.VMEM(w.shape, BF16) for w in weights]
                       + [pltpu.VMEM((MOD_ROWS, D_MODEL), F32),
                          pltpu.VMEM((2, D_MODEL, W_CHUNK), F32),
                          pltpu.SemaphoreType.DMA((2,)),
                          pltpu.SemaphoreType.DMA((n_chunks,))],
        compiler_params=_params(("arbitrary",)),
        name="ctx",
    )(x2d, c_ctx, c, w_mod, b_mod, w_in, w_gate, b_gate, lam_params, subln_g, ret_decay, gn_g,
      w_pa, w_pr, w_out, ln_g, ln_b)


def _rope_tables(n_tokens):
    rows = n_tokens // GRID_W
    r = np.repeat(np.arange(rows, dtype=np.float32), GRID_W)
    col = np.tile(np.arange(GRID_W, dtype=np.float32), rows)
    n_freq = DK_A // 4
    inv = np.float32(ROPE_BASE) ** (-np.arange(n_freq, dtype=np.float32) / np.float32(n_freq))
    ang = np.concatenate([r[:, None] * inv, col[:, None] * inv], axis=-1).astype(np.float32)
    cos = np.repeat(np.cos(ang), 2, axis=-1)
    sin = np.repeat(np.sin(ang), 2, axis=-1)
    even = (np.arange(DK_A) % 2 == 0)[None, :]
    sin_even = np.where(even, -sin, 0.0)
    sin_odd = np.where(even, 0.0, sin)
    two = lambda t: jnp.asarray(np.concatenate([t, t], axis=-1), F32)
    return two(cos), two(sin_even), two(sin_odd)


def kernel(x_prompt, x_sample, cache_attn_k, cache_attn_v, state_ret_fwd, state_ret_bwd,
           c, c_ctx, w_mod, b_mod, w_in, lam_params, subln_g, ret_decay, ret_gn_g,
           w_pa, w_pr, w_gate, b_gate, w_out, ln_g, ln_b):
    batch, seq, _ = x_prompt.shape
    dbatch, dseq, _ = x_sample.shape
    past = cache_attn_k.shape[2]
    l = 0

    bg = b_gate[l][None, :]
    lp, sg = lam_params[l], subln_g[l][None, :]
    rd, gg = ret_decay[l], ret_gn_g[l][None, :]
    lng, lnb = ln_g[l][None, :], ln_b[l][None, :]

    xc = x_prompt.reshape(batch * seq, D_MODEL)
    y_c, k_c, v_c, sf_c, sb_c, mod, win, wg, wpa, wpr, wo = _ctx_call(
        xc, c_ctx[None, :], c, w_mod[l], b_mod[l][None, :], w_in[l], w_gate[l], bg, lp, sg, rd, gg,
        w_pa[l], w_pr[l], w_out[l], lng, lnb, batch=batch, seq=seq)

    xs = x_sample.reshape(dbatch * dseq, D_MODEL)
    p_s = _proj_lat_call(xs, mod, win, wg, bg, _rope_tables(dseq), seq=dseq, mod_row0=1)
    ck = cache_attn_k[:, l].reshape(dbatch, past * N_HEADS, 2 * DK_A)
    cv = cache_attn_v[:, l].reshape(dbatch, past * N_HEADS, DV)
    oa_s = _attn_lat_call(p_s, ck, cv, lp, sg, batch=dbatch, seq=dseq, past=past)
    or_s = _ret_lat_call(p_s, rd, gg, state_ret_fwd[:, l], state_ret_bwd[:, l],
                         batch=dbatch, seq=dseq)
    y_s = _out_lat_call(oa_s, or_s, p_s, xs, mod, wpa, wpr, wo, lng, lnb, seq=dseq, mod_row0=1)

    return (y_c.reshape(batch, seq, D_MODEL),
            y_s.reshape(dbatch, dseq, D_MODEL),
            k_c.reshape(batch, 1, seq, N_HEADS, 2 * DK_A),
            v_c.reshape(batch, 1, seq, N_HEADS, DV),
            sf_c.reshape(batch, 1, N_HEADS, DK_R, DV),
            sb_c.reshape(batch, 1, N_HEADS, DK_R, DV))
```

```python
import functools
import math

import jax
import jax.numpy as jnp
import numpy as np
from jax import lax
from jax.experimental import pallas as pl
from jax.experimental.pallas import tpu as pltpu

F32 = jnp.float32
BF16 = jnp.bfloat16

D_MODEL = 1024
N_HEADS = 8
DK_A = 64
DV = 128
DK_R = 64
CHUNK = 256
GRID_W = 64
ROPE_BASE = 10000.0
MOD_EPS = 1e-6
LN_EPS = 1e-5
DEPTH = 1
DEEPNORM_ALPHA = (2.0 * DEPTH) ** 0.25
LAM_INIT = 0.8 - 0.6 * math.exp(-0.3 * 0)
LOG2E = math.log2(math.e)

LANES = 128
MOD_ROWS = 8
QA_BLK, KA_BLK, VA_BLK, ZA_BLK = 0, 8, 16, 24
QR_BLK, KR_BLK, VR_BLK, ZR_BLK = 32, 36, 40, 48
GA_BLK, GR_BLK = 56, 64
IN_SEGS = 7
P_WIDTH = (IN_SEGS + 2) * D_MODEL
VMEM_LIMIT = 56 * 1024 * 1024


def _params(sem):
    return pltpu.CompilerParams(dimension_semantics=sem, vmem_limit_bytes=VMEM_LIMIT)


def _resident(shape):
    return pl.BlockSpec(shape, lambda *_: tuple(0 for _ in shape), pipeline_mode=pl.Buffered(1))


def _silu(z):
    return z * (1.0 / (1.0 + jnp.exp(-z)))


def _dot(a, b):
    return jnp.dot(a, b, preferred_element_type=F32)


def _dot_nt(a, b):
    return lax.dot_general(a, b, (((1,), (1,)), ((), ())), preferred_element_type=F32)


def _dot_tn(a, b):
    return lax.dot_general(a, b, (((0,), (0,)), ((), ())), preferred_element_type=F32)


def _rope(acc, cos, sin_even, sin_odd):
    outs = []
    for hh in range(N_HEADS):
        xs = acc[:, hh * LANES:(hh + 1) * LANES]
        nxt = pltpu.roll(xs, LANES - 1, 1)
        prv = pltpu.roll(xs, 1, 1)
        outs.append(xs * cos + nxt * sin_even + prv * sin_odd)
    return jnp.concatenate(outs, axis=1)


def _mod_row(mod_ref, row):
    r = pl.ds(row, 1)
    return (mod_ref[r, 0:D_MODEL], mod_ref[r, D_MODEL:2 * D_MODEL], mod_ref[r, 2 * D_MODEL:3 * D_MODEL])


def _modulated_ln(x_ref, shift, scale):
    x = x_ref[...]
    mu = jnp.mean(x, axis=-1, keepdims=True)
    xc = x - mu
    var = jnp.mean(xc * xc, axis=-1, keepdims=True)
    return (xc * lax.rsqrt(var + MOD_EPS) * (1.0 + scale) + shift).astype(BF16)


def _store_heads(ref, val):
    for h in range(N_HEADS):
        ref[pl.ds(h, val.shape[0], stride=N_HEADS), :] = val[:, h * LANES:(h + 1) * LANES]


def _project(h, win_ref, wg_ref, bg_ref, p_ref, rope=None, kv_out=None):
    def seg(j):
        if j < IN_SEGS:
            return _dot(h, win_ref[:, j * D_MODEL:(j + 1) * D_MODEL])
        return _dot(h, wg_ref[:, (j - IN_SEGS) * D_MODEL:(j - IN_SEGS + 1) * D_MODEL])

    def put(j, val):
        p_ref[:, j * D_MODEL:(j + 1) * D_MODEL] = val.astype(BF16)

    qa = seg(0)
    ka = seg(1)
    if rope is not None:
        cos, se, so = (t[...] for t in rope)
        qa = _rope(qa, cos, se, so)
        ka = _rope(ka, cos, se, so)
    if kv_out is not None:
        _store_heads(kv_out[0], ka)
    put(0, qa * (DK_A ** -0.5 * LOG2E))
    put(1, ka)
    va = seg(2)
    if kv_out is not None:
        _store_heads(kv_out[1], va)
    put(2, va)
    put(3, _silu(seg(3)))
    qk = seg(4)
    put(4, jnp.concatenate([qk[:, :D_MODEL // 2], qk[:, D_MODEL // 2:] * (DK_R ** -0.5)], axis=1))
    put(5, seg(5))
    put(6, _silu(seg(6)))
    for j in (IN_SEGS, IN_SEGS + 1):
        g = seg(j) + bg_ref[:, (j - IN_SEGS) * D_MODEL:(j - IN_SEGS + 1) * D_MODEL]
        put(j, 1.0 / (1.0 + jnp.exp(-g)))


def _proj_lat_kernel(x_ref, mod_ref, win_ref, wg_ref, bg_ref, cos_ref, se_ref, so_ref, p_ref,
                     *, per_seq, mod_row0):
    shift, scale, _ = _mod_row(mod_ref, mod_row0 + pl.program_id(0) // per_seq)
    _project(_modulated_ln(x_ref, shift, scale), win_ref, wg_ref, bg_ref, p_ref,
             rope=(cos_ref, se_ref, so_ref))


def _proj_lat_call(x2d, mod, w_in, w_gate, b_gate, rope, *, seq, mod_row0):
    m = x2d.shape[0]
    tm = 256
    per_seq = seq // tm
    return pl.pallas_call(
        functools.partial(_proj_lat_kernel, per_seq=per_seq, mod_row0=mod_row0),
        grid=(m // tm,),
        in_specs=[
            pl.BlockSpec((tm, D_MODEL), lambda i: (i, 0)),
            pl.BlockSpec((MOD_ROWS, 3 * D_MODEL), lambda i: (0, 0)),
            _resident((D_MODEL, IN_SEGS * D_MODEL)),
            _resident((D_MODEL, 2 * D_MODEL)),
            pl.BlockSpec((1, 2 * D_MODEL), lambda i: (0, 0)),
        ] + [pl.BlockSpec((tm, LANES), lambda i: (i % per_seq, 0))] * 3,
        out_specs=pl.BlockSpec((tm, P_WIDTH), lambda i: (i, 0)),
        out_shape=jax.ShapeDtypeStruct((m, P_WIDTH), BF16),
        compiler_params=_params(("arbitrary",)),
        name="proj_lat",
    )(x2d, mod, w_in, w_gate, b_gate, *rope)


def _lam(lp_ref):
    lp = lp_ref[...]
    a = jnp.sum(lp[0:1] * lp[1:2], axis=-1, keepdims=True)
    b = jnp.sum(lp[2:3] * lp[3:4], axis=-1, keepdims=True)
    return jnp.exp(a) - jnp.exp(b) + LAM_INIT


def _attn_rows(q, k_all, v_ones, z, lam, g):
    lane = lax.broadcasted_iota(jnp.int32, (1, LANES), 1)
    zero = jnp.zeros_like(q)
    outs = []
    for qm in (jnp.where(lane < DK_A, q, zero), jnp.where(lane >= DK_A, q, zero)):
        s = _dot_nt(qm, k_all)
        mx = jnp.max(s, axis=-1, keepdims=True)
        p = jnp.exp2(s - mx).astype(BF16)
        ol = _dot(p, v_ones)
        outs.append(ol[:, :DV] * (1.0 / ol[:, DV:]))
    oa = outs[0] - lam * outs[1]
    oa = oa * lax.rsqrt(jnp.mean(oa * oa, axis=-1, keepdims=True) + LN_EPS)
    oa = oa * g * (1.0 - LAM_INIT)
    return oa * z.astype(F32)


def _attn_ctx_heads(q_ref, k_ref, v_ref, z_ref, lam, g, o_ref):
    ones = jnp.ones((q_ref.shape[0], DV), BF16)
    for h in range(N_HEADS):
        sl = slice(h * LANES, (h + 1) * LANES)
        v_ones = jnp.concatenate([v_ref[:, sl], ones], axis=1)
        o_ref[:, sl] = _attn_rows(q_ref[:, sl], k_ref[:, sl], v_ones, z_ref[:, sl],
                                  lam, g).astype(BF16)


def _attn_lat_head(q_ref, k_ref, v_ref, z_ref, kc_ref, vc_ref, head, lam, g, o_ref, k_all, v_all,
                   *, past, tc):
    head_rows = pl.ds(head, past, stride=N_HEADS)
    k_all[0:past, :] = kc_ref[head_rows, :].astype(BF16)
    v_all[0:past, 0:DV] = vc_ref[head_rows, :].astype(BF16)
    k_all[past:, :] = k_ref[...]
    v_all[past:, 0:DV] = v_ref[...]
    v_all[:, DV:] = jnp.ones((v_all.shape[0], DV), BF16)
    for lo in range(0, q_ref.shape[0], tc):
        o_ref[lo:lo + tc, :] = _attn_rows(q_ref[lo:lo + tc, :], k_all[...], v_all[...],
                                          z_ref[lo:lo + tc, :], lam, g).astype(BF16)


PW, VW = 2 * DK_R, 2 * DV


def _ret_scratch(npairs, nc):
    stacked_state = (npairs, nc, 2 * PW, VW)
    return [pltpu.VMEM(stacked_state, F32),
            pltpu.VMEM(stacked_state, BF16),
            pltpu.VMEM((npairs, 2, CHUNK, CHUNK), F32),
            pltpu.VMEM((npairs, CHUNK, 2 * PW), F32),
            pltpu.VMEM((npairs, CHUNK, 2 * PW), F32)]


def _log_gammas(rd_ref, pair):
    def one(d, hh):
        r = jnp.full((1, VW), rd_ref[d, 2 * pair + hh], F32)
        return jnp.log1p(-jnp.exp2(r))
    return [[one(d, hh) for hh in range(2)] for d in range(2)]


def _ret_build_tables(rd_ref, dec_scr, qdec_scr, kdec_scr, *, npairs, pair0):
    lane_q = lax.broadcasted_iota(jnp.int32, (1, PW), 1)
    rel = (lax.broadcasted_iota(jnp.int32, (CHUNK, CHUNK), 0)
           - lax.broadcasted_iota(jnp.int32, (CHUNK, CHUNK), 1)).astype(F32)
    row_q = lax.broadcasted_iota(jnp.int32, (CHUNK, PW), 0).astype(F32)
    for pi in range(npairs):
        lg = _log_gammas(rd_ref, pair0 + pi)
        lg_q = [jnp.where(lane_q < DK_R, lg[d][0][:, :PW], lg[d][1][:, :PW])
                for d in range(2)]
        for hh in range(2):
            dec_scr[pi, hh] = (
                jnp.where(rel >= 0, jnp.exp(lg[0][hh][:, :1] * jnp.maximum(rel, 0.0)), 0.0)
                + jnp.where(rel <= 0, jnp.exp(lg[1][hh][:, :1] * jnp.maximum(-rel, 0.0)), 0.0))
        qdec_scr[pi] = jnp.concatenate([jnp.exp(lg_q[0] * (row_q + 1.0)),
                                        jnp.exp(lg_q[1] * (CHUNK - row_q))], axis=1)
        kdec_scr[pi] = jnp.concatenate([jnp.exp(lg_q[0] * (CHUNK - 1.0 - row_q)),
                                        jnp.exp(lg_q[1] * row_q)], axis=1)


def _ret_pairs(rd_ref, q_ref, k_ref, v_ref, z_ref, g_ref, s0_refs, o_ref, sout_refs,
               u_scr, s_scr, dec_scr, qdec_scr, kdec_scr, *, nc, npairs, pair0, tbl0=0,
               with_states=True, out_chunks=None):
    cross = s0_refs is not None or nc > 1
    out_chunks = range(nc) if out_chunks is None else out_chunks
    lane_q = lax.broadcasted_iota(jnp.int32, (1, PW), 1)
    lane_v = lax.broadcasted_iota(jnp.int32, (1, VW), 1)
    srow = lax.broadcasted_iota(jnp.int32, (2 * PW, VW), 0)
    scol = lax.broadcasted_iota(jnp.int32, (2 * PW, VW), 1)
    diag_blocks = ((srow % PW) // DK_R) == (scol // DV)

    def rows(c):
        return slice(c * CHUNK, (c + 1) * CHUNK)

    for pi in range(npairs):
        qs = slice(pi * PW, (pi + 1) * PW)
        vs = slice(pi * VW, (pi + 1) * VW)
        lg = _log_gammas(rd_ref, pair0 + pi)
        lg_v = [jnp.where(lane_v < DV, lg[d][0], lg[d][1]) for d in range(2)]
        chunk_decay = [jnp.exp(lg_v[d] * CHUNK) for d in range(2)]

        for c in range(nc if with_states else 0):
            k = k_ref[rows(c), qs].astype(F32)
            kd = (jnp.concatenate([k, k], axis=1) * kdec_scr[tbl0 + pi]).astype(BF16)
            u_scr[pi, c] = jnp.where(diag_blocks, _dot_tn(kd, v_ref[rows(c), vs]), 0.0)

        for d, order in ((0, range(nc)), (1, reversed(range(nc)))) if with_states else ():
            half = slice(d * PW, (d + 1) * PW)
            if s0_refs is None:
                s = jnp.zeros((PW, VW), F32)
            else:
                zero = jnp.zeros((DK_R, DV), F32)
                s = jnp.concatenate(
                    [jnp.concatenate([s0_refs[d][2 * pi], zero], axis=1),
                     jnp.concatenate([zero, s0_refs[d][2 * pi + 1]], axis=1)], axis=0)
            for c in order:
                if cross:
                    s_scr[pi, c, half, :] = s.astype(BF16)
                s = s * chunk_decay[d] + u_scr[pi, c, half, :]
            if sout_refs is not None:
                sout_refs[d][2 * pi] = s[:DK_R, :DV]
                sout_refs[d][2 * pi + 1] = s[DK_R:, DV:]

        g = g_ref[:, vs]
        for c in out_chunks:
            q = q_ref[rows(c), qs]
            k = k_ref[rows(c), qs]
            v = v_ref[rows(c), vs]
            zero = jnp.zeros_like(q)
            q2 = jnp.concatenate([jnp.where(lane_q < DK_R, q, zero),
                                  jnp.where(lane_q >= DK_R, q, zero)], axis=0)
            a2 = _dot_nt(q2, k)
            o = jnp.concatenate(
                [_dot((a2[hh * CHUNK:(hh + 1) * CHUNK] * dec_scr[tbl0 + pi, hh]).astype(BF16),
                      v[:, hh * DV:(hh + 1) * DV]) for hh in range(2)], axis=1)
            if cross:
                qf = q.astype(F32)
                qq = (jnp.concatenate([qf, qf], axis=1) * qdec_scr[tbl0 + pi]).astype(BF16)
                o = o + _dot(qq, s_scr[pi, c])
            outs = []
            for hh in range(2):
                oh = o[:, hh * DV:(hh + 1) * DV]
                mu = jnp.mean(oh, axis=-1, keepdims=True)
                oc = oh - mu
                var = jnp.mean(oc * oc, axis=-1, keepdims=True)
                outs.append(oc * lax.rsqrt(var + LN_EPS))
            o = jnp.concatenate(outs, axis=1) * g * z_ref[rows(c), vs].astype(F32)
            o_ref[rows(c), vs] = o.astype(BF16)


def _mix_lat_kernel(rd_ref, qa_ref, ka_ref, va_ref, za_ref, kc_ref, vc_ref, lp_ref, sg_ref,
                    qr_ref, kr_ref, vr_ref, zr_ref, gg_ref, s0f_ref, s0b_ref, oa_ref, or_ref,
                    k_all, v_all, u_scr, s_scr, dec_scr, qdec_scr, kdec_scr, *, past, tc, nc):
    b, h = pl.program_id(0), pl.program_id(1)
    pair = h // 2

    @pl.when((b == 0) & (h == 0))
    def _():
        _ret_build_tables(rd_ref, dec_scr, qdec_scr, kdec_scr, npairs=N_HEADS // 2, pair0=0)

    def step(with_states, out_chunks):
        _attn_lat_head(qa_ref, ka_ref, va_ref, za_ref, kc_ref, vc_ref, h, _lam(lp_ref), sg_ref[...],
                       oa_ref, k_all, v_all, past=past, tc=tc)
        _ret_pairs(rd_ref, qr_ref, kr_ref, vr_ref, zr_ref, gg_ref, (s0f_ref, s0b_ref), or_ref, None,
                   u_scr, s_scr, dec_scr, qdec_scr, kdec_scr, nc=nc, npairs=1, pair0=pair,
                   tbl0=pair, with_states=with_states, out_chunks=out_chunks)

    @pl.when(h % 2 == 0)
    def _():
        step(True, range(0, nc // 2))

    @pl.when(h % 2 == 1)
    def _():
        step(False, range(nc // 2, nc))


def _mix_lat_call(p, cache_k, cache_v, lam_params, subln_g, ret_decay, gn_g, s0f, s0b,
                  *, batch, seq, past):
    tc = 128
    nc = seq // CHUNK
    head_col = lambda blk: pl.BlockSpec((seq, LANES), lambda b, h: (b, blk + h))
    pair_q = lambda blk: pl.BlockSpec((seq, PW), lambda b, h: (b, blk * LANES // PW + h // 2))
    pair_v = lambda blk: pl.BlockSpec((seq, VW), lambda b, h: (b, blk * LANES // VW + h // 2))
    cache = pl.BlockSpec((None, past * N_HEADS, LANES), lambda b, h: (b, 0, 0))
    state = pl.BlockSpec((None, 2, DK_R, DV), lambda b, h: (b, h // 2, 0, 0))
    tok = jax.ShapeDtypeStruct((batch * seq, D_MODEL), BF16)
    npairs = N_HEADS // 2
    return pl.pallas_call(
        functools.partial(_mix_lat_kernel, past=past, tc=tc, nc=nc),
        grid=(batch, N_HEADS),
        in_specs=[pl.BlockSpec(memory_space=pltpu.SMEM),
                  head_col(QA_BLK), head_col(KA_BLK), head_col(VA_BLK), head_col(ZA_BLK),
                  cache, cache,
                  pl.BlockSpec((4, DK_A), lambda b, h: (0, 0)),
                  pl.BlockSpec((1, DV), lambda b, h: (0, 0)),
                  pair_q(QR_BLK), pair_q(KR_BLK), pair_v(VR_BLK), pair_v(ZR_BLK),
                  pl.BlockSpec((1, VW), lambda b, h: (0, h // 2)), state, state],
        out_specs=[pl.BlockSpec((seq, LANES), lambda b, h: (b, h)),
                   pl.BlockSpec((seq, VW), lambda b, h: (b, h // 2))],
        out_shape=[tok, tok],
        scratch_shapes=[pltpu.VMEM((past + seq, LANES), BF16),
                        pltpu.VMEM((past + seq, 2 * DV), BF16)]
                       + _ret_scratch(1, nc)[:2] + _ret_scratch(npairs, nc)[2:],
        compiler_params=_params(("arbitrary", "arbitrary")),
        name="mix_lat",
    )(ret_decay, p, p, p, p, cache_k, cache_v, lam_params, subln_g, p, p, p, p, gn_g, s0f, s0b)


def _out_rows(oa_ref, or_ref, ga_ref, gr_ref, x_ref, gate, wpa_ref, wpr_ref, wo_ref,
              lng_ref, lnb_ref, y_ref):
    a = _dot(oa_ref[...], wpa_ref[...])
    r = _dot(or_ref[...], wpr_ref[...])
    m = ga_ref[...].astype(F32) * a + gr_ref[...].astype(F32) * r
    out = _dot(m.astype(BF16), wo_ref[...])
    t = DEEPNORM_ALPHA * x_ref[...] + gate * out
    mu = jnp.mean(t, axis=-1, keepdims=True)
    tc = t - mu
    var = jnp.mean(tc * tc, axis=-1, keepdims=True)
    y_ref[...] = tc * lax.rsqrt(var + LN_EPS) * lng_ref[...] + lnb_ref[...]


def _out_lat_kernel(oa_ref, or_ref, ga_ref, gr_ref, x_ref, mod_ref, wpa_ref, wpr_ref, wo_ref,
                    lng_ref, lnb_ref, y_ref, *, per_seq, mod_row0):
    _, _, gate = _mod_row(mod_ref, mod_row0 + pl.program_id(0) // per_seq)
    _out_rows(oa_ref, or_ref, ga_ref, gr_ref, x_ref, gate, wpa_ref, wpr_ref, wo_ref,
              lng_ref, lnb_ref, y_ref)


def _out_lat_call(oa, orr, p, x2d, mod, w_pa, w_pr, w_out, ln_g, ln_b, *, seq, mod_row0):
    m = x2d.shape[0]
    tm = 512
    per_seq = seq // tm
    tile = lambda j: pl.BlockSpec((tm, D_MODEL), lambda i: (i, j))
    vec = pl.BlockSpec((1, D_MODEL), lambda i: (0, 0))
    return pl.pallas_call(
        functools.partial(_out_lat_kernel, per_seq=per_seq, mod_row0=mod_row0),
        grid=(m // tm,),
        in_specs=[tile(0), tile(0), tile(GA_BLK * LANES // D_MODEL), tile(GR_BLK * LANES // D_MODEL),
                  tile(0),
                  pl.BlockSpec((MOD_ROWS, 3 * D_MODEL), lambda i: (0, 0)),
                  _resident((D_MODEL, D_MODEL)), _resident((D_MODEL, D_MODEL)),
                  _resident((D_MODEL, D_MODEL)), vec, vec],
        out_specs=tile(0),
        out_shape=jax.ShapeDtypeStruct((m, D_MODEL), F32),
        compiler_params=_params(("arbitrary",)),
        name="out_lat",
    )(oa, orr, p, p, x2d, mod, w_pa, w_pr, w_out, ln_g, ln_b)


W_CHUNK = D_MODEL


def _weight_chunks(w_hbm):
    return [(k, slice(j * W_CHUNK, (j + 1) * W_CHUNK))
            for k, w in enumerate(w_hbm) for j in range(w.shape[1] // W_CHUNK)]


def _weight_store(n, chunks, w_scr, w_bf_hbm, out_sem):
    k, cols = chunks[n]
    return pltpu.make_async_copy(w_scr[k].at[:, cols], w_bf_hbm[k].at[:, cols], out_sem.at[n])


def _stream_f32_chunks(jobs, stage, in_sem):
    def load(n):
        return pltpu.make_async_copy(jobs[n][0], stage.at[n % 2], in_sem.at[n % 2])

    load(0).start()
    for n, (_, sink) in enumerate(jobs):
        if n + 1 < len(jobs):
            load(n + 1).start()
        load(n).wait()
        sink(stage.at[n % 2])


def _ctx_kernel(x_ref, cctx_ref, c_ref, wmod_hbm, bmod_ref, win_hbm, wg_hbm, bg_ref, lp_ref, sg_ref,
                rd_ref, gg_ref, wpa_hbm, wpr_hbm, wo_hbm, lng_ref, lnb_ref,
                y_ref, k_ref, v_ref, sf_ref, sb_ref, mod_ref, win_bf, wg_bf, wpa_bf, wpr_bf, wo_bf,
                p_scr, oa_scr, or_scr, u_scr, s_scr, dec_scr, qdec_scr, kdec_scr,
                win_ref, wg_ref, wpa_ref, wpr_ref, wo_ref, cond_scr, stage, in_sem, out_sem,
                *, npairs):
    w_hbm = (win_hbm, wg_hbm, wpa_hbm, wpr_hbm, wo_hbm)
    w_scr = (win_ref, wg_ref, wpa_ref, wpr_ref, wo_ref)
    w_bf = (win_bf, wg_bf, wpa_bf, wpr_bf, wo_bf)

    @pl.when(pl.program_id(0) == 0)
    def _():
        cond_scr[...] = jnp.zeros(cond_scr.shape, F32)
        cond_scr[0:1, :] = cctx_ref[...]
        cond_scr[1:1 + c_ref.shape[0], :] = c_ref[...]
        cond = _silu(cond_scr[...]).astype(BF16)
        jobs = []
        for j in range(wmod_hbm.shape[1] // W_CHUNK):
            cols = slice(j * W_CHUNK, (j + 1) * W_CHUNK)

            def mod_sink(staged, cols=cols):
                mod_ref[:, cols] = _dot(cond, staged[...].astype(BF16)) + bmod_ref[:, cols]
            jobs.append((wmod_hbm.at[:, cols], mod_sink))
        chunks = _weight_chunks(w_hbm)
        for k, cols in chunks:
            def cast_sink(staged, k=k, cols=cols):
                w_scr[k][:, cols] = staged[...].astype(BF16)
            jobs.append((w_hbm[k].at[:, cols], cast_sink))
        _stream_f32_chunks(jobs, stage, in_sem)
        for n in range(len(chunks)):
            _weight_store(n, chunks, w_scr, w_bf, out_sem).start()
        _ret_build_tables(rd_ref, dec_scr, qdec_scr, kdec_scr, npairs=npairs, pair0=0)

    cols = lambda blk, w: p_scr.at[:, blk * LANES:blk * LANES + w]
    shift, scale, gate = _mod_row(mod_ref, 0)
    _project(_modulated_ln(x_ref, shift, scale), win_ref, wg_ref, bg_ref, p_scr,
             kv_out=(k_ref, v_ref))
    _attn_ctx_heads(cols(QA_BLK, D_MODEL), cols(KA_BLK, D_MODEL), cols(VA_BLK, D_MODEL),
                    cols(ZA_BLK, D_MODEL), _lam(lp_ref), sg_ref[...], oa_scr)
    _ret_pairs(rd_ref, cols(QR_BLK, D_MODEL // 2), cols(KR_BLK, D_MODEL // 2),
               cols(VR_BLK, D_MODEL), cols(ZR_BLK, D_MODEL), gg_ref, None, or_scr,
               (sf_ref, sb_ref), u_scr, s_scr, dec_scr, qdec_scr, kdec_scr,
               nc=x_ref.shape[0] // CHUNK, npairs=npairs, pair0=0)
    _out_rows(oa_scr, or_scr, cols(GA_BLK, D_MODEL), cols(GR_BLK, D_MODEL), x_ref, gate,
              wpa_ref, wpr_ref, wo_ref, lng_ref, lnb_ref, y_ref)

    @pl.when(pl.program_id(0) == pl.num_programs(0) - 1)
    def _():
        chunks = _weight_chunks(w_hbm)
        for n in range(len(chunks)):
            _weight_store(n, chunks, w_scr, w_bf, out_sem).wait()


def _ctx_call(x2d, c_ctx, c, w_mod, b_mod, w_in, w_gate, b_gate, lam_params, subln_g, ret_decay,
              gn_g, w_pa, w_pr, w_out, ln_g, ln_b, *, batch, seq):
    npairs = N_HEADS // 2
    nc = seq // CHUNK
    tile = pl.BlockSpec((seq, D_MODEL), lambda b: (b, 0))
    vec = lambda w: pl.BlockSpec((1, w), lambda b: (0, 0))
    state = pl.BlockSpec((None, N_HEADS, DK_R, DV), lambda b: (b, 0, 0, 0))
    heads = pl.BlockSpec((seq * N_HEADS, LANES), lambda b: (b, 0))
    tok = jax.ShapeDtypeStruct((batch * seq, D_MODEL), F32)
    tok_heads = jax.ShapeDtypeStruct((batch * seq * N_HEADS, LANES), F32)
    st = jax.ShapeDtypeStruct((batch, N_HEADS, DK_R, DV), F32)
    hbm = pl.BlockSpec(memory_space=pl.ANY)
    weights = (w_in, w_gate, w_pa, w_pr, w_out)
    n_chunks = sum(w.shape[1] // W_CHUNK for w in weights)
    return pl.pallas_call(
        functools.partial(_ctx_kernel, npairs=npairs),
        grid=(batch,),
        in_specs=[tile, vec(D_MODEL), pl.BlockSpec(c.shape, lambda b: (0, 0)), hbm,
                  vec(3 * D_MODEL), hbm, hbm, vec(2 * D_MODEL),
                  pl.BlockSpec((4, DK_A), lambda b: (0, 0)), vec(DV),
                  pl.BlockSpec(memory_space=pltpu.SMEM), vec(D_MODEL),
                  hbm, hbm, hbm, vec(D_MODEL), vec(D_MODEL)],
        out_specs=[tile, heads, heads, state, state,
                   pl.BlockSpec((MOD_ROWS, 3 * D_MODEL), lambda b: (0, 0))] + [hbm] * len(weights),
        out_shape=[tok, tok_heads, tok_heads, st, st, jax.ShapeDtypeStruct((MOD_ROWS, 3 * D_MODEL), F32)]
                  + [jax.ShapeDtypeStruct(w.shape, BF16) for w in weights],
        scratch_shapes=[pltpu.VMEM((seq, P_WIDTH), BF16),
                        pltpu.VMEM((seq, D_MODEL), BF16),
                        pltpu.VMEM((seq, D_MODEL), BF16)] + _ret_scratch(npairs, nc)
                       + [pltpu.VMEM(w.shape, BF16) for w in weights]
                       + [pltpu.VMEM((MOD_ROWS, D_MODEL), F32),
                          pltpu.VMEM((2, D_MODEL, W_CHUNK), F32),
                          pltpu.SemaphoreType.DMA((2,)),
                          pltpu.SemaphoreType.DMA((n_chunks,))],
        compiler_params=_params(("arbitrary",)),
        name="ctx",
    )(x2d, c_ctx, c, w_mod, b_mod, w_in, w_gate, b_gate, lam_params, subln_g, ret_decay, gn_g,
      w_pa, w_pr, w_out, ln_g, ln_b)


def _rope_tables(n_tokens):
    rows = n_tokens // GRID_W
    r = np.repeat(np.arange(rows, dtype=np.float32), GRID_W)
    col = np.tile(np.arange(GRID_W, dtype=np.float32), rows)
    n_freq = DK_A // 4
    inv = np.float32(ROPE_BASE) ** (-np.arange(n_freq, dtype=np.float32) / np.float32(n_freq))
    ang = np.concatenate([r[:, None] * inv, col[:, None] * inv], axis=-1).astype(np.float32)
    cos = np.repeat(np.cos(ang), 2, axis=-1)
    sin = np.repeat(np.sin(ang), 2, axis=-1)
    even = (np.arange(DK_A) % 2 == 0)[None, :]
    sin_even = np.where(even, -sin, 0.0)
    sin_odd = np.where(even, 0.0, sin)
    two = lambda t: jnp.asarray(np.concatenate([t, t], axis=-1), F32)
    return two(cos), two(sin_even), two(sin_odd)


def kernel(x_prompt, x_sample, cache_attn_k, cache_attn_v, state_ret_fwd, state_ret_bwd,
           c, c_ctx, w_mod, b_mod, w_in, lam_params, subln_g, ret_decay, ret_gn_g,
           w_pa, w_pr, w_gate, b_gate, w_out, ln_g, ln_b):
    batch, seq, _ = x_prompt.shape
    dbatch, dseq, _ = x_sample.shape
    past = cache_attn_k.shape[2]
    l = 0

    bg = b_gate[l][None, :]
    lp, sg = lam_params[l], subln_g[l][None, :]
    rd, gg = ret_decay[l], ret_gn_g[l][None, :]
    lng, lnb = ln_g[l][None, :], ln_b[l][None, :]

    xc = x_prompt.reshape(batch * seq, D_MODEL)
    y_c, k_c, v_c, sf_c, sb_c, mod, win, wg, wpa, wpr, wo = _ctx_call(
        xc, c_ctx[None, :], c, w_mod[l], b_mod[l][None, :], w_in[l], w_gate[l], bg, lp, sg, rd, gg,
        w_pa[l], w_pr[l], w_out[l], lng, lnb, batch=batch, seq=seq)

    xs = x_sample.reshape(dbatch * dseq, D_MODEL)
    p_s = _proj_lat_call(xs, mod, win, wg, bg, _rope_tables(dseq), seq=dseq, mod_row0=1)
    ck = cache_attn_k[:, l].reshape(dbatch, past * N_HEADS, 2 * DK_A)
    cv = cache_attn_v[:, l].reshape(dbatch, past * N_HEADS, DV)
    oa_s, or_s = _mix_lat_call(p_s, ck, cv, lp, sg, rd, gg, state_ret_fwd[:, l], state_ret_bwd[:, l],
                               batch=dbatch, seq=dseq, past=past)
    y_s = _out_lat_call(oa_s, or_s, p_s, xs, mod, wpa, wpr, wo, lng, lnb, seq=dseq, mod_row0=1)

    return (y_c.reshape(batch, seq, D_MODEL),
            y_s.reshape(dbatch, dseq, D_MODEL),
            k_c.reshape(batch, 1, seq, N_HEADS, 2 * DK_A),
            v_c.reshape(batch, 1, seq, N_HEADS, DV),
            sf_c.reshape(batch, 1, N_HEADS, DK_R, DV),
            sb_c.reshape(batch, 1, N_HEADS, DK_R, DV))
```

```python
import functools
import math

import jax
import jax.numpy as jnp
import numpy as np
from jax import lax
from jax.experimental import pallas as pl
from jax.experimental.pallas import tpu as pltpu

F32 = jnp.float32
BF16 = jnp.bfloat16

D_MODEL = 1024
N_HEADS = 8
DK_A = 64
DV = 128
DK_R = 64
CHUNK = 256
GRID_W = 64
ROPE_BASE = 10000.0
MOD_EPS = 1e-6
LN_EPS = 1e-5
DEPTH = 1
DEEPNORM_ALPHA = (2.0 * DEPTH) ** 0.25
LAM_INIT = 0.8 - 0.6 * math.exp(-0.3 * 0)
LOG2E = math.log2(math.e)

LANES = 128
MOD_ROWS = 8
QA_BLK, KA_BLK, VA_BLK, ZA_BLK = 0, 8, 16, 24
QR_BLK, KR_BLK, VR_BLK, ZR_BLK = 32, 36, 40, 48
GA_BLK, GR_BLK = 56, 64
IN_SEGS = 7
P_WIDTH = (IN_SEGS + 2) * D_MODEL
VMEM_LIMIT = 56 * 1024 * 1024


def _params(sem):
    return pltpu.CompilerParams(dimension_semantics=sem, vmem_limit_bytes=VMEM_LIMIT,
                                skip_device_barrier=True)


def _resident(shape):
    return pl.BlockSpec(shape, lambda *_: tuple(0 for _ in shape), pipeline_mode=pl.Buffered(1))


def _silu(z):
    return z * (1.0 / (1.0 + jnp.exp(-z)))


def _dot(a, b):
    return jnp.dot(a, b, preferred_element_type=F32)


def _dot_nt(a, b):
    return lax.dot_general(a, b, (((1,), (1,)), ((), ())), preferred_element_type=F32)


def _dot_tn(a, b):
    return lax.dot_general(a, b, (((0,), (0,)), ((), ())), preferred_element_type=F32)


def _rope(acc, cos, sin_even, sin_odd):
    outs = []
    for hh in range(N_HEADS):
        xs = acc[:, hh * LANES:(hh + 1) * LANES]
        nxt = pltpu.roll(xs, LANES - 1, 1)
        prv = pltpu.roll(xs, 1, 1)
        outs.append(xs * cos + nxt * sin_even + prv * sin_odd)
    return jnp.concatenate(outs, axis=1)


def _mod_row(mod_ref, row):
    r = pl.ds(row, 1)
    return (mod_ref[r, 0:D_MODEL], mod_ref[r, D_MODEL:2 * D_MODEL], mod_ref[r, 2 * D_MODEL:3 * D_MODEL])


def _modulated_ln(x_ref, shift, scale):
    x = x_ref[...]
    mu = jnp.mean(x, axis=-1, keepdims=True)
    xc = x - mu
    var = jnp.mean(xc * xc, axis=-1, keepdims=True)
    return (xc * lax.rsqrt(var + MOD_EPS) * (1.0 + scale) + shift).astype(BF16)


def _store_heads(ref, val):
    for h in range(N_HEADS):
        ref[pl.ds(h, val.shape[0], stride=N_HEADS), :] = val[:, h * LANES:(h + 1) * LANES]


def _project(h, win_ref, wg_ref, bg_ref, p_ref, rope=None, kv_out=None):
    def seg(j):
        if j < IN_SEGS:
            return _dot(h, win_ref[:, j * D_MODEL:(j + 1) * D_MODEL])
        return _dot(h, wg_ref[:, (j - IN_SEGS) * D_MODEL:(j - IN_SEGS + 1) * D_MODEL])

    def put(j, val):
        p_ref[:, j * D_MODEL:(j + 1) * D_MODEL] = val.astype(BF16)

    qa = seg(0)
    ka = seg(1)
    if rope is not None:
        cos, se, so = (t[...] for t in rope)
        qa = _rope(qa, cos, se, so)
        ka = _rope(ka, cos, se, so)
    if kv_out is not None:
        _store_heads(kv_out[0], ka)
    put(0, qa * (DK_A ** -0.5 * LOG2E))
    put(1, ka)
    va = seg(2)
    if kv_out is not None:
        _store_heads(kv_out[1], va)
    put(2, va)
    put(3, _silu(seg(3)))
    qk = seg(4)
    put(4, jnp.concatenate([qk[:, :D_MODEL // 2], qk[:, D_MODEL // 2:] * (DK_R ** -0.5)], axis=1))
    put(5, seg(5))
    put(6, _silu(seg(6)))
    for j in (IN_SEGS, IN_SEGS + 1):
        g = seg(j) + bg_ref[:, (j - IN_SEGS) * D_MODEL:(j - IN_SEGS + 1) * D_MODEL]
        put(j, 1.0 / (1.0 + jnp.exp(-g)))


def _proj_lat_kernel(x_ref, mod_ref, win_ref, wg_ref, bg_ref, cos_ref, se_ref, so_ref, p_ref,
                     *, per_seq, mod_row0):
    shift, scale, _ = _mod_row(mod_ref, mod_row0 + pl.program_id(0) // per_seq)
    _project(_modulated_ln(x_ref, shift, scale), win_ref, wg_ref, bg_ref, p_ref,
             rope=(cos_ref, se_ref, so_ref))


def _proj_lat_call(x2d, mod, w_in, w_gate, b_gate, rope, *, seq, mod_row0):
    m = x2d.shape[0]
    tm = 256
    per_seq = seq // tm
    return pl.pallas_call(
        functools.partial(_proj_lat_kernel, per_seq=per_seq, mod_row0=mod_row0),
        grid=(m // tm,),
        in_specs=[
            pl.BlockSpec((tm, D_MODEL), lambda i: (i, 0)),
            pl.BlockSpec((MOD_ROWS, 3 * D_MODEL), lambda i: (0, 0)),
            _resident((D_MODEL, IN_SEGS * D_MODEL)),
            _resident((D_MODEL, 2 * D_MODEL)),
            pl.BlockSpec((1, 2 * D_MODEL), lambda i: (0, 0)),
        ] + [pl.BlockSpec((tm, LANES), lambda i: (i % per_seq, 0))] * 3,
        out_specs=pl.BlockSpec((tm, P_WIDTH), lambda i: (i, 0)),
        out_shape=jax.ShapeDtypeStruct((m, P_WIDTH), BF16),
        compiler_params=_params(("arbitrary",)),
        name="proj_lat",
    )(x2d, mod, w_in, w_gate, b_gate, *rope)


def _lam(lp_ref):
    lp = lp_ref[...]
    a = jnp.sum(lp[0:1] * lp[1:2], axis=-1, keepdims=True)
    b = jnp.sum(lp[2:3] * lp[3:4], axis=-1, keepdims=True)
    return jnp.exp(a) - jnp.exp(b) + LAM_INIT


def _attn_rows(q, k_all, v_ones, z, lam, g):
    lane = lax.broadcasted_iota(jnp.int32, (1, LANES), 1)
    zero = jnp.zeros_like(q)
    outs = []
    for qm in (jnp.where(lane < DK_A, q, zero), jnp.where(lane >= DK_A, q, zero)):
        s = _dot_nt(qm, k_all)
        mx = jnp.max(s, axis=-1, keepdims=True)
        p = jnp.exp2(s - mx).astype(BF16)
        ol = _dot(p, v_ones)
        outs.append(ol[:, :DV] * (1.0 / ol[:, DV:]))
    oa = outs[0] - lam * outs[1]
    oa = oa * lax.rsqrt(jnp.mean(oa * oa, axis=-1, keepdims=True) + LN_EPS)
    oa = oa * g * (1.0 - LAM_INIT)
    return oa * z.astype(F32)


def _attn_ctx_heads(q_ref, k_ref, v_ref, z_ref, lam, g, o_ref):
    ones = jnp.ones((q_ref.shape[0], DV), BF16)
    for h in range(N_HEADS):
        sl = slice(h * LANES, (h + 1) * LANES)
        v_ones = jnp.concatenate([v_ref[:, sl], ones], axis=1)
        o_ref[:, sl] = _attn_rows(q_ref[:, sl], k_ref[:, sl], v_ones, z_ref[:, sl],
                                  lam, g).astype(BF16)


def _attn_lat_kernel(q_ref, k_ref, v_ref, z_ref, kc_ref, vc_ref, lp_ref, g_ref, o_ref,
                     k_all, v_all, *, past, tc):
    @pl.when(pl.program_id(2) == 0)
    def _():
        head_rows = pl.ds(pl.program_id(1), past, stride=N_HEADS)
        k_all[0:past, :] = kc_ref[head_rows, :].astype(BF16)
        v_all[0:past, 0:DV] = vc_ref[head_rows, :].astype(BF16)
        k_all[past:, :] = k_ref[...]
        v_all[past:, 0:DV] = v_ref[...]
        v_all[:, DV:] = jnp.ones((v_all.shape[0], DV), BF16)

    lam = _lam(lp_ref)
    g = g_ref[...]
    for lo in range(0, q_ref.shape[0], tc):
        o_ref[lo:lo + tc, :] = _attn_rows(q_ref[lo:lo + tc, :], k_all[...], v_all[...],
                                          z_ref[lo:lo + tc, :], lam, g).astype(BF16)


def _attn_lat_call(p, cache_k, cache_v, lam_params, subln_g, *, batch, seq, past):
    tq, tc = 2048, 128
    nq = seq // tq
    return pl.pallas_call(
        functools.partial(_attn_lat_kernel, past=past, tc=tc),
        grid=(batch, N_HEADS, nq),
        in_specs=[
            pl.BlockSpec((tq, LANES), lambda b, h, i: (b * nq + i, QA_BLK + h)),
            pl.BlockSpec((seq, LANES), lambda b, h, i: (b, KA_BLK + h)),
            pl.BlockSpec((seq, LANES), lambda b, h, i: (b, VA_BLK + h)),
            pl.BlockSpec((tq, LANES), lambda b, h, i: (b * nq + i, ZA_BLK + h)),
            pl.BlockSpec((None, past * N_HEADS, LANES), lambda b, h, i: (b, 0, 0)),
            pl.BlockSpec((None, past * N_HEADS, LANES), lambda b, h, i: (b, 0, 0)),
            pl.BlockSpec((4, DK_A), lambda b, h, i: (0, 0)),
            pl.BlockSpec((1, DV), lambda b, h, i: (0, 0)),
        ],
        out_specs=pl.BlockSpec((tq, LANES), lambda b, h, i: (b * nq + i, h)),
        out_shape=jax.ShapeDtypeStruct((batch * seq, D_MODEL), BF16),
        scratch_shapes=[pltpu.VMEM((past + seq, LANES), BF16),
                        pltpu.VMEM((past + seq, 2 * DV), BF16)],
        compiler_params=_params(("arbitrary", "arbitrary", "arbitrary")),
        name="attn_lat",
    )(p, p, p, p, cache_k, cache_v, lam_params, subln_g)


PW, VW = 2 * DK_R, 2 * DV


def _ret_scratch(npairs, nc):
    stacked_state = (npairs, nc, 2 * PW, VW)
    return [pltpu.VMEM(stacked_state, F32),
            pltpu.VMEM(stacked_state, BF16),
            pltpu.VMEM((npairs, 2, CHUNK, CHUNK), F32),
            pltpu.VMEM((npairs, CHUNK, 2 * PW), F32),
            pltpu.VMEM((npairs, CHUNK, 2 * PW), F32)]


def _log_gammas(rd_ref, pair):
    def one(d, hh):
        r = jnp.full((1, VW), rd_ref[d, 2 * pair + hh], F32)
        return jnp.log1p(-jnp.exp2(r))
    return [[one(d, hh) for hh in range(2)] for d in range(2)]


def _ret_build_tables(rd_ref, dec_scr, qdec_scr, kdec_scr, *, npairs, pair0):
    lane_q = lax.broadcasted_iota(jnp.int32, (1, PW), 1)
    rel = (lax.broadcasted_iota(jnp.int32, (CHUNK, CHUNK), 0)
           - lax.broadcasted_iota(jnp.int32, (CHUNK, CHUNK), 1)).astype(F32)
    row_q = lax.broadcasted_iota(jnp.int32, (CHUNK, PW), 0).astype(F32)
    for pi in range(npairs):
        lg = _log_gammas(rd_ref, pair0 + pi)
        lg_q = [jnp.where(lane_q < DK_R, lg[d][0][:, :PW], lg[d][1][:, :PW])
                for d in range(2)]
        for hh in range(2):
            dec_scr[pi, hh] = (
                jnp.where(rel >= 0, jnp.exp(lg[0][hh][:, :1] * jnp.maximum(rel, 0.0)), 0.0)
                + jnp.where(rel <= 0, jnp.exp(lg[1][hh][:, :1] * jnp.maximum(-rel, 0.0)), 0.0))
        qdec_scr[pi] = jnp.concatenate([jnp.exp(lg_q[0] * (row_q + 1.0)),
                                        jnp.exp(lg_q[1] * (CHUNK - row_q))], axis=1)
        kdec_scr[pi] = jnp.concatenate([jnp.exp(lg_q[0] * (CHUNK - 1.0 - row_q)),
                                        jnp.exp(lg_q[1] * row_q)], axis=1)


def _ret_pairs(rd_ref, q_ref, k_ref, v_ref, z_ref, g_ref, s0_refs, o_ref, sout_refs,
               u_scr, s_scr, dec_scr, qdec_scr, kdec_scr, *, nc, npairs, pair0):
    cross = s0_refs is not None or nc > 1
    lane_q = lax.broadcasted_iota(jnp.int32, (1, PW), 1)
    lane_v = lax.broadcasted_iota(jnp.int32, (1, VW), 1)
    srow = lax.broadcasted_iota(jnp.int32, (2 * PW, VW), 0)
    scol = lax.broadcasted_iota(jnp.int32, (2 * PW, VW), 1)
    diag_blocks = ((srow % PW) // DK_R) == (scol // DV)

    def rows(c):
        return slice(c * CHUNK, (c + 1) * CHUNK)

    for pi in range(npairs):
        qs = slice(pi * PW, (pi + 1) * PW)
        vs = slice(pi * VW, (pi + 1) * VW)
        lg = _log_gammas(rd_ref, pair0 + pi)
        lg_v = [jnp.where(lane_v < DV, lg[d][0], lg[d][1]) for d in range(2)]
        chunk_decay = [jnp.exp(lg_v[d] * CHUNK) for d in range(2)]

        for c in range(nc):
            k = k_ref[rows(c), qs].astype(F32)
            kd = (jnp.concatenate([k, k], axis=1) * kdec_scr[pi]).astype(BF16)
            u_scr[pi, c] = jnp.where(diag_blocks, _dot_tn(kd, v_ref[rows(c), vs]), 0.0)

        for d, order in ((0, range(nc)), (1, reversed(range(nc)))):
            half = slice(d * PW, (d + 1) * PW)
            if s0_refs is None:
                s = jnp.zeros((PW, VW), F32)
            else:
                zero = jnp.zeros((DK_R, DV), F32)
                s = jnp.concatenate(
                    [jnp.concatenate([s0_refs[d][2 * pi], zero], axis=1),
                     jnp.concatenate([zero, s0_refs[d][2 * pi + 1]], axis=1)], axis=0)
            for c in order:
                if cross:
                    s_scr[pi, c, half, :] = s.astype(BF16)
                s = s * chunk_decay[d] + u_scr[pi, c, half, :]
            if sout_refs is not None:
                sout_refs[d][2 * pi] = s[:DK_R, :DV]
                sout_refs[d][2 * pi + 1] = s[DK_R:, DV:]

        g = g_ref[:, vs]
        for c in range(nc):
            q = q_ref[rows(c), qs]
            k = k_ref[rows(c), qs]
            v = v_ref[rows(c), vs]
            zero = jnp.zeros_like(q)
            q2 = jnp.concatenate([jnp.where(lane_q < DK_R, q, zero),
                                  jnp.where(lane_q >= DK_R, q, zero)], axis=0)
            a2 = _dot_nt(q2, k)
            o = jnp.concatenate(
                [_dot((a2[hh * CHUNK:(hh + 1) * CHUNK] * dec_scr[pi, hh]).astype(BF16),
                      v[:, hh * DV:(hh + 1) * DV]) for hh in range(2)], axis=1)
            if cross:
                qf = q.astype(F32)
                qq = (jnp.concatenate([qf, qf], axis=1) * qdec_scr[pi]).astype(BF16)
                o = o + _dot(qq, s_scr[pi, c])
            outs = []
            for hh in range(2):
                oh = o[:, hh * DV:(hh + 1) * DV]
                mu = jnp.mean(oh, axis=-1, keepdims=True)
                oc = oh - mu
                var = jnp.mean(oc * oc, axis=-1, keepdims=True)
                outs.append(oc * lax.rsqrt(var + LN_EPS))
            o = jnp.concatenate(outs, axis=1) * g * z_ref[rows(c), vs].astype(F32)
            o_ref[rows(c), vs] = o.astype(BF16)


def _ret_lat_kernel(rd_ref, q_ref, k_ref, v_ref, z_ref, g_ref, s0f_ref, s0b_ref, o_ref,
                    u_scr, s_scr, dec_scr, qdec_scr, kdec_scr, *, nc, npairs):
    pair0 = pl.program_id(0) * npairs

    @pl.when(pl.program_id(1) == 0)
    def _():
        _ret_build_tables(rd_ref, dec_scr, qdec_scr, kdec_scr, npairs=npairs, pair0=pair0)

    _ret_pairs(rd_ref, q_ref, k_ref, v_ref, z_ref, g_ref, (s0f_ref, s0b_ref), o_ref, None,
               u_scr, s_scr, dec_scr, qdec_scr, kdec_scr, nc=nc, npairs=npairs, pair0=pair0)


def _ret_lat_call(p, ret_decay, gn_g, s0f, s0b, *, batch, seq):
    nc = seq // CHUNK
    hps = 2
    npairs = hps // 2
    qw, vw = DK_R * hps, DV * hps
    col = lambda w, blk: pl.BlockSpec((seq, w), lambda j, b: (b, blk * LANES // w + j))
    state = pl.BlockSpec((None, hps, DK_R, DV), lambda j, b: (b, j, 0, 0))
    return pl.pallas_call(
        functools.partial(_ret_lat_kernel, nc=nc, npairs=npairs),
        grid=(N_HEADS // hps, batch),
        in_specs=[pl.BlockSpec(memory_space=pltpu.SMEM),
                  col(qw, QR_BLK), col(qw, KR_BLK), col(vw, VR_BLK), col(vw, ZR_BLK),
                  pl.BlockSpec((1, vw), lambda j, b: (0, j)), state, state],
        out_specs=pl.BlockSpec((seq, vw), lambda j, b: (b, j)),
        out_shape=jax.ShapeDtypeStruct((batch * seq, D_MODEL), BF16),
        scratch_shapes=_ret_scratch(npairs, nc),
        compiler_params=_params(("arbitrary", "arbitrary")),
        name="ret_lat",
    )(ret_decay, p, p, p, p, gn_g, s0f, s0b)


def _out_rows(oa_ref, or_ref, ga_ref, gr_ref, x_ref, gate, wpa_ref, wpr_ref, wo_ref,
              lng_ref, lnb_ref, y_ref):
    a = _dot(oa_ref[...], wpa_ref[...])
    r = _dot(or_ref[...], wpr_ref[...])
    m = ga_ref[...].astype(F32) * a + gr_ref[...].astype(F32) * r
    out = _dot(m.astype(BF16), wo_ref[...])
    t = DEEPNORM_ALPHA * x_ref[...] + gate * out
    mu = jnp.mean(t, axis=-1, keepdims=True)
    tc = t - mu
    var = jnp.mean(tc * tc, axis=-1, keepdims=True)
    y_ref[...] = tc * lax.rsqrt(var + LN_EPS) * lng_ref[...] + lnb_ref[...]


def _out_lat_kernel(oa_ref, or_ref, ga_ref, gr_ref, x_ref, mod_ref, wpa_ref, wpr_ref, wo_ref,
                    lng_ref, lnb_ref, y_ref, *, per_seq, mod_row0):
    _, _, gate = _mod_row(mod_ref, mod_row0 + pl.program_id(0) // per_seq)
    _out_rows(oa_ref, or_ref, ga_ref, gr_ref, x_ref, gate, wpa_ref, wpr_ref, wo_ref,
              lng_ref, lnb_ref, y_ref)


def _out_lat_call(oa, orr, p, x2d, mod, w_pa, w_pr, w_out, ln_g, ln_b, *, seq, mod_row0):
    m = x2d.shape[0]
    tm = 512
    per_seq = seq // tm
    tile = lambda j: pl.BlockSpec((tm, D_MODEL), lambda i: (i, j))
    vec = pl.BlockSpec((1, D_MODEL), lambda i: (0, 0))
    return pl.pallas_call(
        functools.partial(_out_lat_kernel, per_seq=per_seq, mod_row0=mod_row0),
        grid=(m // tm,),
        in_specs=[tile(0), tile(0), tile(GA_BLK * LANES // D_MODEL), tile(GR_BLK * LANES // D_MODEL),
                  tile(0),
                  pl.BlockSpec((MOD_ROWS, 3 * D_MODEL), lambda i: (0, 0)),
                  _resident((D_MODEL, D_MODEL)), _resident((D_MODEL, D_MODEL)),
                  _resident((D_MODEL, D_MODEL)), vec, vec],
        out_specs=tile(0),
        out_shape=jax.ShapeDtypeStruct((m, D_MODEL), F32),
        compiler_params=_params(("arbitrary",)),
        name="out_lat",
    )(oa, orr, p, p, x2d, mod, w_pa, w_pr, w_out, ln_g, ln_b)


W_CHUNK = D_MODEL


def _weight_chunks(w_hbm):
    return [(k, slice(j * W_CHUNK, (j + 1) * W_CHUNK))
            for k, w in enumerate(w_hbm) for j in range(w.shape[1] // W_CHUNK)]


def _weight_store(n, chunks, w_scr, w_bf_hbm, out_sem):
    k, cols = chunks[n]
    return pltpu.make_async_copy(w_scr[k].at[:, cols], w_bf_hbm[k].at[:, cols], out_sem.at[n])


def _stream_f32_chunks(jobs, stage, in_sem):
    def load(n):
        return pltpu.make_async_copy(jobs[n][0], stage.at[n % 2], in_sem.at[n % 2])

    load(0).start()
    for n, (_, sink) in enumerate(jobs):
        if n + 1 < len(jobs):
            load(n + 1).start()
        load(n).wait()
        sink(stage.at[n % 2])


def _ctx_kernel(x_ref, cctx_ref, c_ref, wmod_hbm, bmod_ref, win_hbm, wg_hbm, bg_ref, lp_ref, sg_ref,
                rd_ref, gg_ref, wpa_hbm, wpr_hbm, wo_hbm, lng_ref, lnb_ref,
                y_ref, k_ref, v_ref, sf_ref, sb_ref, mod_ref, win_bf, wg_bf, wpa_bf, wpr_bf, wo_bf,
                p_scr, oa_scr, or_scr, u_scr, s_scr, dec_scr, qdec_scr, kdec_scr,
                win_ref, wg_ref, wpa_ref, wpr_ref, wo_ref, cond_scr, stage, in_sem, out_sem,
                *, npairs):
    w_hbm = (win_hbm, wg_hbm, wpa_hbm, wpr_hbm, wo_hbm)
    w_scr = (win_ref, wg_ref, wpa_ref, wpr_ref, wo_ref)
    w_bf = (win_bf, wg_bf, wpa_bf, wpr_bf, wo_bf)

    @pl.when(pl.program_id(0) == 0)
    def _():
        cond_scr[...] = jnp.zeros(cond_scr.shape, F32)
        cond_scr[0:1, :] = cctx_ref[...]
        cond_scr[1:1 + c_ref.shape[0], :] = c_ref[...]
        cond = _silu(cond_scr[...]).astype(BF16)
        jobs = []
        for j in range(wmod_hbm.shape[1] // W_CHUNK):
            cols = slice(j * W_CHUNK, (j + 1) * W_CHUNK)

            def mod_sink(staged, cols=cols):
                mod_ref[:, cols] = _dot(cond, staged[...].astype(BF16)) + bmod_ref[:, cols]
            jobs.append((wmod_hbm.at[:, cols], mod_sink))
        chunks = _weight_chunks(w_hbm)
        for k, cols in chunks:
            def cast_sink(staged, k=k, cols=cols):
                w_scr[k][:, cols] = staged[...].astype(BF16)
            jobs.append((w_hbm[k].at[:, cols], cast_sink))
        _stream_f32_chunks(jobs, stage, in_sem)
        for n in range(len(chunks)):
            _weight_store(n, chunks, w_scr, w_bf, out_sem).start()
        _ret_build_tables(rd_ref, dec_scr, qdec_scr, kdec_scr, npairs=npairs, pair0=0)

    cols = lambda blk, w: p_scr.at[:, blk * LANES:blk * LANES + w]
    shift, scale, gate = _mod_row(mod_ref, 0)
    _project(_modulated_ln(x_ref, shift, scale), win_ref, wg_ref, bg_ref, p_scr,
             kv_out=(k_ref, v_ref))
    _attn_ctx_heads(cols(QA_BLK, D_MODEL), cols(KA_BLK, D_MODEL), cols(VA_BLK, D_MODEL),
                    cols(ZA_BLK, D_MODEL), _lam(lp_ref), sg_ref[...], oa_scr)
    _ret_pairs(rd_ref, cols(QR_BLK, D_MODEL // 2), cols(KR_BLK, D_MODEL // 2),
               cols(VR_BLK, D_MODEL), cols(ZR_BLK, D_MODEL), gg_ref, None, or_scr,
               (sf_ref, sb_ref), u_scr, s_scr, dec_scr, qdec_scr, kdec_scr,
               nc=x_ref.shape[0] // CHUNK, npairs=npairs, pair0=0)
    _out_rows(oa_scr, or_scr, cols(GA_BLK, D_MODEL), cols(GR_BLK, D_MODEL), x_ref, gate,
              wpa_ref, wpr_ref, wo_ref, lng_ref, lnb_ref, y_ref)

    @pl.when(pl.program_id(0) == pl.num_programs(0) - 1)
    def _():
        chunks = _weight_chunks(w_hbm)
        for n in range(len(chunks)):
            _weight_store(n, chunks, w_scr, w_bf, out_sem).wait()


def _ctx_call(x2d, c_ctx, c, w_mod, b_mod, w_in, w_gate, b_gate, lam_params, subln_g, ret_decay,
              gn_g, w_pa, w_pr, w_out, ln_g, ln_b, *, batch, seq):
    npairs = N_HEADS // 2
    nc = seq // CHUNK
    tile = pl.BlockSpec((seq, D_MODEL), lambda b: (b, 0))
    vec = lambda w: pl.BlockSpec((1, w), lambda b: (0, 0))
    state = pl.BlockSpec((None, N_HEADS, DK_R, DV), lambda b: (b, 0, 0, 0))
    heads = pl.BlockSpec((seq * N_HEADS, LANES), lambda b: (b, 0))
    tok = jax.ShapeDtypeStruct((batch * seq, D_MODEL), F32)
    tok_heads = jax.ShapeDtypeStruct((batch * seq * N_HEADS, LANES), F32)
    st = jax.ShapeDtypeStruct((batch, N_HEADS, DK_R, DV), F32)
    hbm = pl.BlockSpec(memory_space=pl.ANY)
    weights = (w_in, w_gate, w_pa, w_pr, w_out)
    n_chunks = sum(w.shape[1] // W_CHUNK for w in weights)
    return pl.pallas_call(
        functools.partial(_ctx_kernel, npairs=npairs),
        grid=(batch,),
        in_specs=[tile, vec(D_MODEL), pl.BlockSpec(c.shape, lambda b: (0, 0)), hbm,
                  vec(3 * D_MODEL), hbm, hbm, vec(2 * D_MODEL),
                  pl.BlockSpec((4, DK_A), lambda b: (0, 0)), vec(DV),
                  pl.BlockSpec(memory_space=pltpu.SMEM), vec(D_MODEL),
                  hbm, hbm, hbm, vec(D_MODEL), vec(D_MODEL)],
        out_specs=[tile, heads, heads, state, state,
                   pl.BlockSpec((MOD_ROWS, 3 * D_MODEL), lambda b: (0, 0))] + [hbm] * len(weights),
        out_shape=[tok, tok_heads, tok_heads, st, st, jax.ShapeDtypeStruct((MOD_ROWS, 3 * D_MODEL), F32)]
                  + [jax.ShapeDtypeStruct(w.shape, BF16) for w in weights],
        scratch_shapes=[pltpu.VMEM((seq, P_WIDTH), BF16),
                        pltpu.VMEM((seq, D_MODEL), BF16),
                        pltpu.VMEM((seq, D_MODEL), BF16)] + _ret_scratch(npairs, nc)
                       + [pltpu.VMEM(w.shape, BF16) for w in weights]
                       + [pltpu.VMEM((MOD_ROWS, D_MODEL), F32),
                          pltpu.VMEM((2, D_MODEL, W_CHUNK), F32),
                          pltpu.SemaphoreType.DMA((2,)),
                          pltpu.SemaphoreType.DMA((n_chunks,))],
        compiler_params=_params(("arbitrary",)),
        name="ctx",
    )(x2d, c_ctx, c, w_mod, b_mod, w_in, w_gate, b_gate, lam_params, subln_g, ret_decay, gn_g,
      w_pa, w_pr, w_out, ln_g, ln_b)


def _rope_tables(n_tokens):
    rows = n_tokens // GRID_W
    r = np.repeat(np.arange(rows, dtype=np.float32), GRID_W)
    col = np.tile(np.arange(GRID_W, dtype=np.float32), rows)
    n_freq = DK_A // 4
    inv = np.float32(ROPE_BASE) ** (-np.arange(n_freq, dtype=np.float32) / np.float32(n_freq))
    ang = np.concatenate([r[:, None] * inv, col[:, None] * inv], axis=-1).astype(np.float32)
    cos = np.repeat(np.cos(ang), 2, axis=-1)
    sin = np.repeat(np.sin(ang), 2, axis=-1)
    even = (np.arange(DK_A) % 2 == 0)[None, :]
    sin_even = np.where(even, -sin, 0.0)
    sin_odd = np.where(even, 0.0, sin)
    two = lambda t: jnp.asarray(np.concatenate([t, t], axis=-1), F32)
    return two(cos), two(sin_even), two(sin_odd)


def kernel(x_prompt, x_sample, cache_attn_k, cache_attn_v, state_ret_fwd, state_ret_bwd,
           c, c_ctx, w_mod, b_mod, w_in, lam_params, subln_g, ret_decay, ret_gn_g,
           w_pa, w_pr, w_gate, b_gate, w_out, ln_g, ln_b):
    batch, seq, _ = x_prompt.shape
    dbatch, dseq, _ = x_sample.shape
    past = cache_attn_k.shape[2]
    l = 0

    bg = b_gate[l][None, :]
    lp, sg = lam_params[l], subln_g[l][None, :]
    rd, gg = ret_decay[l], ret_gn_g[l][None, :]
    lng, lnb = ln_g[l][None, :], ln_b[l][None, :]

    xc = x_prompt.reshape(batch * seq, D_MODEL)
    y_c, k_c, v_c, sf_c, sb_c, mod, win, wg, wpa, wpr, wo = _ctx_call(
        xc, c_ctx[None, :], c, w_mod[l], b_mod[l][None, :], w_in[l], w_gate[l], bg, lp, sg, rd, gg,
        w_pa[l], w_pr[l], w_out[l], lng, lnb, batch=batch, seq=seq)

    xs = x_sample.reshape(dbatch * dseq, D_MODEL)
    p_s = _proj_lat_call(xs, mod, win, wg, bg, _rope_tables(dseq), seq=dseq, mod_row0=1)
    ck = cache_attn_k[:, l].reshape(dbatch, past * N_HEADS, 2 * DK_A)
    cv = cache_attn_v[:, l].reshape(dbatch, past * N_HEADS, DV)
    oa_s = _attn_lat_call(p_s, ck, cv, lp, sg, batch=dbatch, seq=dseq, past=past)
    or_s = _ret_lat_call(p_s, rd, gg, state_ret_fwd[:, l], state_ret_bwd[:, l],
                         batch=dbatch, seq=dseq)
    y_s = _out_lat_call(oa_s, or_s, p_s, xs, mod, wpa, wpr, wo, lng, lnb, seq=dseq, mod_row0=1)

    return (y_c.reshape(batch, seq, D_MODEL),
            y_s.reshape(dbatch, dseq, D_MODEL),
            k_c.reshape(batch, 1, seq, N_HEADS, 2 * DK_A),
            v_c.reshape(batch, 1, seq, N_HEADS, DV),
            sf_c.reshape(batch, 1, N_HEADS, DK_R, DV),
            sb_c.reshape(batch, 1, N_HEADS, DK_R, DV))
```

```python
import functools
import math

import jax
import jax.numpy as jnp
import numpy as np
from jax import lax
from jax.experimental import pallas as pl
from jax.experimental.pallas import tpu as pltpu

F32 = jnp.float32
BF16 = jnp.bfloat16

D_MODEL = 1024
N_HEADS = 8
DK_A = 64
DV = 128
DK_R = 64
CHUNK = 256
GRID_W = 64
ROPE_BASE = 10000.0
MOD_EPS = 1e-6
LN_EPS = 1e-5
DEPTH = 1
DEEPNORM_ALPHA = (2.0 * DEPTH) ** 0.25
LAM_INIT = 0.8 - 0.6 * math.exp(-0.3 * 0)
LOG2E = math.log2(math.e)

LANES = 128
MOD_ROWS = 8
QA_BLK, KA_BLK, VA_BLK, ZA_BLK = 0, 8, 16, 24
QR_BLK, KR_BLK, VR_BLK, ZR_BLK = 32, 36, 40, 48
GA_BLK, GR_BLK = 56, 64
IN_SEGS = 7
P_WIDTH = (IN_SEGS + 2) * D_MODEL
VMEM_LIMIT = 56 * 1024 * 1024


def _params(sem):
    return pltpu.CompilerParams(dimension_semantics=sem, vmem_limit_bytes=VMEM_LIMIT)


def _resident(shape):
    return pl.BlockSpec(shape, lambda *_: tuple(0 for _ in shape), pipeline_mode=pl.Buffered(1))


def _silu(z):
    return z * (1.0 / (1.0 + jnp.exp(-z)))


def _dot(a, b):
    return jnp.dot(a, b, preferred_element_type=F32)


def _dot_nt(a, b):
    return lax.dot_general(a, b, (((1,), (1,)), ((), ())), preferred_element_type=F32)


def _dot_tn(a, b):
    return lax.dot_general(a, b, (((0,), (0,)), ((), ())), preferred_element_type=F32)


def _rope(acc, cos, sin_even, sin_odd):
    outs = []
    for hh in range(N_HEADS):
        xs = acc[:, hh * LANES:(hh + 1) * LANES]
        nxt = pltpu.roll(xs, LANES - 1, 1)
        prv = pltpu.roll(xs, 1, 1)
        outs.append(xs * cos + nxt * sin_even + prv * sin_odd)
    return jnp.concatenate(outs, axis=1)


def _mod_row(mod_ref, row):
    r = pl.ds(row, 1)
    return (mod_ref[r, 0:D_MODEL], mod_ref[r, D_MODEL:2 * D_MODEL], mod_ref[r, 2 * D_MODEL:3 * D_MODEL])


def _modulated_ln(x_ref, shift, scale):
    x = x_ref[...]
    mu = jnp.mean(x, axis=-1, keepdims=True)
    xc = x - mu
    var = jnp.mean(xc * xc, axis=-1, keepdims=True)
    return (xc * lax.rsqrt(var + MOD_EPS) * (1.0 + scale) + shift).astype(BF16)


def _store_heads(ref, val):
    for h in range(N_HEADS):
        ref[pl.ds(h, val.shape[0], stride=N_HEADS), :] = val[:, h * LANES:(h + 1) * LANES]


def _project(h, win_ref, wg_ref, bg_ref, p_ref, rope=None, kv_out=None):
    def seg(j):
        if j < IN_SEGS:
            return _dot(h, win_ref[:, j * D_MODEL:(j + 1) * D_MODEL])
        return _dot(h, wg_ref[:, (j - IN_SEGS) * D_MODEL:(j - IN_SEGS + 1) * D_MODEL])

    def put(j, val):
        p_ref[:, j * D_MODEL:(j + 1) * D_MODEL] = val.astype(BF16)

    qa = seg(0)
    ka = seg(1)
    if rope is not None:
        cos, se, so = (t[...] for t in rope)
        qa = _rope(qa, cos, se, so)
        ka = _rope(ka, cos, se, so)
    if kv_out is not None:
        _store_heads(kv_out[0], ka)
    put(0, qa * (DK_A ** -0.5 * LOG2E))
    put(1, ka)
    va = seg(2)
    if kv_out is not None:
        _store_heads(kv_out[1], va)
    put(2, va)
    put(3, _silu(seg(3)))
    qk = seg(4)
    put(4, jnp.concatenate([qk[:, :D_MODEL // 2], qk[:, D_MODEL // 2:] * (DK_R ** -0.5)], axis=1))
    put(5, seg(5))
    put(6, _silu(seg(6)))
    for j in (IN_SEGS, IN_SEGS + 1):
        g = seg(j) + bg_ref[:, (j - IN_SEGS) * D_MODEL:(j - IN_SEGS + 1) * D_MODEL]
        put(j, 1.0 / (1.0 + jnp.exp(-g)))


def _proj_lat_kernel(x_ref, mod_ref, win_ref, wg_ref, bg_ref, cos_ref, se_ref, so_ref, p_ref,
                     *, per_seq, mod_row0):
    shift, scale, _ = _mod_row(mod_ref, mod_row0 + pl.program_id(0) // per_seq)
    _project(_modulated_ln(x_ref, shift, scale), win_ref, wg_ref, bg_ref, p_ref,
             rope=(cos_ref, se_ref, so_ref))


def _proj_lat_call(x2d, mod, w_in, w_gate, b_gate, rope, *, seq, mod_row0):
    m = x2d.shape[0]
    tm = 256
    per_seq = seq // tm
    return pl.pallas_call(
        functools.partial(_proj_lat_kernel, per_seq=per_seq, mod_row0=mod_row0),
        grid=(m // tm,),
        in_specs=[
            pl.BlockSpec((tm, D_MODEL), lambda i: (i, 0)),
            pl.BlockSpec((MOD_ROWS, 3 * D_MODEL), lambda i: (0, 0)),
            _resident((D_MODEL, IN_SEGS * D_MODEL)),
            _resident((D_MODEL, 2 * D_MODEL)),
            pl.BlockSpec((1, 2 * D_MODEL), lambda i: (0, 0)),
        ] + [pl.BlockSpec((tm, LANES), lambda i: (i % per_seq, 0))] * 3,
        out_specs=pl.BlockSpec((tm, P_WIDTH), lambda i: (i, 0)),
        out_shape=jax.ShapeDtypeStruct((m, P_WIDTH), BF16),
        compiler_params=_params(("arbitrary",)),
        name="proj_lat",
    )(x2d, mod, w_in, w_gate, b_gate, *rope)


def _lam(lp_ref):
    lp = lp_ref[...]
    a = jnp.sum(lp[0:1] * lp[1:2], axis=-1, keepdims=True)
    b = jnp.sum(lp[2:3] * lp[3:4], axis=-1, keepdims=True)
    return jnp.exp(a) - jnp.exp(b) + LAM_INIT


def _attn_rows(q, k_all, v_ones, z, lam, g):
    lane = lax.broadcasted_iota(jnp.int32, (1, LANES), 1)
    zero = jnp.zeros_like(q)
    outs = []
    for qm in (jnp.where(lane < DK_A, q, zero), jnp.where(lane >= DK_A, q, zero)):
        s = _dot_nt(qm, k_all)
        mx = jnp.max(s, axis=-1, keepdims=True)
        p = jnp.exp2(s - mx).astype(BF16)
        ol = _dot(p, v_ones)
        outs.append(ol[:, :DV] * (1.0 / ol[:, DV:]))
    oa = outs[0] - lam * outs[1]
    oa = oa * lax.rsqrt(jnp.mean(oa * oa, axis=-1, keepdims=True) + LN_EPS)
    oa = oa * g * (1.0 - LAM_INIT)
    return oa * z.astype(F32)


def _attn_ctx_heads(q_ref, k_ref, v_ref, z_ref, lam, g, o_ref):
    ones = jnp.ones((q_ref.shape[0], DV), BF16)
    for h in range(N_HEADS):
        sl = slice(h * LANES, (h + 1) * LANES)
        v_ones = jnp.concatenate([v_ref[:, sl], ones], axis=1)
        o_ref[:, sl] = _attn_rows(q_ref[:, sl], k_ref[:, sl], v_ones, z_ref[:, sl],
                                  lam, g).astype(BF16)


def _attn_lat_kernel(q_ref, k_ref, v_ref, z_ref, kc_ref, vc_ref, lp_ref, g_ref, o_ref,
                     k_all, v_all, *, past, tc):
    @pl.when(pl.program_id(2) == 0)
    def _():
        head_rows = pl.ds(pl.program_id(1), past, stride=N_HEADS)
        k_all[0:past, :] = kc_ref[head_rows, :].astype(BF16)
        v_all[0:past, 0:DV] = vc_ref[head_rows, :].astype(BF16)
        k_all[past:, :] = k_ref[...]
        v_all[past:, 0:DV] = v_ref[...]
        v_all[:, DV:] = jnp.ones((v_all.shape[0], DV), BF16)

    lam = _lam(lp_ref)
    g = g_ref[...]
    for lo in range(0, q_ref.shape[0], tc):
        o_ref[lo:lo + tc, :] = _attn_rows(q_ref[lo:lo + tc, :], k_all[...], v_all[...],
                                          z_ref[lo:lo + tc, :], lam, g).astype(BF16)


def _attn_lat_call(p, cache_k, cache_v, lam_params, subln_g, *, batch, seq, past):
    tq, tc = 2048, 128
    nq = seq // tq
    return pl.pallas_call(
        functools.partial(_attn_lat_kernel, past=past, tc=tc),
        grid=(batch, N_HEADS, nq),
        in_specs=[
            pl.BlockSpec((tq, LANES), lambda b, h, i: (b * nq + i, QA_BLK + h)),
            pl.BlockSpec((seq, LANES), lambda b, h, i: (b, KA_BLK + h)),
            pl.BlockSpec((seq, LANES), lambda b, h, i: (b, VA_BLK + h)),
            pl.BlockSpec((tq, LANES), lambda b, h, i: (b * nq + i, ZA_BLK + h)),
            pl.BlockSpec((None, past * N_HEADS, LANES), lambda b, h, i: (b, 0, 0)),
            pl.BlockSpec((None, past * N_HEADS, LANES), lambda b, h, i: (b, 0, 0)),
            pl.BlockSpec((4, DK_A), lambda b, h, i: (0, 0)),
            pl.BlockSpec((1, DV), lambda b, h, i: (0, 0)),
        ],
        out_specs=pl.BlockSpec((tq, LANES), lambda b, h, i: (b * nq + i, h)),
        out_shape=jax.ShapeDtypeStruct((batch * seq, D_MODEL), BF16),
        scratch_shapes=[pltpu.VMEM((past + seq, LANES), BF16),
                        pltpu.VMEM((past + seq, 2 * DV), BF16)],
        compiler_params=_params(("arbitrary", "arbitrary", "arbitrary")),
        name="attn_lat",
    )(p, p, p, p, cache_k, cache_v, lam_params, subln_g)


PW, VW = 2 * DK_R, 2 * DV


def _ret_scratch(npairs, nc):
    stacked_state = (npairs, nc, 2 * PW, VW)
    return [pltpu.VMEM(stacked_state, F32),
            pltpu.VMEM(stacked_state, BF16),
            pltpu.VMEM((npairs, 2, CHUNK, CHUNK), F32),
            pltpu.VMEM((npairs, CHUNK, 2 * PW), F32),
            pltpu.VMEM((npairs, CHUNK, 2 * PW), F32)]


def _log_gammas(rd_ref, pair):
    def one(d, hh):
        r = jnp.full((1, VW), rd_ref[d, 2 * pair + hh], F32)
        return jnp.log1p(-jnp.exp2(r))
    return [[one(d, hh) for hh in range(2)] for d in range(2)]


def _ret_build_tables(rd_ref, dec_scr, qdec_scr, kdec_scr, *, npairs, pair0):
    lane_q = lax.broadcasted_iota(jnp.int32, (1, PW), 1)
    rel = (lax.broadcasted_iota(jnp.int32, (CHUNK, CHUNK), 0)
           - lax.broadcasted_iota(jnp.int32, (CHUNK, CHUNK), 1)).astype(F32)
    row_q = lax.broadcasted_iota(jnp.int32, (CHUNK, PW), 0).astype(F32)
    for pi in range(npairs):
        lg = _log_gammas(rd_ref, pair0 + pi)
        lg_q = [jnp.where(lane_q < DK_R, lg[d][0][:, :PW], lg[d][1][:, :PW])
                for d in range(2)]
        for hh in range(2):
            dec_scr[pi, hh] = (
                jnp.where(rel >= 0, jnp.exp(lg[0][hh][:, :1] * jnp.maximum(rel, 0.0)), 0.0)
                + jnp.where(rel <= 0, jnp.exp(lg[1][hh][:, :1] * jnp.maximum(-rel, 0.0)), 0.0))
        qdec_scr[pi] = jnp.concatenate([jnp.exp(lg_q[0] * (row_q + 1.0)),
                                        jnp.exp(lg_q[1] * (CHUNK - row_q))], axis=1)
        kdec_scr[pi] = jnp.concatenate([jnp.exp(lg_q[0] * (CHUNK - 1.0 - row_q)),
                                        jnp.exp(lg_q[1] * row_q)], axis=1)


def _ret_pairs(rd_ref, q_ref, k_ref, v_ref, z_ref, g_ref, s0_refs, o_ref, sout_refs,
               u_scr, s_scr, dec_scr, qdec_scr, kdec_scr, *, nc, npairs, pair0):
    cross = s0_refs is not None or nc > 1
    lane_q = lax.broadcasted_iota(jnp.int32, (1, PW), 1)
    lane_v = lax.broadcasted_iota(jnp.int32, (1, VW), 1)
    srow = lax.broadcasted_iota(jnp.int32, (2 * PW, VW), 0)
    scol = lax.broadcasted_iota(jnp.int32, (2 * PW, VW), 1)
    diag_blocks = ((srow % PW) // DK_R) == (scol // DV)

    def rows(c):
        return slice(c * CHUNK, (c + 1) * CHUNK)

    for pi in range(npairs):
        qs = slice(pi * PW, (pi + 1) * PW)
        vs = slice(pi * VW, (pi + 1) * VW)
        lg = _log_gammas(rd_ref, pair0 + pi)
        lg_v = [jnp.where(lane_v < DV, lg[d][0], lg[d][1]) for d in range(2)]
        chunk_decay = [jnp.exp(lg_v[d] * CHUNK) for d in range(2)]

        for c in range(nc):
            k = k_ref[rows(c), qs].astype(F32)
            kd = (jnp.concatenate([k, k], axis=1) * kdec_scr[pi]).astype(BF16)
            u_scr[pi, c] = jnp.where(diag_blocks, _dot_tn(kd, v_ref[rows(c), vs]), 0.0)

        for d, order in ((0, range(nc)), (1, reversed(range(nc)))):
            half = slice(d * PW, (d + 1) * PW)
            if s0_refs is None:
                s = jnp.zeros((PW, VW), F32)
            else:
                zero = jnp.zeros((DK_R, DV), F32)
                s = jnp.concatenate(
                    [jnp.concatenate([s0_refs[d][2 * pi], zero], axis=1),
                     jnp.concatenate([zero, s0_refs[d][2 * pi + 1]], axis=1)], axis=0)
            for c in order:
                if cross:
                    s_scr[pi, c, half, :] = s.astype(BF16)
                s = s * chunk_decay[d] + u_scr[pi, c, half, :]
            if sout_refs is not None:
                sout_refs[d][2 * pi] = s[:DK_R, :DV]
                sout_refs[d][2 * pi + 1] = s[DK_R:, DV:]

        g = g_ref[:, vs]
        for c in range(nc):
            q = q_ref[rows(c), qs]
            k = k_ref[rows(c), qs]
            v = v_ref[rows(c), vs]
            zero = jnp.zeros_like(q)
            q2 = jnp.concatenate([jnp.where(lane_q < DK_R, q, zero),
                                  jnp.where(lane_q >= DK_R, q, zero)], axis=0)
            a2 = _dot_nt(q2, k)
            o = jnp.concatenate(
                [_dot((a2[hh * CHUNK:(hh + 1) * CHUNK] * dec_scr[pi, hh]).astype(BF16),
                      v[:, hh * DV:(hh + 1) * DV]) for hh in range(2)], axis=1)
            if cross:
                qf = q.astype(F32)
                qq = (jnp.concatenate([qf, qf], axis=1) * qdec_scr[pi]).astype(BF16)
                o = o + _dot(qq, s_scr[pi, c])
            outs = []
            for hh in range(2):
                oh = o[:, hh * DV:(hh + 1) * DV]
                mu = jnp.mean(oh, axis=-1, keepdims=True)
                oc = oh - mu
                var = jnp.mean(oc * oc, axis=-1, keepdims=True)
                outs.append(oc * lax.rsqrt(var + LN_EPS))
            o = jnp.concatenate(outs, axis=1) * g * z_ref[rows(c), vs].astype(F32)
            o_ref[rows(c), vs] = o.astype(BF16)


def _ret_lat_kernel(rd_ref, q_ref, k_ref, v_ref, z_ref, g_ref, s0f_ref, s0b_ref, o_ref,
                    u_scr, s_scr, dec_scr, qdec_scr, kdec_scr, *, nc, npairs):
    pair0 = pl.program_id(0) * npairs

    @pl.when(pl.program_id(1) == 0)
    def _():
        _ret_build_tables(rd_ref, dec_scr, qdec_scr, kdec_scr, npairs=npairs, pair0=pair0)

    _ret_pairs(rd_ref, q_ref, k_ref, v_ref, z_ref, g_ref, (s0f_ref, s0b_ref), o_ref, None,
               u_scr, s_scr, dec_scr, qdec_scr, kdec_scr, nc=nc, npairs=npairs, pair0=pair0)


def _ret_lat_call(p, ret_decay, gn_g, s0f, s0b, *, batch, seq):
    nc = seq // CHUNK
    hps = 4
    npairs = hps // 2
    qw, vw = DK_R * hps, DV * hps
    col = lambda w, blk: pl.BlockSpec((seq, w), lambda j, b: (b, blk * LANES // w + j))
    state = pl.BlockSpec((None, hps, DK_R, DV), lambda j, b: (b, j, 0, 0))
    return pl.pallas_call(
        functools.partial(_ret_lat_kernel, nc=nc, npairs=npairs),
        grid=(N_HEADS // hps, batch),
        in_specs=[pl.BlockSpec(memory_space=pltpu.SMEM),
                  col(qw, QR_BLK), col(qw, KR_BLK), col(vw, VR_BLK), col(vw, ZR_BLK),
                  pl.BlockSpec((1, vw), lambda j, b: (0, j)), state, state],
        out_specs=pl.BlockSpec((seq, vw), lambda j, b: (b, j)),
        out_shape=jax.ShapeDtypeStruct((batch * seq, D_MODEL), BF16),
        scratch_shapes=_ret_scratch(npairs, nc),
        compiler_params=_params(("arbitrary", "arbitrary")),
        name="ret_lat",
    )(ret_decay, p, p, p, p, gn_g, s0f, s0b)


def _out_rows(oa_ref, or_ref, ga_ref, gr_ref, x_ref, gate, wpa_ref, wpr_ref, wo_ref,
              lng_ref, lnb_ref, y_ref):
    a = _dot(oa_ref[...], wpa_ref[...])
    r = _dot(or_ref[...], wpr_ref[...])
    m = ga_ref[...].astype(F32) * a + gr_ref[...].astype(F32) * r
    out = _dot(m.astype(BF16), wo_ref[...])
    t = DEEPNORM_ALPHA * x_ref[...] + gate * out
    mu = jnp.mean(t, axis=-1, keepdims=True)
    tc = t - mu
    var = jnp.mean(tc * tc, axis=-1, keepdims=True)
    y_ref[...] = tc * lax.rsqrt(var + LN_EPS) * lng_ref[...] + lnb_ref[...]


def _out_lat_kernel(oa_ref, or_ref, ga_ref, gr_ref, x_ref, mod_ref, wpa_ref, wpr_ref, wo_ref,
                    lng_ref, lnb_ref, y_ref, *, per_seq, mod_row0):
    _, _, gate = _mod_row(mod_ref, mod_row0 + pl.program_id(0) // per_seq)
    _out_rows(oa_ref, or_ref, ga_ref, gr_ref, x_ref, gate, wpa_ref, wpr_ref, wo_ref,
              lng_ref, lnb_ref, y_ref)


def _out_lat_call(oa, orr, p, x2d, mod, w_pa, w_pr, w_out, ln_g, ln_b, *, seq, mod_row0):
    m = x2d.shape[0]
    tm = 512
    per_seq = seq // tm
    tile = lambda j: pl.BlockSpec((tm, D_MODEL), lambda i: (i, j))
    vec = pl.BlockSpec((1, D_MODEL), lambda i: (0, 0))
    return pl.pallas_call(
        functools.partial(_out_lat_kernel, per_seq=per_seq, mod_row0=mod_row0),
        grid=(m // tm,),
        in_specs=[tile(0), tile(0), tile(GA_BLK * LANES // D_MODEL), tile(GR_BLK * LANES // D_MODEL),
                  tile(0),
                  pl.BlockSpec((MOD_ROWS, 3 * D_MODEL), lambda i: (0, 0)),
                  _resident((D_MODEL, D_MODEL)), _resident((D_MODEL, D_MODEL)),
                  _resident((D_MODEL, D_MODEL)), vec, vec],
        out_specs=tile(0),
        out_shape=jax.ShapeDtypeStruct((m, D_MODEL), F32),
        compiler_params=_params(("arbitrary",)),
        name="out_lat",
    )(oa, orr, p, p, x2d, mod, w_pa, w_pr, w_out, ln_g, ln_b)


W_CHUNK = D_MODEL


def _weight_chunks(w_hbm):
    return [(k, slice(j * W_CHUNK, (j + 1) * W_CHUNK))
            for k, w in enumerate(w_hbm) for j in range(w.shape[1] // W_CHUNK)]


def _weight_store(n, chunks, w_scr, w_bf_hbm, out_sem):
    k, cols = chunks[n]
    return pltpu.make_async_copy(w_scr[k].at[:, cols], w_bf_hbm[k].at[:, cols], out_sem.at[n])


def _stream_f32_chunks(jobs, stage, in_sem):
    def load(n):
        return pltpu.make_async_copy(jobs[n][0], stage.at[n % 2], in_sem.at[n % 2])

    load(0).start()
    for n, (_, sink) in enumerate(jobs):
        if n + 1 < len(jobs):
            load(n + 1).start()
        load(n).wait()
        sink(stage.at[n % 2])


def _ctx_kernel(x_ref, cctx_ref, c_ref, wmod_hbm, bmod_ref, win_hbm, wg_hbm, bg_ref, lp_ref, sg_ref,
                rd_ref, gg_ref, wpa_hbm, wpr_hbm, wo_hbm, lng_ref, lnb_ref,
                y_ref, k_ref, v_ref, sf_ref, sb_ref, mod_ref, win_bf, wg_bf, wpa_bf, wpr_bf, wo_bf,
                p_scr, oa_scr, or_scr, u_scr, s_scr, dec_scr, qdec_scr, kdec_scr,
                win_ref, wg_ref, wpa_ref, wpr_ref, wo_ref, cond_scr, stage, in_sem, out_sem,
                *, npairs):
    w_hbm = (win_hbm, wg_hbm, wpa_hbm, wpr_hbm, wo_hbm)
    w_scr = (win_ref, wg_ref, wpa_ref, wpr_ref, wo_ref)
    w_bf = (win_bf, wg_bf, wpa_bf, wpr_bf, wo_bf)

    @pl.when(pl.program_id(0) == 0)
    def _():
        cond_scr[...] = jnp.zeros(cond_scr.shape, F32)
        cond_scr[0:1, :] = cctx_ref[...]
        cond_scr[1:1 + c_ref.shape[0], :] = c_ref[...]
        cond = _silu(cond_scr[...]).astype(BF16)
        jobs = []
        for j in range(wmod_hbm.shape[1] // W_CHUNK):
            cols = slice(j * W_CHUNK, (j + 1) * W_CHUNK)

            def mod_sink(staged, cols=cols):
                mod_ref[:, cols] = _dot(cond, staged[...].astype(BF16)) + bmod_ref[:, cols]
            jobs.append((wmod_hbm.at[:, cols], mod_sink))
        chunks = _weight_chunks(w_hbm)
        for k, cols in chunks:
            def cast_sink(staged, k=k, cols=cols):
                w_scr[k][:, cols] = staged[...].astype(BF16)
            jobs.append((w_hbm[k].at[:, cols], cast_sink))
        _stream_f32_chunks(jobs, stage, in_sem)
        for n in range(len(chunks)):
            _weight_store(n, chunks, w_scr, w_bf, out_sem).start()
        _ret_build_tables(rd_ref, dec_scr, qdec_scr, kdec_scr, npairs=npairs, pair0=0)

    cols = lambda blk, w: p_scr.at[:, blk * LANES:blk * LANES + w]
    shift, scale, gate = _mod_row(mod_ref, 0)
    _project(_modulated_ln(x_ref, shift, scale), win_ref, wg_ref, bg_ref, p_scr,
             kv_out=(k_ref, v_ref))
    _attn_ctx_heads(cols(QA_BLK, D_MODEL), cols(KA_BLK, D_MODEL), cols(VA_BLK, D_MODEL),
                    cols(ZA_BLK, D_MODEL), _lam(lp_ref), sg_ref[...], oa_scr)
    _ret_pairs(rd_ref, cols(QR_BLK, D_MODEL // 2), cols(KR_BLK, D_MODEL // 2),
               cols(VR_BLK, D_MODEL), cols(ZR_BLK, D_MODEL), gg_ref, None, or_scr,
               (sf_ref, sb_ref), u_scr, s_scr, dec_scr, qdec_scr, kdec_scr,
               nc=x_ref.shape[0] // CHUNK, npairs=npairs, pair0=0)
    _out_rows(oa_scr, or_scr, cols(GA_BLK, D_MODEL), cols(GR_BLK, D_MODEL), x_ref, gate,
              wpa_ref, wpr_ref, wo_ref, lng_ref, lnb_ref, y_ref)

    @pl.when(pl.program_id(0) == pl.num_programs(0) - 1)
    def _():
        chunks = _weight_chunks(w_hbm)
        for n in range(len(chunks)):
            _weight_store(n, chunks, w_scr, w_bf, out_sem).wait()


def _ctx_call(x2d, c_ctx, c, w_mod, b_mod, w_in, w_gate, b_gate, lam_params, subln_g, ret_decay,
              gn_g, w_pa, w_pr, w_out, ln_g, ln_b, *, batch, seq):
    npairs = N_HEADS // 2
    nc = seq // CHUNK
    tile = pl.BlockSpec((seq, D_MODEL), lambda b: (b, 0))
    vec = lambda w: pl.BlockSpec((1, w), lambda b: (0, 0))
    state = pl.BlockSpec((None, N_HEADS, DK_R, DV), lambda b: (b, 0, 0, 0))
    heads = pl.BlockSpec((seq * N_HEADS, LANES), lambda b: (b, 0))
    tok = jax.ShapeDtypeStruct((batch * seq, D_MODEL), F32)
    tok_heads = jax.ShapeDtypeStruct((batch * seq * N_HEADS, LANES), F32)
    st = jax.ShapeDtypeStruct((batch, N_HEADS, DK_R, DV), F32)
    hbm = pl.BlockSpec(memory_space=pl.ANY)
    weights = (w_in, w_gate, w_pa, w_pr, w_out)
    n_chunks = sum(w.shape[1] // W_CHUNK for w in weights)
    return pl.pallas_call(
        functools.partial(_ctx_kernel, npairs=npairs),
        grid=(batch,),
        in_specs=[tile, vec(D_MODEL), pl.BlockSpec(c.shape, lambda b: (0, 0)), hbm,
                  vec(3 * D_MODEL), hbm, hbm, vec(2 * D_MODEL),
                  pl.BlockSpec((4, DK_A), lambda b: (0, 0)), vec(DV),
                  pl.BlockSpec(memory_space=pltpu.SMEM), vec(D_MODEL),
                  hbm, hbm, hbm, vec(D_MODEL), vec(D_MODEL)],
        out_specs=[tile, heads, heads, state, state,
                   pl.BlockSpec((MOD_ROWS, 3 * D_MODEL), lambda b: (0, 0))] + [hbm] * len(weights),
        out_shape=[tok, tok_heads, tok_heads, st, st, jax.ShapeDtypeStruct((MOD_ROWS, 3 * D_MODEL), F32)]
                  + [jax.ShapeDtypeStruct(w.shape, BF16) for w in weights],
        scratch_shapes=[pltpu.VMEM((seq, P_WIDTH), BF16),
                        pltpu.VMEM((seq, D_MODEL), BF16),
                        pltpu.VMEM((seq, D_MODEL), BF16)] + _ret_scratch(npairs, nc)
                       + [pltpu.VMEM(w.shape, BF16) for w in weights]
                       + [pltpu.VMEM((MOD_ROWS, D_MODEL), F32),
                          pltpu.VMEM((2, D_MODEL, W_CHUNK), F32),
                          pltpu.SemaphoreType.DMA((2,)),
                          pltpu.SemaphoreType.DMA((n_chunks,))],
        compiler_params=_params(("arbitrary",)),
        name="ctx",
    )(x2d, c_ctx, c, w_mod, b_mod, w_in, w_gate, b_gate, lam_params, subln_g, ret_decay, gn_g,
      w_pa, w_pr, w_out, ln_g, ln_b)


def _rope_tables(n_tokens):
    rows = n_tokens // GRID_W
    r = np.repeat(np.arange(rows, dtype=np.float32), GRID_W)
    col = np.tile(np.arange(GRID_W, dtype=np.float32), rows)
    n_freq = DK_A // 4
    inv = np.float32(ROPE_BASE) ** (-np.arange(n_freq, dtype=np.float32) / np.float32(n_freq))
    ang = np.concatenate([r[:, None] * inv, col[:, None] * inv], axis=-1).astype(np.float32)
    cos = np.repeat(np.cos(ang), 2, axis=-1)
    sin = np.repeat(np.sin(ang), 2, axis=-1)
    even = (np.arange(DK_A) % 2 == 0)[None, :]
    sin_even = np.where(even, -sin, 0.0)
    sin_odd = np.where(even, 0.0, sin)
    two = lambda t: jnp.asarray(np.concatenate([t, t], axis=-1), F32)
    return two(cos), two(sin_even), two(sin_odd)


def kernel(x_prompt, x_sample, cache_attn_k, cache_attn_v, state_ret_fwd, state_ret_bwd,
           c, c_ctx, w_mod, b_mod, w_in, lam_params, subln_g, ret_decay, ret_gn_g,
           w_pa, w_pr, w_gate, b_gate, w_out, ln_g, ln_b):
    batch, seq, _ = x_prompt.shape
    dbatch, dseq, _ = x_sample.shape
    past = cache_attn_k.shape[2]
    l = 0

    bg = b_gate[l][None, :]
    lp, sg = lam_params[l], subln_g[l][None, :]
    rd, gg = ret_decay[l], ret_gn_g[l][None, :]
    lng, lnb = ln_g[l][None, :], ln_b[l][None, :]

    xc = x_prompt.reshape(batch * seq, D_MODEL)
    y_c, k_c, v_c, sf_c, sb_c, mod, win, wg, wpa, wpr, wo = _ctx_call(
        xc, c_ctx[None, :], c, w_mod[l], b_mod[l][None, :], w_in[l], w_gate[l], bg, lp, sg, rd, gg,
        w_pa[l], w_pr[l], w_out[l], lng, lnb, batch=batch, seq=seq)

    xs = x_sample.reshape(dbatch * dseq, D_MODEL)
    p_s = _proj_lat_call(xs, mod, win, wg, bg, _rope_tables(dseq), seq=dseq, mod_row0=1)
    ck = cache_attn_k[:, l].reshape(dbatch, past * N_HEADS, 2 * DK_A)
    cv = cache_attn_v[:, l].reshape(dbatch, past * N_HEADS, DV)
    oa_s = _attn_lat_call(p_s, ck, cv, lp, sg, batch=dbatch, seq=dseq, past=past)
    or_s = _ret_lat_call(p_s, rd, gg, state_ret_fwd[:, l], state_ret_bwd[:, l],
                         batch=dbatch, seq=dseq)
    y_s = _out_lat_call(oa_s, or_s, p_s, xs, mod, wpa, wpr, wo, lng, lnb, seq=dseq, mod_row0=1)

    return (y_c.reshape(batch, seq, D_MODEL),
            y_s.reshape(dbatch, dseq, D_MODEL),
            k_c.reshape(batch, 1, seq, N_HEADS, 2 * DK_A),
            v_c.reshape(batch, 1, seq, N_HEADS, DV),
            sf_c.reshape(batch, 1, N_HEADS, DK_R, DV),
            sb_c.reshape(batch, 1, N_HEADS, DK_R, DV))
```

```python
import functools
import math

import jax
import jax.numpy as jnp
import numpy as np
from jax import lax
from jax.experimental import pallas as pl
from jax.experimental.pallas import tpu as pltpu

F32 = jnp.float32
BF16 = jnp.bfloat16

D_MODEL = 1024
N_HEADS = 8
DK_A = 64
DV = 128
DK_R = 64
CHUNK = 256
GRID_W = 64
ROPE_BASE = 10000.0
MOD_EPS = 1e-6
LN_EPS = 1e-5
DEPTH = 1
DEEPNORM_ALPHA = (2.0 * DEPTH) ** 0.25
LAM_INIT = 0.8 - 0.6 * math.exp(-0.3 * 0)
LOG2E = math.log2(math.e)

LANES = 128
MOD_ROWS = 8
QA_BLK, KA_BLK, VA_BLK, ZA_BLK = 0, 8, 16, 24
QR_BLK, KR_BLK, VR_BLK, ZR_BLK = 32, 36, 40, 48
GA_BLK, GR_BLK = 56, 64
IN_SEGS = 7
P_WIDTH = (IN_SEGS + 2) * D_MODEL
VMEM_LIMIT = 56 * 1024 * 1024


def _params(sem):
    return pltpu.CompilerParams(dimension_semantics=sem, vmem_limit_bytes=VMEM_LIMIT)


def _resident(shape):
    return pl.BlockSpec(shape, lambda *_: tuple(0 for _ in shape), pipeline_mode=pl.Buffered(1))


def _silu(z):
    return z * (1.0 / (1.0 + jnp.exp(-z)))


def _dot(a, b):
    return jnp.dot(a, b, preferred_element_type=F32)


def _dot_nt(a, b):
    return lax.dot_general(a, b, (((1,), (1,)), ((), ())), preferred_element_type=F32)


def _dot_tn(a, b):
    return lax.dot_general(a, b, (((0,), (0,)), ((), ())), preferred_element_type=F32)


def _rope(acc, cos, sin_even, sin_odd):
    outs = []
    for hh in range(N_HEADS):
        xs = acc[:, hh * LANES:(hh + 1) * LANES]
        nxt = pltpu.roll(xs, LANES - 1, 1)
        prv = pltpu.roll(xs, 1, 1)
        outs.append(xs * cos + nxt * sin_even + prv * sin_odd)
    return jnp.concatenate(outs, axis=1)


def _mod_row(mod_ref, row):
    r = pl.ds(row, 1)
    return (mod_ref[r, 0:D_MODEL], mod_ref[r, D_MODEL:2 * D_MODEL], mod_ref[r, 2 * D_MODEL:3 * D_MODEL])


def _modulated_ln(x_ref, shift, scale):
    x = x_ref[...]
    mu = jnp.mean(x, axis=-1, keepdims=True)
    xc = x - mu
    var = jnp.mean(xc * xc, axis=-1, keepdims=True)
    return (xc * lax.rsqrt(var + MOD_EPS) * (1.0 + scale) + shift).astype(BF16)


def _store_heads(ref, val):
    for h in range(N_HEADS):
        ref[pl.ds(h, val.shape[0], stride=N_HEADS), :] = val[:, h * LANES:(h + 1) * LANES]


def _project(h, win_ref, wg_ref, bg_ref, p_ref, rope=None, kv_out=None):
    def seg(j):
        if j < IN_SEGS:
            return _dot(h, win_ref[:, j * D_MODEL:(j + 1) * D_MODEL])
        return _dot(h, wg_ref[:, (j - IN_SEGS) * D_MODEL:(j - IN_SEGS + 1) * D_MODEL])

    def put(j, val):
        p_ref[:, j * D_MODEL:(j + 1) * D_MODEL] = val.astype(BF16)

    qa = seg(0)
    ka = seg(1)
    if rope is not None:
        cos, se, so = (t[...] for t in rope)
        qa = _rope(qa, cos, se, so)
        ka = _rope(ka, cos, se, so)
    if kv_out is not None:
        _store_heads(kv_out[0], ka)
    put(0, qa * (DK_A ** -0.5 * LOG2E))
    put(1, ka)
    va = seg(2)
    if kv_out is not None:
        _store_heads(kv_out[1], va)
    put(2, va)
    put(3, _silu(seg(3)))
    qk = seg(4)
    put(4, jnp.concatenate([qk[:, :D_MODEL // 2], qk[:, D_MODEL // 2:] * (DK_R ** -0.5)], axis=1))
    put(5, seg(5))
    put(6, _silu(seg(6)))
    for j in (IN_SEGS, IN_SEGS + 1):
        g = seg(j) + bg_ref[:, (j - IN_SEGS) * D_MODEL:(j - IN_SEGS + 1) * D_MODEL]
        put(j, 1.0 / (1.0 + jnp.exp(-g)))


def _proj_lat_kernel(x_ref, mod_ref, win_ref, wg_ref, bg_ref, cos_ref, se_ref, so_ref, p_ref,
                     *, per_seq, mod_row0):
    shift, scale, _ = _mod_row(mod_ref, mod_row0 + pl.program_id(0) // per_seq)
    _project(_modulated_ln(x_ref, shift, scale), win_ref, wg_ref, bg_ref, p_ref,
             rope=(cos_ref, se_ref, so_ref))


def _proj_lat_call(x2d, mod, w_in, w_gate, b_gate, rope, *, seq, mod_row0):
    m = x2d.shape[0]
    tm = 256
    per_seq = seq // tm
    return pl.pallas_call(
        functools.partial(_proj_lat_kernel, per_seq=per_seq, mod_row0=mod_row0),
        grid=(m // tm,),
        in_specs=[
            pl.BlockSpec((tm, D_MODEL), lambda i: (i, 0)),
            pl.BlockSpec((MOD_ROWS, 3 * D_MODEL), lambda i: (0, 0)),
            _resident((D_MODEL, IN_SEGS * D_MODEL)),
            _resident((D_MODEL, 2 * D_MODEL)),
            pl.BlockSpec((1, 2 * D_MODEL), lambda i: (0, 0)),
        ] + [pl.BlockSpec((tm, LANES), lambda i: (i % per_seq, 0))] * 3,
        out_specs=pl.BlockSpec((tm, P_WIDTH), lambda i: (i, 0)),
        out_shape=jax.ShapeDtypeStruct((m, P_WIDTH), BF16),
        compiler_params=_params(("arbitrary",)),
        name="proj_lat",
    )(x2d, mod, w_in, w_gate, b_gate, *rope)


def _lam(lp_ref):
    lp = lp_ref[...]
    a = jnp.sum(lp[0:1] * lp[1:2], axis=-1, keepdims=True)
    b = jnp.sum(lp[2:3] * lp[3:4], axis=-1, keepdims=True)
    return jnp.exp(a) - jnp.exp(b) + LAM_INIT


def _attn_rows(q, k_all, v_ones, z, lam, g):
    lane = lax.broadcasted_iota(jnp.int32, (1, LANES), 1)
    zero = jnp.zeros_like(q)
    outs = []
    for qm in (jnp.where(lane < DK_A, q, zero), jnp.where(lane >= DK_A, q, zero)):
        s = _dot_nt(qm, k_all)
        mx = jnp.max(s, axis=-1, keepdims=True)
        p = jnp.exp2(s - mx).astype(BF16)
        ol = _dot(p, v_ones)
        outs.append(ol[:, :DV] * (1.0 / ol[:, DV:]))
    oa = outs[0] - lam * outs[1]
    oa = oa * lax.rsqrt(jnp.mean(oa * oa, axis=-1, keepdims=True) + LN_EPS)
    oa = oa * g * (1.0 - LAM_INIT)
    return oa * z.astype(F32)


def _attn_ctx_heads(q_ref, k_ref, v_ref, z_ref, lam, g, o_ref):
    ones = jnp.ones((q_ref.shape[0], DV), BF16)
    for h in range(N_HEADS):
        sl = slice(h * LANES, (h + 1) * LANES)
        v_ones = jnp.concatenate([v_ref[:, sl], ones], axis=1)
        o_ref[:, sl] = _attn_rows(q_ref[:, sl], k_ref[:, sl], v_ones, z_ref[:, sl],
                                  lam, g).astype(BF16)


def _attn_lat_kernel(q_ref, k_ref, v_ref, z_ref, kc_ref, vc_ref, lp_ref, g_ref, o_ref,
                     k_all, v_all, *, past, tc):
    @pl.when(pl.program_id(2) == 0)
    def _():
        head_rows = pl.ds(pl.program_id(1), past, stride=N_HEADS)
        k_all[0:past, :] = kc_ref[head_rows, :].astype(BF16)
        v_all[0:past, 0:DV] = vc_ref[head_rows, :].astype(BF16)
        k_all[past:, :] = k_ref[...]
        v_all[past:, 0:DV] = v_ref[...]
        v_all[:, DV:] = jnp.ones((v_all.shape[0], DV), BF16)

    lam = _lam(lp_ref)
    g = g_ref[...]
    for lo in range(0, q_ref.shape[0], tc):
        o_ref[lo:lo + tc, :] = _attn_rows(q_ref[lo:lo + tc, :], k_all[...], v_all[...],
                                          z_ref[lo:lo + tc, :], lam, g).astype(BF16)


def _attn_lat_call(p, cache_k, cache_v, lam_params, subln_g, *, batch, seq, past):
    tq, tc = 2048, 128
    nq = seq // tq
    return pl.pallas_call(
        functools.partial(_attn_lat_kernel, past=past, tc=tc),
        grid=(batch, N_HEADS, nq),
        in_specs=[
            pl.BlockSpec((tq, LANES), lambda b, h, i: (b * nq + i, QA_BLK + h)),
            pl.BlockSpec((seq, LANES), lambda b, h, i: (b, KA_BLK + h)),
            pl.BlockSpec((seq, LANES), lambda b, h, i: (b, VA_BLK + h)),
            pl.BlockSpec((tq, LANES), lambda b, h, i: (b * nq + i, ZA_BLK + h)),
            pl.BlockSpec((None, past * N_HEADS, LANES), lambda b, h, i: (b, 0, 0)),
            pl.BlockSpec((None, past * N_HEADS, LANES), lambda b, h, i: (b, 0, 0)),
            pl.BlockSpec((4, DK_A), lambda b, h, i: (0, 0)),
            pl.BlockSpec((1, DV), lambda b, h, i: (0, 0)),
        ],
        out_specs=pl.BlockSpec((tq, LANES), lambda b, h, i: (b * nq + i, h)),
        out_shape=jax.ShapeDtypeStruct((batch * seq, D_MODEL), BF16),
        scratch_shapes=[pltpu.VMEM((past + seq, LANES), BF16),
                        pltpu.VMEM((past + seq, 2 * DV), BF16)],
        compiler_params=_params(("arbitrary", "arbitrary", "arbitrary")),
        name="attn_lat",
    )(p, p, p, p, cache_k, cache_v, lam_params, subln_g)


PW, VW = 2 * DK_R, 2 * DV


def _ret_scratch(npairs, nc):
    stacked_state = (npairs, nc, 2 * PW, VW)
    return [pltpu.VMEM(stacked_state, F32),
            pltpu.VMEM(stacked_state, BF16),
            pltpu.VMEM((npairs, 2, CHUNK, CHUNK), F32),
            pltpu.VMEM((npairs, CHUNK, 2 * PW), F32),
            pltpu.VMEM((npairs, CHUNK, 2 * PW), F32)]


def _log_gammas(rd_ref, pair):
    def one(d, hh):
        r = jnp.full((1, VW), rd_ref[d, 2 * pair + hh], F32)
        return jnp.log1p(-jnp.exp2(r))
    return [[one(d, hh) for hh in range(2)] for d in range(2)]


def _ret_build_tables(rd_ref, dec_scr, qdec_scr, kdec_scr, *, npairs, pair0):
    lane_q = lax.broadcasted_iota(jnp.int32, (1, PW), 1)
    rel = (lax.broadcasted_iota(jnp.int32, (CHUNK, CHUNK), 0)
           - lax.broadcasted_iota(jnp.int32, (CHUNK, CHUNK), 1)).astype(F32)
    row_q = lax.broadcasted_iota(jnp.int32, (CHUNK, PW), 0).astype(F32)
    for pi in range(npairs):
        lg = _log_gammas(rd_ref, pair0 + pi)
        lg_q = [jnp.where(lane_q < DK_R, lg[d][0][:, :PW], lg[d][1][:, :PW])
                for d in range(2)]
        for hh in range(2):
            dec_scr[pi, hh] = (
                jnp.where(rel >= 0, jnp.exp(lg[0][hh][:, :1] * jnp.maximum(rel, 0.0)), 0.0)
                + jnp.where(rel <= 0, jnp.exp(lg[1][hh][:, :1] * jnp.maximum(-rel, 0.0)), 0.0))
        qdec_scr[pi] = jnp.concatenate([jnp.exp(lg_q[0] * (row_q + 1.0)),
                                        jnp.exp(lg_q[1] * (CHUNK - row_q))], axis=1)
        kdec_scr[pi] = jnp.concatenate([jnp.exp(lg_q[0] * (CHUNK - 1.0 - row_q)),
                                        jnp.exp(lg_q[1] * row_q)], axis=1)


def _ret_pairs(rd_ref, q_ref, k_ref, v_ref, z_ref, g_ref, s0_refs, o_ref, sout_refs,
               u_scr, s_scr, dec_scr, qdec_scr, kdec_scr, *, nc, npairs, pair0):
    cross = s0_refs is not None or nc > 1
    lane_q = lax.broadcasted_iota(jnp.int32, (1, PW), 1)
    lane_v = lax.broadcasted_iota(jnp.int32, (1, VW), 1)
    srow = lax.broadcasted_iota(jnp.int32, (2 * PW, VW), 0)
    scol = lax.broadcasted_iota(jnp.int32, (2 * PW, VW), 1)
    diag_blocks = ((srow % PW) // DK_R) == (scol // DV)

    def rows(c):
        return slice(c * CHUNK, (c + 1) * CHUNK)

    for pi in range(npairs):
        qs = slice(pi * PW, (pi + 1) * PW)
        vs = slice(pi * VW, (pi + 1) * VW)
        lg = _log_gammas(rd_ref, pair0 + pi)
        lg_v = [jnp.where(lane_v < DV, lg[d][0], lg[d][1]) for d in range(2)]
        chunk_decay = [jnp.exp(lg_v[d] * CHUNK) for d in range(2)]

        for c in range(nc):
            k = k_ref[rows(c), qs].astype(F32)
            kd = (jnp.concatenate([k, k], axis=1) * kdec_scr[pi]).astype(BF16)
            u_scr[pi, c] = jnp.where(diag_blocks, _dot_tn(kd, v_ref[rows(c), vs]), 0.0)

        for d, order in ((0, range(nc)), (1, reversed(range(nc)))):
            half = slice(d * PW, (d + 1) * PW)
            if s0_refs is None:
                s = jnp.zeros((PW, VW), F32)
            else:
                zero = jnp.zeros((DK_R, DV), F32)
                s = jnp.concatenate(
                    [jnp.concatenate([s0_refs[d][2 * pi], zero], axis=1),
                     jnp.concatenate([zero, s0_refs[d][2 * pi + 1]], axis=1)], axis=0)
            for c in order:
                if cross:
                    s_scr[pi, c, half, :] = s.astype(BF16)
                s = s * chunk_decay[d] + u_scr[pi, c, half, :]
            if sout_refs is not None:
                sout_refs[d][2 * pi] = s[:DK_R, :DV]
                sout_refs[d][2 * pi + 1] = s[DK_R:, DV:]

        g = g_ref[:, vs]
        for c in range(nc):
            q = q_ref[rows(c), qs]
            k = k_ref[rows(c), qs]
            v = v_ref[rows(c), vs]
            zero = jnp.zeros_like(q)
            q2 = jnp.concatenate([jnp.where(lane_q < DK_R, q, zero),
                                  jnp.where(lane_q >= DK_R, q, zero)], axis=0)
            a2 = _dot_nt(q2, k)
            o = jnp.concatenate(
                [_dot((a2[hh * CHUNK:(hh + 1) * CHUNK] * dec_scr[pi, hh]).astype(BF16),
                      v[:, hh * DV:(hh + 1) * DV]) for hh in range(2)], axis=1)
            if cross:
                qf = q.astype(F32)
                qq = (jnp.concatenate([qf, qf], axis=1) * qdec_scr[pi]).astype(BF16)
                o = o + _dot(qq, s_scr[pi, c])
            outs = []
            for hh in range(2):
                oh = o[:, hh * DV:(hh + 1) * DV]
                mu = jnp.mean(oh, axis=-1, keepdims=True)
                oc = oh - mu
                var = jnp.mean(oc * oc, axis=-1, keepdims=True)
                outs.append(oc * lax.rsqrt(var + LN_EPS))
            o = jnp.concatenate(outs, axis=1) * g * z_ref[rows(c), vs].astype(F32)
            o_ref[rows(c), vs] = o.astype(BF16)


def _ret_lat_kernel(rd_ref, q_ref, k_ref, v_ref, z_ref, g_ref, s0f_ref, s0b_ref, o_ref,
                    u_scr, s_scr, dec_scr, qdec_scr, kdec_scr, *, nc, npairs):
    pair0 = pl.program_id(0) * npairs

    @pl.when(pl.program_id(1) == 0)
    def _():
        _ret_build_tables(rd_ref, dec_scr, qdec_scr, kdec_scr, npairs=npairs, pair0=pair0)

    _ret_pairs(rd_ref, q_ref, k_ref, v_ref, z_ref, g_ref, (s0f_ref, s0b_ref), o_ref, None,
               u_scr, s_scr, dec_scr, qdec_scr, kdec_scr, nc=nc, npairs=npairs, pair0=pair0)


def _ret_lat_call(p, ret_decay, gn_g, s0f, s0b, *, batch, seq):
    nc = seq // CHUNK
    hps = 8
    npairs = hps // 2
    qw, vw = DK_R * hps, DV * hps
    col = lambda w, blk: pl.BlockSpec((seq, w), lambda j, b: (b, blk * LANES // w + j))
    state = pl.BlockSpec((None, hps, DK_R, DV), lambda j, b: (b, j, 0, 0))
    return pl.pallas_call(
        functools.partial(_ret_lat_kernel, nc=nc, npairs=npairs),
        grid=(N_HEADS // hps, batch),
        in_specs=[pl.BlockSpec(memory_space=pltpu.SMEM),
                  col(qw, QR_BLK), col(qw, KR_BLK), col(vw, VR_BLK), col(vw, ZR_BLK),
                  pl.BlockSpec((1, vw), lambda j, b: (0, j)), state, state],
        out_specs=pl.BlockSpec((seq, vw), lambda j, b: (b, j)),
        out_shape=jax.ShapeDtypeStruct((batch * seq, D_MODEL), BF16),
        scratch_shapes=_ret_scratch(npairs, nc),
        compiler_params=_params(("arbitrary", "arbitrary")),
        name="ret_lat",
    )(ret_decay, p, p, p, p, gn_g, s0f, s0b)


def _out_rows(oa_ref, or_ref, ga_ref, gr_ref, x_ref, gate, wpa_ref, wpr_ref, wo_ref,
              lng_ref, lnb_ref, y_ref):
    a = _dot(oa_ref[...], wpa_ref[...])
    r = _dot(or_ref[...], wpr_ref[...])
    m = ga_ref[...].astype(F32) * a + gr_ref[...].astype(F32) * r
    out = _dot(m.astype(BF16), wo_ref[...])
    t = DEEPNORM_ALPHA * x_ref[...] + gate * out
    mu = jnp.mean(t, axis=-1, keepdims=True)
    tc = t - mu
    var = jnp.mean(tc * tc, axis=-1, keepdims=True)
    y_ref[...] = tc * lax.rsqrt(var + LN_EPS) * lng_ref[...] + lnb_ref[...]


def _out_lat_kernel(oa_ref, or_ref, ga_ref, gr_ref, x_ref, mod_ref, wpa_ref, wpr_ref, wo_ref,
                    lng_ref, lnb_ref, y_ref, *, per_seq, mod_row0):
    _, _, gate = _mod_row(mod_ref, mod_row0 + pl.program_id(0) // per_seq)
    _out_rows(oa_ref, or_ref, ga_ref, gr_ref, x_ref, gate, wpa_ref, wpr_ref, wo_ref,
              lng_ref, lnb_ref, y_ref)


def _out_lat_call(oa, orr, p, x2d, mod, w_pa, w_pr, w_out, ln_g, ln_b, *, seq, mod_row0):
    m = x2d.shape[0]
    tm = 512
    per_seq = seq // tm
    tile = lambda j: pl.BlockSpec((tm, D_MODEL), lambda i: (i, j))
    vec = pl.BlockSpec((1, D_MODEL), lambda i: (0, 0))
    return pl.pallas_call(
        functools.partial(_out_lat_kernel, per_seq=per_seq, mod_row0=mod_row0),
        grid=(m // tm,),
        in_specs=[tile(0), tile(0), tile(GA_BLK * LANES // D_MODEL), tile(GR_BLK * LANES // D_MODEL),
                  tile(0),
                  pl.BlockSpec((MOD_ROWS, 3 * D_MODEL), lambda i: (0, 0)),
                  _resident((D_MODEL, D_MODEL)), _resident((D_MODEL, D_MODEL)),
                  _resident((D_MODEL, D_MODEL)), vec, vec],
        out_specs=tile(0),
        out_shape=jax.ShapeDtypeStruct((m, D_MODEL), F32),
        compiler_params=_params(("arbitrary",)),
        name="out_lat",
    )(oa, orr, p, p, x2d, mod, w_pa, w_pr, w_out, ln_g, ln_b)


W_CHUNK = D_MODEL


def _weight_chunks(w_hbm):
    return [(k, slice(j * W_CHUNK, (j + 1) * W_CHUNK))
            for k, w in enumerate(w_hbm) for j in range(w.shape[1] // W_CHUNK)]


def _weight_store(n, chunks, w_scr, w_bf_hbm, out_sem):
    k, cols = chunks[n]
    return pltpu.make_async_copy(w_scr[k].at[:, cols], w_bf_hbm[k].at[:, cols], out_sem.at[n])


def _stream_f32_chunks(jobs, stage, in_sem):
    def load(n):
        return pltpu.make_async_copy(jobs[n][0], stage.at[n % 2], in_sem.at[n % 2])

    load(0).start()
    for n, (_, sink) in enumerate(jobs):
        if n + 1 < len(jobs):
            load(n + 1).start()
        load(n).wait()
        sink(stage.at[n % 2])


def _ctx_kernel(x_ref, cctx_ref, c_ref, wmod_hbm, bmod_ref, win_hbm, wg_hbm, bg_ref, lp_ref, sg_ref,
                rd_ref, gg_ref, wpa_hbm, wpr_hbm, wo_hbm, lng_ref, lnb_ref,
                y_ref, k_ref, v_ref, sf_ref, sb_ref, mod_ref, win_bf, wg_bf, wpa_bf, wpr_bf, wo_bf,
                p_scr, oa_scr, or_scr, u_scr, s_scr, dec_scr, qdec_scr, kdec_scr,
                win_ref, wg_ref, wpa_ref, wpr_ref, wo_ref, cond_scr, stage, in_sem, out_sem,
                *, npairs):
    w_hbm = (win_hbm, wg_hbm, wpa_hbm, wpr_hbm, wo_hbm)
    w_scr = (win_ref, wg_ref, wpa_ref, wpr_ref, wo_ref)
    w_bf = (win_bf, wg_bf, wpa_bf, wpr_bf, wo_bf)

    @pl.when(pl.program_id(0) == 0)
    def _():
        cond_scr[...] = jnp.zeros(cond_scr.shape, F32)
        cond_scr[0:1, :] = cctx_ref[...]
        cond_scr[1:1 + c_ref.shape[0], :] = c_ref[...]
        cond = _silu(cond_scr[...]).astype(BF16)
        jobs = []
        for j in range(wmod_hbm.shape[1] // W_CHUNK):
            cols = slice(j * W_CHUNK, (j + 1) * W_CHUNK)

            def mod_sink(staged, cols=cols):
                mod_ref[:, cols] = _dot(cond, staged[...].astype(BF16)) + bmod_ref[:, cols]
            jobs.append((wmod_hbm.at[:, cols], mod_sink))
        chunks = _weight_chunks(w_hbm)
        for k, cols in chunks:
            def cast_sink(staged, k=k, cols=cols):
                w_scr[k][:, cols] = staged[...].astype(BF16)
            jobs.append((w_hbm[k].at[:, cols], cast_sink))
        _stream_f32_chunks(jobs, stage, in_sem)
        for n in range(len(chunks)):
            _weight_store(n, chunks, w_scr, w_bf, out_sem).start()
        _ret_build_tables(rd_ref, dec_scr, qdec_scr, kdec_scr, npairs=npairs, pair0=0)

    cols = lambda blk, w: p_scr.at[:, blk * LANES:blk * LANES + w]
    shift, scale, gate = _mod_row(mod_ref, 0)
    _project(_modulated_ln(x_ref, shift, scale), win_ref, wg_ref, bg_ref, p_scr,
             kv_out=(k_ref, v_ref))
    _attn_ctx_heads(cols(QA_BLK, D_MODEL), cols(KA_BLK, D_MODEL), cols(VA_BLK, D_MODEL),
                    cols(ZA_BLK, D_MODEL), _lam(lp_ref), sg_ref[...], oa_scr)
    _ret_pairs(rd_ref, cols(QR_BLK, D_MODEL // 2), cols(KR_BLK, D_MODEL // 2),
               cols(VR_BLK, D_MODEL), cols(ZR_BLK, D_MODEL), gg_ref, None, or_scr,
               (sf_ref, sb_ref), u_scr, s_scr, dec_scr, qdec_scr, kdec_scr,
               nc=x_ref.shape[0] // CHUNK, npairs=npairs, pair0=0)
    _out_rows(oa_scr, or_scr, cols(GA_BLK, D_MODEL), cols(GR_BLK, D_MODEL), x_ref, gate,
              wpa_ref, wpr_ref, wo_ref, lng_ref, lnb_ref, y_ref)

    @pl.when(pl.program_id(0) == pl.num_programs(0) - 1)
    def _():
        chunks = _weight_chunks(w_hbm)
        for n in range(len(chunks)):
            _weight_store(n, chunks, w_scr, w_bf, out_sem).wait()


def _ctx_call(x2d, c_ctx, c, w_mod, b_mod, w_in, w_gate, b_gate, lam_params, subln_g, ret_decay,
              gn_g, w_pa, w_pr, w_out, ln_g, ln_b, *, batch, seq):
    npairs = N_HEADS // 2
    nc = seq // CHUNK
    tile = pl.BlockSpec((seq, D_MODEL), lambda b: (b, 0))
    vec = lambda w: pl.BlockSpec((1, w), lambda b: (0, 0))
    state = pl.BlockSpec((None, N_HEADS, DK_R, DV), lambda b: (b, 0, 0, 0))
    heads = pl.BlockSpec((seq * N_HEADS, LANES), lambda b: (b, 0))
    tok = jax.ShapeDtypeStruct((batch * seq, D_MODEL), F32)
    tok_heads = jax.ShapeDtypeStruct((batch * seq * N_HEADS, LANES), F32)
    st = jax.ShapeDtypeStruct((batch, N_HEADS, DK_R, DV), F32)
    hbm = pl.BlockSpec(memory_space=pl.ANY)
    weights = (w_in, w_gate, w_pa, w_pr, w_out)
    n_chunks = sum(w.shape[1] // W_CHUNK for w in weights)
    return pl.pallas_call(
        functools.partial(_ctx_kernel, npairs=npairs),
        grid=(batch,),
        in_specs=[tile, vec(D_MODEL), pl.BlockSpec(c.shape, lambda b: (0, 0)), hbm,
                  vec(3 * D_MODEL), hbm, hbm, vec(2 * D_MODEL),
                  pl.BlockSpec((4, DK_A), lambda b: (0, 0)), vec(DV),
                  pl.BlockSpec(memory_space=pltpu.SMEM), vec(D_MODEL),
                  hbm, hbm, hbm, vec(D_MODEL), vec(D_MODEL)],
        out_specs=[tile, heads, heads, state, state,
                   pl.BlockSpec((MOD_ROWS, 3 * D_MODEL), lambda b: (0, 0))] + [hbm] * len(weights),
        out_shape=[tok, tok_heads, tok_heads, st, st, jax.ShapeDtypeStruct((MOD_ROWS, 3 * D_MODEL), F32)]
                  + [jax.ShapeDtypeStruct(w.shape, BF16) for w in weights],
        scratch_shapes=[pltpu.VMEM((seq, P_WIDTH), BF16),
                        pltpu.VMEM((seq, D_MODEL), BF16),
                        pltpu.VMEM((seq, D_MODEL), BF16)] + _ret_scratch(npairs, nc)
                       + [pltpu.VMEM(w.shape, BF16) for w in weights]
                       + [pltpu.VMEM((MOD_ROWS, D_MODEL), F32),
                          pltpu.VMEM((2, D_MODEL, W_CHUNK), F32),
                          pltpu.SemaphoreType.DMA((2,)),
                          pltpu.SemaphoreType.DMA((n_chunks,))],
        compiler_params=_params(("arbitrary",)),
        name="ctx",
    )(x2d, c_ctx, c, w_mod, b_mod, w_in, w_gate, b_gate, lam_params, subln_g, ret_decay, gn_g,
      w_pa, w_pr, w_out, ln_g, ln_b)


def _rope_tables(n_tokens):
    rows = n_tokens // GRID_W
    r = np.repeat(np.arange(rows, dtype=np.float32), GRID_W)
    col = np.tile(np.arange(GRID_W, dtype=np.float32), rows)
    n_freq = DK_A // 4
    inv = np.float32(ROPE_BASE) ** (-np.arange(n_freq, dtype=np.float32) / np.float32(n_freq))
    ang = np.concatenate([r[:, None] * inv, col[:, None] * inv], axis=-1).astype(np.float32)
    cos = np.repeat(np.cos(ang), 2, axis=-1)
    sin = np.repeat(np.sin(ang), 2, axis=-1)
    even = (np.arange(DK_A) % 2 == 0)[None, :]
    sin_even = np.where(even, -sin, 0.0)
    sin_odd = np.where(even, 0.0, sin)
    two = lambda t: jnp.asarray(np.concatenate([t, t], axis=-1), F32)
    return two(cos), two(sin_even), two(sin_odd)


def kernel(x_prompt, x_sample, cache_attn_k, cache_attn_v, state_ret_fwd, state_ret_bwd,
           c, c_ctx, w_mod, b_mod, w_in, lam_params, subln_g, ret_decay, ret_gn_g,
           w_pa, w_pr, w_gate, b_gate, w_out, ln_g, ln_b):
    batch, seq, _ = x_prompt.shape
    dbatch, dseq, _ = x_sample.shape
    past = cache_attn_k.shape[2]
    l = 0

    bg = b_gate[l][None, :]
    lp, sg = lam_params[l], subln_g[l][None, :]
    rd, gg = ret_decay[l], ret_gn_g[l][None, :]
    lng, lnb = ln_g[l][None, :], ln_b[l][None, :]

    xc = x_prompt.reshape(batch * seq, D_MODEL)
    y_c, k_c, v_c, sf_c, sb_c, mod, win, wg, wpa, wpr, wo = _ctx_call(
        xc, c_ctx[None, :], c, w_mod[l], b_mod[l][None, :], w_in[l], w_gate[l], bg, lp, sg, rd, gg,
        w_pa[l], w_pr[l], w_out[l], lng, lnb, batch=batch, seq=seq)

    xs = x_sample.reshape(dbatch * dseq, D_MODEL)
    p_s = _proj_lat_call(xs, mod, win, wg, bg, _rope_tables(dseq), seq=dseq, mod_row0=1)
    ck = cache_attn_k[:, l].reshape(dbatch, past * N_HEADS, 2 * DK_A)
    cv = cache_attn_v[:, l].reshape(dbatch, past * N_HEADS, DV)
    oa_s = _attn_lat_call(p_s, ck, cv, lp, sg, batch=dbatch, seq=dseq, past=past)
    or_s = _ret_lat_call(p_s, rd, gg, state_ret_fwd[:, l], state_ret_bwd[:, l],
                         batch=dbatch, seq=dseq)
    y_s = _out_lat_call(oa_s, or_s, p_s, xs, mod, wpa, wpr, wo, lng, lnb, seq=dseq, mod_row0=1)

    return (y_c.reshape(batch, seq, D_MODEL),
            y_s.reshape(dbatch, dseq, D_MODEL),
            k_c.reshape(batch, 1, seq, N_HEADS, 2 * DK_A),
            v_c.reshape(batch, 1, seq, N_HEADS, DV),
            sf_c.reshape(batch, 1, N_HEADS, DK_R, DV),
            sb_c.reshape(batch, 1, N_HEADS, DK_R, DV))
```

```python
import functools
import math

import jax
import jax.numpy as jnp
import numpy as np
from jax import lax
from jax.experimental import pallas as pl
from jax.experimental.pallas import tpu as pltpu

F32 = jnp.float32
BF16 = jnp.bfloat16

D_MODEL = 1024
N_HEADS = 8
DK_A = 64
DV = 128
DK_R = 64
CHUNK = 256
GRID_W = 64
ROPE_BASE = 10000.0
MOD_EPS = 1e-6
LN_EPS = 1e-5
DEPTH = 1
DEEPNORM_ALPHA = (2.0 * DEPTH) ** 0.25
LAM_INIT = 0.8 - 0.6 * math.exp(-0.3 * 0)
LOG2E = math.log2(math.e)

LANES = 128
MOD_ROWS = 8
QA_BLK, KA_BLK, VA_BLK, ZA_BLK = 0, 8, 16, 24
QR_BLK, KR_BLK, VR_BLK, ZR_BLK = 32, 36, 40, 48
GA_BLK, GR_BLK = 56, 64
IN_SEGS = 7
P_WIDTH = (IN_SEGS + 2) * D_MODEL
VMEM_LIMIT = 56 * 1024 * 1024


def _params(sem):
    return pltpu.CompilerParams(dimension_semantics=sem, vmem_limit_bytes=VMEM_LIMIT)


def _resident(shape):
    return pl.BlockSpec(shape, lambda *_: tuple(0 for _ in shape), pipeline_mode=pl.Buffered(1))


def _silu(z):
    return z * (1.0 / (1.0 + jnp.exp(-z)))


def _dot(a, b):
    return jnp.dot(a, b, preferred_element_type=F32)


def _dot_nt(a, b):
    return lax.dot_general(a, b, (((1,), (1,)), ((), ())), preferred_element_type=F32)


def _dot_tn(a, b):
    return lax.dot_general(a, b, (((0,), (0,)), ((), ())), preferred_element_type=F32)


def _rope(acc, cos, sin_even, sin_odd):
    outs = []
    for hh in range(N_HEADS):
        xs = acc[:, hh * LANES:(hh + 1) * LANES]
        nxt = pltpu.roll(xs, LANES - 1, 1)
        prv = pltpu.roll(xs, 1, 1)
        outs.append(xs * cos + nxt * sin_even + prv * sin_odd)
    return jnp.concatenate(outs, axis=1)


def _mod_row(mod_ref, row):
    r = pl.ds(row, 1)
    return (mod_ref[r, 0:D_MODEL], mod_ref[r, D_MODEL:2 * D_MODEL], mod_ref[r, 2 * D_MODEL:3 * D_MODEL])


def _modulated_ln(x_ref, shift, scale):
    x = x_ref[...]
    mu = jnp.mean(x, axis=-1, keepdims=True)
    xc = x - mu
    var = jnp.mean(xc * xc, axis=-1, keepdims=True)
    return (xc * lax.rsqrt(var + MOD_EPS) * (1.0 + scale) + shift).astype(BF16)


def _store_heads(ref, val):
    for h in range(N_HEADS):
        ref[pl.ds(h, val.shape[0], stride=N_HEADS), :] = val[:, h * LANES:(h + 1) * LANES]


def _project(h, win_ref, wg_ref, bg_ref, p_ref, rope=None, kv_out=None):
    def seg(j):
        if j < IN_SEGS:
            return _dot(h, win_ref[:, j * D_MODEL:(j + 1) * D_MODEL])
        return _dot(h, wg_ref[:, (j - IN_SEGS) * D_MODEL:(j - IN_SEGS + 1) * D_MODEL])

    def put(j, val):
        p_ref[:, j * D_MODEL:(j + 1) * D_MODEL] = val.astype(BF16)

    qa = seg(0)
    ka = seg(1)
    if rope is not None:
        cos, se, so = (t[...] for t in rope)
        qa = _rope(qa, cos, se, so)
        ka = _rope(ka, cos, se, so)
    if kv_out is not None:
        _store_heads(kv_out[0], ka)
    put(0, qa * (DK_A ** -0.5 * LOG2E))
    put(1, ka)
    va = seg(2)
    if kv_out is not None:
        _store_heads(kv_out[1], va)
    put(2, va)
    put(3, _silu(seg(3)))
    qk = seg(4)
    put(4, jnp.concatenate([qk[:, :D_MODEL // 2], qk[:, D_MODEL // 2:] * (DK_R ** -0.5)], axis=1))
    put(5, seg(5))
    put(6, _silu(seg(6)))
    for j in (IN_SEGS, IN_SEGS + 1):
        g = seg(j) + bg_ref[:, (j - IN_SEGS) * D_MODEL:(j - IN_SEGS + 1) * D_MODEL]
        put(j, 1.0 / (1.0 + jnp.exp(-g)))


def _proj_lat_kernel(x_ref, mod_ref, win_ref, wg_ref, bg_ref, cos_ref, se_ref, so_ref, p_ref,
                     *, per_seq, mod_row0):
    shift, scale, _ = _mod_row(mod_ref, mod_row0 + pl.program_id(0) // per_seq)
    _project(_modulated_ln(x_ref, shift, scale), win_ref, wg_ref, bg_ref, p_ref,
             rope=(cos_ref, se_ref, so_ref))


def _proj_lat_call(x2d, mod, w_in, w_gate, b_gate, rope, *, seq, mod_row0):
    m = x2d.shape[0]
    tm = 256
    per_seq = seq // tm
    return pl.pallas_call(
        functools.partial(_proj_lat_kernel, per_seq=per_seq, mod_row0=mod_row0),
        grid=(m // tm,),
        in_specs=[
            pl.BlockSpec((tm, D_MODEL), lambda i: (i, 0)),
            pl.BlockSpec((MOD_ROWS, 3 * D_MODEL), lambda i: (0, 0)),
            _resident((D_MODEL, IN_SEGS * D_MODEL)),
            _resident((D_MODEL, 2 * D_MODEL)),
            pl.BlockSpec((1, 2 * D_MODEL), lambda i: (0, 0)),
        ] + [pl.BlockSpec((tm, LANES), lambda i: (i % per_seq, 0))] * 3,
        out_specs=pl.BlockSpec((tm, P_WIDTH), lambda i: (i, 0)),
        out_shape=jax.ShapeDtypeStruct((m, P_WIDTH), BF16),
        compiler_params=_params(("arbitrary",)),
        name="proj_lat",
    )(x2d, mod, w_in, w_gate, b_gate, *rope)


def _lam(lp_ref):
    lp = lp_ref[...]
    a = jnp.sum(lp[0:1] * lp[1:2], axis=-1, keepdims=True)
    b = jnp.sum(lp[2:3] * lp[3:4], axis=-1, keepdims=True)
    return jnp.exp(a) - jnp.exp(b) + LAM_INIT


def _attn_rows(q, k_all, v_ones, z, lam, g):
    lane = lax.broadcasted_iota(jnp.int32, (1, LANES), 1)
    zero = jnp.zeros_like(q)
    outs = []
    for qm in (jnp.where(lane < DK_A, q, zero), jnp.where(lane >= DK_A, q, zero)):
        s = _dot_nt(qm, k_all)
        mx = jnp.max(s, axis=-1, keepdims=True)
        p = jnp.exp2(s - mx).astype(BF16)
        ol = _dot(p, v_ones)
        outs.append(ol[:, :DV] * (1.0 / ol[:, DV:]))
    oa = outs[0] - lam * outs[1]
    oa = oa * lax.rsqrt(jnp.mean(oa * oa, axis=-1, keepdims=True) + LN_EPS)
    oa = oa * g * (1.0 - LAM_INIT)
    return oa * z.astype(F32)


def _attn_ctx_heads(q_ref, k_ref, v_ref, z_ref, lam, g, o_ref):
    ones = jnp.ones((q_ref.shape[0], DV), BF16)
    for h in range(N_HEADS):
        sl = slice(h * LANES, (h + 1) * LANES)
        v_ones = jnp.concatenate([v_ref[:, sl], ones], axis=1)
        o_ref[:, sl] = _attn_rows(q_ref[:, sl], k_ref[:, sl], v_ones, z_ref[:, sl],
                                  lam, g).astype(BF16)


def _attn_lat_kernel(q_ref, k_ref, v_ref, z_ref, kc_ref, vc_ref, lp_ref, g_ref, o_ref,
                     k_all, v_all, *, past, tc):
    @pl.when(pl.program_id(2) == 0)
    def _():
        head_rows = pl.ds(pl.program_id(1), past, stride=N_HEADS)
        k_all[0:past, :] = kc_ref[head_rows, :].astype(BF16)
        v_all[0:past, 0:DV] = vc_ref[head_rows, :].astype(BF16)
        k_all[past:, :] = k_ref[...]
        v_all[past:, 0:DV] = v_ref[...]
        v_all[:, DV:] = jnp.ones((v_all.shape[0], DV), BF16)

    lam = _lam(lp_ref)
    g = g_ref[...]
    for lo in range(0, q_ref.shape[0], tc):
        o_ref[lo:lo + tc, :] = _attn_rows(q_ref[lo:lo + tc, :], k_all[...], v_all[...],
                                          z_ref[lo:lo + tc, :], lam, g).astype(BF16)


def _attn_lat_call(p, cache_k, cache_v, lam_params, subln_g, *, batch, seq, past):
    tq, tc = 2048, 128
    nq = seq // tq
    return pl.pallas_call(
        functools.partial(_attn_lat_kernel, past=past, tc=tc),
        grid=(batch, N_HEADS, nq),
        in_specs=[
            pl.BlockSpec((tq, LANES), lambda b, h, i: (b * nq + i, QA_BLK + h)),
            pl.BlockSpec((seq, LANES), lambda b, h, i: (b, KA_BLK + h)),
            pl.BlockSpec((seq, LANES), lambda b, h, i: (b, VA_BLK + h)),
            pl.BlockSpec((tq, LANES), lambda b, h, i: (b * nq + i, ZA_BLK + h)),
            pl.BlockSpec((None, past * N_HEADS, LANES), lambda b, h, i: (b, 0, 0)),
            pl.BlockSpec((None, past * N_HEADS, LANES), lambda b, h, i: (b, 0, 0)),
            pl.BlockSpec((4, DK_A), lambda b, h, i: (0, 0)),
            pl.BlockSpec((1, DV), lambda b, h, i: (0, 0)),
        ],
        out_specs=pl.BlockSpec((tq, LANES), lambda b, h, i: (b * nq + i, h)),
        out_shape=jax.ShapeDtypeStruct((batch * seq, D_MODEL), BF16),
        scratch_shapes=[pltpu.VMEM((past + seq, LANES), BF16),
                        pltpu.VMEM((past + seq, 2 * DV), BF16)],
        compiler_params=_params(("arbitrary", "arbitrary", "arbitrary")),
        name="attn_lat",
    )(p, p, p, p, cache_k, cache_v, lam_params, subln_g)


PW, VW = 2 * DK_R, 2 * DV


def _ret_scratch(npairs, nc):
    stacked_state = (npairs, nc, 2 * PW, VW)
    return [pltpu.VMEM(stacked_state, F32),
            pltpu.VMEM(stacked_state, BF16),
            pltpu.VMEM((npairs, 2, CHUNK, CHUNK), F32),
            pltpu.VMEM((npairs, CHUNK, 2 * PW), F32),
            pltpu.VMEM((npairs, CHUNK, 2 * PW), F32)]


def _log_gammas(rd_ref, pair):
    def one(d, hh):
        r = jnp.full((1, VW), rd_ref[d, 2 * pair + hh], F32)
        return jnp.log1p(-jnp.exp2(r))
    return [[one(d, hh) for hh in range(2)] for d in range(2)]


def _ret_build_tables(rd_ref, dec_scr, qdec_scr, kdec_scr, *, npairs, pair0):
    lane_q = lax.broadcasted_iota(jnp.int32, (1, PW), 1)
    rel = (lax.broadcasted_iota(jnp.int32, (CHUNK, CHUNK), 0)
           - lax.broadcasted_iota(jnp.int32, (CHUNK, CHUNK), 1)).astype(F32)
    row_q = lax.broadcasted_iota(jnp.int32, (CHUNK, PW), 0).astype(F32)
    for pi in range(npairs):
        lg = _log_gammas(rd_ref, pair0 + pi)
        lg_q = [jnp.where(lane_q < DK_R, lg[d][0][:, :PW], lg[d][1][:, :PW])
                for d in range(2)]
        for hh in range(2):
            dec_scr[pi, hh] = (
                jnp.where(rel >= 0, jnp.exp(lg[0][hh][:, :1] * jnp.maximum(rel, 0.0)), 0.0)
                + jnp.where(rel <= 0, jnp.exp(lg[1][hh][:, :1] * jnp.maximum(-rel, 0.0)), 0.0))
        qdec_scr[pi] = jnp.concatenate([jnp.exp(lg_q[0] * (row_q + 1.0)),
                                        jnp.exp(lg_q[1] * (CHUNK - row_q))], axis=1)
        kdec_scr[pi] = jnp.concatenate([jnp.exp(lg_q[0] * (CHUNK - 1.0 - row_q)),
                                        jnp.exp(lg_q[1] * row_q)], axis=1)


def _ret_pairs(rd_ref, q_ref, k_ref, v_ref, z_ref, g_ref, s0_refs, o_ref, sout_refs,
               u_scr, s_scr, dec_scr, qdec_scr, kdec_scr, *, nc, npairs, pair0):
    cross = s0_refs is not None or nc > 1
    lane_q = lax.broadcasted_iota(jnp.int32, (1, PW), 1)
    lane_v = lax.broadcasted_iota(jnp.int32, (1, VW), 1)
    srow = lax.broadcasted_iota(jnp.int32, (2 * PW, VW), 0)
    scol = lax.broadcasted_iota(jnp.int32, (2 * PW, VW), 1)
    diag_blocks = ((srow % PW) // DK_R) == (scol // DV)

    def rows(c):
        return slice(c * CHUNK, (c + 1) * CHUNK)

    for pi in range(npairs):
        qs = slice(pi * PW, (pi + 1) * PW)
        vs = slice(pi * VW, (pi + 1) * VW)
        lg = _log_gammas(rd_ref, pair0 + pi)
        lg_v = [jnp.where(lane_v < DV, lg[d][0], lg[d][1]) for d in range(2)]
        chunk_decay = [jnp.exp(lg_v[d] * CHUNK) for d in range(2)]

        for c in range(nc):
            k = k_ref[rows(c), qs].astype(F32)
            kd = (jnp.concatenate([k, k], axis=1) * kdec_scr[pi]).astype(BF16)
            u_scr[pi, c] = jnp.where(diag_blocks, _dot_tn(kd, v_ref[rows(c), vs]), 0.0)

        for d, order in ((0, range(nc)), (1, reversed(range(nc)))):
            half = slice(d * PW, (d + 1) * PW)
            if s0_refs is None:
                s = jnp.zeros((PW, VW), F32)
            else:
                zero = jnp.zeros((DK_R, DV), F32)
                s = jnp.concatenate(
                    [jnp.concatenate([s0_refs[d][2 * pi], zero], axis=1),
                     jnp.concatenate([zero, s0_refs[d][2 * pi + 1]], axis=1)], axis=0)
            for c in order:
                if cross:
                    s_scr[pi, c, half, :] = s.astype(BF16)
                s = s * chunk_decay[d] + u_scr[pi, c, half, :]
            if sout_refs is not None:
                sout_refs[d][2 * pi] = s[:DK_R, :DV]
                sout_refs[d][2 * pi + 1] = s[DK_R:, DV:]

        g = g_ref[:, vs]
        for c in range(nc):
            q = q_ref[rows(c), qs]
            k = k_ref[rows(c), qs]
            v = v_ref[rows(c), vs]
            zero = jnp.zeros_like(q)
            q2 = jnp.concatenate([jnp.where(lane_q < DK_R, q, zero),
                                  jnp.where(lane_q >= DK_R, q, zero)], axis=0)
            a2 = _dot_nt(q2, k)
            o = jnp.concatenate(
                [_dot((a2[hh * CHUNK:(hh + 1) * CHUNK] * dec_scr[pi, hh]).astype(BF16),
                      v[:, hh * DV:(hh + 1) * DV]) for hh in range(2)], axis=1)
            if cross:
                qf = q.astype(F32)
                qq = (jnp.concatenate([qf, qf], axis=1) * qdec_scr[pi]).astype(BF16)
                o = o + _dot(qq, s_scr[pi, c])
            outs = []
            for hh in range(2):
                oh = o[:, hh * DV:(hh + 1) * DV]
                mu = jnp.mean(oh, axis=-1, keepdims=True)
                oc = oh - mu
                var = jnp.mean(oc * oc, axis=-1, keepdims=True)
                outs.append(oc * lax.rsqrt(var + LN_EPS))
            o = jnp.concatenate(outs, axis=1) * g * z_ref[rows(c), vs].astype(F32)
            o_ref[rows(c), vs] = o.astype(BF16)


def _ret_lat_kernel(rd_ref, q_ref, k_ref, v_ref, z_ref, g_ref, s0f_ref, s0b_ref, o_ref,
                    u_scr, s_scr, dec_scr, qdec_scr, kdec_scr, *, nc, npairs):
    pair0 = pl.program_id(0) * npairs

    @pl.when(pl.program_id(1) == 0)
    def _():
        _ret_build_tables(rd_ref, dec_scr, qdec_scr, kdec_scr, npairs=npairs, pair0=pair0)

    _ret_pairs(rd_ref, q_ref, k_ref, v_ref, z_ref, g_ref, (s0f_ref, s0b_ref), o_ref, None,
               u_scr, s_scr, dec_scr, qdec_scr, kdec_scr, nc=nc, npairs=npairs, pair0=pair0)


def _ret_lat_call(p, ret_decay, gn_g, s0f, s0b, *, batch, seq):
    nc = seq // CHUNK
    hps = 4
    npairs = hps // 2
    qw, vw = DK_R * hps, DV * hps
    col = lambda w, blk: pl.BlockSpec((seq, w), lambda j, b: (b, blk * LANES // w + j))
    state = pl.BlockSpec((None, hps, DK_R, DV), lambda j, b: (b, j, 0, 0))
    return pl.pallas_call(
        functools.partial(_ret_lat_kernel, nc=nc, npairs=npairs),
        grid=(N_HEADS // hps, batch),
        in_specs=[pl.BlockSpec(memory_space=pltpu.SMEM),
                  col(qw, QR_BLK), col(qw, KR_BLK), col(vw, VR_BLK), col(vw, ZR_BLK),
                  pl.BlockSpec((1, vw), lambda j, b: (0, j)), state, state],
        out_specs=pl.BlockSpec((seq, vw), lambda j, b: (b, j)),
        out_shape=jax.ShapeDtypeStruct((batch * seq, D_MODEL), BF16),
        scratch_shapes=_ret_scratch(npairs, nc),
        compiler_params=_params(("arbitrary", "arbitrary")),
        name="ret_lat",
    )(ret_decay, p, p, p, p, gn_g, s0f, s0b)


def _out_rows(oa_ref, or_ref, ga_ref, gr_ref, x_ref, gate, wpa_ref, wpr_ref, wo_ref,
              lng_ref, lnb_ref, y_ref):
    a = _dot(oa_ref[...], wpa_ref[...])
    r = _dot(or_ref[...], wpr_ref[...])
    m = ga_ref[...].astype(F32) * a + gr_ref[...].astype(F32) * r
    out = _dot(m.astype(BF16), wo_ref[...])
    t = DEEPNORM_ALPHA * x_ref[...] + gate * out
    mu = jnp.mean(t, axis=-1, keepdims=True)
    tc = t - mu
    var = jnp.mean(tc * tc, axis=-1, keepdims=True)
    y_ref[...] = tc * lax.rsqrt(var + LN_EPS) * lng_ref[...] + lnb_ref[...]


def _out_lat_kernel(oa_ref, or_ref, ga_ref, gr_ref, x_ref, mod_ref, wpa_ref, wpr_ref, wo_ref,
                    lng_ref, lnb_ref, y_ref, *, per_seq, mod_row0):
    _, _, gate = _mod_row(mod_ref, mod_row0 + pl.program_id(0) // per_seq)
    _out_rows(oa_ref, or_ref, ga_ref, gr_ref, x_ref, gate, wpa_ref, wpr_ref, wo_ref,
              lng_ref, lnb_ref, y_ref)


def _out_lat_call(oa, orr, p, x2d, mod, w_pa, w_pr, w_out, ln_g, ln_b, *, seq, mod_row0):
    m = x2d.shape[0]
    tm = 512
    per_seq = seq // tm
    tile = lambda j: pl.BlockSpec((tm, D_MODEL), lambda i: (i, j))
    vec = pl.BlockSpec((1, D_MODEL), lambda i: (0, 0))
    return pl.pallas_call(
        functools.partial(_out_lat_kernel, per_seq=per_seq, mod_row0=mod_row0),
        grid=(m // tm,),
        in_specs=[tile(0), tile(0), tile(GA_BLK * LANES // D_MODEL), tile(GR_BLK * LANES // D_MODEL),
                  tile(0),
                  pl.BlockSpec((MOD_ROWS, 3 * D_MODEL), lambda i: (0, 0)),
                  _resident((D_MODEL, D_MODEL)), _resident((D_MODEL, D_MODEL)),
                  _resident((D_MODEL, D_MODEL)), vec, vec],
        out_specs=tile(0),
        out_shape=jax.ShapeDtypeStruct((m, D_MODEL), F32),
        compiler_params=_params(("arbitrary",)),
        name="out_lat",
    )(oa, orr, p, p, x2d, mod, w_pa, w_pr, w_out, ln_g, ln_b)


W_CHUNK = D_MODEL


def _weight_chunks(w_hbm):
    return [(k, slice(j * W_CHUNK, (j + 1) * W_CHUNK))
            for k, w in enumerate(w_hbm) for j in range(w.shape[1] // W_CHUNK)]


def _weight_store(n, chunks, w_scr, w_bf_hbm, out_sem):
    k, cols = chunks[n]
    return pltpu.make_async_copy(w_scr[k].at[:, cols], w_bf_hbm[k].at[:, cols], out_sem.at[n])


def _stream_f32_chunks(jobs, stage, in_sem, while_first_loads):
    def load(n):
        return pltpu.make_async_copy(jobs[n][0], stage.at[n % 2], in_sem.at[n % 2])

    load(0).start()
    load(1).start()
    while_first_loads()
    for n, (_, sink) in enumerate(jobs):
        if 1 <= n and n + 1 < len(jobs):
            load(n + 1).start()
        load(n).wait()
        sink(stage.at[n % 2])


def _ctx_kernel(x_ref, cctx_ref, c_ref, wmod_hbm, bmod_ref, win_hbm, wg_hbm, bg_ref, lp_ref, sg_ref,
                rd_ref, gg_ref, wpa_hbm, wpr_hbm, wo_hbm, lng_ref, lnb_ref,
                y_ref, k_ref, v_ref, sf_ref, sb_ref, mod_ref, win_bf, wg_bf, wpa_bf, wpr_bf, wo_bf,
                p_scr, oa_scr, or_scr, u_scr, s_scr, dec_scr, qdec_scr, kdec_scr,
                win_ref, wg_ref, wpa_ref, wpr_ref, wo_ref, cond_scr, stage, in_sem, out_sem,
                *, npairs):
    w_hbm = (win_hbm, wg_hbm, wpa_hbm, wpr_hbm, wo_hbm)
    w_scr = (win_ref, wg_ref, wpa_ref, wpr_ref, wo_ref)
    w_bf = (win_bf, wg_bf, wpa_bf, wpr_bf, wo_bf)

    @pl.when(pl.program_id(0) == 0)
    def _():
        cond_scr[...] = jnp.zeros(cond_scr.shape, F32)
        cond_scr[0:1, :] = cctx_ref[...]
        cond_scr[1:1 + c_ref.shape[0], :] = c_ref[...]
        cond = _silu(cond_scr[...]).astype(BF16)
        jobs = []
        for j in range(wmod_hbm.shape[1] // W_CHUNK):
            cols = slice(j * W_CHUNK, (j + 1) * W_CHUNK)

            def mod_sink(staged, cols=cols):
                mod_ref[:, cols] = _dot(cond, staged[...].astype(BF16)) + bmod_ref[:, cols]
            jobs.append((wmod_hbm.at[:, cols], mod_sink))
        chunks = _weight_chunks(w_hbm)
        for k, cols in chunks:
            def cast_sink(staged, k=k, cols=cols):
                w_scr[k][:, cols] = staged[...].astype(BF16)
            jobs.append((w_hbm[k].at[:, cols], cast_sink))
        _stream_f32_chunks(jobs, stage, in_sem, functools.partial(
            _ret_build_tables, rd_ref, dec_scr, qdec_scr, kdec_scr, npairs=npairs, pair0=0))
        for n in range(len(chunks)):
            _weight_store(n, chunks, w_scr, w_bf, out_sem).start(priority=1)

    cols = lambda blk, w: p_scr.at[:, blk * LANES:blk * LANES + w]
    shift, scale, gate = _mod_row(mod_ref, 0)
    _project(_modulated_ln(x_ref, shift, scale), win_ref, wg_ref, bg_ref, p_scr,
             kv_out=(k_ref, v_ref))
    _attn_ctx_heads(cols(QA_BLK, D_MODEL), cols(KA_BLK, D_MODEL), cols(VA_BLK, D_MODEL),
                    cols(ZA_BLK, D_MODEL), _lam(lp_ref), sg_ref[...], oa_scr)
    _ret_pairs(rd_ref, cols(QR_BLK, D_MODEL // 2), cols(KR_BLK, D_MODEL // 2),
               cols(VR_BLK, D_MODEL), cols(ZR_BLK, D_MODEL), gg_ref, None, or_scr,
               (sf_ref, sb_ref), u_scr, s_scr, dec_scr, qdec_scr, kdec_scr,
               nc=x_ref.shape[0] // CHUNK, npairs=npairs, pair0=0)
    _out_rows(oa_scr, or_scr, cols(GA_BLK, D_MODEL), cols(GR_BLK, D_MODEL), x_ref, gate,
              wpa_ref, wpr_ref, wo_ref, lng_ref, lnb_ref, y_ref)

    @pl.when(pl.program_id(0) == pl.num_programs(0) - 1)
    def _():
        chunks = _weight_chunks(w_hbm)
        for n in range(len(chunks)):
            _weight_store(n, chunks, w_scr, w_bf, out_sem).wait()


def _ctx_call(x2d, c_ctx, c, w_mod, b_mod, w_in, w_gate, b_gate, lam_params, subln_g, ret_decay,
              gn_g, w_pa, w_pr, w_out, ln_g, ln_b, *, batch, seq):
    npairs = N_HEADS // 2
    nc = seq // CHUNK
    tile = pl.BlockSpec((seq, D_MODEL), lambda b: (b, 0))
    vec = lambda w: pl.BlockSpec((1, w), lambda b: (0, 0))
    state = pl.BlockSpec((None, N_HEADS, DK_R, DV), lambda b: (b, 0, 0, 0))
    heads = pl.BlockSpec((seq * N_HEADS, LANES), lambda b: (b, 0))
    tok = jax.ShapeDtypeStruct((batch * seq, D_MODEL), F32)
    tok_heads = jax.ShapeDtypeStruct((batch * seq * N_HEADS, LANES), F32)
    st = jax.ShapeDtypeStruct((batch, N_HEADS, DK_R, DV), F32)
    hbm = pl.BlockSpec(memory_space=pl.ANY)
    weights = (w_in, w_gate, w_pa, w_pr, w_out)
    n_chunks = sum(w.shape[1] // W_CHUNK for w in weights)
    return pl.pallas_call(
        functools.partial(_ctx_kernel, npairs=npairs),
        grid=(batch,),
        in_specs=[tile, vec(D_MODEL), pl.BlockSpec(c.shape, lambda b: (0, 0)), hbm,
                  vec(3 * D_MODEL), hbm, hbm, vec(2 * D_MODEL),
                  pl.BlockSpec((4, DK_A), lambda b: (0, 0)), vec(DV),
                  pl.BlockSpec(memory_space=pltpu.SMEM), vec(D_MODEL),
                  hbm, hbm, hbm, vec(D_MODEL), vec(D_MODEL)],
        out_specs=[tile, heads, heads, state, state,
                   pl.BlockSpec((MOD_ROWS, 3 * D_MODEL), lambda b: (0, 0))] + [hbm] * len(weights),
        out_shape=[tok, tok_heads, tok_heads, st, st, jax.ShapeDtypeStruct((MOD_ROWS, 3 * D_MODEL), F32)]
                  + [jax.ShapeDtypeStruct(w.shape, BF16) for w in weights],
        scratch_shapes=[pltpu.VMEM((seq, P_WIDTH), BF16),
                        pltpu.VMEM((seq, D_MODEL), BF16),
                        pltpu.VMEM((seq, D_MODEL), BF16)] + _ret_scratch(npairs, nc)
                       + [pltpu.VMEM(w.shape, BF16) for w in weights]
                       + [pltpu.VMEM((MOD_ROWS, D_MODEL), F32),
                          pltpu.VMEM((2, D_MODEL, W_CHUNK), F32),
                          pltpu.SemaphoreType.DMA((2,)),
                          pltpu.SemaphoreType.DMA((n_chunks,))],
        compiler_params=_params(("arbitrary",)),
        name="ctx",
    )(x2d, c_ctx, c, w_mod, b_mod, w_in, w_gate, b_gate, lam_params, subln_g, ret_decay, gn_g,
      w_pa, w_pr, w_out, ln_g, ln_b)


def _rope_tables(n_tokens):
    rows = n_tokens // GRID_W
    r = np.repeat(np.arange(rows, dtype=np.float32), GRID_W)
    col = np.tile(np.arange(GRID_W, dtype=np.float32), rows)
    n_freq = DK_A // 4
    inv = np.float32(ROPE_BASE) ** (-np.arange(n_freq, dtype=np.float32) / np.float32(n_freq))
    ang = np.concatenate([r[:, None] * inv, col[:, None] * inv], axis=-1).astype(np.float32)
    cos = np.repeat(np.cos(ang), 2, axis=-1)
    sin = np.repeat(np.sin(ang), 2, axis=-1)
    even = (np.arange(DK_A) % 2 == 0)[None, :]
    sin_even = np.where(even, -sin, 0.0)
    sin_odd = np.where(even, 0.0, sin)
    two = lambda t: jnp.asarray(np.concatenate([t, t], axis=-1), F32)
    return two(cos), two(sin_even), two(sin_odd)


def kernel(x_prompt, x_sample, cache_attn_k, cache_attn_v, state_ret_fwd, state_ret_bwd,
           c, c_ctx, w_mod, b_mod, w_in, lam_params, subln_g, ret_decay, ret_gn_g,
           w_pa, w_pr, w_gate, b_gate, w_out, ln_g, ln_b):
    batch, seq, _ = x_prompt.shape
    dbatch, dseq, _ = x_sample.shape
    past = cache_attn_k.shape[2]
    l = 0

    bg = b_gate[l][None, :]
    lp, sg = lam_params[l], subln_g[l][None, :]
    rd, gg = ret_decay[l], ret_gn_g[l][None, :]
    lng, lnb = ln_g[l][None, :], ln_b[l][None, :]

    xc = x_prompt.reshape(batch * seq, D_MODEL)
    y_c, k_c, v_c, sf_c, sb_c, mod, win, wg, wpa, wpr, wo = _ctx_call(
        xc, c_ctx[None, :], c, w_mod[l], b_mod[l][None, :], w_in[l], w_gate[l], bg, lp, sg, rd, gg,
        w_pa[l], w_pr[l], w_out[l], lng, lnb, batch=batch, seq=seq)

    xs = x_sample.reshape(dbatch * dseq, D_MODEL)
    p_s = _proj_lat_call(xs, mod, win, wg, bg, _rope_tables(dseq), seq=dseq, mod_row0=1)
    ck = cache_attn_k[:, l].reshape(dbatch, past * N_HEADS, 2 * DK_A)
    cv = cache_attn_v[:, l].reshape(dbatch, past * N_HEADS, DV)
    oa_s = _attn_lat_call(p_s, ck, cv, lp, sg, batch=dbatch, seq=dseq, past=past)
    or_s = _ret_lat_call(p_s, rd, gg, state_ret_fwd[:, l], state_ret_bwd[:, l],
                         batch=dbatch, seq=dseq)
    y_s = _out_lat_call(oa_s, or_s, p_s, xs, mod, wpa, wpr, wo, lng, lnb, seq=dseq, mod_row0=1)

    return (y_c.reshape(batch, seq, D_MODEL),
            y_s.reshape(dbatch, dseq, D_MODEL),
            k_c.reshape(batch, 1, seq, N_HEADS, 2 * DK_A),
            v_c.reshape(batch, 1, seq, N_HEADS, DV),
            sf_c.reshape(batch, 1, N_HEADS, DK_R, DV),
            sb_c.reshape(batch, 1, N_HEADS, DK_R, DV))
```

```python
import functools
import math

import jax
import jax.numpy as jnp
import numpy as np
from jax import lax
from jax.experimental import pallas as pl
from jax.experimental.pallas import tpu as pltpu

F32 = jnp.float32
BF16 = jnp.bfloat16

D_MODEL = 1024
N_HEADS = 8
DK_A = 64
DV = 128
DK_R = 64
CHUNK = 256
GRID_W = 64
ROPE_BASE = 10000.0
MOD_EPS = 1e-6
LN_EPS = 1e-5
DEPTH = 1
DEEPNORM_ALPHA = (2.0 * DEPTH) ** 0.25
LAM_INIT = 0.8 - 0.6 * math.exp(-0.3 * 0)
LOG2E = math.log2(math.e)

LANES = 128
MOD_ROWS = 8
QA_BLK, KA_BLK, VA_BLK, ZA_BLK = 0, 8, 16, 24
QR_BLK, KR_BLK, VR_BLK, ZR_BLK = 32, 36, 40, 48
GA_BLK, GR_BLK = 56, 64
IN_SEGS = 7
P_WIDTH = (IN_SEGS + 2) * D_MODEL
VMEM_LIMIT = 56 * 1024 * 1024


def _params(sem):
    return pltpu.CompilerParams(dimension_semantics=sem, vmem_limit_bytes=VMEM_LIMIT)


def _resident(shape):
    return pl.BlockSpec(shape, lambda *_: tuple(0 for _ in shape), pipeline_mode=pl.Buffered(1))


def _silu(z):
    return z * (1.0 / (1.0 + jnp.exp(-z)))


def _dot(a, b):
    return jnp.dot(a, b, preferred_element_type=F32)


def _dot_nt(a, b):
    return lax.dot_general(a, b, (((1,), (1,)), ((), ())), preferred_element_type=F32)


def _dot_tn(a, b):
    return lax.dot_general(a, b, (((0,), (0,)), ((), ())), preferred_element_type=F32)


def _rope(acc, cos, sin_even, sin_odd):
    outs = []
    for hh in range(N_HEADS):
        xs = acc[:, hh * LANES:(hh + 1) * LANES]
        nxt = pltpu.roll(xs, LANES - 1, 1)
        prv = pltpu.roll(xs, 1, 1)
        outs.append(xs * cos + nxt * sin_even + prv * sin_odd)
    return jnp.concatenate(outs, axis=1)


def _mod_row(mod_ref, row):
    r = pl.ds(row, 1)
    return (mod_ref[r, 0:D_MODEL], mod_ref[r, D_MODEL:2 * D_MODEL], mod_ref[r, 2 * D_MODEL:3 * D_MODEL])


def _modulated_ln(x_ref, shift, scale):
    x = x_ref[...]
    mu = jnp.mean(x, axis=-1, keepdims=True)
    xc = x - mu
    var = jnp.mean(xc * xc, axis=-1, keepdims=True)
    return (xc * lax.rsqrt(var + MOD_EPS) * (1.0 + scale) + shift).astype(BF16)


def _store_heads(ref, val):
    for h in range(N_HEADS):
        ref[pl.ds(h, val.shape[0], stride=N_HEADS), :] = val[:, h * LANES:(h + 1) * LANES]


def _project(h, win_ref, wg_ref, bg_ref, p_ref, rope=None, kv_out=None):
    def seg(j):
        if j < IN_SEGS:
            return _dot(h, win_ref[:, j * D_MODEL:(j + 1) * D_MODEL])
        return _dot(h, wg_ref[:, (j - IN_SEGS) * D_MODEL:(j - IN_SEGS + 1) * D_MODEL])

    def put(j, val):
        p_ref[:, j * D_MODEL:(j + 1) * D_MODEL] = val.astype(BF16)

    qa = seg(0)
    ka = seg(1)
    if rope is not None:
        cos, se, so = (t[...] for t in rope)
        qa = _rope(qa, cos, se, so)
        ka = _rope(ka, cos, se, so)
    if kv_out is not None:
        _store_heads(kv_out[0], ka)
    put(0, qa * (DK_A ** -0.5 * LOG2E))
    put(1, ka)
    va = seg(2)
    if kv_out is not None:
        _store_heads(kv_out[1], va)
    put(2, va)
    put(3, _silu(seg(3)))
    qk = seg(4)
    put(4, jnp.concatenate([qk[:, :D_MODEL // 2], qk[:, D_MODEL // 2:] * (DK_R ** -0.5)], axis=1))
    put(5, seg(5))
    put(6, _silu(seg(6)))
    for j in (IN_SEGS, IN_SEGS + 1):
        g = seg(j) + bg_ref[:, (j - IN_SEGS) * D_MODEL:(j - IN_SEGS + 1) * D_MODEL]
        put(j, 1.0 / (1.0 + jnp.exp(-g)))


def _proj_lat_kernel(x_ref, mod_ref, win_ref, wg_ref, bg_ref, cos_ref, se_ref, so_ref, p_ref,
                     *, per_seq, mod_row0):
    shift, scale, _ = _mod_row(mod_ref, mod_row0 + pl.program_id(0) // per_seq)
    _project(_modulated_ln(x_ref, shift, scale), win_ref, wg_ref, bg_ref, p_ref,
             rope=(cos_ref, se_ref, so_ref))


def _proj_lat_call(x2d, mod, w_in, w_gate, b_gate, rope, *, seq, mod_row0):
    m = x2d.shape[0]
    tm = 256
    per_seq = seq // tm
    return pl.pallas_call(
        functools.partial(_proj_lat_kernel, per_seq=per_seq, mod_row0=mod_row0),
        grid=(m // tm,),
        in_specs=[
            pl.BlockSpec((tm, D_MODEL), lambda i: (i, 0)),
            pl.BlockSpec((MOD_ROWS, 3 * D_MODEL), lambda i: (0, 0)),
            _resident((D_MODEL, IN_SEGS * D_MODEL)),
            _resident((D_MODEL, 2 * D_MODEL)),
            pl.BlockSpec((1, 2 * D_MODEL), lambda i: (0, 0)),
        ] + [pl.BlockSpec((tm, LANES), lambda i: (i % per_seq, 0))] * 3,
        out_specs=pl.BlockSpec((tm, P_WIDTH), lambda i: (i, 0)),
        out_shape=jax.ShapeDtypeStruct((m, P_WIDTH), BF16),
        compiler_params=_params(("arbitrary",)),
        name="proj_lat",
    )(x2d, mod, w_in, w_gate, b_gate, *rope)


def _lam(lp_ref):
    lp = lp_ref[...]
    a = jnp.sum(lp[0:1] * lp[1:2], axis=-1, keepdims=True)
    b = jnp.sum(lp[2:3] * lp[3:4], axis=-1, keepdims=True)
    return jnp.exp(a) - jnp.exp(b) + LAM_INIT


def _attn_rows(q, k_all, v_ones, z, lam, g):
    lane = lax.broadcasted_iota(jnp.int32, (1, LANES), 1)
    zero = jnp.zeros_like(q)
    outs = []
    for qm in (jnp.where(lane < DK_A, q, zero), jnp.where(lane >= DK_A, q, zero)):
        s = _dot_nt(qm, k_all)
        mx = jnp.max(s, axis=-1, keepdims=True)
        p = jnp.exp2(s - mx).astype(BF16)
        ol = _dot(p, v_ones)
        outs.append(ol[:, :DV] * (1.0 / ol[:, DV:]))
    oa = outs[0] - lam * outs[1]
    oa = oa * lax.rsqrt(jnp.mean(oa * oa, axis=-1, keepdims=True) + LN_EPS)
    oa = oa * g * (1.0 - LAM_INIT)
    return oa * z.astype(F32)


def _attn_ctx_heads(q_ref, k_ref, v_ref, z_ref, lam, g, o_ref):
    ones = jnp.ones((q_ref.shape[0], DV), BF16)
    for h in range(N_HEADS):
        sl = slice(h * LANES, (h + 1) * LANES)
        v_ones = jnp.concatenate([v_ref[:, sl], ones], axis=1)
        o_ref[:, sl] = _attn_rows(q_ref[:, sl], k_ref[:, sl], v_ones, z_ref[:, sl],
                                  lam, g).astype(BF16)


def _attn_lat_kernel(q_ref, k_ref, v_ref, z_ref, kc_ref, vc_ref, lp_ref, g_ref, o_ref,
                     k_all, v_all, *, past, tc):
    @pl.when(pl.program_id(2) == 0)
    def _():
        head_rows = pl.ds(pl.program_id(1), past, stride=N_HEADS)
        k_all[0:past, :] = kc_ref[head_rows, :].astype(BF16)
        v_all[0:past, 0:DV] = vc_ref[head_rows, :].astype(BF16)
        k_all[past:, :] = k_ref[...]
        v_all[past:, 0:DV] = v_ref[...]
        v_all[:, DV:] = jnp.ones((v_all.shape[0], DV), BF16)

    lam = _lam(lp_ref)
    g = g_ref[...]
    for lo in range(0, q_ref.shape[0], tc):
        o_ref[lo:lo + tc, :] = _attn_rows(q_ref[lo:lo + tc, :], k_all[...], v_all[...],
                                          z_ref[lo:lo + tc, :], lam, g).astype(BF16)


def _attn_lat_call(p, cache_k, cache_v, lam_params, subln_g, *, batch, seq, past):
    tq, tc = 2048, 128
    nq = seq // tq
    return pl.pallas_call(
        functools.partial(_attn_lat_kernel, past=past, tc=tc),
        grid=(batch, N_HEADS, nq),
        in_specs=[
            pl.BlockSpec((tq, LANES), lambda b, h, i: (b * nq + i, QA_BLK + h)),
            pl.BlockSpec((seq, LANES), lambda b, h, i: (b, KA_BLK + h)),
            pl.BlockSpec((seq, LANES), lambda b, h, i: (b, VA_BLK + h)),
            pl.BlockSpec((tq, LANES), lambda b, h, i: (b * nq + i, ZA_BLK + h)),
            pl.BlockSpec((None, past * N_HEADS, LANES), lambda b, h, i: (b, 0, 0)),
            pl.BlockSpec((None, past * N_HEADS, LANES), lambda b, h, i: (b, 0, 0)),
            pl.BlockSpec((4, DK_A), lambda b, h, i: (0, 0)),
            pl.BlockSpec((1, DV), lambda b, h, i: (0, 0)),
        ],
        out_specs=pl.BlockSpec((tq, LANES), lambda b, h, i: (b * nq + i, h)),
        out_shape=jax.ShapeDtypeStruct((batch * seq, D_MODEL), BF16),
        scratch_shapes=[pltpu.VMEM((past + seq, LANES), BF16),
                        pltpu.VMEM((past + seq, 2 * DV), BF16)],
        compiler_params=_params(("arbitrary", "arbitrary", "arbitrary")),
        name="attn_lat",
    )(p, p, p, p, cache_k, cache_v, lam_params, subln_g)


PW, VW = 2 * DK_R, 2 * DV


def _ret_scratch(npairs, nc):
    stacked_state = (npairs, nc, 2 * PW, VW)
    return [pltpu.VMEM(stacked_state, F32),
            pltpu.VMEM(stacked_state, BF16),
            pltpu.VMEM((npairs, 2, CHUNK, CHUNK), F32),
            pltpu.VMEM((npairs, CHUNK, 2 * PW), BF16),
            pltpu.VMEM((npairs, CHUNK, 2 * PW), BF16)]


def _log_gammas(rd_ref, pair):
    def one(d, hh):
        r = jnp.full((1, VW), rd_ref[d, 2 * pair + hh], F32)
        return jnp.log1p(-jnp.exp2(r))
    return [[one(d, hh) for hh in range(2)] for d in range(2)]


def _ret_build_tables(rd_ref, dec_scr, qdec_scr, kdec_scr, *, npairs, pair0):
    lane_q = lax.broadcasted_iota(jnp.int32, (1, PW), 1)
    rel = (lax.broadcasted_iota(jnp.int32, (CHUNK, CHUNK), 0)
           - lax.broadcasted_iota(jnp.int32, (CHUNK, CHUNK), 1)).astype(F32)
    row_q = lax.broadcasted_iota(jnp.int32, (CHUNK, PW), 0).astype(F32)
    for pi in range(npairs):
        lg = _log_gammas(rd_ref, pair0 + pi)
        lg_q = [jnp.where(lane_q < DK_R, lg[d][0][:, :PW], lg[d][1][:, :PW])
                for d in range(2)]
        for hh in range(2):
            dec_scr[pi, hh] = (
                jnp.where(rel >= 0, jnp.exp(lg[0][hh][:, :1] * jnp.maximum(rel, 0.0)), 0.0)
                + jnp.where(rel <= 0, jnp.exp(lg[1][hh][:, :1] * jnp.maximum(-rel, 0.0)), 0.0))
        qdec_scr[pi] = jnp.concatenate([jnp.exp(lg_q[0] * (row_q + 1.0)),
                                        jnp.exp(lg_q[1] * (CHUNK - row_q))], axis=1).astype(BF16)
        kdec_scr[pi] = jnp.concatenate([jnp.exp(lg_q[0] * (CHUNK - 1.0 - row_q)),
                                        jnp.exp(lg_q[1] * row_q)], axis=1).astype(BF16)


def _ret_pairs(rd_ref, q_ref, k_ref, v_ref, z_ref, g_ref, s0_refs, o_ref, sout_refs,
               u_scr, s_scr, dec_scr, qdec_scr, kdec_scr, *, nc, npairs, pair0):
    cross = s0_refs is not None or nc > 1
    lane_q = lax.broadcasted_iota(jnp.int32, (1, PW), 1)
    lane_v = lax.broadcasted_iota(jnp.int32, (1, VW), 1)
    srow = lax.broadcasted_iota(jnp.int32, (2 * PW, VW), 0)
    scol = lax.broadcasted_iota(jnp.int32, (2 * PW, VW), 1)
    diag_blocks = ((srow % PW) // DK_R) == (scol // DV)

    def rows(c):
        return slice(c * CHUNK, (c + 1) * CHUNK)

    for pi in range(npairs):
        qs = slice(pi * PW, (pi + 1) * PW)
        vs = slice(pi * VW, (pi + 1) * VW)
        lg = _log_gammas(rd_ref, pair0 + pi)
        lg_v = [jnp.where(lane_v < DV, lg[d][0], lg[d][1]) for d in range(2)]
        chunk_decay = [jnp.exp(lg_v[d] * CHUNK) for d in range(2)]

        for c in range(nc):
            k = k_ref[rows(c), qs]
            kd = jnp.concatenate([k, k], axis=1) * kdec_scr[pi]
            u_scr[pi, c] = jnp.where(diag_blocks, _dot_tn(kd, v_ref[rows(c), vs]), 0.0)

        for d, order in ((0, range(nc)), (1, reversed(range(nc)))):
            half = slice(d * PW, (d + 1) * PW)
            if s0_refs is None:
                s = jnp.zeros((PW, VW), F32)
            else:
                zero = jnp.zeros((DK_R, DV), F32)
                s = jnp.concatenate(
                    [jnp.concatenate([s0_refs[d][2 * pi], zero], axis=1),
                     jnp.concatenate([zero, s0_refs[d][2 * pi + 1]], axis=1)], axis=0)
            for c in order:
                if cross:
                    s_scr[pi, c, half, :] = s.astype(BF16)
                s = s * chunk_decay[d] + u_scr[pi, c, half, :]
            if sout_refs is not None:
                sout_refs[d][2 * pi] = s[:DK_R, :DV]
                sout_refs[d][2 * pi + 1] = s[DK_R:, DV:]

        g = g_ref[:, vs]
        for c in range(nc):
            q = q_ref[rows(c), qs]
            k = k_ref[rows(c), qs]
            v = v_ref[rows(c), vs]
            zero = jnp.zeros_like(q)
            q2 = jnp.concatenate([jnp.where(lane_q < DK_R, q, zero),
                                  jnp.where(lane_q >= DK_R, q, zero)], axis=0)
            a2 = _dot_nt(q2, k)
            o = jnp.concatenate(
                [_dot((a2[hh * CHUNK:(hh + 1) * CHUNK] * dec_scr[pi, hh]).astype(BF16),
                      v[:, hh * DV:(hh + 1) * DV]) for hh in range(2)], axis=1)
            if cross:
                qq = jnp.concatenate([q, q], axis=1) * qdec_scr[pi]
                o = o + _dot(qq, s_scr[pi, c])
            outs = []
            for hh in range(2):
                oh = o[:, hh * DV:(hh + 1) * DV]
                mu = jnp.mean(oh, axis=-1, keepdims=True)
                oc = oh - mu
                var = jnp.mean(oc * oc, axis=-1, keepdims=True)
                outs.append(oc * lax.rsqrt(var + LN_EPS))
            o = jnp.concatenate(outs, axis=1) * g * z_ref[rows(c), vs].astype(F32)
            o_ref[rows(c), vs] = o.astype(BF16)


def _ret_lat_kernel(rd_ref, q_ref, k_ref, v_ref, z_ref, g_ref, s0f_ref, s0b_ref, o_ref,
                    u_scr, s_scr, dec_scr, qdec_scr, kdec_scr, *, nc, npairs):
    pair0 = pl.program_id(0) * npairs

    @pl.when(pl.program_id(1) == 0)
    def _():
        _ret_build_tables(rd_ref, dec_scr, qdec_scr, kdec_scr, npairs=npairs, pair0=pair0)

    _ret_pairs(rd_ref, q_ref, k_ref, v_ref, z_ref, g_ref, (s0f_ref, s0b_ref), o_ref, None,
               u_scr, s_scr, dec_scr, qdec_scr, kdec_scr, nc=nc, npairs=npairs, pair0=pair0)


def _ret_lat_call(p, ret_decay, gn_g, s0f, s0b, *, batch, seq):
    nc = seq // CHUNK
    hps = 4
    npairs = hps // 2
    qw, vw = DK_R * hps, DV * hps
    col = lambda w, blk: pl.BlockSpec((seq, w), lambda j, b: (b, blk * LANES // w + j))
    state = pl.BlockSpec((None, hps, DK_R, DV), lambda j, b: (b, j, 0, 0))
    return pl.pallas_call(
        functools.partial(_ret_lat_kernel, nc=nc, npairs=npairs),
        grid=(N_HEADS // hps, batch),
        in_specs=[pl.BlockSpec(memory_space=pltpu.SMEM),
                  col(qw, QR_BLK), col(qw, KR_BLK), col(vw, VR_BLK), col(vw, ZR_BLK),
                  pl.BlockSpec((1, vw), lambda j, b: (0, j)), state, state],
        out_specs=pl.BlockSpec((seq, vw), lambda j, b: (b, j)),
        out_shape=jax.ShapeDtypeStruct((batch * seq, D_MODEL), BF16),
        scratch_shapes=_ret_scratch(npairs, nc),
        compiler_params=_params(("arbitrary", "arbitrary")),
        name="ret_lat",
    )(ret_decay, p, p, p, p, gn_g, s0f, s0b)


def _out_rows(oa_ref, or_ref, ga_ref, gr_ref, x_ref, gate, wpa_ref, wpr_ref, wo_ref,
              lng_ref, lnb_ref, y_ref):
    a = _dot(oa_ref[...], wpa_ref[...])
    r = _dot(or_ref[...], wpr_ref[...])
    m = ga_ref[...].astype(F32) * a + gr_ref[...].astype(F32) * r
    out = _dot(m.astype(BF16), wo_ref[...])
    t = DEEPNORM_ALPHA * x_ref[...] + gate * out
    mu = jnp.mean(t, axis=-1, keepdims=True)
    tc = t - mu
    var = jnp.mean(tc * tc, axis=-1, keepdims=True)
    y_ref[...] = tc * lax.rsqrt(var + LN_EPS) * lng_ref[...] + lnb_ref[...]


def _out_lat_kernel(oa_ref, or_ref, ga_ref, gr_ref, x_ref, mod_ref, wpa_ref, wpr_ref, wo_ref,
                    lng_ref, lnb_ref, y_ref, *, per_seq, mod_row0):
    _, _, gate = _mod_row(mod_ref, mod_row0 + pl.program_id(0) // per_seq)
    _out_rows(oa_ref, or_ref, ga_ref, gr_ref, x_ref, gate, wpa_ref, wpr_ref, wo_ref,
              lng_ref, lnb_ref, y_ref)


def _out_lat_call(oa, orr, p, x2d, mod, w_pa, w_pr, w_out, ln_g, ln_b, *, seq, mod_row0):
    m = x2d.shape[0]
    tm = 512
    per_seq = seq // tm
    tile = lambda j: pl.BlockSpec((tm, D_MODEL), lambda i: (i, j))
    vec = pl.BlockSpec((1, D_MODEL), lambda i: (0, 0))
    return pl.pallas_call(
        functools.partial(_out_lat_kernel, per_seq=per_seq, mod_row0=mod_row0),
        grid=(m // tm,),
        in_specs=[tile(0), tile(0), tile(GA_BLK * LANES // D_MODEL), tile(GR_BLK * LANES // D_MODEL),
                  tile(0),
                  pl.BlockSpec((MOD_ROWS, 3 * D_MODEL), lambda i: (0, 0)),
                  _resident((D_MODEL, D_MODEL)), _resident((D_MODEL, D_MODEL)),
                  _resident((D_MODEL, D_MODEL)), vec, vec],
        out_specs=tile(0),
        out_shape=jax.ShapeDtypeStruct((m, D_MODEL), F32),
        compiler_params=_params(("arbitrary",)),
        name="out_lat",
    )(oa, orr, p, p, x2d, mod, w_pa, w_pr, w_out, ln_g, ln_b)


W_CHUNK = D_MODEL


def _weight_chunks(w_hbm):
    return [(k, slice(j * W_CHUNK, (j + 1) * W_CHUNK))
            for k, w in enumerate(w_hbm) for j in range(w.shape[1] // W_CHUNK)]


def _weight_store(n, chunks, w_scr, w_bf_hbm, out_sem):
    k, cols = chunks[n]
    return pltpu.make_async_copy(w_scr[k].at[:, cols], w_bf_hbm[k].at[:, cols], out_sem.at[n])


def _stream_f32_chunks(jobs, stage, in_sem):
    def load(n):
        return pltpu.make_async_copy(jobs[n][0], stage.at[n % 2], in_sem.at[n % 2])

    load(0).start()
    for n, (_, sink) in enumerate(jobs):
        if n + 1 < len(jobs):
            load(n + 1).start()
        load(n).wait()
        sink(stage.at[n % 2])


def _ctx_kernel(x_ref, cctx_ref, c_ref, wmod_hbm, bmod_ref, win_hbm, wg_hbm, bg_ref, lp_ref, sg_ref,
                rd_ref, gg_ref, wpa_hbm, wpr_hbm, wo_hbm, lng_ref, lnb_ref,
                y_ref, k_ref, v_ref, sf_ref, sb_ref, mod_ref, win_bf, wg_bf, wpa_bf, wpr_bf, wo_bf,
                p_scr, oa_scr, or_scr, u_scr, s_scr, dec_scr, qdec_scr, kdec_scr,
                win_ref, wg_ref, wpa_ref, wpr_ref, wo_ref, cond_scr, stage, in_sem, out_sem,
                *, npairs):
    w_hbm = (win_hbm, wg_hbm, wpa_hbm, wpr_hbm, wo_hbm)
    w_scr = (win_ref, wg_ref, wpa_ref, wpr_ref, wo_ref)
    w_bf = (win_bf, wg_bf, wpa_bf, wpr_bf, wo_bf)

    @pl.when(pl.program_id(0) == 0)
    def _():
        cond_scr[...] = jnp.zeros(cond_scr.shape, F32)
        cond_scr[0:1, :] = cctx_ref[...]
        cond_scr[1:1 + c_ref.shape[0], :] = c_ref[...]
        cond = _silu(cond_scr[...]).astype(BF16)
        jobs = []
        for j in range(wmod_hbm.shape[1] // W_CHUNK):
            cols = slice(j * W_CHUNK, (j + 1) * W_CHUNK)

            def mod_sink(staged, cols=cols):
                mod_ref[:, cols] = _dot(cond, staged[...].astype(BF16)) + bmod_ref[:, cols]
            jobs.append((wmod_hbm.at[:, cols], mod_sink))
        chunks = _weight_chunks(w_hbm)
        for k, cols in chunks:
            def cast_sink(staged, k=k, cols=cols):
                w_scr[k][:, cols] = staged[...].astype(BF16)
            jobs.append((w_hbm[k].at[:, cols], cast_sink))
        _stream_f32_chunks(jobs, stage, in_sem)
        for n in range(len(chunks)):
            _weight_store(n, chunks, w_scr, w_bf, out_sem).start()
        _ret_build_tables(rd_ref, dec_scr, qdec_scr, kdec_scr, npairs=npairs, pair0=0)

    cols = lambda blk, w: p_scr.at[:, blk * LANES:blk * LANES + w]
    shift, scale, gate = _mod_row(mod_ref, 0)
    _project(_modulated_ln(x_ref, shift, scale), win_ref, wg_ref, bg_ref, p_scr,
             kv_out=(k_ref, v_ref))
    _attn_ctx_heads(cols(QA_BLK, D_MODEL), cols(KA_BLK, D_MODEL), cols(VA_BLK, D_MODEL),
                    cols(ZA_BLK, D_MODEL), _lam(lp_ref), sg_ref[...], oa_scr)
    _ret_pairs(rd_ref, cols(QR_BLK, D_MODEL // 2), cols(KR_BLK, D_MODEL // 2),
               cols(VR_BLK, D_MODEL), cols(ZR_BLK, D_MODEL), gg_ref, None, or_scr,
               (sf_ref, sb_ref), u_scr, s_scr, dec_scr, qdec_scr, kdec_scr,
               nc=x_ref.shape[0] // CHUNK, npairs=npairs, pair0=0)
    _out_rows(oa_scr, or_scr, cols(GA_BLK, D_MODEL), cols(GR_BLK, D_MODEL), x_ref, gate,
              wpa_ref, wpr_ref, wo_ref, lng_ref, lnb_ref, y_ref)

    @pl.when(pl.program_id(0) == pl.num_programs(0) - 1)
    def _():
        chunks = _weight_chunks(w_hbm)
        for n in range(len(chunks)):
            _weight_store(n, chunks, w_scr, w_bf, out_sem).wait()


def _ctx_call(x2d, c_ctx, c, w_mod, b_mod, w_in, w_gate, b_gate, lam_params, subln_g, ret_decay,
              gn_g, w_pa, w_pr, w_out, ln_g, ln_b, *, batch, seq):
    npairs = N_HEADS // 2
    nc = seq // CHUNK
    tile = pl.BlockSpec((seq, D_MODEL), lambda b: (b, 0))
    vec = lambda w: pl.BlockSpec((1, w), lambda b: (0, 0))
    state = pl.BlockSpec((None, N_HEADS, DK_R, DV), lambda b: (b, 0, 0, 0))
    heads = pl.BlockSpec((seq * N_HEADS, LANES), lambda b: (b, 0))
    tok = jax.ShapeDtypeStruct((batch * seq, D_MODEL), F32)
    tok_heads = jax.ShapeDtypeStruct((batch * seq * N_HEADS, LANES), F32)
    st = jax.ShapeDtypeStruct((batch, N_HEADS, DK_R, DV), F32)
    hbm = pl.BlockSpec(memory_space=pl.ANY)
    weights = (w_in, w_gate, w_pa, w_pr, w_out)
    n_chunks = sum(w.shape[1] // W_CHUNK for w in weights)
    return pl.pallas_call(
        functools.partial(_ctx_kernel, npairs=npairs),
        grid=(batch,),
        in_specs=[tile, vec(D_MODEL), pl.BlockSpec(c.shape, lambda b: (0, 0)), hbm,
                  vec(3 * D_MODEL), hbm, hbm, vec(2 * D_MODEL),
                  pl.BlockSpec((4, DK_A), lambda b: (0, 0)), vec(DV),
                  pl.BlockSpec(memory_space=pltpu.SMEM), vec(D_MODEL),
                  hbm, hbm, hbm, vec(D_MODEL), vec(D_MODEL)],
        out_specs=[tile, heads, heads, state, state,
                   pl.BlockSpec((MOD_ROWS, 3 * D_MODEL), lambda b: (0, 0))] + [hbm] * len(weights),
        out_shape=[tok, tok_heads, tok_heads, st, st, jax.ShapeDtypeStruct((MOD_ROWS, 3 * D_MODEL), F32)]
                  + [jax.ShapeDtypeStruct(w.shape, BF16) for w in weights],
        scratch_shapes=[pltpu.VMEM((seq, P_WIDTH), BF16),
                        pltpu.VMEM((seq, D_MODEL), BF16),
                        pltpu.VMEM((seq, D_MODEL), BF16)] + _ret_scratch(npairs, nc)
                       + [pltpu.VMEM(w.shape, BF16) for w in weights]
                       + [pltpu.VMEM((MOD_ROWS, D_MODEL), F32),
                          pltpu.VMEM((2, D_MODEL, W_CHUNK), F32),
                          pltpu.SemaphoreType.DMA((2,)),
                          pltpu.SemaphoreType.DMA((n_chunks,))],
        compiler_params=_params(("arbitrary",)),
        name="ctx",
    )(x2d, c_ctx, c, w_mod, b_mod, w_in, w_gate, b_gate, lam_params, subln_g, ret_decay, gn_g,
      w_pa, w_pr, w_out, ln_g, ln_b)


def _rope_tables(n_tokens):
    rows = n_tokens // GRID_W
    r = np.repeat(np.arange(rows, dtype=np.float32), GRID_W)
    col = np.tile(np.arange(GRID_W, dtype=np.float32), rows)
    n_freq = DK_A // 4
    inv = np.float32(ROPE_BASE) ** (-np.arange(n_freq, dtype=np.float32) / np.float32(n_freq))
    ang = np.concatenate([r[:, None] * inv, col[:, None] * inv], axis=-1).astype(np.float32)
    cos = np.repeat(np.cos(ang), 2, axis=-1)
    sin = np.repeat(np.sin(ang), 2, axis=-1)
    even = (np.arange(DK_A) % 2 == 0)[None, :]
    sin_even = np.where(even, -sin, 0.0)
    sin_odd = np.where(even, 0.0, sin)
    two = lambda t: jnp.asarray(np.concatenate([t, t], axis=-1), F32)
    return two(cos), two(sin_even), two(sin_odd)


def kernel(x_prompt, x_sample, cache_attn_k, cache_attn_v, state_ret_fwd, state_ret_bwd,
           c, c_ctx, w_mod, b_mod, w_in, lam_params, subln_g, ret_decay, ret_gn_g,
           w_pa, w_pr, w_gate, b_gate, w_out, ln_g, ln_b):
    batch, seq, _ = x_prompt.shape
    dbatch, dseq, _ = x_sample.shape
    past = cache_attn_k.shape[2]
    l = 0

    bg = b_gate[l][None, :]
    lp, sg = lam_params[l], subln_g[l][None, :]
    rd, gg = ret_decay[l], ret_gn_g[l][None, :]
    lng, lnb = ln_g[l][None, :], ln_b[l][None, :]

    xc = x_prompt.reshape(batch * seq, D_MODEL)
    y_c, k_c, v_c, sf_c, sb_c, mod, win, wg, wpa, wpr, wo = _ctx_call(
        xc, c_ctx[None, :], c, w_mod[l], b_mod[l][None, :], w_in[l], w_gate[l], bg, lp, sg, rd, gg,
        w_pa[l], w_pr[l], w_out[l], lng, lnb, batch=batch, seq=seq)

    xs = x_sample.reshape(dbatch * dseq, D_MODEL)
    p_s = _proj_lat_call(xs, mod, win, wg, bg, _rope_tables(dseq), seq=dseq, mod_row0=1)
    ck = cache_attn_k[:, l].reshape(dbatch, past * N_HEADS, 2 * DK_A)
    cv = cache_attn_v[:, l].reshape(dbatch, past * N_HEADS, DV)
    oa_s = _attn_lat_call(p_s, ck, cv, lp, sg, batch=dbatch, seq=dseq, past=past)
    or_s = _ret_lat_call(p_s, rd, gg, state_ret_fwd[:, l], state_ret_bwd[:, l],
                         batch=dbatch, seq=dseq)
    y_s = _out_lat_call(oa_s, or_s, p_s, xs, mod, wpa, wpr, wo, lng, lnb, seq=dseq, mod_row0=1)

    return (y_c.reshape(batch, seq, D_MODEL),
            y_s.reshape(dbatch, dseq, D_MODEL),
            k_c.reshape(batch, 1, seq, N_HEADS, 2 * DK_A),
            v_c.reshape(batch, 1, seq, N_HEADS, DV),
            sf_c.reshape(batch, 1, N_HEADS, DK_R, DV),
            sb_c.reshape(batch, 1, N_HEADS, DK_R, DV))
```
